```python
import math
import jax
import jax.numpy as jnp
from jax import lax
import numpy as np

D_MODEL = 2048
BATCH = 4
SEQ = 4096
DEPTH = 4

RMS_EPS = 1e-6
S5_WIDTH = D_MODEL // 2
S5_GROUP = 16
S5_GROUPS = S5_WIDTH // S5_GROUP
S5_STATE = 64
GDN_HEAD_DIM = 128
GDN_WIDTH = D_MODEL // 2
GDN_HEADS = GDN_WIDTH // GDN_HEAD_DIM
CONV_K = 4
CHUNK = 64
N_EXPERTS = 16
N_GROUPS = 4
EXPERTS_PER_GROUP = N_EXPERTS // N_GROUPS
TOP_K = 2
D_FF = D_MODEL // 2
IN_SPLITS = (S5_WIDTH, GDN_WIDTH, GDN_WIDTH, GDN_WIDTH, GDN_WIDTH, GDN_HEADS, GDN_HEADS, D_MODEL, D_MODEL)
IN_COLS = S5_WIDTH + 4 * GDN_WIDTH + 2 * GDN_HEADS + 2 * D_MODEL

kernel_name = "hybrid_s5_gdn_grouped_moe_adaln"


def rmsnorm(x, w):
    xf = x.astype(jnp.float32)
    y = xf * lax.rsqrt(jnp.mean(xf * xf, axis=-1, keepdims=True) + RMS_EPS)
    return (y * w.astype(jnp.float32)).astype(x.dtype)


def l2norm(x):
    return x * lax.rsqrt(jnp.sum(x * x, axis=-1, keepdims=True) + RMS_EPS)


def s5_mixer(u, a_re, a_im, log_dt, b_re, b_im, c_re, c_im, d_skip, w_glu, b_glu):
    bsz, seq, _ = u.shape
    f32 = jnp.float32
    uf = u.astype(f32)
    ug = uf.reshape(bsz, seq, S5_GROUPS, S5_GROUP)
    lam = lax.complex(a_re.astype(f32), a_im.astype(f32))
    dt = jnp.exp(log_dt.astype(f32))[:, None]
    a_bar = jnp.exp(lam * dt)
    b_bar = ((a_bar - 1.0) / lam)[..., None] * lax.complex(b_re.astype(f32), b_im.astype(f32))
    bu = lax.complex(jnp.einsum('blgn,gpn->blgp', ug, b_bar.real),
                     jnp.einsum('blgn,gpn->blgp', ug, b_bar.imag))
    a_seq = jnp.broadcast_to(a_bar, bu.shape)

    def combine(left, right):
        a_l, s_l = left
        a_r, s_r = right
        return a_r * a_l, a_r * s_l + s_r

    _, states = lax.associative_scan(combine, (a_seq, bu), axis=1)
    c_mat = lax.complex(c_re.astype(f32), c_im.astype(f32))
    y = jnp.real(jnp.einsum('blgp,gnp->blgn', states, c_mat)).reshape(bsz, seq, S5_WIDTH)
    y = jax.nn.gelu(y + d_skip.astype(f32) * uf)
    y = y * jax.nn.sigmoid(y @ w_glu.astype(f32) + b_glu.astype(f32))
    return y.astype(u.dtype)


def causal_conv(x, w):
    ch = x.shape[-1]
    return lax.conv_general_dilated(x, w[:, None, :].astype(x.dtype), window_strides=(1,),
                                    padding=[(CONV_K - 1, 0)],
                                    dimension_numbers=('NWC', 'WIO', 'NWC'),
                                    feature_group_count=ch)


def gated_delta_chunked(q, k, v, g, beta):
    bsz, seq, heads, dk = q.shape
    dv = v.shape[-1]
    n_chunks = seq // CHUNK

    def to_chunks(t):
        t = jnp.swapaxes(t, 1, 2)
        return t.reshape(bsz, heads, n_chunks, CHUNK, *t.shape[3:])

    q, k, v, g, beta = (to_chunks(t) for t in (q, k, v, g, beta))
    g = jnp.cumsum(g, axis=-1)
    causal = jnp.tril(jnp.ones((CHUNK, CHUNK), dtype=bool))
    strict = jnp.tril(jnp.ones((CHUNK, CHUNK), dtype=bool), k=-1)
    decay = jnp.exp(jnp.where(causal, g[..., :, None] - g[..., None, :], -jnp.inf))
    k_beta = k * beta[..., None]
    v_beta = v * beta[..., None]
    lmat = jnp.where(strict, jnp.einsum('bhncd,bhnsd->bhncs', k_beta, k) * decay, 0.0)
    eye = jnp.eye(CHUNK, dtype=q.dtype)
    t_inv = lax.linalg.triangular_solve(eye + lmat, jnp.broadcast_to(eye, lmat.shape),
                                        left_side=True, lower=True)
    u = jnp.einsum('bhncs,bhnse->bhnce', t_inv, v_beta)
    w = jnp.einsum('bhncs,bhnsd->bhncd', t_inv, k_beta * jnp.exp(g)[..., None])
    attn = jnp.where(causal, jnp.einsum('bhncd,bhnsd->bhncs', q, k) * decay, 0.0)
    q_dec = q * jnp.exp(g)[..., None]
    g_last = g[..., -1]
    k_dec = k * jnp.exp(g_last[..., None] - g)[..., None]
    xs = tuple(jnp.moveaxis(t, 2, 0) for t in (q_dec, k_dec, u, w, attn, g_last))

    def step(state, inp):
        qd, kd, ui, wi, ai, gl = inp
        v_new = ui - jnp.einsum('bhcd,bhde->bhce', wi, state)
        o = jnp.einsum('bhcd,bhde->bhce', qd, state) + jnp.einsum('bhcs,bhse->bhce', ai, v_new)
        state = state * jnp.exp(gl)[..., None, None] + jnp.einsum('bhcd,bhce->bhde', kd, v_new)
        return state, o

    s0 = jnp.zeros((bsz, heads, dk, dv), q.dtype)
    _, o = lax.scan(step, s0, xs)
    o = jnp.moveaxis(o, 0, 2).reshape(bsz, heads, seq, dv)
    return jnp.swapaxes(o, 1, 2)


def gdn_mixer(q, k, v, z, b_raw, a_raw, conv_w, a_log, dt_bias, out_norm_w):
    bsz, seq, _ = q.shape
    f32 = jnp.float32
    qkv = jax.nn.silu(causal_conv(jnp.concatenate([q, k, v], axis=-1), conv_w)).astype(f32)
    q, k, v = jnp.split(qkv, 3, axis=-1)
    shp = (bsz, seq, GDN_HEADS, GDN_HEAD_DIM)
    q = l2norm(q.reshape(shp)) * (GDN_HEAD_DIM ** -0.5)
    k = l2norm(k.reshape(shp))
    v = v.reshape(shp)
    beta = jax.nn.sigmoid(b_raw.astype(f32))
    g = -jnp.exp(a_log.astype(f32)) * jax.nn.softplus(a_raw.astype(f32) + dt_bias.astype(f32))
    o = gated_delta_chunked(q, k, v, g, beta)
    o = rmsnorm(o, out_norm_w) * jax.nn.silu(z.astype(f32).reshape(shp))
    return o.reshape(bsz, seq, GDN_WIDTH).astype(z.dtype)


def moe(h, w_router, router_bias, w_gate, w_up, w_down):
    bsz, seq, d = h.shape
    x = h.reshape(-1, d)
    f32 = jnp.float32
    scores = jax.nn.sigmoid(x.astype(f32) @ w_router.astype(f32))
    sel = (scores + router_bias.astype(f32)).reshape(-1, N_GROUPS, EXPERTS_PER_GROUP)
    group_score = jnp.sum(lax.top_k(sel, 2)[0], axis=-1)
    best_group = jnp.argmax(group_score, axis=-1)
    in_group = jnp.take_along_axis(sel, best_group[:, None, None], axis=1)[:, 0]
    _, local_idx = lax.top_k(in_group, TOP_K)
    expert_idx = best_group[:, None] * EXPERTS_PER_GROUP + local_idx
    gate_w = jnp.take_along_axis(scores, expert_idx, axis=-1)
    gate_w = gate_w / jnp.sum(gate_w, axis=-1, keepdims=True)
    flat_e = expert_idx.reshape(-1)
    order = jnp.argsort(flat_e)
    tok = order // TOP_K
    group_sizes = jnp.bincount(flat_e, length=N_EXPERTS).astype(jnp.int32)
    xs = x[tok]
    hid = jax.nn.silu(lax.ragged_dot(xs, w_gate, group_sizes)) * lax.ragged_dot(xs, w_up, group_sizes)
    ys = lax.ragged_dot(hid, w_down, group_sizes) * gate_w.reshape(-1)[order][:, None].astype(x.dtype)
    out = jnp.zeros_like(x).at[tok].add(ys)
    return out.reshape(bsz, seq, d)


def setup_inputs(seed: int = 0) -> dict:
    key = jax.random.key(seed)
    ks = jax.random.split(key, 32)
    f32 = jnp.float32
    L = DEPTH

    def nrm(k, shape, scale):
        return jax.random.normal(k, shape, f32) * scale

    x = nrm(ks[0], (BATCH, SEQ, D_MODEL), 1.0)
    c = nrm(ks[1], (BATCH, D_MODEL), 1.0)
    w_ada = nrm(ks[2], (L, D_MODEL, 6 * D_MODEL), 0.5 * D_MODEL ** -0.5)
    b_ada = nrm(ks[3], (L, 6 * D_MODEL), 0.02)
    norm1_w = 1.0 + nrm(ks[4], (L, D_MODEL), 0.02)
    norm2_w = 1.0 + nrm(ks[5], (L, D_MODEL), 0.02)
    w_in = nrm(ks[6], (L, D_MODEL, IN_COLS), D_MODEL ** -0.5)
    n_idx = jnp.arange(S5_STATE, dtype=f32)
    s5_a_re = -0.5 * jnp.exp(nrm(ks[7], (L, S5_GROUPS, S5_STATE), 0.05))
    s5_a_im = math.pi * n_idx + nrm(ks[8], (L, S5_GROUPS, S5_STATE), 0.01)
    s5_log_dt = jax.random.uniform(ks[9], (L, S5_GROUPS), f32, math.log(1e-3), math.log(1e-1))
    s5_b_re = nrm(ks[10], (L, S5_GROUPS, S5_STATE, S5_GROUP), (2 * S5_GROUP) ** -0.5)
    s5_b_im = nrm(ks[11], (L, S5_GROUPS, S5_STATE, S5_GROUP), (2 * S5_GROUP) ** -0.5)
    s5_c_re = nrm(ks[12], (L, S5_GROUPS, S5_GROUP, S5_STATE), 0.5)
    s5_c_im = nrm(ks[13], (L, S5_GROUPS, S5_GROUP, S5_STATE), 0.5)
    s5_d = nrm(ks[14], (L, S5_WIDTH), 1.0)
    s5_w_glu = nrm(ks[15], (L, S5_WIDTH, S5_WIDTH), S5_WIDTH ** -0.5)
    s5_b_glu = nrm(ks[16], (L, S5_WIDTH), 0.02)
    gdn_conv_w = nrm(ks[17], (L, CONV_K, 3 * GDN_WIDTH), CONV_K ** -0.5)
    gdn_a_log = jnp.log(jax.random.uniform(ks[18], (L, GDN_HEADS), f32, 1.0, 16.0))
    gdn_dt = jnp.exp(jax.random.uniform(ks[19], (L, GDN_HEADS), f32, math.log(1e-3), math.log(1e-1)))
    gdn_dt_bias = gdn_dt + jnp.log(-jnp.expm1(-gdn_dt))
    gdn_norm_w = 1.0 + nrm(ks[20], (L, GDN_HEAD_DIM), 0.02)
    w_proj_a = nrm(ks[21], (L, S5_WIDTH, D_MODEL), S5_WIDTH ** -0.5)
    w_proj_b = nrm(ks[22], (L, GDN_WIDTH, D_MODEL), GDN_WIDTH ** -0.5)
    w_out = nrm(ks[23], (L, D_MODEL, D_MODEL), D_MODEL ** -0.5)
    w_router = nrm(ks[24], (D_MODEL, N_EXPERTS), D_MODEL ** -0.5)
    router_bias = nrm(ks[25], (N_EXPERTS,), 0.01)
    w_gate = nrm(ks[26], (L, N_EXPERTS, D_MODEL, D_FF), D_MODEL ** -0.5)
    w_up = nrm(ks[27], (L, N_EXPERTS, D_MODEL, D_FF), D_MODEL ** -0.5)
    w_down = nrm(ks[28], (L, N_EXPERTS, D_FF, D_MODEL), D_FF ** -0.5)
    final_norm_w = 1.0 + nrm(ks[29], (D_MODEL,), 0.02)
    return {"x": x, "c": c, "w_ada": w_ada, "b_ada": b_ada, "norm1_w": norm1_w, "norm2_w": norm2_w,
            "w_in": w_in, "s5_a_re": s5_a_re, "s5_a_im": s5_a_im, "s5_log_dt": s5_log_dt,
            "s5_b_re": s5_b_re, "s5_b_im": s5_b_im, "s5_c_re": s5_c_re, "s5_c_im": s5_c_im,
            "s5_d": s5_d, "s5_w_glu": s5_w_glu, "s5_b_glu": s5_b_glu, "gdn_conv_w": gdn_conv_w,
            "gdn_a_log": gdn_a_log, "gdn_dt_bias": gdn_dt_bias, "gdn_norm_w": gdn_norm_w,
            "w_proj_a": w_proj_a, "w_proj_b": w_proj_b, "w_out": w_out, "w_router": w_router,
            "router_bias": router_bias, "w_gate": w_gate, "w_up": w_up, "w_down": w_down,
            "final_norm_w": final_norm_w}


def reference(x, c, w_ada, b_ada, norm1_w, norm2_w, w_in, s5_a_re, s5_a_im, s5_log_dt,
              s5_b_re, s5_b_im, s5_c_re, s5_c_im, s5_d, s5_w_glu, s5_b_glu, gdn_conv_w,
              gdn_a_log, gdn_dt_bias, gdn_norm_w, w_proj_a, w_proj_b, w_out, w_router,
              router_bias, w_gate, w_up, w_down, final_norm_w):
    split_at = [int(i) for i in np.cumsum(IN_SPLITS)[:-1]]
    c_act = jax.nn.silu(c)
    for l in range(DEPTH):
        mod = (c_act @ w_ada[l] + b_ada[l])[:, None, :]
        sh1, sc1, g1, sh2, sc2, g2 = jnp.split(mod, 6, axis=-1)
        h = rmsnorm(x, norm1_w[l]) * (1.0 + sc1) + sh1
        proj = h @ w_in[l]
        u_s5, q, k, v, z, b_raw, a_raw, gate_a, gate_b = jnp.split(proj, split_at, axis=-1)
        y_a = s5_mixer(u_s5, s5_a_re[l], s5_a_im[l], s5_log_dt[l], s5_b_re[l], s5_b_im[l],
                       s5_c_re[l], s5_c_im[l], s5_d[l], s5_w_glu[l], s5_b_glu[l])
        y_b = gdn_mixer(q, k, v, z, b_raw, a_raw, gdn_conv_w[l], gdn_a_log[l], gdn_dt_bias[l],
                        gdn_norm_w[l])
        merged = (jax.nn.sigmoid(gate_a) * (y_a @ w_proj_a[l])
                  + jax.nn.sigmoid(gate_b) * (y_b @ w_proj_b[l]))
        x = x + g1 * (merged @ w_out[l])
        h2 = rmsnorm(x, norm2_w[l]) * (1.0 + sc2) + sh2
        x = x + g2 * moe(h2, w_router, router_bias, w_gate[l], w_up[l], w_down[l])
    return rmsnorm(x, final_norm_w)
```

```python
import functools
import math

import jax
import jax.numpy as jnp
from jax import lax
from jax.experimental import pallas as pl
from jax.experimental.pallas import tpu as pltpu

F32 = jnp.float32
BF16 = jnp.bfloat16
HIGHEST = lax.Precision.HIGHEST

RMS_EPS = 1e-6
S5_GROUP = 16
S5_STATE = 64
S5_CHUNK = 16
HEAD_DIM = 128
GDN_CHUNK = 64
GDN_TILE = 256
CONV_K = 4
N_EXPERTS = 16
N_GROUPS = 4
EXPERTS_PER_GROUP = N_EXPERTS // N_GROUPS
TOP_K = 2
MIB = 1024 * 1024


def _cparams(semantics, vmem_mib):
    return pltpu.CompilerParams(dimension_semantics=semantics, vmem_limit_bytes=vmem_mib * MIB)


def _silu(x):
    return x * jax.nn.sigmoid(x)


def _softplus(x):
    return jnp.maximum(x, 0.0) + jnp.log1p(jnp.exp(-jnp.abs(x)))


def _rms_scale(x):
    return lax.rsqrt(jnp.mean(x * x, axis=-1, keepdims=True) + RMS_EPS)


def _nt_dot(a, b):
    return lax.dot_general(a, b, (((1,), (1,)), ((), ())), preferred_element_type=F32)


def _tn_dot(a, b):
    return lax.dot_general(a, b, (((0,), (0,)), ((), ())), preferred_element_type=F32)


def _ada_kernel(c_ref, w_ref, b_ref, o_ref):
    ca = _silu(c_ref[...]).astype(BF16)
    o_ref[...] = jnp.dot(ca, w_ref[...].astype(BF16), preferred_element_type=F32) + b_ref[...]


def _ada(c_pad, w_ada, b_ada):
    depth, d, n = w_ada.shape
    rows = c_pad.shape[0]
    tn = min(n, 1024)
    return pl.pallas_call(
        _ada_kernel,
        grid=(depth, n // tn),
        in_specs=[pl.BlockSpec((rows, d), lambda l, j: (0, 0)),
                  pl.BlockSpec((None, d, tn), lambda l, j: (l, 0, j)),
                  pl.BlockSpec((None, 1, tn), lambda l, j: (l, 0, j))],
        out_specs=pl.BlockSpec((None, rows, tn), lambda l, j: (l, 0, j)),
        out_shape=jax.ShapeDtypeStruct((depth, rows, n), F32),
        compiler_params=_cparams(("parallel", "parallel"), 40),
    )(c_pad, w_ada, b_ada.reshape(depth, 1, n))


def _inproj_kernel(x_ref, mod_ref, nw_ref, w_ref, wsm_ref, o_ref, osm_ref, h_scr):
    @pl.when(pl.program_id(1) == 0)
    def _():
        x = x_ref[...]
        h = x * _rms_scale(x) * nw_ref[...] * (1.0 + mod_ref[1:2, :]) + mod_ref[0:1, :]
        hb = h.astype(BF16)
        h_scr[...] = hb
        osm_ref[...] = jnp.dot(hb, wsm_ref[...], preferred_element_type=F32)

    o_ref[...] = jnp.dot(h_scr[...], w_ref[...], preferred_element_type=F32)


def _inproj(x2, modl, nw, w_big, w_small, seq):
    t, d = x2.shape
    n = w_big.shape[1]
    tm = min(seq, 1024)
    tn = 512
    per_batch = seq // tm
    return pl.pallas_call(
        _inproj_kernel,
        grid=(t // tm, n // tn),
        in_specs=[pl.BlockSpec((tm, d), lambda i, j: (i, 0)),
                  pl.BlockSpec((None, 6, d), lambda i, j: (i // per_batch, 0, 0)),
                  pl.BlockSpec((1, d), lambda i, j: (0, 0)),
                  pl.BlockSpec((d, tn), lambda i, j: (0, j)),
                  pl.BlockSpec((d, 128), lambda i, j: (0, 0))],
        out_specs=[pl.BlockSpec((tm, tn), lambda i, j: (i, j)),
                   pl.BlockSpec((tm, 128), lambda i, j: (i, 0))],
        out_shape=[jax.ShapeDtypeStruct((t, n), F32), jax.ShapeDtypeStruct((t, 128), F32)],
        scratch_shapes=[pltpu.VMEM((tm, d), BF16)],
        compiler_params=_cparams(("parallel", "arbitrary"), 48),
    )(x2, modl, nw, w_big, w_small)


def _s5_weights(a_re, a_im, log_dt, b_re, b_im, c_re, c_im, n_levels):
    tc = S5_CHUNK
    groups, p = a_re.shape
    n = b_re.shape[-1]
    lam = lax.complex(a_re, a_im)
    dt = jnp.exp(log_dt)[:, None]
    log_a = lam * dt
    a_bar = jnp.exp(log_a)
    b_bar = ((a_bar - 1.0) / lam)[..., None] * lax.complex(b_re, b_im)
    c_mat = lax.complex(c_re, c_im)
    taus = jnp.arange(tc + 1, dtype=F32)
    apow = jnp.exp(log_a[None] * taus[:, None, None])
    kern = jnp.real(jnp.einsum('gnp,tgp,gpm->tgnm', c_mat, apow[:tc], b_bar, precision=HIGHEST))
    lag = jnp.arange(tc)[None, :] - jnp.arange(tc)[:, None]
    kt = jnp.where((lag >= 0)[:, :, None, None, None], kern[jnp.clip(lag, 0, tc - 1)], 0.0)
    toep = jnp.transpose(kt, (2, 0, 4, 1, 3)).reshape(groups, tc * n, tc * n)
    hin_c = apow[tc - 1 - jnp.arange(tc)][:, :, :, None] * b_bar[None]
    hin_c = jnp.transpose(hin_c, (1, 0, 3, 2)).reshape(groups, tc * n, p)
    hin = jnp.concatenate([jnp.real(hin_c), jnp.imag(hin_c)], axis=-1)
    coef = c_mat[None] * apow[1:tc + 1][:, :, None, :]
    hout = jnp.concatenate([jnp.real(coef), -jnp.imag(coef)], axis=-1)
    hout = jnp.transpose(hout, (1, 3, 0, 2)).reshape(groups, 2 * p, tc * n)
    steps = (tc * (2.0 ** jnp.arange(n_levels, dtype=F32)))
    amul = jnp.exp(log_a[:, None, :] * steps[None, :, None])
    a1 = jnp.concatenate([jnp.real(amul), jnp.real(amul)], axis=-1)
    a2 = jnp.concatenate([-jnp.imag(amul), jnp.imag(amul)], axis=-1)
    return toep.astype(BF16), hin.astype(BF16), hout.astype(BF16), a1.astype(F32), a2.astype(F32)


def _s5_core_kernel(u_ref, toep_ref, hin_ref, hout_ref, a1_ref, a2_ref, y_ref, *, chunks_per_seq, n_levels):
    u = u_ref[...]
    s = jnp.dot(u, hin_ref[...], preferred_element_type=F32)
    rows, width = s.shape
    half = width // 2
    pos = lax.broadcasted_iota(jnp.int32, (rows, width), 0) % chunks_per_seq
    for k in range(n_levels):
        d = 1 << k
        prev = jnp.where(pos >= d, pltpu.roll(s, d, axis=0), 0.0)
        s = s + a1_ref[k:k + 1, :] * prev + a2_ref[k:k + 1, :] * pltpu.roll(prev, half, axis=1)
    s_in = jnp.where(pos >= 1, pltpu.roll(s, 1, axis=0), 0.0)
    y_ref[...] = (jnp.dot(u, toep_ref[...], preferred_element_type=F32)
                  + jnp.dot(s_in.astype(BF16), hout_ref[...], preferred_element_type=F32))


def _s5_core(u_g, toep, hin, hout, a1, a2, chunks_per_seq):
    groups, rows, width = u_g.shape
    n_levels = a1.shape[1]
    sw = hin.shape[-1]
    kern = functools.partial(_s5_core_kernel, chunks_per_seq=chunks_per_seq, n_levels=n_levels)
    return pl.pallas_call(
        kern,
        grid=(groups,),
        in_specs=[pl.BlockSpec((None, rows, width), lambda g: (g, 0, 0)),
                  pl.BlockSpec((None, width, width), lambda g: (g, 0, 0)),
                  pl.BlockSpec((None, width, sw), lambda g: (g, 0, 0)),
                  pl.BlockSpec((None, sw, width), lambda g: (g, 0, 0)),
                  pl.BlockSpec((None, n_levels, sw), lambda g: (g, 0, 0)),
                  pl.BlockSpec((None, n_levels, sw), lambda g: (g, 0, 0))],
        out_specs=pl.BlockSpec((None, rows, width), lambda g: (g, 0, 0)),
        out_shape=jax.ShapeDtypeStruct((groups, rows, width), F32),
        compiler_params=_cparams(("parallel",), 32),
    )(u_g, toep, hin, hout, a1, a2)


def _s5_post_kernel(y_ref, u_ref, d_ref, w_ref, b_ref, o_ref):
    y = jax.nn.gelu(y_ref[...] + d_ref[...] * u_ref[...])
    gate = jnp.dot(y.astype(BF16), w_ref[...], preferred_element_type=F32) + b_ref[...]
    o_ref[...] = (y * jax.nn.sigmoid(gate)).astype(o_ref.dtype)


def _s5_post(y_lin, proj, u_col, d_skip, w_glu, b_glu):
    t, w = y_lin.shape
    tm = min(t, 512)
    return pl.pallas_call(
        _s5_post_kernel,
        grid=(t // tm,),
        in_specs=[pl.BlockSpec((tm, w), lambda i: (i, 0)),
                  pl.BlockSpec((tm, w), lambda i: (i, u_col)),
                  pl.BlockSpec((1, w), lambda i: (0, 0)),
                  pl.BlockSpec((w, w), lambda i: (0, 0)),
                  pl.BlockSpec((1, w), lambda i: (0, 0))],
        out_specs=pl.BlockSpec((tm, w), lambda i: (i, 0)),
        out_shape=jax.ShapeDtypeStruct((t, w), BF16),
        compiler_params=_cparams(("parallel",), 32),
    )(y_lin, proj, d_skip, w_glu, b_glu)


def _gdn_kernel(q_ref, k_ref, v_ref, qh_ref, kh_ref, vh_ref, z_ref, sm_ref, cq_ref, ck_ref, cv_ref,
                hp_ref, nw_ref, tril_ref, o_ref, betab, gcb, grow, st_scr, *, heads):
    tt = GDN_TILE
    hd = HEAD_DIM
    nck = tt // GDN_CHUNK
    first = pl.program_id(1) == 0

    @pl.when(first)
    def _():
        st_scr[...] = jnp.zeros_like(st_scr)

    sm = sm_ref[...]
    hp = hp_ref[...]
    beta_all = jax.nn.sigmoid(sm)
    g_all = hp[0:1, :] * _softplus(sm + hp[1:2, :])
    gc_all = jnp.dot(tril_ref[...], g_all, precision=HIGHEST, preferred_element_type=F32)
    gc_t = gc_all.T
    for h in range(heads):
        betab[h] = jnp.broadcast_to(beta_all[:, h:h + 1], (tt, hd))
        gcb[h] = jnp.broadcast_to(gc_all[:, heads + h:heads + h + 1], (tt, hd))
        grow[h:h + 1, :] = gc_t[heads + h:heads + h + 1, :]

    ri = lax.broadcasted_iota(jnp.int32, (tt, tt), 0)
    ci = lax.broadcasted_iota(jnp.int32, (tt, tt), 1)
    same = (ri // GDN_CHUNK) == (ci // GDN_CHUNK)
    causal = same & (ci <= ri)
    strict = same & (ci < ri)
    eye = (ri == ci).astype(F32)
    row8 = lax.broadcasted_iota(jnp.int32, (8, hd), 0)

    def conv_silu(x_ref, halo_ref, cw_ref, hs):
        x = x_ref[:, pl.ds(hs, hd)]
        halo = jnp.where(first, 0.0, halo_ref[:, pl.ds(hs, hd)])
        w = cw_ref[:, pl.ds(hs, hd)]
        acc = x * w[CONV_K - 1:CONV_K, :]
        for s in range(1, CONV_K):
            rolled = pltpu.roll(x, s, axis=0)
            head8 = jnp.where(row8 < s, pltpu.roll(halo, s, axis=0), rolled[:8])
            shifted = jnp.concatenate([head8, rolled[8:]], axis=0)
            acc = acc + shifted * w[CONV_K - 1 - s:CONV_K - s, :]
        return _silu(acc)

    def head_body(h, carry):
        hs = pl.multiple_of(h * hd, hd)
        q = conv_silu(q_ref, qh_ref, cq_ref, hs)
        k = conv_silu(k_ref, kh_ref, ck_ref, hs)
        v = conv_silu(v_ref, vh_ref, cv_ref, hs)
        qn = q * (lax.rsqrt(jnp.sum(q * q, axis=-1, keepdims=True) + RMS_EPS) * (hd ** -0.5))
        kn = k * lax.rsqrt(jnp.sum(k * k, axis=-1, keepdims=True) + RMS_EPS)
        bb = betab[h]
        gc = gcb[h]
        gr = grow[pl.ds(h, 1), :]
        diff = jnp.concatenate([gc, gc], axis=1) - gr
        decay = jnp.exp(jnp.where(causal, diff, -1e30))
        kb = kn * bb
        knb = kn.astype(BF16)
        lmat = jnp.where(strict, _nt_dot(kb.astype(BF16), knb) * decay, 0.0)
        half = ((ri % 2) == 1) & ((ci // 1) == (ri // 1) - 1)
        tinv = eye - jnp.where(half, lmat, 0.0)
        d = 2
        while d < GDN_CHUNK:
            lower = (((ri // d) % 2) == 1) & ((ci // d) == (ri // d) - 1)
            bd = jnp.where(lower, lmat, 0.0).astype(BF16)
            tb = tinv.astype(BF16)
            tinv = tinv - jnp.dot(jnp.dot(tb, bd, preferred_element_type=F32).astype(BF16), tb,
                                  preferred_element_type=F32)
            d *= 2
        eg = jnp.exp(gc)
        rhs = jnp.concatenate([kb * eg, v * bb], axis=1).astype(BF16)
        wu = jnp.dot(tinv.astype(BF16), rhs, preferred_element_type=F32).astype(BF16)
        attn = jnp.where(causal, _nt_dot(qn.astype(BF16), knb) * decay, 0.0)
        awu = jnp.dot(attn.astype(BF16), wu, preferred_element_type=F32)
        qp = (qn * eg - awu[:, :hd]).astype(BF16)
        o_intra = awu[:, hd:]
        s = st_scr[h]
        outs = []
        for c in range(nck):
            r0 = c * GDN_CHUNK
            r1 = r0 + GDN_CHUNK
            gl = gc[r1 - 1:r1, :]
            kd = (kn[r0:r1] * jnp.exp(gl - gc[r0:r1])).astype(BF16)
            m = _tn_dot(kd, wu[r0:r1])
            sb = s.astype(BF16)
            outs.append(jnp.dot(qp[r0:r1], sb, preferred_element_type=F32) + o_intra[r0:r1])
            s = s * jnp.exp(gl) - jnp.dot(m[:, :hd].astype(BF16), sb, preferred_element_type=F32) + m[:, hd:]
        st_scr[h] = s
        o = jnp.concatenate(outs, axis=0)
        z = z_ref[:, pl.ds(hs, hd)]
        o_ref[:, pl.ds(hs, hd)] = (o * _rms_scale(o) * nw_ref[...] * _silu(z)).astype(o_ref.dtype)
        return carry

    lax.fori_loop(0, heads, head_body, 0)


def _gdn(proj, small, conv_w, head_params, norm_w, tril, cols, batch, seq, heads):
    t = proj.shape[0]
    tt = GDN_TILE
    width = heads * HEAD_DIM
    tiles = seq // tt
    cq, ck, cv, cz = cols

    def cur(col):
        return pl.BlockSpec((tt, width), lambda b, i: (b * tiles + i, col))

    def halo(col):
        return pl.BlockSpec((8, width), lambda b, i: (jnp.maximum((b * tiles + i) * (tt // 8) - 1, 0), col))

    def cw(col):
        return pl.BlockSpec((CONV_K, width), lambda b, i: (0, col))

    kern = functools.partial(_gdn_kernel, heads=heads)
    return pl.pallas_call(
        kern,
        grid=(batch, tiles),
        in_specs=[cur(cq), cur(ck), cur(cv), halo(cq), halo(ck), halo(cv), cur(cz),
                  pl.BlockSpec((tt, 128), lambda b, i: (b * tiles + i, 0)),
                  cw(0), cw(1), cw(2),
                  pl.BlockSpec((8, 128), lambda b, i: (0, 0)),
                  pl.BlockSpec((1, HEAD_DIM), lambda b, i: (0, 0)),
                  pl.BlockSpec((tt, tt), lambda b, i: (0, 0))],
        out_specs=pl.BlockSpec((tt, width), lambda b, i: (b * tiles + i, 0)),
        out_shape=jax.ShapeDtypeStruct((t, width), BF16),
        scratch_shapes=[pltpu.VMEM((heads, tt, HEAD_DIM), F32),
                        pltpu.VMEM((heads, tt, HEAD_DIM), F32),
                        pltpu.VMEM((8, tt), F32),
                        pltpu.VMEM((heads, HEAD_DIM, HEAD_DIM), F32)],
        compiler_params=_cparams(("parallel", "arbitrary"), 40),
    )(proj, proj, proj, proj, proj, proj, proj, small, conv_w, conv_w, conv_w, head_params, norm_w, tril)


def _merge_kernel(ya_ref, yb_ref, ga_ref, gb_ref, x_ref, mod_ref, wa_ref, wb_ref, wo_ref, o_ref):
    pa = jnp.dot(ya_ref[...], wa_ref[...], preferred_element_type=F32)
    pb = jnp.dot(yb_ref[...], wb_ref[...], preferred_element_type=F32)
    merged = jax.nn.sigmoid(ga_ref[...]) * pa + jax.nn.sigmoid(gb_ref[...]) * pb
    out = jnp.dot(merged.astype(BF16), wo_ref[...], preferred_element_type=F32)
    o_ref[...] = x_ref[...] + mod_ref[2:3, :] * out


def _merge(ya, yb, proj, x2, modl, wa, wb, wo, seq, ga_col, gb_col):
    t, d = x2.shape
    w = ya.shape[1]
    tm = min(seq, 256)
    per_batch = seq // tm
    const = dict(pipeline_mode=pl.Buffered(1))
    return pl.pallas_call(
        _merge_kernel,
        grid=(t // tm,),
        in_specs=[pl.BlockSpec((tm, w), lambda i: (i, 0)),
                  pl.BlockSpec((tm, w), lambda i: (i, 0)),
                  pl.BlockSpec((tm, d), lambda i: (i, ga_col)),
                  pl.BlockSpec((tm, d), lambda i: (i, gb_col)),
                  pl.BlockSpec((tm, d), lambda i: (i, 0)),
                  pl.BlockSpec((None, 6, d), lambda i: (i // per_batch, 0, 0)),
                  pl.BlockSpec((w, d), lambda i: (0, 0), **const),
                  pl.BlockSpec((w, d), lambda i: (0, 0), **const),
                  pl.BlockSpec((d, d), lambda i: (0, 0), **const)],
        out_specs=pl.BlockSpec((tm, d), lambda i: (i, 0)),
        out_shape=jax.ShapeDtypeStruct((t, d), F32),
        compiler_params=_cparams(("parallel",), 48),
    )(ya, yb, proj, proj, x2, modl, wa, wb, wo)


def _router_kernel(x_ref, mod_ref, nw_ref, wrt_ref, rb_ref, h_ref, r_ref):
    x = x_ref[...]
    h = x * _rms_scale(x) * nw_ref[...] * (1.0 + mod_ref[4:5, :]) + mod_ref[3:4, :]
    h_ref[...] = h.astype(h_ref.dtype)
    logits = lax.dot_general(wrt_ref[...], h, (((1,), (1,)), ((), ())), precision=HIGHEST,
                             preferred_element_type=F32)
    scores = jax.nn.sigmoid(logits)
    sel = scores + rb_ref[...]
    sel_rows = [sel[e:e + 1, :] for e in range(N_EXPERTS)]
    score_rows = [scores[e:e + 1, :] for e in range(N_EXPERTS)]
    epg = EXPERTS_PER_GROUP
    best_score = None
    best_group = None
    for g in range(N_GROUPS):
        rows = sel_rows[g * epg:(g + 1) * epg]
        gs = None
        for a in range(epg):
            for b in range(a + 1, epg):
                pair = rows[a] + rows[b]
                gs = pair if gs is None else jnp.maximum(gs, pair)
        if g == 0:
            best_score, best_group = gs, jnp.zeros_like(gs)
        else:
            better = gs > best_score
            best_group = jnp.where(better, float(g), best_group)
            best_score = jnp.where(better, gs, best_score)
    in_sel = []
    in_score = []
    for j in range(epg):
        a = sel_rows[j]
        b = score_rows[j]
        for g in range(1, N_GROUPS):
            pick = best_group == float(g)
            a = jnp.where(pick, sel_rows[g * epg + j], a)
            b = jnp.where(pick, score_rows[g * epg + j], b)
        in_sel.append(a)
        in_score.append(b)
    i1, m1, w1 = jnp.zeros_like(in_sel[0]), in_sel[0], in_score[0]
    for j in range(1, epg):
        better = in_sel[j] > m1
        i1 = jnp.where(better, float(j), i1)
        m1 = jnp.where(better, in_sel[j], m1)
        w1 = jnp.where(better, in_score[j], w1)
    i2 = m2 = w2 = None
    for j in range(epg):
        cand = jnp.where(i1 == float(j), -jnp.inf, in_sel[j])
        if j == 0:
            i2, m2, w2 = jnp.zeros_like(cand), cand, in_score[0]
        else:
            better = cand > m2
            i2 = jnp.where(better, float(j), i2)
            m2 = jnp.where(better, cand, m2)
            w2 = jnp.where(better, in_score[j], w2)
    total = w1 + w2
    zero = jnp.zeros_like(w1)
    r_ref[...] = jnp.concatenate([best_group * epg + i1, best_group * epg + i2, w1 / total, w2 / total,
                                  zero, zero, zero, zero], axis=0)


def _router(x2, modl, nw, wr_t, rb, seq):
    t, d = x2.shape
    tm = min(seq, 512)
    per_batch = seq // tm
    return pl.pallas_call(
        _router_kernel,
        grid=(t // tm,),
        in_specs=[pl.BlockSpec((tm, d), lambda i: (i, 0)),
                  pl.BlockSpec((None, 6, d), lambda i: (i // per_batch, 0, 0)),
                  pl.BlockSpec((1, d), lambda i: (0, 0)),
                  pl.BlockSpec((N_EXPERTS, d), lambda i: (0, 0)),
                  pl.BlockSpec((N_EXPERTS, 1), lambda i: (0, 0))],
        out_specs=[pl.BlockSpec((tm, d), lambda i: (i, 0)),
                   pl.BlockSpec((8, tm), lambda i: (0, i))],
        out_shape=[jax.ShapeDtypeStruct((t, d), BF16), jax.ShapeDtypeStruct((8, t), F32)],
        compiler_params=_cparams(("parallel",), 32),
    )(x2, modl, nw, wr_t, rb)


def _moe_kernel(te_ref, nu_ref, x_ref, gw_ref, wg_ref, wu_ref, wd_ref, o_ref):
    used = pl.program_id(0) < nu_ref[0]

    @pl.when(used)
    def _():
        x = x_ref[...]
        a = jnp.dot(x, wg_ref[...], preferred_element_type=F32)
        b = jnp.dot(x, wu_ref[...], preferred_element_type=F32)
        hid = (_silu(a) * b).astype(BF16)
        o_ref[...] = jnp.dot(hid, wd_ref[...], preferred_element_type=F32) * gw_ref[...]

    @pl.when(jnp.logical_not(used))
    def _():
        o_ref[...] = jnp.zeros_like(o_ref)


def _moe_experts(tile_expert, n_used, xs, gw, wg, wu, wd, tm):
    p, d = xs.shape
    f = wg.shape[-1]
    grid_spec = pltpu.PrefetchScalarGridSpec(
        num_scalar_prefetch=2,
        grid=(p // tm,),
        in_specs=[pl.BlockSpec((tm, d), lambda i, te, nu: (i, 0)),
                  pl.BlockSpec((tm, 1), lambda i, te, nu: (i, 0)),
                  pl.BlockSpec((None, d, f), lambda i, te, nu: (te[i], 0, 0)),
                  pl.BlockSpec((None, d, f), lambda i, te, nu: (te[i], 0, 0)),
                  pl.BlockSpec((None, f, d), lambda i, te, nu: (te[i], 0, 0))],
        out_specs=pl.BlockSpec((tm, d), lambda i, te, nu: (i, 0)),
    )
    return pl.pallas_call(
        _moe_kernel,
        grid_spec=grid_spec,
        out_shape=jax.ShapeDtypeStruct((p, d), F32),
        compiler_params=_cparams(("arbitrary",), 48),
    )(tile_expert, n_used, xs, gw, wg, wu, wd)


def _dispatch_plan(expert_idx, tm, n_tiles):
    flat_e = expert_idx.reshape(-1)
    onehot = (flat_e[:, None] == jnp.arange(N_EXPERTS, dtype=jnp.int32)[None, :]).astype(jnp.int32)
    csum = jnp.cumsum(onehot, axis=0)
    rank = jnp.sum((csum - onehot) * onehot, axis=1)
    sizes = csum[-1]
    padded = ((sizes + tm - 1) // tm) * tm
    pad_end = jnp.cumsum(padded)
    pad_start = pad_end - padded
    dest = pad_start[flat_e] + rank
    tile_start = jnp.arange(n_tiles, dtype=jnp.int32) * tm
    tile_expert = jnp.minimum(jnp.sum((tile_start[:, None] >= pad_end[None, :]).astype(jnp.int32), axis=1),
                              N_EXPERTS - 1).astype(jnp.int32)
    n_used = (pad_end[-1] // tm).astype(jnp.int32).reshape(1)
    return dest.astype(jnp.int32), tile_expert, n_used


def _final_norm_kernel(x_ref, w_ref, o_ref):
    x = x_ref[...]
    o_ref[...] = x * _rms_scale(x) * w_ref[...]


def _final_norm(x2, w):
    t, d = x2.shape
    tm = min(t, 512)
    return pl.pallas_call(
        _final_norm_kernel,
        grid=(t // tm,),
        in_specs=[pl.BlockSpec((tm, d), lambda i: (i, 0)), pl.BlockSpec((1, d), lambda i: (0, 0))],
        out_specs=pl.BlockSpec((tm, d), lambda i: (i, 0)),
        out_shape=jax.ShapeDtypeStruct((t, d), F32),
        compiler_params=_cparams(("parallel",), 32),
    )(x2, w)


def kernel(x, c, w_ada, b_ada, norm1_w, norm2_w, w_in, s5_a_re, s5_a_im, s5_log_dt, s5_b_re, s5_b_im, s5_c_re, s5_c_im, s5_d, s5_w_glu, s5_b_glu, gdn_conv_w, gdn_a_log, gdn_dt_bias, gdn_norm_w, w_proj_a, w_proj_b, w_out, w_router, router_bias, w_gate, w_up, w_down, final_norm_w):
    batch, seq, d = x.shape
    depth = w_ada.shape[0]
    t = batch * seq
    s5w = s5_d.shape[1]
    gdw = w_proj_b.shape[1]
    heads = gdn_a_log.shape[1]
    groups = s5w // S5_GROUP
    assert gdw == heads * HEAD_DIM and s5w == gdw and d == 2 * gdw
    assert seq % GDN_TILE == 0 and seq % S5_CHUNK == 0
    chunks_per_seq = seq // S5_CHUNK
    n_levels = int(math.log2(chunks_per_seq))
    assert 1 << n_levels == chunks_per_seq

    c_pad = jnp.zeros((8, d), F32).at[:batch].set(c)
    mod = _ada(c_pad, w_ada, b_ada)

    o_u, o_q, o_k, o_v, o_z = 0, s5w, 2 * s5w, 3 * s5w, 4 * s5w
    o_ba = 5 * s5w
    o_ga = o_ba + 2 * heads
    o_gb = o_ga + d
    col_u, col_q, col_k, col_v, col_z = 4, 5, 6, 7, 8

    blk = lax.broadcasted_iota(jnp.int32, (GDN_TILE, GDN_TILE), 0)
    blk_c = lax.broadcasted_iota(jnp.int32, (GDN_TILE, GDN_TILE), 1)
    tril = (((blk // GDN_CHUNK) == (blk_c // GDN_CHUNK)) & (blk_c <= blk)).astype(F32)

    wr_t = w_router.T.astype(F32)
    rb = router_bias.reshape(N_EXPERTS, 1).astype(F32)

    tm_e = min(512, t)
    n_tiles = (TOP_K * t) // tm_e + N_EXPERTS
    p_rows = n_tiles * tm_e

    x2 = x.reshape(t, d)
    for l in range(depth):
        modl = mod[l, :batch].reshape(batch, 6, d)
        wl = w_in[l]
        w_big = jnp.concatenate([wl[:, o_ga:o_ga + d], wl[:, o_gb:o_gb + d], wl[:, o_u:o_ba]], axis=1).astype(BF16)
        w_small = jnp.zeros((d, 128), F32).at[:, :2 * heads].set(wl[:, o_ba:o_ga]).astype(BF16)
        proj, small = _inproj(x2, modl, norm1_w[l].reshape(1, d), w_big, w_small, seq)

        toep, hin, hout, a1, a2 = _s5_weights(s5_a_re[l], s5_a_im[l], s5_log_dt[l], s5_b_re[l], s5_b_im[l],
                                              s5_c_re[l], s5_c_im[l], n_levels)
        u = proj[:, col_u * s5w:(col_u + 1) * s5w].astype(BF16)
        u_g = jnp.transpose(u.reshape(t // S5_CHUNK, S5_CHUNK, groups, S5_GROUP), (2, 0, 1, 3))
        u_g = u_g.reshape(groups, t // S5_CHUNK, S5_CHUNK * S5_GROUP)
        y_g = _s5_core(u_g, toep, hin, hout, a1, a2, chunks_per_seq)
        y_lin = jnp.transpose(y_g.reshape(groups, t // S5_CHUNK, S5_CHUNK, S5_GROUP), (1, 2, 0, 3)).reshape(t, s5w)
        y_a = _s5_post(y_lin, proj, col_u, s5_d[l].reshape(1, s5w), s5_w_glu[l].astype(BF16),
                       s5_b_glu[l].reshape(1, s5w))

        head_params = jnp.zeros((8, 128), F32)
        head_params = head_params.at[0, heads:2 * heads].set(-jnp.exp(gdn_a_log[l]))
        head_params = head_params.at[1, heads:2 * heads].set(gdn_dt_bias[l])
        y_b = _gdn(proj, small, gdn_conv_w[l], head_params, gdn_norm_w[l].reshape(1, HEAD_DIM), tril,
                   (col_q, col_k, col_v, col_z), batch, seq, heads)

        x2 = _merge(y_a, y_b, proj, x2, modl, w_proj_a[l].astype(BF16), w_proj_b[l].astype(BF16),
                    w_out[l].astype(BF16), seq, 0, 1)

        h2, route = _router(x2, modl, norm2_w[l].reshape(1, d), wr_t, rb, seq)
        expert_idx = route[0:TOP_K].T.astype(jnp.int32)
        gate_w = route[TOP_K:2 * TOP_K].T
        dest, tile_expert, n_used = _dispatch_plan(expert_idx, tm_e, n_tiles)
        src_tok = jnp.zeros((p_rows,), jnp.int32).at[dest].set(jnp.arange(TOP_K * t, dtype=jnp.int32) // TOP_K)
        gw_rows = jnp.zeros((p_rows,), F32).at[dest].set(gate_w.reshape(-1))
        xs = jnp.take(h2, src_tok, axis=0)
        ys = _moe_experts(tile_expert, n_used, xs, gw_rows.reshape(p_rows, 1), w_gate[l].astype(BF16),
                          w_up[l].astype(BF16), w_down[l].astype(BF16), tm_e)
        dest2 = dest.reshape(t, TOP_K)
        moe_out = jnp.take(ys, dest2[:, 0], axis=0) + jnp.take(ys, dest2[:, 1], axis=0)
        g2 = jnp.broadcast_to(modl[:, None, 5, :], (batch, seq, d)).reshape(t, d)
        x2 = x2 + g2 * moe_out

    return _final_norm(x2, final_norm_w.reshape(1, d)).reshape(batch, seq, d)
```

```python
import functools
import math

import jax
import jax.numpy as jnp
from jax import lax
from jax.experimental import pallas as pl
from jax.experimental.pallas import tpu as pltpu

F32 = jnp.float32
BF16 = jnp.bfloat16
HIGHEST = lax.Precision.HIGHEST

RMS_EPS = 1e-6
S5_GROUP = 16
S5_STATE = 64
S5_CHUNK = 16
S5_OCT = 8
HEAD_DIM = 128
GDN_CHUNK = 64
GDN_TILE = 256
CONV_K = 4
N_EXPERTS = 16
N_GROUPS = 4
EXPERTS_PER_GROUP = N_EXPERTS // N_GROUPS
TOP_K = 2
MIB = 1024 * 1024


def _cparams(semantics, vmem_mib):
    return pltpu.CompilerParams(dimension_semantics=semantics, vmem_limit_bytes=vmem_mib * MIB)


def _silu(x):
    return x * jax.nn.sigmoid(x)


def _softplus(x):
    return jnp.maximum(x, 0.0) + jnp.log1p(jnp.exp(-jnp.abs(x)))


def _rms_scale(x):
    return lax.rsqrt(jnp.mean(x * x, axis=-1, keepdims=True) + RMS_EPS)


def _nt_dot(a, b):
    return lax.dot_general(a, b, (((1,), (1,)), ((), ())), preferred_element_type=F32)


def _tn_dot(a, b):
    return lax.dot_general(a, b, (((0,), (0,)), ((), ())), preferred_element_type=F32)


def _ada_kernel(c_ref, w_ref, b_ref, o_ref):
    ca = _silu(c_ref[...]).astype(BF16)
    o_ref[...] = jnp.dot(ca, w_ref[...].astype(BF16), preferred_element_type=F32) + b_ref[...]


def _ada(c_pad, w_ada, b_ada):
    depth, d, n = w_ada.shape
    rows = c_pad.shape[0]
    tn = min(n, 1024)
    return pl.pallas_call(
        _ada_kernel,
        grid=(depth, n // tn),
        in_specs=[pl.BlockSpec((rows, d), lambda l, j: (0, 0)),
                  pl.BlockSpec((None, d, tn), lambda l, j: (l, 0, j)),
                  pl.BlockSpec((None, 1, tn), lambda l, j: (l, 0, j))],
        out_specs=pl.BlockSpec((None, rows, tn), lambda l, j: (l, 0, j)),
        out_shape=jax.ShapeDtypeStruct((depth, rows, n), F32),
        compiler_params=_cparams(("parallel", "parallel"), 40),
    )(c_pad, w_ada, b_ada.reshape(depth, 1, n))


def _inproj_kernel(x_ref, mod_ref, nw_ref, w_ref, wsm_ref, o_ref, osm_ref, h_scr):
    @pl.when(pl.program_id(1) == 0)
    def _():
        x = x_ref[...]
        h = x * _rms_scale(x) * nw_ref[...] * (1.0 + mod_ref[1:2, :]) + mod_ref[0:1, :]
        hb = h.astype(BF16)
        h_scr[...] = hb
        osm_ref[...] = jnp.dot(hb, wsm_ref[...], preferred_element_type=F32)

    o_ref[...] = jnp.dot(h_scr[...], w_ref[...], preferred_element_type=F32)


def _inproj(x2, modl, nw, w_big, w_small, seq):
    t, d = x2.shape
    n = w_big.shape[1]
    tm = min(seq, 1024)
    tn = 512
    per_batch = seq // tm
    return pl.pallas_call(
        _inproj_kernel,
        grid=(t // tm, n // tn),
        in_specs=[pl.BlockSpec((tm, d), lambda i, j: (i, 0)),
                  pl.BlockSpec((None, 6, d), lambda i, j: (i // per_batch, 0, 0)),
                  pl.BlockSpec((1, d), lambda i, j: (0, 0)),
                  pl.BlockSpec((d, tn), lambda i, j: (0, j)),
                  pl.BlockSpec((d, 128), lambda i, j: (0, 0))],
        out_specs=[pl.BlockSpec((tm, tn), lambda i, j: (i, j)),
                   pl.BlockSpec((tm, 128), lambda i, j: (i, 0))],
        out_shape=[jax.ShapeDtypeStruct((t, n), F32), jax.ShapeDtypeStruct((t, 128), F32)],
        scratch_shapes=[pltpu.VMEM((tm, d), BF16)],
        compiler_params=_cparams(("parallel", "arbitrary"), 48),
    )(x2, modl, nw, w_big, w_small)


def _s5_weights(a_re, a_im, log_dt, b_re, b_im, c_re, c_im, n_levels):
    tc = S5_CHUNK
    groups, p = a_re.shape
    n = b_re.shape[-1]
    oc = S5_OCT
    octs = groups // oc
    lam = lax.complex(a_re, a_im)
    dt = jnp.exp(log_dt)[:, None]
    log_a = lam * dt
    a_bar = jnp.exp(log_a)
    b_bar = ((a_bar - 1.0) / lam)[..., None] * lax.complex(b_re, b_im)
    c_mat = lax.complex(c_re, c_im)
    taus = jnp.arange(tc + 1, dtype=F32)
    apow = jnp.exp(log_a[None] * taus[:, None, None])
    eye = jnp.eye(oc, dtype=F32)
    kern = jnp.real(jnp.einsum('gnp,tgp,gpm->tgnm', c_mat, apow[:tc], b_bar, precision=HIGHEST))
    lag = jnp.arange(tc)[None, :] - jnp.arange(tc)[:, None]
    kt = jnp.where((lag >= 0)[:, :, None, None, None], kern[jnp.clip(lag, 0, tc - 1)], 0.0)
    kt = jnp.transpose(kt, (2, 0, 4, 1, 3)).reshape(octs, oc, tc, n, tc, n)
    toep = kt[:, :, :, :, :, None, :] * eye[None, :, None, None, None, :, None]
    toep = jnp.transpose(toep, (0, 2, 1, 3, 4, 5, 6)).reshape(octs, tc * oc * n, tc * oc * n)
    hin_c = apow[tc - 1 - jnp.arange(tc)][:, :, :, None] * b_bar[None]
    hin_c = jnp.transpose(hin_c, (1, 0, 3, 2)).reshape(octs, oc, tc, n, p)
    hin_ri = jnp.stack([jnp.real(hin_c), jnp.imag(hin_c)], axis=4)
    hin = hin_ri[:, :, :, :, :, None, :] * eye[None, :, None, None, None, :, None]
    hin = jnp.transpose(hin, (0, 2, 1, 3, 4, 5, 6)).reshape(octs, tc * oc * n, 2 * oc * p)
    coef = c_mat[None] * apow[1:tc + 1][:, :, None, :]
    coef = jnp.transpose(coef, (1, 3, 0, 2)).reshape(octs, oc, p, tc, n)
    hout_ri = jnp.stack([jnp.real(coef), -jnp.imag(coef)], axis=1)
    hout = hout_ri[:, :, :, :, :, None, :] * eye[None, None, :, None, None, :, None]
    hout = hout.reshape(octs, 2 * oc * p, tc * oc * n)
    steps = (tc * (2.0 ** jnp.arange(n_levels, dtype=F32)))
    amul = jnp.exp(log_a[:, None, :] * steps[None, :, None])
    amul = jnp.transpose(amul.reshape(octs, oc, n_levels, p), (0, 2, 1, 3)).reshape(octs, n_levels, oc * p)
    a1 = jnp.concatenate([jnp.real(amul), jnp.real(amul)], axis=-1)
    a2 = jnp.concatenate([-jnp.imag(amul), jnp.imag(amul)], axis=-1)
    return toep.astype(BF16), hin.astype(BF16), hout.astype(BF16), a1.astype(F32), a2.astype(F32)


def _s5_core_kernel(u_ref, toep_ref, hin_ref, hout_ref, a1_ref, a2_ref, y_ref, *, n_levels):
    tc = S5_CHUNK
    lanes = u_ref.shape[1]
    rows = u_ref.shape[0] // tc
    x = jnp.concatenate([u_ref[pl.ds(t, rows, stride=tc), :] for t in range(tc)], axis=1).astype(BF16)
    s = jnp.dot(x, hin_ref[...], preferred_element_type=F32)
    half = s.shape[1] // 2
    pos = lax.broadcasted_iota(jnp.int32, s.shape, 0)
    for k in range(n_levels):
        d = 1 << k
        prev = jnp.where(pos >= d, pltpu.roll(s, d, axis=0), 0.0)
        s = s + a1_ref[k:k + 1, :] * prev + a2_ref[k:k + 1, :] * pltpu.roll(prev, half, axis=1)
    s_in = jnp.where(pos >= 1, pltpu.roll(s, 1, axis=0), 0.0)
    y = (jnp.dot(x, toep_ref[...], preferred_element_type=F32)
         + jnp.dot(s_in.astype(BF16), hout_ref[...], preferred_element_type=F32))
    for t in range(tc):
        y_ref[pl.ds(t, rows, stride=tc), :] = y[:, t * lanes:(t + 1) * lanes]


def _s5_core(proj, u_lane_block, toep, hin, hout, a1, a2, batch, seq):
    t = proj.shape[0]
    octs, kdim, _ = toep.shape
    sw = hin.shape[-1]
    n_levels = a1.shape[1]
    lanes = kdim // S5_CHUNK
    kern = functools.partial(_s5_core_kernel, n_levels=n_levels)
    return pl.pallas_call(
        kern,
        grid=(octs, batch),
        in_specs=[pl.BlockSpec((seq, lanes), lambda o, b: (b, u_lane_block + o)),
                  pl.BlockSpec((None, kdim, kdim), lambda o, b: (o, 0, 0)),
                  pl.BlockSpec((None, kdim, sw), lambda o, b: (o, 0, 0)),
                  pl.BlockSpec((None, sw, kdim), lambda o, b: (o, 0, 0)),
                  pl.BlockSpec((None, n_levels, sw), lambda o, b: (o, 0, 0)),
                  pl.BlockSpec((None, n_levels, sw), lambda o, b: (o, 0, 0))],
        out_specs=pl.BlockSpec((seq, lanes), lambda o, b: (b, o)),
        out_shape=jax.ShapeDtypeStruct((t, octs * lanes), F32),
        compiler_params=_cparams(("parallel", "parallel"), 56),
    )(proj, toep, hin, hout, a1, a2)


def _s5_post_kernel(y_ref, u_ref, d_ref, w_ref, b_ref, o_ref):
    y = jax.nn.gelu(y_ref[...] + d_ref[...] * u_ref[...])
    gate = jnp.dot(y.astype(BF16), w_ref[...], preferred_element_type=F32) + b_ref[...]
    o_ref[...] = (y * jax.nn.sigmoid(gate)).astype(o_ref.dtype)


def _s5_post(y_lin, proj, u_col, d_skip, w_glu, b_glu):
    t, w = y_lin.shape
    tm = min(t, 512)
    return pl.pallas_call(
        _s5_post_kernel,
        grid=(t // tm,),
        in_specs=[pl.BlockSpec((tm, w), lambda i: (i, 0)),
                  pl.BlockSpec((tm, w), lambda i: (i, u_col)),
                  pl.BlockSpec((1, w), lambda i: (0, 0)),
                  pl.BlockSpec((w, w), lambda i: (0, 0)),
                  pl.BlockSpec((1, w), lambda i: (0, 0))],
        out_specs=pl.BlockSpec((tm, w), lambda i: (i, 0)),
        out_shape=jax.ShapeDtypeStruct((t, w), BF16),
        compiler_params=_cparams(("parallel",), 32),
    )(y_lin, proj, d_skip, w_glu, b_glu)


def _gdn_kernel(q_ref, k_ref, v_ref, qh_ref, kh_ref, vh_ref, z_ref, sm_ref, cq_ref, ck_ref, cv_ref,
                hp_ref, nw_ref, tril_ref, cmask_ref, lmask_ref, o_ref, betab, gcb, grow, st_scr, *, heads):
    tt = GDN_TILE
    hd = HEAD_DIM
    nck = tt // GDN_CHUNK
    first = pl.program_id(1) == 0

    @pl.when(first)
    def _():
        st_scr[...] = jnp.zeros_like(st_scr)

    sm = sm_ref[...]
    hp = hp_ref[...]
    beta_all = jax.nn.sigmoid(sm)
    g_all = hp[0:1, :] * _softplus(sm + hp[1:2, :])
    gc_all = jnp.dot(tril_ref[...], g_all, precision=HIGHEST, preferred_element_type=F32)
    gc_t = gc_all.T
    for h in range(heads):
        betab[h] = jnp.broadcast_to(beta_all[:, h:h + 1], (tt, hd))
        gcb[h] = jnp.broadcast_to(gc_all[:, heads + h:heads + h + 1], (tt, hd))
        grow[h:h + 1, :] = gc_t[heads + h:heads + h + 1, :]

    row8 = lax.broadcasted_iota(jnp.int32, (8, hd), 0)
    eye = (lax.broadcasted_iota(jnp.int32, (tt, tt), 0) == lax.broadcasted_iota(jnp.int32, (tt, tt), 1)).astype(F32)
    n_lv = lmask_ref.shape[0]

    def conv_silu(x_ref, halo_ref, cw_ref, hs):
        x = x_ref[:, pl.ds(hs, hd)]
        halo = jnp.where(first, 0.0, halo_ref[:, pl.ds(hs, hd)])
        w = cw_ref[:, pl.ds(hs, hd)]
        acc = x * w[CONV_K - 1:CONV_K, :]
        for s in range(1, CONV_K):
            rolled = pltpu.roll(x, s, axis=0)
            head8 = jnp.where(row8 < s, pltpu.roll(halo, s, axis=0), rolled[:8])
            shifted = jnp.concatenate([head8, rolled[8:]], axis=0)
            acc = acc + shifted * w[CONV_K - 1 - s:CONV_K - s, :]
        return _silu(acc)

    def one_head(h):
        hs = pl.multiple_of(h * hd, hd)
        q = conv_silu(q_ref, qh_ref, cq_ref, hs)
        k = conv_silu(k_ref, kh_ref, ck_ref, hs)
        v = conv_silu(v_ref, vh_ref, cv_ref, hs)
        qn = q * (lax.rsqrt(jnp.sum(q * q, axis=-1, keepdims=True) + RMS_EPS) * (hd ** -0.5))
        kn = k * lax.rsqrt(jnp.sum(k * k, axis=-1, keepdims=True) + RMS_EPS)
        bb = betab[h]
        gc = gcb[h]
        gr = grow[pl.ds(h, 1), :]
        diff = jnp.concatenate([gc, gc], axis=1) - gr
        decay = jnp.exp(jnp.minimum(diff, 0.0)) * cmask_ref[0]
        kb = kn * bb
        knb = kn.astype(BF16)
        lmat = (_nt_dot(kb.astype(BF16), knb) * (decay * cmask_ref[1])).astype(BF16)
        tinv = eye - (lmat * lmask_ref[0]).astype(F32)
        for lv in range(1, n_lv):
            tb = tinv.astype(BF16)
            tinv = tinv - jnp.dot(jnp.dot(tb, lmat * lmask_ref[lv], preferred_element_type=F32).astype(BF16), tb,
                                  preferred_element_type=F32)
        eg = jnp.exp(gc)
        rhs = jnp.concatenate([kb * eg, v * bb], axis=1).astype(BF16)
        wu = jnp.dot(tinv.astype(BF16), rhs, preferred_element_type=F32).astype(BF16)
        attn = (_nt_dot(qn.astype(BF16), knb) * decay).astype(BF16)
        awu = jnp.dot(attn, wu, preferred_element_type=F32)
        qp = (qn * eg - awu[:, :hd]).astype(BF16)
        o_intra = awu[:, hd:]
        s = st_scr[h]
        outs = []
        for c in range(nck):
            r0 = c * GDN_CHUNK
            r1 = r0 + GDN_CHUNK
            gl = gc[r1 - 1:r1, :]
            kd = (kn[r0:r1] * jnp.exp(gl - gc[r0:r1])).astype(BF16)
            m = _tn_dot(kd, wu[r0:r1])
            sb = s.astype(BF16)
            outs.append(jnp.dot(qp[r0:r1], sb, preferred_element_type=F32) + o_intra[r0:r1])
            s = s * jnp.exp(gl) - jnp.dot(m[:, :hd].astype(BF16), sb, preferred_element_type=F32) + m[:, hd:]
        st_scr[h] = s
        o = jnp.concatenate(outs, axis=0)
        z = z_ref[:, pl.ds(hs, hd)]
        o_ref[:, pl.ds(hs, hd)] = (o * _rms_scale(o) * nw_ref[...] * _silu(z)).astype(o_ref.dtype)

    def pair_body(j, carry):
        one_head(2 * j)
        one_head(2 * j + 1)
        return carry

    lax.fori_loop(0, heads // 2, pair_body, 0)


def _gdn_masks():
    tt = GDN_TILE
    ri = lax.broadcasted_iota(jnp.int32, (tt, tt), 0)
    ci = lax.broadcasted_iota(jnp.int32, (tt, tt), 1)
    same = (ri // GDN_CHUNK) == (ci // GDN_CHUNK)
    causal = same & (ci <= ri)
    tril = causal.astype(F32)
    cmask = jnp.stack([tril, (ri != ci).astype(F32)])
    levels = []
    d = 1
    while d < GDN_CHUNK:
        levels.append((((ri // d) % 2) == 1) & ((ci // d) == (ri // d) - 1))
        d *= 2
    return tril, cmask, jnp.stack(levels).astype(BF16)


def _gdn(proj, small, conv_w, head_params, norm_w, masks, cols, batch, seq, heads):
    t = proj.shape[0]
    tt = GDN_TILE
    width = heads * HEAD_DIM
    tiles = seq // tt
    cq, ck, cv, cz = cols
    tril, cmask, lmask = masks

    def cur(col):
        return pl.BlockSpec((tt, width), lambda b, i: (b * tiles + i, col))

    def halo(col):
        return pl.BlockSpec((8, width), lambda b, i: (jnp.maximum((b * tiles + i) * (tt // 8) - 1, 0), col))

    def cw(col):
        return pl.BlockSpec((CONV_K, width), lambda b, i: (0, col))

    kern = functools.partial(_gdn_kernel, heads=heads)
    return pl.pallas_call(
        kern,
        grid=(batch, tiles),
        in_specs=[cur(cq), cur(ck), cur(cv), halo(cq), halo(ck), halo(cv), cur(cz),
                  pl.BlockSpec((tt, 128), lambda b, i: (b * tiles + i, 0)),
                  cw(0), cw(1), cw(2),
                  pl.BlockSpec((8, 128), lambda b, i: (0, 0)),
                  pl.BlockSpec((1, HEAD_DIM), lambda b, i: (0, 0)),
                  pl.BlockSpec((tt, tt), lambda b, i: (0, 0)),
                  pl.BlockSpec(cmask.shape, lambda b, i: (0, 0, 0)),
                  pl.BlockSpec(lmask.shape, lambda b, i: (0, 0, 0))],
        out_specs=pl.BlockSpec((tt, width), lambda b, i: (b * tiles + i, 0)),
        out_shape=jax.ShapeDtypeStruct((t, width), BF16),
        scratch_shapes=[pltpu.VMEM((heads, tt, HEAD_DIM), F32),
                        pltpu.VMEM((heads, tt, HEAD_DIM), F32),
                        pltpu.VMEM((8, tt), F32),
                        pltpu.VMEM((heads, HEAD_DIM, HEAD_DIM), F32)],
        compiler_params=_cparams(("parallel", "arbitrary"), 40),
    )(proj, proj, proj, proj, proj, proj, proj, small, conv_w, conv_w, conv_w, head_params, norm_w,
      tril, cmask, lmask)


def _merge_kernel(ya_ref, yb_ref, ga_ref, gb_ref, x_ref, mod_ref, wa_ref, wb_ref, wo_ref, o_ref):
    pa = jnp.dot(ya_ref[...], wa_ref[...], preferred_element_type=F32)
    pb = jnp.dot(yb_ref[...], wb_ref[...], preferred_element_type=F32)
    merged = jax.nn.sigmoid(ga_ref[...]) * pa + jax.nn.sigmoid(gb_ref[...]) * pb
    out = jnp.dot(merged.astype(BF16), wo_ref[...], preferred_element_type=F32)
    o_ref[...] = x_ref[...] + mod_ref[2:3, :] * out


def _merge(ya, yb, proj, x2, modl, wa, wb, wo, seq, ga_col, gb_col):
    t, d = x2.shape
    w = ya.shape[1]
    tm = min(seq, 256)
    per_batch = seq // tm
    const = dict(pipeline_mode=pl.Buffered(1))
    return pl.pallas_call(
        _merge_kernel,
        grid=(t // tm,),
        in_specs=[pl.BlockSpec((tm, w), lambda i: (i, 0)),
                  pl.BlockSpec((tm, w), lambda i: (i, 0)),
                  pl.BlockSpec((tm, d), lambda i: (i, ga_col)),
                  pl.BlockSpec((tm, d), lambda i: (i, gb_col)),
                  pl.BlockSpec((tm, d), lambda i: (i, 0)),
                  pl.BlockSpec((None, 6, d), lambda i: (i // per_batch, 0, 0)),
                  pl.BlockSpec((w, d), lambda i: (0, 0), **const),
                  pl.BlockSpec((w, d), lambda i: (0, 0), **const),
                  pl.BlockSpec((d, d), lambda i: (0, 0), **const)],
        out_specs=pl.BlockSpec((tm, d), lambda i: (i, 0)),
        out_shape=jax.ShapeDtypeStruct((t, d), F32),
        compiler_params=_cparams(("parallel",), 48),
    )(ya, yb, proj, proj, x2, modl, wa, wb, wo)


def _router_kernel(x_ref, mod_ref, nw_ref, wrt_ref, rb_ref, h_ref, r_ref):
    x = x_ref[...]
    h = x * _rms_scale(x) * nw_ref[...] * (1.0 + mod_ref[4:5, :]) + mod_ref[3:4, :]
    h_ref[...] = h.astype(h_ref.dtype)
    logits = lax.dot_general(wrt_ref[...], h, (((1,), (1,)), ((), ())), precision=HIGHEST,
                             preferred_element_type=F32)
    scores = jax.nn.sigmoid(logits)
    sel = scores + rb_ref[...]
    sel_rows = [sel[e:e + 1, :] for e in range(N_EXPERTS)]
    score_rows = [scores[e:e + 1, :] for e in range(N_EXPERTS)]
    epg = EXPERTS_PER_GROUP
    best_score = None
    best_group = None
    for g in range(N_GROUPS):
        rows = sel_rows[g * epg:(g + 1) * epg]
        gs = None
        for a in range(epg):
            for b in range(a + 1, epg):
                pair = rows[a] + rows[b]
                gs = pair if gs is None else jnp.maximum(gs, pair)
        if g == 0:
            best_score, best_group = gs, jnp.zeros_like(gs)
        else:
            better = gs > best_score
            best_group = jnp.where(better, float(g), best_group)
            best_score = jnp.where(better, gs, best_score)
    in_sel = []
    in_score = []
    for j in range(epg):
        a = sel_rows[j]
        b = score_rows[j]
        for g in range(1, N_GROUPS):
            pick = best_group == float(g)
            a = jnp.where(pick, sel_rows[g * epg + j], a)
            b = jnp.where(pick, score_rows[g * epg + j], b)
        in_sel.append(a)
        in_score.append(b)
    i1, m1, w1 = jnp.zeros_like(in_sel[0]), in_sel[0], in_score[0]
    for j in range(1, epg):
        better = in_sel[j] > m1
        i1 = jnp.where(better, float(j), i1)
        m1 = jnp.where(better, in_sel[j], m1)
        w1 = jnp.where(better, in_score[j], w1)
    i2 = m2 = w2 = None
    for j in range(epg):
        cand = jnp.where(i1 == float(j), -jnp.inf, in_sel[j])
        if j == 0:
            i2, m2, w2 = jnp.zeros_like(cand), cand, in_score[0]
        else:
            better = cand > m2
            i2 = jnp.where(better, float(j), i2)
            m2 = jnp.where(better, cand, m2)
            w2 = jnp.where(better, in_score[j], w2)
    total = w1 + w2
    zero = jnp.zeros_like(w1)
    r_ref[...] = jnp.concatenate([best_group * epg + i1, best_group * epg + i2, w1 / total, w2 / total,
                                  zero, zero, zero, zero], axis=0)


def _router(x2, modl, nw, wr_t, rb, seq):
    t, d = x2.shape
    tm = min(seq, 512)
    per_batch = seq // tm
    return pl.pallas_call(
        _router_kernel,
        grid=(t // tm,),
        in_specs=[pl.BlockSpec((tm, d), lambda i: (i, 0)),
                  pl.BlockSpec((None, 6, d), lambda i: (i // per_batch, 0, 0)),
                  pl.BlockSpec((1, d), lambda i: (0, 0)),
                  pl.BlockSpec((N_EXPERTS, d), lambda i: (0, 0)),
                  pl.BlockSpec((N_EXPERTS, 1), lambda i: (0, 0))],
        out_specs=[pl.BlockSpec((tm, d), lambda i: (i, 0)),
                   pl.BlockSpec((8, tm), lambda i: (0, i))],
        out_shape=[jax.ShapeDtypeStruct((t, d), BF16), jax.ShapeDtypeStruct((8, t), F32)],
        compiler_params=_cparams(("parallel",), 32),
    )(x2, modl, nw, wr_t, rb)


def _moe_kernel(te_ref, nu_ref, x_ref, wg_ref, wu_ref, wd_ref, o_ref):
    used = pl.program_id(0) < nu_ref[0]

    @pl.when(used)
    def _():
        x = x_ref[...]
        a = jnp.dot(x, wg_ref[...], preferred_element_type=F32)
        b = jnp.dot(x, wu_ref[...], preferred_element_type=F32)
        hid = (_silu(a) * b).astype(BF16)
        o_ref[...] = jnp.dot(hid, wd_ref[...], preferred_element_type=F32)

    @pl.when(jnp.logical_not(used))
    def _():
        o_ref[...] = jnp.zeros_like(o_ref)


def _moe_experts(tile_expert, n_used, xs, wg, wu, wd, layer, tm):
    p, d = xs.shape
    f = wg.shape[-1]
    grid_spec = pltpu.PrefetchScalarGridSpec(
        num_scalar_prefetch=2,
        grid=(p // tm,),
        in_specs=[pl.BlockSpec((tm, d), lambda i, te, nu: (i, 0)),
                  pl.BlockSpec((None, None, d, f), lambda i, te, nu: (layer, te[i], 0, 0)),
                  pl.BlockSpec((None, None, d, f), lambda i, te, nu: (layer, te[i], 0, 0)),
                  pl.BlockSpec((None, None, f, d), lambda i, te, nu: (layer, te[i], 0, 0))],
        out_specs=pl.BlockSpec((tm, d), lambda i, te, nu: (i, 0)),
    )
    return pl.pallas_call(
        _moe_kernel,
        grid_spec=grid_spec,
        out_shape=jax.ShapeDtypeStruct((p, d), F32),
        compiler_params=_cparams(("arbitrary",), 48),
    )(tile_expert, n_used, xs, wg, wu, wd)


def _dispatch_plan(expert_idx, tm, n_tiles):
    flat_e = expert_idx.reshape(-1)
    onehot = (flat_e[:, None] == jnp.arange(N_EXPERTS, dtype=jnp.int32)[None, :]).astype(jnp.int32)
    csum = jnp.cumsum(onehot, axis=0)
    rank = jnp.sum((csum - onehot) * onehot, axis=1)
    sizes = csum[-1]
    padded = ((sizes + tm - 1) // tm) * tm
    pad_end = jnp.cumsum(padded)
    pad_start = pad_end - padded
    dest = pad_start[flat_e] + rank
    tile_start = jnp.arange(n_tiles, dtype=jnp.int32) * tm
    tile_expert = jnp.minimum(jnp.sum((tile_start[:, None] >= pad_end[None, :]).astype(jnp.int32), axis=1),
                              N_EXPERTS - 1).astype(jnp.int32)
    n_used = (pad_end[-1] // tm).astype(jnp.int32).reshape(1)
    return dest.astype(jnp.int32), tile_expert, n_used


def _combine_kernel(x_ref, y0_ref, y1_ref, gw_ref, mod_ref, o_ref):
    gw = gw_ref[...]
    moe = gw[:, 0:1] * y0_ref[...] + gw[:, 1:2] * y1_ref[...]
    o_ref[...] = x_ref[...] + mod_ref[5:6, :] * moe


def _combine(x2, y_pairs, gate_w, modl, seq):
    t, d = x2.shape
    tm = min(seq, 512)
    per_batch = seq // tm
    nblk = t // tm
    return pl.pallas_call(
        _combine_kernel,
        grid=(nblk,),
        in_specs=[pl.BlockSpec((tm, d), lambda i: (i, 0)),
                  pl.BlockSpec((tm, d), lambda i: (i, 0)),
                  pl.BlockSpec((tm, d), lambda i: (i + nblk, 0)),
                  pl.BlockSpec((tm, TOP_K), lambda i: (i, 0)),
                  pl.BlockSpec((None, 6, d), lambda i: (i // per_batch, 0, 0))],
        out_specs=pl.BlockSpec((tm, d), lambda i: (i, 0)),
        out_shape=jax.ShapeDtypeStruct((t, d), F32),
        compiler_params=_cparams(("parallel",), 40),
    )(x2, y_pairs, y_pairs, gate_w, modl)


def _final_norm_kernel(x_ref, w_ref, o_ref):
    x = x_ref[...]
    o_ref[...] = x * _rms_scale(x) * w_ref[...]


def _final_norm(x2, w):
    t, d = x2.shape
    tm = min(t, 512)
    return pl.pallas_call(
        _final_norm_kernel,
        grid=(t // tm,),
        in_specs=[pl.BlockSpec((tm, d), lambda i: (i, 0)), pl.BlockSpec((1, d), lambda i: (0, 0))],
        out_specs=pl.BlockSpec((tm, d), lambda i: (i, 0)),
        out_shape=jax.ShapeDtypeStruct((t, d), F32),
        compiler_params=_cparams(("parallel",), 32),
    )(x2, w)


def kernel(x, c, w_ada, b_ada, norm1_w, norm2_w, w_in, s5_a_re, s5_a_im, s5_log_dt, s5_b_re, s5_b_im, s5_c_re, s5_c_im, s5_d, s5_w_glu, s5_b_glu, gdn_conv_w, gdn_a_log, gdn_dt_bias, gdn_norm_w, w_proj_a, w_proj_b, w_out, w_router, router_bias, w_gate, w_up, w_down, final_norm_w):
    batch, seq, d = x.shape
    depth = w_ada.shape[0]
    t = batch * seq
    s5w = s5_d.shape[1]
    gdw = w_proj_b.shape[1]
    heads = gdn_a_log.shape[1]
    assert gdw == heads * HEAD_DIM and s5w == gdw and d == 2 * gdw
    assert seq % GDN_TILE == 0 and seq % S5_CHUNK == 0
    n_levels = int(math.log2(seq // S5_CHUNK))
    assert S5_CHUNK << n_levels == seq and S5_OCT * S5_GROUP == 128

    c_pad = jnp.zeros((8, d), F32).at[:batch].set(c)
    mod = _ada(c_pad, w_ada, b_ada)

    o_u, o_q, o_k, o_v, o_z = 0, s5w, 2 * s5w, 3 * s5w, 4 * s5w
    o_ba = 5 * s5w
    o_ga = o_ba + 2 * heads
    o_gb = o_ga + d
    col_u, col_q, col_k, col_v, col_z = 4, 5, 6, 7, 8

    gdn_masks = _gdn_masks()

    wr_t = w_router.T.astype(F32)
    rb = router_bias.reshape(N_EXPERTS, 1).astype(F32)

    tm_e = min(512, t)
    n_tiles = (TOP_K * t) // tm_e + N_EXPERTS
    p_rows = n_tiles * tm_e

    w_gate_b = w_gate.astype(BF16)
    w_up_b = w_up.astype(BF16)
    w_down_b = w_down.astype(BF16)

    x2 = x.reshape(t, d)
    for l in range(depth):
        modl = mod[l, :batch].reshape(batch, 6, d)
        wl = w_in[l]
        w_big = jnp.concatenate([wl[:, o_ga:o_ga + d], wl[:, o_gb:o_gb + d], wl[:, o_u:o_ba]], axis=1).astype(BF16)
        w_small = jnp.zeros((d, 128), F32).at[:, :2 * heads].set(wl[:, o_ba:o_ga]).astype(BF16)
        proj, small = _inproj(x2, modl, norm1_w[l].reshape(1, d), w_big, w_small, seq)

        toep, hin, hout, a1, a2 = _s5_weights(s5_a_re[l], s5_a_im[l], s5_log_dt[l], s5_b_re[l], s5_b_im[l],
                                              s5_c_re[l], s5_c_im[l], n_levels)
        y_lin = _s5_core(proj, col_u * s5w // 128, toep, hin, hout, a1, a2, batch, seq)
        y_a = _s5_post(y_lin, proj, col_u, s5_d[l].reshape(1, s5w), s5_w_glu[l].astype(BF16),
                       s5_b_glu[l].reshape(1, s5w))

        head_params = jnp.zeros((8, 128), F32)
        head_params = head_params.at[0, heads:2 * heads].set(-jnp.exp(gdn_a_log[l]))
        head_params = head_params.at[1, heads:2 * heads].set(gdn_dt_bias[l])
        y_b = _gdn(proj, small, gdn_conv_w[l], head_params, gdn_norm_w[l].reshape(1, HEAD_DIM), gdn_masks,
                   (col_q, col_k, col_v, col_z), batch, seq, heads)

        x2 = _merge(y_a, y_b, proj, x2, modl, w_proj_a[l].astype(BF16), w_proj_b[l].astype(BF16),
                    w_out[l].astype(BF16), seq, 0, 1)

        h2, route = _router(x2, modl, norm2_w[l].reshape(1, d), wr_t, rb, seq)
        expert_idx = route[0:TOP_K].T.astype(jnp.int32)
        gate_w = route[TOP_K:2 * TOP_K].T
        dest, tile_expert, n_used = _dispatch_plan(expert_idx, tm_e, n_tiles)
        src_tok = jnp.zeros((p_rows,), jnp.int32).at[dest].set(jnp.arange(TOP_K * t, dtype=jnp.int32) // TOP_K)
        xs = jnp.take(h2, src_tok, axis=0, mode='clip')
        ys = _moe_experts(tile_expert, n_used, xs, w_gate_b, w_up_b, w_down_b, l, tm_e)
        y_pairs = jnp.take(ys, dest.reshape(t, TOP_K).T.reshape(-1), axis=0, mode='clip')
        x2 = _combine(x2, y_pairs, gate_w, modl, seq)

    return _final_norm(x2, final_norm_w.reshape(1, d)).reshape(batch, seq, d)
```

```python
import functools
import math

import jax
import jax.numpy as jnp
from jax import lax
from jax.experimental import pallas as pl
from jax.experimental.pallas import tpu as pltpu

F32 = jnp.float32
BF16 = jnp.bfloat16
HIGHEST = lax.Precision.HIGHEST

RMS_EPS = 1e-6
S5_GROUP = 16
S5_STATE = 64
S5_CHUNK = 16
S5_OCT = 8
HEAD_DIM = 128
GDN_CHUNK = 64
GDN_TILE = 256
GDN_HEADS_PER_TRIP = 4
CONV_K = 4
N_EXPERTS = 16
N_GROUPS = 4
EXPERTS_PER_GROUP = N_EXPERTS // N_GROUPS
TOP_K = 2
MIB = 1024 * 1024


def _cparams(semantics, vmem_mib):
    return pltpu.CompilerParams(dimension_semantics=semantics, vmem_limit_bytes=vmem_mib * MIB)


def _silu(x):
    return x * jax.nn.sigmoid(x)


def _softplus(x):
    return jnp.maximum(x, 0.0) + jnp.log1p(jnp.exp(-jnp.abs(x)))


def _rms_scale(x):
    return lax.rsqrt(jnp.mean(x * x, axis=-1, keepdims=True) + RMS_EPS)


def _nt_dot(a, b):
    return lax.dot_general(a, b, (((1,), (1,)), ((), ())), preferred_element_type=F32)


def _tn_dot(a, b):
    return lax.dot_general(a, b, (((0,), (0,)), ((), ())), preferred_element_type=F32)


def _ada_kernel(c_ref, w_ref, b_ref, o_ref):
    ca = _silu(c_ref[...]).astype(BF16)
    o_ref[...] = jnp.dot(ca, w_ref[...].astype(BF16), preferred_element_type=F32) + b_ref[...]


def _ada(c_pad, w_ada, b_ada):
    depth, d, n = w_ada.shape
    rows = c_pad.shape[0]
    tn = min(n, 1024)
    return pl.pallas_call(
        _ada_kernel,
        grid=(depth, n // tn),
        in_specs=[pl.BlockSpec((rows, d), lambda l, j: (0, 0)),
                  pl.BlockSpec((None, d, tn), lambda l, j: (l, 0, j)),
                  pl.BlockSpec((None, 1, tn), lambda l, j: (l, 0, j))],
        out_specs=pl.BlockSpec((None, rows, tn), lambda l, j: (l, 0, j)),
        out_shape=jax.ShapeDtypeStruct((depth, rows, n), F32),
        compiler_params=_cparams(("parallel", "parallel"), 40),
    )(c_pad, w_ada, b_ada.reshape(depth, 1, n))


def _inproj_kernel(x_ref, mod_ref, nw_ref, w_ref, wsm_ref, o_ref, osm_ref, h_scr):
    @pl.when(pl.program_id(1) == 0)
    def _():
        x = x_ref[...]
        h = x * _rms_scale(x) * nw_ref[...] * (1.0 + mod_ref[1:2, :]) + mod_ref[0:1, :]
        hb = h.astype(BF16)
        h_scr[...] = hb
        osm_ref[...] = jnp.dot(hb, wsm_ref[...], preferred_element_type=F32)

    o_ref[...] = jnp.dot(h_scr[...], w_ref[...], preferred_element_type=F32)


def _inproj(x2, modl, nw, w_big, w_small, layer, seq):
    t, d = x2.shape
    n = w_big.shape[-1]
    tm = min(seq, 1024)
    tn = 512
    per_batch = seq // tm
    return pl.pallas_call(
        _inproj_kernel,
        grid=(t // tm, n // tn),
        in_specs=[pl.BlockSpec((tm, d), lambda i, j: (i, 0)),
                  pl.BlockSpec((None, 6, d), lambda i, j: (i // per_batch, 0, 0)),
                  pl.BlockSpec((1, d), lambda i, j: (0, 0)),
                  pl.BlockSpec((None, d, tn), lambda i, j: (layer, 0, j)),
                  pl.BlockSpec((None, d, 128), lambda i, j: (layer, 0, 0))],
        out_specs=[pl.BlockSpec((tm, tn), lambda i, j: (i, j)),
                   pl.BlockSpec((tm, 128), lambda i, j: (i, 0))],
        out_shape=[jax.ShapeDtypeStruct((t, n), F32), jax.ShapeDtypeStruct((t, 128), F32)],
        scratch_shapes=[pltpu.VMEM((tm, d), BF16)],
        compiler_params=_cparams(("parallel", "arbitrary"), 48),
    )(x2, modl, nw, w_big, w_small)


def _s5_weights(a_re, a_im, log_dt, b_re, b_im, c_re, c_im, n_levels):
    tc = S5_CHUNK
    groups, p = a_re.shape
    n = b_re.shape[-1]
    oc = S5_OCT
    octs = groups // oc
    lam = lax.complex(a_re, a_im)
    dt = jnp.exp(log_dt)[:, None]
    log_a = lam * dt
    a_bar = jnp.exp(log_a)
    b_bar = ((a_bar - 1.0) / lam)[..., None] * lax.complex(b_re, b_im)
    c_mat = lax.complex(c_re, c_im)
    taus = jnp.arange(tc + 1, dtype=F32)
    apow = jnp.exp(log_a[None] * taus[:, None, None])
    kern = jnp.real(jnp.einsum('gnp,tgp,gpm->gmtn', c_mat, apow[:tc], b_bar, precision=HIGHEST))
    tcomp = jnp.stack([jnp.pad(kern[:, :, :tc - s, :], ((0, 0), (0, 0), (s, 0), (0, 0))) for s in range(tc)],
                      axis=1)
    tcomp = jnp.transpose(tcomp.reshape(octs, oc, tc, n, tc * n), (0, 2, 1, 3, 4)).reshape(octs, tc * oc * n, tc * n)
    ap_rev = jnp.transpose(apow[tc - 1 - jnp.arange(tc)], (1, 0, 2))
    hin_c = ap_rev[:, :, None, :] * jnp.transpose(b_bar, (0, 2, 1))[:, None, :, :]
    hc = jnp.concatenate([jnp.real(hin_c), jnp.imag(hin_c)], axis=-1)
    hc = jnp.transpose(hc.reshape(octs, oc, tc, n, 2 * p), (0, 2, 1, 3, 4)).reshape(octs, tc * oc * n, 2 * p)
    ap1 = jnp.transpose(apow[1:tc + 1], (1, 2, 0))
    coef = ap1[:, :, :, None] * jnp.transpose(c_mat, (0, 2, 1))[:, :, None, :]
    hoc = jnp.stack([jnp.real(coef), -jnp.imag(coef)], axis=1).reshape(octs, oc * 2 * p, tc * n)
    steps = (tc * (2.0 ** jnp.arange(n_levels, dtype=F32)))
    amul = jnp.exp(log_a[:, None, :] * steps[None, :, None])
    a1 = jnp.concatenate([jnp.real(amul), jnp.real(amul)], axis=-1)
    a2 = jnp.concatenate([-jnp.imag(amul), jnp.imag(amul)], axis=-1)

    def lanes_by_group(a):
        return jnp.transpose(a.reshape(octs, oc, n_levels, 2 * p), (0, 2, 1, 3)).reshape(octs, n_levels, oc * 2 * p)

    return (tcomp.astype(BF16), hc.astype(BF16), hoc.astype(BF16),
            lanes_by_group(a1).astype(F32), lanes_by_group(a2).astype(F32))


def _s5_replicate():
    r = lax.broadcasted_iota(jnp.int32, (S5_CHUNK * S5_GROUP, S5_CHUNK * 128), 0)
    c = lax.broadcasted_iota(jnp.int32, (S5_CHUNK * S5_GROUP, S5_CHUNK * 128), 1)
    return ((r // S5_GROUP == c // 128) & (r % S5_GROUP == c % S5_GROUP)).astype(BF16)


def _s5_core_kernel(u_ref, tc_ref, hc_ref, hoc_ref, a1_ref, a2_ref, rep_ref, y_ref, toep_scr, hin_scr, hout_scr,
                    *, n_levels):
    tc = S5_CHUNK
    lanes = u_ref.shape[1]
    state_lanes = hc_ref.shape[1]

    @pl.when(pl.program_id(1) == 0)
    def _():
        kd = toep_scr.shape[1]
        rb = 256
        col_g = (lax.broadcasted_iota(jnp.int32, (rb, kd), 1) // S5_GROUP) % S5_OCT
        row_i = lax.broadcasted_iota(jnp.int32, (rb, kd), 0)
        for i in range(toep_scr.shape[0] // rb):
            full = jnp.dot(tc_ref[i * rb:(i + 1) * rb, :], rep_ref[...], preferred_element_type=F32)
            row_g = ((row_i + i * rb) // S5_GROUP) % S5_OCT
            toep_scr[i * rb:(i + 1) * rb, :] = jnp.where(row_g == col_g, full, 0.0).astype(BF16)
        for i in range(hout_scr.shape[0] // rb):
            full = jnp.dot(hoc_ref[i * rb:(i + 1) * rb, :], rep_ref[...], preferred_element_type=F32)
            row_g = (row_i + i * rb) // state_lanes
            hout_scr[i * rb:(i + 1) * rb, :] = jnp.where(row_g == col_g, full, 0.0).astype(BF16)
        hrow_g = (lax.broadcasted_iota(jnp.int32, hc_ref.shape, 0) // S5_GROUP) % S5_OCT
        for g in range(S5_OCT):
            hin_scr[:, g * state_lanes:(g + 1) * state_lanes] = jnp.where(hrow_g == g, hc_ref[...], 0.0).astype(BF16)

    rows = u_ref.shape[0] // tc
    x = jnp.concatenate([u_ref[pl.ds(t, rows, stride=tc), :] for t in range(tc)], axis=1).astype(BF16)
    s_loc = jnp.dot(x, hin_scr[...], preferred_element_type=F32)
    pos = lax.broadcasted_iota(jnp.int32, (rows, state_lanes), 0)
    s_in = []
    for g in range(S5_OCT):
        ln = slice(g * state_lanes, (g + 1) * state_lanes)
        s = s_loc[:, ln]
        for k in range(n_levels):
            d = 1 << k
            prev = jnp.where(pos >= d, pltpu.roll(s, d, axis=0), 0.0)
            s = s + a1_ref[k:k + 1, ln] * prev + a2_ref[k:k + 1, ln] * pltpu.roll(prev, state_lanes // 2, axis=1)
        s_in.append(jnp.where(pos >= 1, pltpu.roll(s, 1, axis=0), 0.0).astype(BF16))
    y = (jnp.dot(x, toep_scr[...], preferred_element_type=F32)
         + jnp.dot(jnp.concatenate(s_in, axis=1), hout_scr[...], preferred_element_type=F32))
    for t in range(tc):
        y_ref[pl.ds(t, rows, stride=tc), :] = y[:, t * lanes:(t + 1) * lanes]


def _s5_core(proj, u_lane_block, tcomp, hc, hoc, a1, a2, rep, batch, seq):
    t = proj.shape[0]
    octs, kdim, tn = tcomp.shape
    sl = hc.shape[-1]
    sw = hoc.shape[1]
    n_levels = a1.shape[1]
    lanes = kdim // S5_CHUNK
    kern = functools.partial(_s5_core_kernel, n_levels=n_levels)
    return pl.pallas_call(
        kern,
        grid=(octs, batch),
        in_specs=[pl.BlockSpec((seq, lanes), lambda o, b: (b, u_lane_block + o)),
                  pl.BlockSpec((None, kdim, tn), lambda o, b: (o, 0, 0)),
                  pl.BlockSpec((None, kdim, sl), lambda o, b: (o, 0, 0)),
                  pl.BlockSpec((None, sw, tn), lambda o, b: (o, 0, 0)),
                  pl.BlockSpec((None, n_levels, sw), lambda o, b: (o, 0, 0)),
                  pl.BlockSpec((None, n_levels, sw), lambda o, b: (o, 0, 0)),
                  pl.BlockSpec((tn, kdim), lambda o, b: (0, 0))],
        out_specs=pl.BlockSpec((seq, lanes), lambda o, b: (b, o)),
        out_shape=jax.ShapeDtypeStruct((t, octs * lanes), F32),
        scratch_shapes=[pltpu.VMEM((kdim, kdim), BF16), pltpu.VMEM((kdim, sw), BF16), pltpu.VMEM((sw, kdim), BF16)],
        compiler_params=_cparams(("parallel", "arbitrary"), 48),
    )(proj, tcomp, hc, hoc, a1, a2, rep)


def _s5_post_kernel(y_ref, u_ref, d_ref, w_ref, b_ref, o_ref):
    y = jax.nn.gelu(y_ref[...] + d_ref[...] * u_ref[...])
    gate = jnp.dot(y.astype(BF16), w_ref[...], preferred_element_type=F32) + b_ref[...]
    o_ref[...] = (y * jax.nn.sigmoid(gate)).astype(o_ref.dtype)


def _s5_post(y_lin, proj, u_col, d_skip, w_glu, b_glu):
    t, w = y_lin.shape
    tm = min(t, 512)
    return pl.pallas_call(
        _s5_post_kernel,
        grid=(t // tm,),
        in_specs=[pl.BlockSpec((tm, w), lambda i: (i, 0)),
                  pl.BlockSpec((tm, w), lambda i: (i, u_col)),
                  pl.BlockSpec((1, w), lambda i: (0, 0)),
                  pl.BlockSpec((w, w), lambda i: (0, 0)),
                  pl.BlockSpec((1, w), lambda i: (0, 0))],
        out_specs=pl.BlockSpec((tm, w), lambda i: (i, 0)),
        out_shape=jax.ShapeDtypeStruct((t, w), BF16),
        compiler_params=_cparams(("parallel",), 32),
    )(y_lin, proj, d_skip, w_glu, b_glu)


def _gdn_kernel(q_ref, k_ref, v_ref, qh_ref, kh_ref, vh_ref, z_ref, sm_ref, cq_ref, ck_ref, cv_ref,
                hp_ref, nw_ref, tril_ref, cmask_ref, lmask_ref, o_ref, betab, gcb, grow, st_scr, *, heads):
    tt = GDN_TILE
    hd = HEAD_DIM
    nck = tt // GDN_CHUNK
    first = pl.program_id(1) == 0

    @pl.when(first)
    def _():
        st_scr[...] = jnp.zeros_like(st_scr)

    sm = sm_ref[...]
    hp = hp_ref[...]
    beta_all = jax.nn.sigmoid(sm)
    g_all = hp[0:1, :] * _softplus(sm + hp[1:2, :])
    gc_all = jnp.dot(tril_ref[...], g_all, precision=HIGHEST, preferred_element_type=F32)
    gc_t = gc_all.T
    for h in range(heads):
        betab[h] = jnp.broadcast_to(beta_all[:, h:h + 1], (tt, hd))
        gcb[h] = jnp.broadcast_to(gc_all[:, heads + h:heads + h + 1], (tt, hd))
        grow[h:h + 1, :] = gc_t[heads + h:heads + h + 1, :]

    row8 = lax.broadcasted_iota(jnp.int32, (8, hd), 0)
    eye = (lax.broadcasted_iota(jnp.int32, (tt, tt), 0) == lax.broadcasted_iota(jnp.int32, (tt, tt), 1)).astype(F32)
    n_lv = lmask_ref.shape[0]

    def conv_silu(x_ref, halo_ref, cw_ref, hs):
        x = x_ref[:, pl.ds(hs, hd)]
        halo = jnp.where(first, 0.0, halo_ref[:, pl.ds(hs, hd)])
        w = cw_ref[:, pl.ds(hs, hd)]
        acc = x * w[CONV_K - 1:CONV_K, :]
        for s in range(1, CONV_K):
            rolled = pltpu.roll(x, s, axis=0)
            head8 = jnp.where(row8 < s, pltpu.roll(halo, s, axis=0), rolled[:8])
            shifted = jnp.concatenate([head8, rolled[8:]], axis=0)
            acc = acc + shifted * w[CONV_K - 1 - s:CONV_K - s, :]
        return _silu(acc)

    def heads_lockstep(hs_idx):
        each = lambda f, *ls: [f(*a) for a in zip(*ls)]
        hs = [pl.multiple_of(h * hd, hd) for h in hs_idx]
        q = [conv_silu(q_ref, qh_ref, cq_ref, o) for o in hs]
        k = [conv_silu(k_ref, kh_ref, ck_ref, o) for o in hs]
        v = [conv_silu(v_ref, vh_ref, cv_ref, o) for o in hs]
        qn = each(lambda a: a * (lax.rsqrt(jnp.sum(a * a, axis=-1, keepdims=True) + RMS_EPS) * (hd ** -0.5)), q)
        kn = each(lambda a: a * lax.rsqrt(jnp.sum(a * a, axis=-1, keepdims=True) + RMS_EPS), k)
        bb = [betab[h] for h in hs_idx]
        gc = [gcb[h] for h in hs_idx]
        gr = [grow[pl.ds(h, 1), :] for h in hs_idx]
        decay = each(lambda c, r: jnp.exp(jnp.minimum(jnp.concatenate([c, c], axis=1) - r, 0.0)) * cmask_ref[0],
                     gc, gr)
        kb = each(lambda a, b: a * b, kn, bb)
        knb = each(lambda a: a.astype(BF16), kn)
        lmat = each(lambda a, b, dc: (_nt_dot(a.astype(BF16), b) * (dc * cmask_ref[1])).astype(BF16),
                    kb, knb, decay)
        tinv = each(lambda m: eye - (m * lmask_ref[0]).astype(F32), lmat)
        for lv in range(1, n_lv):
            tb = each(lambda a: a.astype(BF16), tinv)
            tl = each(lambda a, m: jnp.dot(a, m * lmask_ref[lv], preferred_element_type=F32).astype(BF16), tb, lmat)
            tinv = each(lambda a, b, c: a - jnp.dot(b, c, preferred_element_type=F32), tinv, tl, tb)
        eg = each(jnp.exp, gc)
        rhs = each(lambda a, e, b, c: jnp.concatenate([a * e, b * c], axis=1).astype(BF16), kb, eg, v, bb)
        wu = each(lambda a, b: jnp.dot(a.astype(BF16), b, preferred_element_type=F32).astype(BF16), tinv, rhs)
        attn = each(lambda a, b, dc: (_nt_dot(a.astype(BF16), b) * dc).astype(BF16), qn, knb, decay)
        awu = each(lambda a, b: jnp.dot(a, b, preferred_element_type=F32), attn, wu)
        qp = each(lambda a, e, b: (a * e - b[:, :hd]).astype(BF16), qn, eg, awu)
        s = [st_scr[h] for h in hs_idx]
        outs = [[] for _ in hs_idx]
        for c in range(nck):
            r0 = c * GDN_CHUNK
            r1 = r0 + GDN_CHUNK
            gl = each(lambda a: a[r1 - 1:r1, :], gc)
            kd = each(lambda a, l, g: (a[r0:r1] * jnp.exp(l - g[r0:r1])).astype(BF16), kn, gl, gc)
            m = each(lambda a, b: _tn_dot(a, b[r0:r1]), kd, wu)
            sb = each(lambda a: a.astype(BF16), s)
            for i, (a, b, y) in enumerate(zip(qp, sb, awu)):
                outs[i].append(jnp.dot(a[r0:r1], b, preferred_element_type=F32) + y[r0:r1, hd:])
            s = each(lambda a, l, mm, b: a * jnp.exp(l) - jnp.dot(mm[:, :hd].astype(BF16), b,
                                                                  preferred_element_type=F32) + mm[:, hd:],
                     s, gl, m, sb)
        for i, h in enumerate(hs_idx):
            st_scr[h] = s[i]
            o = jnp.concatenate(outs[i], axis=0)
            z = z_ref[:, pl.ds(hs[i], hd)]
            o_ref[:, pl.ds(hs[i], hd)] = (o * _rms_scale(o) * nw_ref[...] * _silu(z)).astype(o_ref.dtype)

    def trip(j, carry):
        heads_lockstep([GDN_HEADS_PER_TRIP * j + i for i in range(GDN_HEADS_PER_TRIP)])
        return carry

    lax.fori_loop(0, heads // GDN_HEADS_PER_TRIP, trip, 0)


def _gdn_masks():
    tt = GDN_TILE
    ri = lax.broadcasted_iota(jnp.int32, (tt, tt), 0)
    ci = lax.broadcasted_iota(jnp.int32, (tt, tt), 1)
    same = (ri // GDN_CHUNK) == (ci // GDN_CHUNK)
    causal = same & (ci <= ri)
    tril = causal.astype(F32)
    cmask = jnp.stack([tril, (ri != ci).astype(F32)])
    levels = []
    d = 1
    while d < GDN_CHUNK:
        levels.append((((ri // d) % 2) == 1) & ((ci // d) == (ri // d) - 1))
        d *= 2
    return tril, cmask, jnp.stack(levels).astype(BF16)


def _gdn(proj, small, conv_w, head_params, norm_w, masks, cols, batch, seq, heads):
    t = proj.shape[0]
    tt = GDN_TILE
    width = heads * HEAD_DIM
    tiles = seq // tt
    cq, ck, cv, cz = cols
    tril, cmask, lmask = masks

    def cur(col):
        return pl.BlockSpec((tt, width), lambda b, i: (b * tiles + i, col))

    def halo(col):
        return pl.BlockSpec((8, width), lambda b, i: (jnp.maximum((b * tiles + i) * (tt // 8) - 1, 0), col))

    def cw(col):
        return pl.BlockSpec((CONV_K, width), lambda b, i: (0, col))

    kern = functools.partial(_gdn_kernel, heads=heads)
    return pl.pallas_call(
        kern,
        grid=(batch, tiles),
        in_specs=[cur(cq), cur(ck), cur(cv), halo(cq), halo(ck), halo(cv), cur(cz),
                  pl.BlockSpec((tt, 128), lambda b, i: (b * tiles + i, 0)),
                  cw(0), cw(1), cw(2),
                  pl.BlockSpec((8, 128), lambda b, i: (0, 0)),
                  pl.BlockSpec((1, HEAD_DIM), lambda b, i: (0, 0)),
                  pl.BlockSpec((tt, tt), lambda b, i: (0, 0)),
                  pl.BlockSpec(cmask.shape, lambda b, i: (0, 0, 0)),
                  pl.BlockSpec(lmask.shape, lambda b, i: (0, 0, 0))],
        out_specs=pl.BlockSpec((tt, width), lambda b, i: (b * tiles + i, 0)),
        out_shape=jax.ShapeDtypeStruct((t, width), BF16),
        scratch_shapes=[pltpu.VMEM((heads, tt, HEAD_DIM), F32),
                        pltpu.VMEM((heads, tt, HEAD_DIM), F32),
                        pltpu.VMEM((8, tt), F32),
                        pltpu.VMEM((heads, HEAD_DIM, HEAD_DIM), F32)],
        compiler_params=_cparams(("parallel", "arbitrary"), 40),
    )(proj, proj, proj, proj, proj, proj, proj, small, conv_w, conv_w, conv_w, head_params, norm_w,
      tril, cmask, lmask)


def _merge_kernel(ya_ref, yb_ref, ga_ref, gb_ref, x_ref, mod_ref, wa_ref, wb_ref, wo_ref, o_ref):
    pa = jnp.dot(ya_ref[...], wa_ref[...], preferred_element_type=F32)
    pb = jnp.dot(yb_ref[...], wb_ref[...], preferred_element_type=F32)
    merged = jax.nn.sigmoid(ga_ref[...]) * pa + jax.nn.sigmoid(gb_ref[...]) * pb
    out = jnp.dot(merged.astype(BF16), wo_ref[...], preferred_element_type=F32)
    o_ref[...] = x_ref[...] + mod_ref[2:3, :] * out


def _merge(ya, yb, proj, x2, modl, wa, wb, wo, seq, ga_col, gb_col):
    t, d = x2.shape
    w = ya.shape[1]
    tm = min(seq, 256)
    per_batch = seq // tm
    const = dict(pipeline_mode=pl.Buffered(1))
    return pl.pallas_call(
        _merge_kernel,
        grid=(t // tm,),
        in_specs=[pl.BlockSpec((tm, w), lambda i: (i, 0)),
                  pl.BlockSpec((tm, w), lambda i: (i, 0)),
                  pl.BlockSpec((tm, d), lambda i: (i, ga_col)),
                  pl.BlockSpec((tm, d), lambda i: (i, gb_col)),
                  pl.BlockSpec((tm, d), lambda i: (i, 0)),
                  pl.BlockSpec((None, 6, d), lambda i: (i // per_batch, 0, 0)),
                  pl.BlockSpec((w, d), lambda i: (0, 0), **const),
                  pl.BlockSpec((w, d), lambda i: (0, 0), **const),
                  pl.BlockSpec((d, d), lambda i: (0, 0), **const)],
        out_specs=pl.BlockSpec((tm, d), lambda i: (i, 0)),
        out_shape=jax.ShapeDtypeStruct((t, d), F32),
        compiler_params=_cparams(("parallel",), 48),
    )(ya, yb, proj, proj, x2, modl, wa, wb, wo)


def _router_kernel(x_ref, mod_ref, nw_ref, wrt_ref, rb_ref, h_ref, r_ref):
    x = x_ref[...]
    h = x * _rms_scale(x) * nw_ref[...] * (1.0 + mod_ref[4:5, :]) + mod_ref[3:4, :]
    h_ref[...] = h.astype(h_ref.dtype)
    logits = lax.dot_general(wrt_ref[...], h, (((1,), (1,)), ((), ())), precision=HIGHEST,
                             preferred_element_type=F32)
    scores = jax.nn.sigmoid(logits)
    sel = scores + rb_ref[...]
    sel_rows = [sel[e:e + 1, :] for e in range(N_EXPERTS)]
    score_rows = [scores[e:e + 1, :] for e in range(N_EXPERTS)]
    epg = EXPERTS_PER_GROUP
    best_score = None
    best_group = None
    for g in range(N_GROUPS):
        rows = sel_rows[g * epg:(g + 1) * epg]
        gs = None
        for a in range(epg):
            for b in range(a + 1, epg):
                pair = rows[a] + rows[b]
                gs = pair if gs is None else jnp.maximum(gs, pair)
        if g == 0:
            best_score, best_group = gs, jnp.zeros_like(gs)
        else:
            better = gs > best_score
            best_group = jnp.where(better, float(g), best_group)
            best_score = jnp.where(better, gs, best_score)
    in_sel = []
    in_score = []
    for j in range(epg):
        a = sel_rows[j]
        b = score_rows[j]
        for g in range(1, N_GROUPS):
            pick = best_group == float(g)
            a = jnp.where(pick, sel_rows[g * epg + j], a)
            b = jnp.where(pick, score_rows[g * epg + j], b)
        in_sel.append(a)
        in_score.append(b)
    i1, m1, w1 = jnp.zeros_like(in_sel[0]), in_sel[0], in_score[0]
    for j in range(1, epg):
        better = in_sel[j] > m1
        i1 = jnp.where(better, float(j), i1)
        m1 = jnp.where(better, in_sel[j], m1)
        w1 = jnp.where(better, in_score[j], w1)
    i2 = m2 = w2 = None
    for j in range(epg):
        cand = jnp.where(i1 == float(j), -jnp.inf, in_sel[j])
        if j == 0:
            i2, m2, w2 = jnp.zeros_like(cand), cand, in_score[0]
        else:
            better = cand > m2
            i2 = jnp.where(better, float(j), i2)
            m2 = jnp.where(better, cand, m2)
            w2 = jnp.where(better, in_score[j], w2)
    total = w1 + w2
    zero = jnp.zeros_like(w1)
    r_ref[...] = jnp.concatenate([best_group * epg + i1, best_group * epg + i2, w1 / total, w2 / total,
                                  zero, zero, zero, zero], axis=0)


def _router(x2, modl, nw, wr_t, rb, seq):
    t, d = x2.shape
    tm = min(seq, 512)
    per_batch = seq // tm
    return pl.pallas_call(
        _router_kernel,
        grid=(t // tm,),
        in_specs=[pl.BlockSpec((tm, d), lambda i: (i, 0)),
                  pl.BlockSpec((None, 6, d), lambda i: (i // per_batch, 0, 0)),
                  pl.BlockSpec((1, d), lambda i: (0, 0)),
                  pl.BlockSpec((N_EXPERTS, d), lambda i: (0, 0)),
                  pl.BlockSpec((N_EXPERTS, 1), lambda i: (0, 0))],
        out_specs=[pl.BlockSpec((tm, d), lambda i: (i, 0)),
                   pl.BlockSpec((8, tm), lambda i: (0, i))],
        out_shape=[jax.ShapeDtypeStruct((t, d), F32), jax.ShapeDtypeStruct((8, t), F32)],
        compiler_params=_cparams(("parallel",), 40),
    )(x2, modl, nw, wr_t, rb)


def _moe_kernel(te_ref, nu_ref, x_ref, wg_ref, wu_ref, wd_ref, o_ref):
    used = pl.program_id(0) < nu_ref[0]

    @pl.when(used)
    def _():
        x = x_ref[...].astype(BF16)
        a = jnp.dot(x, wg_ref[...], preferred_element_type=F32)
        b = jnp.dot(x, wu_ref[...], preferred_element_type=F32)
        hid = (_silu(a) * b).astype(BF16)
        o_ref[...] = jnp.dot(hid, wd_ref[...], preferred_element_type=F32)

    @pl.when(jnp.logical_not(used))
    def _():
        o_ref[...] = jnp.zeros_like(o_ref)


def _moe_experts(tile_expert, n_used, xs, wg, wu, wd, layer, tm):
    p, d = xs.shape
    f = wg.shape[-1]
    grid_spec = pltpu.PrefetchScalarGridSpec(
        num_scalar_prefetch=2,
        grid=(p // tm,),
        in_specs=[pl.BlockSpec((tm, d), lambda i, te, nu: (i, 0)),
                  pl.BlockSpec((None, None, d, f), lambda i, te, nu: (layer, te[i], 0, 0)),
                  pl.BlockSpec((None, None, d, f), lambda i, te, nu: (layer, te[i], 0, 0)),
                  pl.BlockSpec((None, None, f, d), lambda i, te, nu: (layer, te[i], 0, 0))],
        out_specs=pl.BlockSpec((tm, d), lambda i, te, nu: (i, 0)),
    )
    return pl.pallas_call(
        _moe_kernel,
        grid_spec=grid_spec,
        out_shape=jax.ShapeDtypeStruct((p, d), F32),
        compiler_params=_cparams(("arbitrary",), 52),
    )(tile_expert, n_used, xs, wg, wu, wd)


def _dispatch_plan(expert_idx, tm, n_tiles):
    flat_e = expert_idx.reshape(-1)
    onehot = (flat_e[:, None] == jnp.arange(N_EXPERTS, dtype=jnp.int32)[None, :]).astype(jnp.int32)
    csum = jnp.cumsum(onehot, axis=0)
    rank = jnp.sum((csum - onehot) * onehot, axis=1)
    sizes = csum[-1]
    padded = ((sizes + tm - 1) // tm) * tm
    pad_end = jnp.cumsum(padded)
    pad_start = pad_end - padded
    dest = pad_start[flat_e] + rank
    tile_start = jnp.arange(n_tiles, dtype=jnp.int32) * tm
    tile_expert = jnp.minimum(jnp.sum((tile_start[:, None] >= pad_end[None, :]).astype(jnp.int32), axis=1),
                              N_EXPERTS - 1).astype(jnp.int32)
    n_used = (pad_end[-1] // tm).astype(jnp.int32).reshape(1)
    return dest.astype(jnp.int32), tile_expert, n_used


def _combine_kernel(x_ref, y0_ref, y1_ref, gw_ref, mod_ref, o_ref):
    gw = gw_ref[...]
    moe = gw[:, 0:1] * y0_ref[...] + gw[:, 1:2] * y1_ref[...]
    o_ref[...] = x_ref[...] + mod_ref[5:6, :] * moe


def _combine(x2, y_pairs, gate_w, modl, seq):
    t, d = x2.shape
    tm = min(seq, 512)
    per_batch = seq // tm
    nblk = t // tm
    return pl.pallas_call(
        _combine_kernel,
        grid=(nblk,),
        in_specs=[pl.BlockSpec((tm, d), lambda i: (i, 0)),
                  pl.BlockSpec((tm, d), lambda i: (i, 0)),
                  pl.BlockSpec((tm, d), lambda i: (i + nblk, 0)),
                  pl.BlockSpec((tm, TOP_K), lambda i: (i, 0)),
                  pl.BlockSpec((None, 6, d), lambda i: (i // per_batch, 0, 0))],
        out_specs=pl.BlockSpec((tm, d), lambda i: (i, 0)),
        out_shape=jax.ShapeDtypeStruct((t, d), F32),
        compiler_params=_cparams(("parallel",), 40),
    )(x2, y_pairs, y_pairs, gate_w, modl)


def _final_norm_kernel(x_ref, w_ref, o_ref):
    x = x_ref[...]
    o_ref[...] = x * _rms_scale(x) * w_ref[...]


def _final_norm(x2, w):
    t, d = x2.shape
    tm = min(t, 512)
    return pl.pallas_call(
        _final_norm_kernel,
        grid=(t // tm,),
        in_specs=[pl.BlockSpec((tm, d), lambda i: (i, 0)), pl.BlockSpec((1, d), lambda i: (0, 0))],
        out_specs=pl.BlockSpec((tm, d), lambda i: (i, 0)),
        out_shape=jax.ShapeDtypeStruct((t, d), F32),
        compiler_params=_cparams(("parallel",), 32),
    )(x2, w)


def kernel(x, c, w_ada, b_ada, norm1_w, norm2_w, w_in, s5_a_re, s5_a_im, s5_log_dt, s5_b_re, s5_b_im, s5_c_re, s5_c_im, s5_d, s5_w_glu, s5_b_glu, gdn_conv_w, gdn_a_log, gdn_dt_bias, gdn_norm_w, w_proj_a, w_proj_b, w_out, w_router, router_bias, w_gate, w_up, w_down, final_norm_w):
    batch, seq, d = x.shape
    depth = w_ada.shape[0]
    t = batch * seq
    s5w = s5_d.shape[1]
    gdw = w_proj_b.shape[1]
    heads = gdn_a_log.shape[1]
    assert gdw == heads * HEAD_DIM and s5w == gdw and d == 2 * gdw
    assert seq % GDN_TILE == 0 and seq % S5_CHUNK == 0
    n_levels = int(math.log2(seq // S5_CHUNK))
    assert S5_CHUNK << n_levels == seq and S5_OCT * S5_GROUP == 128

    c_pad = jnp.zeros((8, d), F32).at[:batch].set(c)
    mod = _ada(c_pad, w_ada, b_ada)

    o_u, o_q, o_k, o_v, o_z = 0, s5w, 2 * s5w, 3 * s5w, 4 * s5w
    o_ba = 5 * s5w
    o_ga = o_ba + 2 * heads
    o_gb = o_ga + d
    col_u, col_q, col_k, col_v, col_z = 4, 5, 6, 7, 8

    gdn_masks = _gdn_masks()
    s5_rep = _s5_replicate()

    wr_t = w_router.T.astype(F32)
    rb = router_bias.reshape(N_EXPERTS, 1).astype(F32)

    tm_e = min(512, t)
    n_tiles = (TOP_K * t) // tm_e + N_EXPERTS
    p_rows = n_tiles * tm_e

    w_big = jnp.concatenate([w_in[:, :, o_ga:o_ga + d], w_in[:, :, o_gb:o_gb + d], w_in[:, :, o_u:o_ba]],
                            axis=2).astype(BF16)
    w_small = jnp.pad(w_in[:, :, o_ba:o_ga], ((0, 0), (0, 0), (0, 128 - 2 * heads))).astype(BF16)
    w_gate_b = w_gate.astype(BF16)
    w_up_b = w_up.astype(BF16)
    w_down_b = w_down.astype(BF16)

    x2 = x.reshape(t, d)
    for l in range(depth):
        modl = mod[l, :batch].reshape(batch, 6, d)
        proj, small = _inproj(x2, modl, norm1_w[l].reshape(1, d), w_big, w_small, l, seq)

        s5_ops = _s5_weights(s5_a_re[l], s5_a_im[l], s5_log_dt[l], s5_b_re[l], s5_b_im[l],
                             s5_c_re[l], s5_c_im[l], n_levels)
        y_lin = _s5_core(proj, col_u * s5w // 128, *s5_ops, s5_rep, batch, seq)
        y_a = _s5_post(y_lin, proj, col_u, s5_d[l].reshape(1, s5w), s5_w_glu[l].astype(BF16),
                       s5_b_glu[l].reshape(1, s5w))

        head_params = jnp.zeros((8, 128), F32)
        head_params = head_params.at[0, heads:2 * heads].set(-jnp.exp(gdn_a_log[l]))
        head_params = head_params.at[1, heads:2 * heads].set(gdn_dt_bias[l])
        y_b = _gdn(proj, small, gdn_conv_w[l], head_params, gdn_norm_w[l].reshape(1, HEAD_DIM), gdn_masks,
                   (col_q, col_k, col_v, col_z), batch, seq, heads)

        x2 = _merge(y_a, y_b, proj, x2, modl, w_proj_a[l].astype(BF16), w_proj_b[l].astype(BF16),
                    w_out[l].astype(BF16), seq, 0, 1)

        h2, route = _router(x2, modl, norm2_w[l].reshape(1, d), wr_t, rb, seq)
        expert_idx = route[0:TOP_K].T.astype(jnp.int32)
        gate_w = route[TOP_K:2 * TOP_K].T
        dest, tile_expert, n_used = _dispatch_plan(expert_idx, tm_e, n_tiles)
        src_tok = jnp.zeros((p_rows,), jnp.int32).at[dest].set(jnp.arange(TOP_K * t, dtype=jnp.int32) // TOP_K)
        xs = jnp.take(h2, src_tok, axis=0, mode='clip')
        ys = _moe_experts(tile_expert, n_used, xs, w_gate_b, w_up_b, w_down_b, l, tm_e)
        y_pairs = jnp.take(ys, dest.reshape(t, TOP_K).T.reshape(-1), axis=0, mode='clip')
        x2 = _combine(x2, y_pairs, gate_w, modl, seq)

    return _final_norm(x2, final_norm_w.reshape(1, d)).reshape(batch, seq, d)
```

```python
import functools
import math

import jax
import jax.numpy as jnp
from jax import lax
from jax.experimental import pallas as pl
from jax.experimental.pallas import tpu as pltpu

F32 = jnp.float32
BF16 = jnp.bfloat16
HIGHEST = lax.Precision.HIGHEST

RMS_EPS = 1e-6
S5_GROUP = 16
S5_STATE = 64
S5_CHUNK = 16
S5_OCT = 8
HEAD_DIM = 128
GDN_CHUNK = 64
GDN_TILE = 256
GDN_HEADS_PER_TRIP = 4
CONV_K = 4
N_EXPERTS = 16
N_GROUPS = 4
EXPERTS_PER_GROUP = N_EXPERTS // N_GROUPS
TOP_K = 2
MIB = 1024 * 1024


def _cparams(semantics, vmem_mib):
    return pltpu.CompilerParams(dimension_semantics=semantics, vmem_limit_bytes=vmem_mib * MIB)


def _silu(x):
    return x * jax.nn.sigmoid(x)


def _softplus(x):
    return jnp.maximum(x, 0.0) + jnp.log1p(jnp.exp(-jnp.abs(x)))


def _rms_scale(x):
    return lax.rsqrt(jnp.mean(x * x, axis=-1, keepdims=True) + RMS_EPS)


def _nt_dot(a, b):
    return lax.dot_general(a, b, (((1,), (1,)), ((), ())), preferred_element_type=F32)


def _tn_dot(a, b):
    return lax.dot_general(a, b, (((0,), (0,)), ((), ())), preferred_element_type=F32)


def _ada_kernel(c_ref, w_ref, b_ref, o_ref):
    ca = _silu(c_ref[...]).astype(BF16)
    o_ref[...] = jnp.dot(ca, w_ref[...].astype(BF16), preferred_element_type=F32) + b_ref[...]


def _ada(c_pad, w_ada, b_ada):
    depth, d, n = w_ada.shape
    rows = c_pad.shape[0]
    tn = min(n, 1024)
    return pl.pallas_call(
        _ada_kernel,
        grid=(depth, n // tn),
        in_specs=[pl.BlockSpec((rows, d), lambda l, j: (0, 0)),
                  pl.BlockSpec((None, d, tn), lambda l, j: (l, 0, j)),
                  pl.BlockSpec((None, 1, tn), lambda l, j: (l, 0, j))],
        out_specs=pl.BlockSpec((None, rows, tn), lambda l, j: (l, 0, j)),
        out_shape=jax.ShapeDtypeStruct((depth, rows, n), F32),
        compiler_params=_cparams(("parallel", "parallel"), 40),
    )(c_pad, w_ada, b_ada.reshape(depth, 1, n))


def _inproj_kernel(x_ref, mod_ref, nw_ref, w_ref, wsm_ref, o_ref, osm_ref, h_scr):
    @pl.when(pl.program_id(1) == 0)
    def _():
        x = x_ref[...]
        h = x * _rms_scale(x) * nw_ref[...] * (1.0 + mod_ref[1:2, :]) + mod_ref[0:1, :]
        hb = h.astype(BF16)
        h_scr[...] = hb
        osm_ref[...] = jnp.dot(hb, wsm_ref[...], preferred_element_type=F32)

    o_ref[...] = jnp.dot(h_scr[...], w_ref[...], preferred_element_type=F32)


def _inproj(x2, modl, nw, w_big, w_small, layer, seq):
    t, d = x2.shape
    n = w_big.shape[-1]
    tm = min(seq, 1024)
    tn = 1024
    per_batch = seq // tm
    return pl.pallas_call(
        _inproj_kernel,
        grid=(t // tm, n // tn),
        in_specs=[pl.BlockSpec((tm, d), lambda i, j: (i, 0)),
                  pl.BlockSpec((None, 6, d), lambda i, j: (i // per_batch, 0, 0)),
                  pl.BlockSpec((1, d), lambda i, j: (0, 0)),
                  pl.BlockSpec((None, d, tn), lambda i, j: (layer, 0, j)),
                  pl.BlockSpec((None, d, 128), lambda i, j: (layer, 0, 0))],
        out_specs=[pl.BlockSpec((tm, tn), lambda i, j: (i, j)),
                   pl.BlockSpec((tm, 128), lambda i, j: (i, 0))],
        out_shape=[jax.ShapeDtypeStruct((t, n), F32), jax.ShapeDtypeStruct((t, 128), F32)],
        scratch_shapes=[pltpu.VMEM((tm, d), BF16)],
        compiler_params=_cparams(("parallel", "arbitrary"), 48),
    )(x2, modl, nw, w_big, w_small)


def _s5_weights(a_re, a_im, log_dt, b_re, b_im, c_re, c_im, n_levels):
    tc = S5_CHUNK
    groups, p = a_re.shape
    n = b_re.shape[-1]
    oc = S5_OCT
    octs = groups // oc
    lam = lax.complex(a_re, a_im)
    dt = jnp.exp(log_dt)[:, None]
    log_a = lam * dt
    a_bar = jnp.exp(log_a)
    b_bar = ((a_bar - 1.0) / lam)[..., None] * lax.complex(b_re, b_im)
    c_mat = lax.complex(c_re, c_im)
    taus = jnp.arange(tc + 1, dtype=F32)
    apow = jnp.exp(log_a[None] * taus[:, None, None])
    kern = jnp.real(jnp.einsum('gnp,tgp,gpm->gmtn', c_mat, apow[:tc], b_bar, precision=HIGHEST))
    tcomp = jnp.stack([jnp.pad(kern[:, :, :tc - s, :], ((0, 0), (0, 0), (s, 0), (0, 0))) for s in range(tc)],
                      axis=1)
    tcomp = jnp.transpose(tcomp.reshape(octs, oc, tc, n, tc * n), (0, 2, 1, 3, 4)).reshape(octs, tc * oc * n, tc * n)
    ap_rev = jnp.transpose(apow[tc - 1 - jnp.arange(tc)], (1, 0, 2))
    hin_c = ap_rev[:, :, None, :] * jnp.transpose(b_bar, (0, 2, 1))[:, None, :, :]
    hc = jnp.concatenate([jnp.real(hin_c), jnp.imag(hin_c)], axis=-1)
    hc = jnp.transpose(hc.reshape(octs, oc, tc, n, 2 * p), (0, 2, 1, 3, 4)).reshape(octs, tc * oc * n, 2 * p)
    ap1 = jnp.transpose(apow[1:tc + 1], (1, 2, 0))
    coef = ap1[:, :, :, None] * jnp.transpose(c_mat, (0, 2, 1))[:, :, None, :]
    hoc = jnp.stack([jnp.real(coef), -jnp.imag(coef)], axis=1).reshape(octs, oc * 2 * p, tc * n)
    steps = (tc * (2.0 ** jnp.arange(n_levels, dtype=F32)))
    amul = jnp.exp(log_a[:, None, :] * steps[None, :, None])
    a1 = jnp.concatenate([jnp.real(amul), jnp.real(amul)], axis=-1)
    a2 = jnp.concatenate([-jnp.imag(amul), jnp.imag(amul)], axis=-1)

    def lanes_by_group(a):
        return jnp.transpose(a.reshape(octs, oc, n_levels, 2 * p), (0, 2, 1, 3)).reshape(octs, n_levels, oc * 2 * p)

    return (tcomp.astype(BF16), hc.astype(BF16), hoc.astype(BF16),
            lanes_by_group(a1).astype(F32), lanes_by_group(a2).astype(F32))


def _s5_replicate():
    r = lax.broadcasted_iota(jnp.int32, (S5_CHUNK * S5_GROUP, S5_CHUNK * 128), 0)
    c = lax.broadcasted_iota(jnp.int32, (S5_CHUNK * S5_GROUP, S5_CHUNK * 128), 1)
    return ((r // S5_GROUP == c // 128) & (r % S5_GROUP == c % S5_GROUP)).astype(BF16)


def _s5_core_kernel(u_ref, tc_ref, hc_ref, hoc_ref, a1_ref, a2_ref, rep_ref, y_ref, toep_scr, hin_scr, hout_scr,
                    *, n_levels):
    tc = S5_CHUNK
    lanes = u_ref.shape[1]
    state_lanes = hc_ref.shape[1]

    @pl.when(pl.program_id(1) == 0)
    def _():
        kd = toep_scr.shape[1]
        rb = 256
        col_g = (lax.broadcasted_iota(jnp.int32, (rb, kd), 1) // S5_GROUP) % S5_OCT
        row_i = lax.broadcasted_iota(jnp.int32, (rb, kd), 0)
        for i in range(toep_scr.shape[0] // rb):
            full = jnp.dot(tc_ref[i * rb:(i + 1) * rb, :], rep_ref[...], preferred_element_type=F32)
            row_g = ((row_i + i * rb) // S5_GROUP) % S5_OCT
            toep_scr[i * rb:(i + 1) * rb, :] = jnp.where(row_g == col_g, full, 0.0).astype(BF16)
        for i in range(hout_scr.shape[0] // rb):
            full = jnp.dot(hoc_ref[i * rb:(i + 1) * rb, :], rep_ref[...], preferred_element_type=F32)
            row_g = (row_i + i * rb) // state_lanes
            hout_scr[i * rb:(i + 1) * rb, :] = jnp.where(row_g == col_g, full, 0.0).astype(BF16)
        hrow_g = (lax.broadcasted_iota(jnp.int32, hc_ref.shape, 0) // S5_GROUP) % S5_OCT
        for g in range(S5_OCT):
            hin_scr[:, g * state_lanes:(g + 1) * state_lanes] = jnp.where(hrow_g == g, hc_ref[...], 0.0).astype(BF16)

    rows = u_ref.shape[0] // tc
    x = jnp.concatenate([u_ref[pl.ds(t, rows, stride=tc), :] for t in range(tc)], axis=1).astype(BF16)
    s_loc = jnp.dot(x, hin_scr[...], preferred_element_type=F32)
    pos = lax.broadcasted_iota(jnp.int32, (rows, state_lanes), 0)
    s_in = []
    for g in range(S5_OCT):
        ln = slice(g * state_lanes, (g + 1) * state_lanes)
        s = s_loc[:, ln]
        for k in range(n_levels):
            d = 1 << k
            prev = jnp.where(pos >= d, pltpu.roll(s, d, axis=0), 0.0)
            s = s + a1_ref[k:k + 1, ln] * prev + a2_ref[k:k + 1, ln] * pltpu.roll(prev, state_lanes // 2, axis=1)
        s_in.append(jnp.where(pos >= 1, pltpu.roll(s, 1, axis=0), 0.0).astype(BF16))
    y = (jnp.dot(x, toep_scr[...], preferred_element_type=F32)
         + jnp.dot(jnp.concatenate(s_in, axis=1), hout_scr[...], preferred_element_type=F32))
    for t in range(tc):
        y_ref[pl.ds(t, rows, stride=tc), :] = y[:, t * lanes:(t + 1) * lanes]


def _s5_core(proj, u_lane_block, tcomp, hc, hoc, a1, a2, rep, layer, batch, seq):
    t = proj.shape[0]
    _, octs, kdim, tn = tcomp.shape
    sl = hc.shape[-1]
    sw = hoc.shape[2]
    n_levels = a1.shape[2]
    lanes = kdim // S5_CHUNK
    kern = functools.partial(_s5_core_kernel, n_levels=n_levels)
    return pl.pallas_call(
        kern,
        grid=(octs, batch),
        in_specs=[pl.BlockSpec((seq, lanes), lambda o, b: (b, u_lane_block + o)),
                  pl.BlockSpec((None, None, kdim, tn), lambda o, b: (layer, o, 0, 0)),
                  pl.BlockSpec((None, None, kdim, sl), lambda o, b: (layer, o, 0, 0)),
                  pl.BlockSpec((None, None, sw, tn), lambda o, b: (layer, o, 0, 0)),
                  pl.BlockSpec((None, None, n_levels, sw), lambda o, b: (layer, o, 0, 0)),
                  pl.BlockSpec((None, None, n_levels, sw), lambda o, b: (layer, o, 0, 0)),
                  pl.BlockSpec((tn, kdim), lambda o, b: (0, 0))],
        out_specs=pl.BlockSpec((seq, lanes), lambda o, b: (b, o)),
        out_shape=jax.ShapeDtypeStruct((t, octs * lanes), F32),
        scratch_shapes=[pltpu.VMEM((kdim, kdim), BF16), pltpu.VMEM((kdim, sw), BF16), pltpu.VMEM((sw, kdim), BF16)],
        compiler_params=_cparams(("parallel", "arbitrary"), 48),
    )(proj, tcomp, hc, hoc, a1, a2, rep)


def _s5_post_kernel(y_ref, u_ref, d_ref, w_ref, b_ref, o_ref):
    y = jax.nn.gelu(y_ref[...] + d_ref[...] * u_ref[...])
    gate = jnp.dot(y.astype(BF16), w_ref[...], preferred_element_type=F32) + b_ref[...]
    o_ref[...] = (y * jax.nn.sigmoid(gate)).astype(o_ref.dtype)


def _s5_post(y_lin, proj, u_col, d_skip, w_glu, b_glu):
    t, w = y_lin.shape
    tm = min(t, 512)
    return pl.pallas_call(
        _s5_post_kernel,
        grid=(t // tm,),
        in_specs=[pl.BlockSpec((tm, w), lambda i: (i, 0)),
                  pl.BlockSpec((tm, w), lambda i: (i, u_col)),
                  pl.BlockSpec((1, w), lambda i: (0, 0)),
                  pl.BlockSpec((w, w), lambda i: (0, 0)),
                  pl.BlockSpec((1, w), lambda i: (0, 0))],
        out_specs=pl.BlockSpec((tm, w), lambda i: (i, 0)),
        out_shape=jax.ShapeDtypeStruct((t, w), BF16),
        compiler_params=_cparams(("parallel",), 32),
    )(y_lin, proj, d_skip, w_glu, b_glu)


def _gdn_kernel(q_ref, k_ref, v_ref, qh_ref, kh_ref, vh_ref, z_ref, sm_ref, cq_ref, ck_ref, cv_ref,
                hp_ref, nw_ref, tril_ref, cmask_ref, lmask_ref, o_ref, betab, gcb, grow, st_scr, *, heads):
    tt = GDN_TILE
    hd = HEAD_DIM
    nck = tt // GDN_CHUNK
    first = pl.program_id(1) == 0

    @pl.when(first)
    def _():
        st_scr[...] = jnp.zeros_like(st_scr)

    sm = sm_ref[...]
    hp = hp_ref[...]
    beta_all = jax.nn.sigmoid(sm)
    g_all = hp[0:1, :] * _softplus(sm + hp[1:2, :])
    gc_all = jnp.dot(tril_ref[...], g_all, precision=HIGHEST, preferred_element_type=F32)
    gc_t = gc_all.T
    for h in range(heads):
        betab[h] = jnp.broadcast_to(beta_all[:, h:h + 1], (tt, hd))
        gcb[h] = jnp.broadcast_to(gc_all[:, heads + h:heads + h + 1], (tt, hd))
        grow[h:h + 1, :] = gc_t[heads + h:heads + h + 1, :]

    row8 = lax.broadcasted_iota(jnp.int32, (8, hd), 0)
    eye = (lax.broadcasted_iota(jnp.int32, (tt, tt), 0) == lax.broadcasted_iota(jnp.int32, (tt, tt), 1)).astype(F32)
    n_lv = lmask_ref.shape[0]

    def conv_silu(x_ref, halo_ref, cw_ref, hs):
        x = x_ref[:, pl.ds(hs, hd)]
        halo = jnp.where(first, 0.0, halo_ref[:, pl.ds(hs, hd)])
        w = cw_ref[:, pl.ds(hs, hd)]
        acc = x * w[CONV_K - 1:CONV_K, :]
        for s in range(1, CONV_K):
            rolled = pltpu.roll(x, s, axis=0)
            head8 = jnp.where(row8 < s, pltpu.roll(halo, s, axis=0), rolled[:8])
            shifted = jnp.concatenate([head8, rolled[8:]], axis=0)
            acc = acc + shifted * w[CONV_K - 1 - s:CONV_K - s, :]
        return _silu(acc)

    def heads_lockstep(hs_idx):
        each = lambda f, *ls: [f(*a) for a in zip(*ls)]
        hs = [pl.multiple_of(h * hd, hd) for h in hs_idx]
        q = [conv_silu(q_ref, qh_ref, cq_ref, o) for o in hs]
        k = [conv_silu(k_ref, kh_ref, ck_ref, o) for o in hs]
        v = [conv_silu(v_ref, vh_ref, cv_ref, o) for o in hs]
        qn = each(lambda a: a * (lax.rsqrt(jnp.sum(a * a, axis=-1, keepdims=True) + RMS_EPS) * (hd ** -0.5)), q)
        kn = each(lambda a: a * lax.rsqrt(jnp.sum(a * a, axis=-1, keepdims=True) + RMS_EPS), k)
        bb = [betab[h] for h in hs_idx]
        gc = [gcb[h] for h in hs_idx]
        gr = [grow[pl.ds(h, 1), :] for h in hs_idx]
        decay = each(lambda c, r: jnp.exp(jnp.minimum(jnp.concatenate([c, c], axis=1) - r, 0.0)) * cmask_ref[0],
                     gc, gr)
        kb = each(lambda a, b: a * b, kn, bb)
        knb = each(lambda a: a.astype(BF16), kn)
        lmat = each(lambda a, b, dc: (_nt_dot(a.astype(BF16), b) * (dc * cmask_ref[1])).astype(BF16),
                    kb, knb, decay)
        tinv = each(lambda m: eye - (m * lmask_ref[0]).astype(F32), lmat)
        for lv in range(1, n_lv):
            tb = each(lambda a: a.astype(BF16), tinv)
            tl = each(lambda a, m: jnp.dot(a, m * lmask_ref[lv], preferred_element_type=F32).astype(BF16), tb, lmat)
            tinv = each(lambda a, b, c: a - jnp.dot(b, c, preferred_element_type=F32), tinv, tl, tb)
        eg = each(jnp.exp, gc)
        rhs = each(lambda a, e, b, c: jnp.concatenate([a * e, b * c], axis=1).astype(BF16), kb, eg, v, bb)
        wu = each(lambda a, b: jnp.dot(a.astype(BF16), b, preferred_element_type=F32).astype(BF16), tinv, rhs)
        attn = each(lambda a, b, dc: (_nt_dot(a.astype(BF16), b) * dc).astype(BF16), qn, knb, decay)
        awu = each(lambda a, b: jnp.dot(a, b, preferred_element_type=F32), attn, wu)
        qp = each(lambda a, e, b: (a * e - b[:, :hd]).astype(BF16), qn, eg, awu)
        s = [st_scr[h] for h in hs_idx]
        outs = [[] for _ in hs_idx]
        for c in range(nck):
            r0 = c * GDN_CHUNK
            r1 = r0 + GDN_CHUNK
            gl = each(lambda a: a[r1 - 1:r1, :], gc)
            kd = each(lambda a, l, g: (a[r0:r1] * jnp.exp(l - g[r0:r1])).astype(BF16), kn, gl, gc)
            m = each(lambda a, b: _tn_dot(a, b[r0:r1]), kd, wu)
            sb = each(lambda a: a.astype(BF16), s)
            for i, (a, b, y) in enumerate(zip(qp, sb, awu)):
                outs[i].append(jnp.dot(a[r0:r1], b, preferred_element_type=F32) + y[r0:r1, hd:])
            s = each(lambda a, l, mm, b: a * jnp.exp(l) - jnp.dot(mm[:, :hd].astype(BF16), b,
                                                                  preferred_element_type=F32) + mm[:, hd:],
                     s, gl, m, sb)
        for i, h in enumerate(hs_idx):
            st_scr[h] = s[i]
            o = jnp.concatenate(outs[i], axis=0)
            z = z_ref[:, pl.ds(hs[i], hd)]
            o_ref[:, pl.ds(hs[i], hd)] = (o * _rms_scale(o) * nw_ref[...] * _silu(z)).astype(o_ref.dtype)

    def trip(j, carry):
        heads_lockstep([GDN_HEADS_PER_TRIP * j + i for i in range(GDN_HEADS_PER_TRIP)])
        return carry

    lax.fori_loop(0, heads // GDN_HEADS_PER_TRIP, trip, 0)


def _gdn_masks():
    tt = GDN_TILE
    ri = lax.broadcasted_iota(jnp.int32, (tt, tt), 0)
    ci = lax.broadcasted_iota(jnp.int32, (tt, tt), 1)
    same = (ri // GDN_CHUNK) == (ci // GDN_CHUNK)
    causal = same & (ci <= ri)
    tril = causal.astype(F32)
    cmask = jnp.stack([tril, (ri != ci).astype(F32)])
    levels = []
    d = 1
    while d < GDN_CHUNK:
        levels.append((((ri // d) % 2) == 1) & ((ci // d) == (ri // d) - 1))
        d *= 2
    return tril, cmask, jnp.stack(levels).astype(BF16)


def _gdn(proj, small, conv_w, head_params, norm_w, masks, cols, batch, seq, heads):
    t = proj.shape[0]
    tt = GDN_TILE
    width = heads * HEAD_DIM
    tiles = seq // tt
    cq, ck, cv, cz = cols
    tril, cmask, lmask = masks

    def cur(col):
        return pl.BlockSpec((tt, width), lambda b, i: (b * tiles + i, col))

    def halo(col):
        return pl.BlockSpec((8, width), lambda b, i: (jnp.maximum((b * tiles + i) * (tt // 8) - 1, 0), col))

    def cw(col):
        return pl.BlockSpec((CONV_K, width), lambda b, i: (0, col))

    kern = functools.partial(_gdn_kernel, heads=heads)
    return pl.pallas_call(
        kern,
        grid=(batch, tiles),
        in_specs=[cur(cq), cur(ck), cur(cv), halo(cq), halo(ck), halo(cv), cur(cz),
                  pl.BlockSpec((tt, 128), lambda b, i: (b * tiles + i, 0)),
                  cw(0), cw(1), cw(2),
                  pl.BlockSpec((8, 128), lambda b, i: (0, 0)),
                  pl.BlockSpec((1, HEAD_DIM), lambda b, i: (0, 0)),
                  pl.BlockSpec((tt, tt), lambda b, i: (0, 0)),
                  pl.BlockSpec(cmask.shape, lambda b, i: (0, 0, 0)),
                  pl.BlockSpec(lmask.shape, lambda b, i: (0, 0, 0))],
        out_specs=pl.BlockSpec((tt, width), lambda b, i: (b * tiles + i, 0)),
        out_shape=jax.ShapeDtypeStruct((t, width), BF16),
        scratch_shapes=[pltpu.VMEM((heads, tt, HEAD_DIM), F32),
                        pltpu.VMEM((heads, tt, HEAD_DIM), F32),
                        pltpu.VMEM((8, tt), F32),
                        pltpu.VMEM((heads, HEAD_DIM, HEAD_DIM), F32)],
        compiler_params=_cparams(("parallel", "arbitrary"), 40),
    )(proj, proj, proj, proj, proj, proj, proj, small, conv_w, conv_w, conv_w, head_params, norm_w,
      tril, cmask, lmask)


def _merge_kernel(ya_ref, yb_ref, ga_ref, gb_ref, x_ref, mod_ref, wa_ref, wb_ref, wo_ref, o_ref):
    pa = jnp.dot(ya_ref[...], wa_ref[...], preferred_element_type=F32)
    pb = jnp.dot(yb_ref[...], wb_ref[...], preferred_element_type=F32)
    merged = jax.nn.sigmoid(ga_ref[...]) * pa + jax.nn.sigmoid(gb_ref[...]) * pb
    out = jnp.dot(merged.astype(BF16), wo_ref[...], preferred_element_type=F32)
    o_ref[...] = x_ref[...] + mod_ref[2:3, :] * out


def _merge(ya, yb, proj, x2, modl, wa, wb, wo, seq, ga_col, gb_col):
    t, d = x2.shape
    w = ya.shape[1]
    tm = min(seq, 256)
    per_batch = seq // tm
    const = dict(pipeline_mode=pl.Buffered(1))
    return pl.pallas_call(
        _merge_kernel,
        grid=(t // tm,),
        in_specs=[pl.BlockSpec((tm, w), lambda i: (i, 0)),
                  pl.BlockSpec((tm, w), lambda i: (i, 0)),
                  pl.BlockSpec((tm, d), lambda i: (i, ga_col)),
                  pl.BlockSpec((tm, d), lambda i: (i, gb_col)),
                  pl.BlockSpec((tm, d), lambda i: (i, 0)),
                  pl.BlockSpec((None, 6, d), lambda i: (i // per_batch, 0, 0)),
                  pl.BlockSpec((w, d), lambda i: (0, 0), **const),
                  pl.BlockSpec((w, d), lambda i: (0, 0), **const),
                  pl.BlockSpec((d, d), lambda i: (0, 0), **const)],
        out_specs=pl.BlockSpec((tm, d), lambda i: (i, 0)),
        out_shape=jax.ShapeDtypeStruct((t, d), F32),
        compiler_params=_cparams(("parallel",), 48),
    )(ya, yb, proj, proj, x2, modl, wa, wb, wo)


def _router_kernel(x_ref, mod_ref, nw_ref, wrt_ref, rb_ref, h_ref, r_ref):
    x = x_ref[...]
    h = x * _rms_scale(x) * nw_ref[...] * (1.0 + mod_ref[4:5, :]) + mod_ref[3:4, :]
    h_ref[...] = h.astype(h_ref.dtype)
    logits = lax.dot_general(wrt_ref[...], h, (((1,), (1,)), ((), ())), precision=HIGHEST,
                             preferred_element_type=F32)
    scores = jax.nn.sigmoid(logits)
    sel = scores + rb_ref[...]
    sel_rows = [sel[e:e + 1, :] for e in range(N_EXPERTS)]
    score_rows = [scores[e:e + 1, :] for e in range(N_EXPERTS)]
    epg = EXPERTS_PER_GROUP
    best_score = None
    best_group = None
    for g in range(N_GROUPS):
        rows = sel_rows[g * epg:(g + 1) * epg]
        gs = None
        for a in range(epg):
            for b in range(a + 1, epg):
                pair = rows[a] + rows[b]
                gs = pair if gs is None else jnp.maximum(gs, pair)
        if g == 0:
            best_score, best_group = gs, jnp.zeros_like(gs)
        else:
            better = gs > best_score
            best_group = jnp.where(better, float(g), best_group)
            best_score = jnp.where(better, gs, best_score)
    in_sel = []
    in_score = []
    for j in range(epg):
        a = sel_rows[j]
        b = score_rows[j]
        for g in range(1, N_GROUPS):
            pick = best_group == float(g)
            a = jnp.where(pick, sel_rows[g * epg + j], a)
            b = jnp.where(pick, score_rows[g * epg + j], b)
        in_sel.append(a)
        in_score.append(b)
    i1, m1, w1 = jnp.zeros_like(in_sel[0]), in_sel[0], in_score[0]
    for j in range(1, epg):
        better = in_sel[j] > m1
        i1 = jnp.where(better, float(j), i1)
        m1 = jnp.where(better, in_sel[j], m1)
        w1 = jnp.where(better, in_score[j], w1)
    i2 = m2 = w2 = None
    for j in range(epg):
        cand = jnp.where(i1 == float(j), -jnp.inf, in_sel[j])
        if j == 0:
            i2, m2, w2 = jnp.zeros_like(cand), cand, in_score[0]
        else:
            better = cand > m2
            i2 = jnp.where(better, float(j), i2)
            m2 = jnp.where(better, cand, m2)
            w2 = jnp.where(better, in_score[j], w2)
    total = w1 + w2
    zero = jnp.zeros_like(w1)
    r_ref[...] = jnp.concatenate([best_group * epg + i1, best_group * epg + i2, w1 / total, w2 / total,
                                  zero, zero, zero, zero], axis=0)


def _router(x2, modl, nw, wr_t, rb, seq):
    t, d = x2.shape
    tm = min(seq, 512)
    per_batch = seq // tm
    return pl.pallas_call(
        _router_kernel,
        grid=(t // tm,),
        in_specs=[pl.BlockSpec((tm, d), lambda i: (i, 0)),
                  pl.BlockSpec((None, 6, d), lambda i: (i // per_batch, 0, 0)),
                  pl.BlockSpec((1, d), lambda i: (0, 0)),
                  pl.BlockSpec((N_EXPERTS, d), lambda i: (0, 0)),
                  pl.BlockSpec((N_EXPERTS, 1), lambda i: (0, 0))],
        out_specs=[pl.BlockSpec((tm, d), lambda i: (i, 0)),
                   pl.BlockSpec((8, tm), lambda i: (0, i))],
        out_shape=[jax.ShapeDtypeStruct((t, d), F32), jax.ShapeDtypeStruct((8, t), F32)],
        compiler_params=_cparams(("parallel",), 40),
    )(x2, modl, nw, wr_t, rb)


def _moe_kernel(te_ref, nu_ref, x_ref, wg_ref, wu_ref, wd_ref, o_ref):
    used = pl.program_id(0) < nu_ref[0]

    @pl.when(used)
    def _():
        x = x_ref[...].astype(BF16)
        a = jnp.dot(x, wg_ref[...], preferred_element_type=F32)
        b = jnp.dot(x, wu_ref[...], preferred_element_type=F32)
        hid = (_silu(a) * b).astype(BF16)
        o_ref[...] = jnp.dot(hid, wd_ref[...], preferred_element_type=F32)

    @pl.when(jnp.logical_not(used))
    def _():
        o_ref[...] = jnp.zeros_like(o_ref)


def _moe_experts(tile_expert, n_used, xs, wg, wu, wd, layer, tm):
    p, d = xs.shape
    f = wg.shape[-1]
    grid_spec = pltpu.PrefetchScalarGridSpec(
        num_scalar_prefetch=2,
        grid=(p // tm,),
        in_specs=[pl.BlockSpec((tm, d), lambda i, te, nu: (i, 0)),
                  pl.BlockSpec((None, None, d, f), lambda i, te, nu: (layer, te[i], 0, 0)),
                  pl.BlockSpec((None, None, d, f), lambda i, te, nu: (layer, te[i], 0, 0)),
                  pl.BlockSpec((None, None, f, d), lambda i, te, nu: (layer, te[i], 0, 0))],
        out_specs=pl.BlockSpec((tm, d), lambda i, te, nu: (i, 0)),
    )
    return pl.pallas_call(
        _moe_kernel,
        grid_spec=grid_spec,
        out_shape=jax.ShapeDtypeStruct((p, d), F32),
        compiler_params=_cparams(("arbitrary",), 52),
    )(tile_expert, n_used, xs, wg, wu, wd)


def _dispatch_plan(expert_idx, tm, n_tiles):
    flat_e = expert_idx.reshape(-1)
    onehot = (flat_e[:, None] == jnp.arange(N_EXPERTS, dtype=jnp.int32)[None, :]).astype(jnp.int32)
    csum = jnp.cumsum(onehot, axis=0)
    rank = jnp.sum((csum - onehot) * onehot, axis=1)
    sizes = csum[-1]
    padded = ((sizes + tm - 1) // tm) * tm
    pad_end = jnp.cumsum(padded)
    pad_start = pad_end - padded
    dest = pad_start[flat_e] + rank
    tile_start = jnp.arange(n_tiles, dtype=jnp.int32) * tm
    tile_expert = jnp.minimum(jnp.sum((tile_start[:, None] >= pad_end[None, :]).astype(jnp.int32), axis=1),
                              N_EXPERTS - 1).astype(jnp.int32)
    n_used = (pad_end[-1] // tm).astype(jnp.int32).reshape(1)
    return dest.astype(jnp.int32), tile_expert, n_used


def _combine_kernel(x_ref, y0_ref, y1_ref, gw_ref, mod_ref, o_ref):
    gw = gw_ref[...]
    moe = gw[:, 0:1] * y0_ref[...] + gw[:, 1:2] * y1_ref[...]
    o_ref[...] = x_ref[...] + mod_ref[5:6, :] * moe


def _combine(x2, y_pairs, gate_w, modl, seq):
    t, d = x2.shape
    tm = min(seq, 512)
    per_batch = seq // tm
    nblk = t // tm
    return pl.pallas_call(
        _combine_kernel,
        grid=(nblk,),
        in_specs=[pl.BlockSpec((tm, d), lambda i: (i, 0)),
                  pl.BlockSpec((tm, d), lambda i: (i, 0)),
                  pl.BlockSpec((tm, d), lambda i: (i + nblk, 0)),
                  pl.BlockSpec((tm, TOP_K), lambda i: (i, 0)),
                  pl.BlockSpec((None, 6, d), lambda i: (i // per_batch, 0, 0))],
        out_specs=pl.BlockSpec((tm, d), lambda i: (i, 0)),
        out_shape=jax.ShapeDtypeStruct((t, d), F32),
        compiler_params=_cparams(("parallel",), 40),
    )(x2, y_pairs, y_pairs, gate_w, modl)


def _final_norm_kernel(x_ref, w_ref, o_ref):
    x = x_ref[...]
    o_ref[...] = x * _rms_scale(x) * w_ref[...]


def _final_norm(x2, w):
    t, d = x2.shape
    tm = min(t, 512)
    return pl.pallas_call(
        _final_norm_kernel,
        grid=(t // tm,),
        in_specs=[pl.BlockSpec((tm, d), lambda i: (i, 0)), pl.BlockSpec((1, d), lambda i: (0, 0))],
        out_specs=pl.BlockSpec((tm, d), lambda i: (i, 0)),
        out_shape=jax.ShapeDtypeStruct((t, d), F32),
        compiler_params=_cparams(("parallel",), 32),
    )(x2, w)


def kernel(x, c, w_ada, b_ada, norm1_w, norm2_w, w_in, s5_a_re, s5_a_im, s5_log_dt, s5_b_re, s5_b_im, s5_c_re, s5_c_im, s5_d, s5_w_glu, s5_b_glu, gdn_conv_w, gdn_a_log, gdn_dt_bias, gdn_norm_w, w_proj_a, w_proj_b, w_out, w_router, router_bias, w_gate, w_up, w_down, final_norm_w):
    batch, seq, d = x.shape
    depth = w_ada.shape[0]
    t = batch * seq
    s5w = s5_d.shape[1]
    gdw = w_proj_b.shape[1]
    heads = gdn_a_log.shape[1]
    assert gdw == heads * HEAD_DIM and s5w == gdw and d == 2 * gdw
    assert seq % GDN_TILE == 0 and seq % S5_CHUNK == 0
    n_levels = int(math.log2(seq // S5_CHUNK))
    assert S5_CHUNK << n_levels == seq and S5_OCT * S5_GROUP == 128

    c_pad = jnp.zeros((8, d), F32).at[:batch].set(c)
    mod = _ada(c_pad, w_ada, b_ada)

    o_u, o_q, o_k, o_v, o_z = 0, s5w, 2 * s5w, 3 * s5w, 4 * s5w
    o_ba = 5 * s5w
    o_ga = o_ba + 2 * heads
    o_gb = o_ga + d
    col_u, col_q, col_k, col_v, col_z = 4, 5, 6, 7, 8

    gdn_masks = _gdn_masks()
    s5_rep = _s5_replicate()
    s5_ops = jax.vmap(functools.partial(_s5_weights, n_levels=n_levels))(
        s5_a_re, s5_a_im, s5_log_dt, s5_b_re, s5_b_im, s5_c_re, s5_c_im)

    wr_t = w_router.T.astype(F32)
    rb = router_bias.reshape(N_EXPERTS, 1).astype(F32)

    tm_e = min(512, t)
    n_tiles = (TOP_K * t) // tm_e + N_EXPERTS
    p_rows = n_tiles * tm_e
    pad_tok = jnp.arange(p_rows, dtype=jnp.int32) % t

    w_big = jnp.concatenate([w_in[:, :, o_ga:o_ga + d], w_in[:, :, o_gb:o_gb + d], w_in[:, :, o_u:o_ba]],
                            axis=2).astype(BF16)
    w_small = jnp.pad(w_in[:, :, o_ba:o_ga], ((0, 0), (0, 0), (0, 128 - 2 * heads))).astype(BF16)
    w_gate_b = w_gate.astype(BF16)
    w_up_b = w_up.astype(BF16)
    w_down_b = w_down.astype(BF16)

    x2 = x.reshape(t, d)
    for l in range(depth):
        modl = mod[l, :batch].reshape(batch, 6, d)
        proj, small = _inproj(x2, modl, norm1_w[l].reshape(1, d), w_big, w_small, l, seq)

        y_lin = _s5_core(proj, col_u * s5w // 128, *s5_ops, s5_rep, l, batch, seq)
        y_a = _s5_post(y_lin, proj, col_u, s5_d[l].reshape(1, s5w), s5_w_glu[l].astype(BF16),
                       s5_b_glu[l].reshape(1, s5w))

        head_params = jnp.zeros((8, 128), F32)
        head_params = head_params.at[0, heads:2 * heads].set(-jnp.exp(gdn_a_log[l]))
        head_params = head_params.at[1, heads:2 * heads].set(gdn_dt_bias[l])
        y_b = _gdn(proj, small, gdn_conv_w[l], head_params, gdn_norm_w[l].reshape(1, HEAD_DIM), gdn_masks,
                   (col_q, col_k, col_v, col_z), batch, seq, heads)

        x2 = _merge(y_a, y_b, proj, x2, modl, w_proj_a[l].astype(BF16), w_proj_b[l].astype(BF16),
                    w_out[l].astype(BF16), seq, 0, 1)

        h2, route = _router(x2, modl, norm2_w[l].reshape(1, d), wr_t, rb, seq)
        expert_idx = route[0:TOP_K].T.astype(jnp.int32)
        gate_w = route[TOP_K:2 * TOP_K].T
        dest, tile_expert, n_used = _dispatch_plan(expert_idx, tm_e, n_tiles)
        src_tok = pad_tok.at[dest].set(jnp.arange(TOP_K * t, dtype=jnp.int32) // TOP_K)
        xs = jnp.take(h2, src_tok, axis=0, mode='clip')
        ys = _moe_experts(tile_expert, n_used, xs, w_gate_b, w_up_b, w_down_b, l, tm_e)
        y_pairs = jnp.take(ys, dest.reshape(t, TOP_K).T.reshape(-1), axis=0, mode='clip')
        x2 = _combine(x2, y_pairs, gate_w, modl, seq)

    return _final_norm(x2, final_norm_w.reshape(1, d)).reshape(batch, seq, d)
```

```python
import functools
import math

import jax
import jax.numpy as jnp
from jax import lax
from jax.experimental import pallas as pl
from jax.experimental.pallas import tpu as pltpu

F32 = jnp.float32
BF16 = jnp.bfloat16
HIGHEST = lax.Precision.HIGHEST

RMS_EPS = 1e-6
S5_GROUP = 16
S5_STATE = 64
S5_CHUNK = 16
S5_OCT = 8
HEAD_DIM = 128
GDN_CHUNK = 64
GDN_TILE = 256
GDN_HEADS_PER_TRIP = 4
CONV_K = 4
N_EXPERTS = 16
N_GROUPS = 4
EXPERTS_PER_GROUP = N_EXPERTS // N_GROUPS
TOP_K = 2
MIB = 1024 * 1024


def _cparams(semantics, vmem_mib):
    return pltpu.CompilerParams(dimension_semantics=semantics, vmem_limit_bytes=vmem_mib * MIB)


def _silu(x):
    return x * jax.nn.sigmoid(x)


def _softplus(x):
    return jnp.maximum(x, 0.0) + jnp.log1p(jnp.exp(-jnp.abs(x)))


def _rms_scale(x):
    return lax.rsqrt(jnp.mean(x * x, axis=-1, keepdims=True) + RMS_EPS)


def _nt_dot(a, b):
    return lax.dot_general(a, b, (((1,), (1,)), ((), ())), preferred_element_type=F32)


def _tn_dot(a, b):
    return lax.dot_general(a, b, (((0,), (0,)), ((), ())), preferred_element_type=F32)


def _ada_kernel(c_ref, w_ref, b_ref, o_ref):
    ca = _silu(c_ref[...]).astype(BF16)
    o_ref[...] = jnp.dot(ca, w_ref[...].astype(BF16), preferred_element_type=F32) + b_ref[...]


def _ada(c_pad, w_ada, b_ada):
    depth, d, n = w_ada.shape
    rows = c_pad.shape[0]
    tn = min(n, 1024)
    return pl.pallas_call(
        _ada_kernel,
        grid=(depth, n // tn),
        in_specs=[pl.BlockSpec((rows, d), lambda l, j: (0, 0)),
                  pl.BlockSpec((None, d, tn), lambda l, j: (l, 0, j)),
                  pl.BlockSpec((None, 1, tn), lambda l, j: (l, 0, j))],
        out_specs=pl.BlockSpec((None, rows, tn), lambda l, j: (l, 0, j)),
        out_shape=jax.ShapeDtypeStruct((depth, rows, n), F32),
        compiler_params=_cparams(("parallel", "parallel"), 40),
    )(c_pad, w_ada, b_ada.reshape(depth, 1, n))


def _inproj_kernel(x_ref, mod_ref, nw_ref, w_ref, wsm_ref, o_ref, osm_ref, h_scr):
    @pl.when(pl.program_id(1) == 0)
    def _():
        x = x_ref[...]
        h = x * _rms_scale(x) * nw_ref[...] * (1.0 + mod_ref[1:2, :]) + mod_ref[0:1, :]
        hb = h.astype(BF16)
        h_scr[...] = hb
        osm_ref[...] = jnp.dot(hb, wsm_ref[...], preferred_element_type=F32)

    o_ref[...] = jnp.dot(h_scr[...], w_ref[...], preferred_element_type=F32)


def _inproj(x2, modl, nw, w_big, w_small, layer, seq):
    t, d = x2.shape
    n = w_big.shape[-1]
    tm = min(seq, 1024)
    tn = 1024
    per_batch = seq // tm
    return pl.pallas_call(
        _inproj_kernel,
        grid=(t // tm, n // tn),
        in_specs=[pl.BlockSpec((tm, d), lambda i, j: (i, 0)),
                  pl.BlockSpec((None, 6, d), lambda i, j: (i // per_batch, 0, 0)),
                  pl.BlockSpec((1, d), lambda i, j: (0, 0)),
                  pl.BlockSpec((None, d, tn), lambda i, j: (layer, 0, j)),
                  pl.BlockSpec((None, d, 128), lambda i, j: (layer, 0, 0))],
        out_specs=[pl.BlockSpec((tm, tn), lambda i, j: (i, j)),
                   pl.BlockSpec((tm, 128), lambda i, j: (i, 0))],
        out_shape=[jax.ShapeDtypeStruct((t, n), F32), jax.ShapeDtypeStruct((t, 128), F32)],
        scratch_shapes=[pltpu.VMEM((tm, d), BF16)],
        compiler_params=_cparams(("parallel", "arbitrary"), 48),
    )(x2, modl, nw, w_big, w_small)


def _s5_weights(a_re, a_im, log_dt, b_re, b_im, c_re, c_im, n_levels):
    tc = S5_CHUNK
    groups, p = a_re.shape
    n = b_re.shape[-1]
    oc = S5_OCT
    octs = groups // oc
    lam = lax.complex(a_re, a_im)
    dt = jnp.exp(log_dt)[:, None]
    log_a = lam * dt
    a_bar = jnp.exp(log_a)
    b_bar = ((a_bar - 1.0) / lam)[..., None] * lax.complex(b_re, b_im)
    c_mat = lax.complex(c_re, c_im)
    apow = jnp.exp(log_a[:, :, None] * jnp.arange(tc + 1, dtype=F32)[None, None, :])
    lane = jnp.arange(tc * n)
    e_t = (jnp.arange(tc)[:, None] == (lane // n)[None, :]).astype(F32)
    e_n = (jnp.arange(n)[:, None] == (lane % n)[None, :]).astype(F32)

    def expand(z, e):
        f = lambda r: jnp.einsum('gpk,kj->gpj', r, e, precision=HIGHEST)
        return lax.complex(f(jnp.real(z)), f(jnp.imag(z)))

    def re_negim_rows(z):
        return jnp.concatenate([jnp.real(z), -jnp.imag(z)], axis=1)

    c_rep = expand(jnp.transpose(c_mat, (0, 2, 1)), e_n)
    lag_c = re_negim_rows(expand(apow[:, :, :tc], e_t) * c_rep)
    hoc = re_negim_rows(expand(apow[:, :, 1:], e_t) * c_rep)
    b_t = jnp.transpose(b_bar, (0, 2, 1))
    b_cat = jnp.concatenate([jnp.real(b_t), jnp.imag(b_t)], axis=-1)
    kern = jnp.einsum('gnk,gkj->gnj', b_cat, lag_c, precision=HIGHEST)
    tcomp = jnp.stack([jnp.pad(kern[:, :, :(tc - s) * n], ((0, 0), (0, 0), (s * n, 0))) for s in range(tc)],
                      axis=1)
    tcomp = jnp.transpose(tcomp.reshape(octs, oc, tc, n, tc * n), (0, 2, 1, 3, 4)).reshape(octs, tc * oc * n, tc * n)
    ap_rev = jnp.transpose(apow[:, :, tc - 1 - jnp.arange(tc)], (0, 2, 1))
    hin_c = ap_rev[:, :, None, :] * b_t[:, None, :, :]
    hc = jnp.concatenate([jnp.real(hin_c), jnp.imag(hin_c)], axis=-1)
    hc = jnp.transpose(hc.reshape(octs, oc, tc, n, 2 * p), (0, 2, 1, 3, 4)).reshape(octs, tc * oc * n, 2 * p)
    hoc = hoc.reshape(octs, oc * 2 * p, tc * n)
    steps = (tc * (2.0 ** jnp.arange(n_levels, dtype=F32)))
    amul = jnp.exp(log_a[:, None, :] * steps[None, :, None])
    a1 = jnp.concatenate([jnp.real(amul), jnp.real(amul)], axis=-1)
    a2 = jnp.concatenate([-jnp.imag(amul), jnp.imag(amul)], axis=-1)

    def lanes_by_group(a):
        return jnp.transpose(a.reshape(octs, oc, n_levels, 2 * p), (0, 2, 1, 3)).reshape(octs, n_levels, oc * 2 * p)

    return (tcomp.astype(BF16), hc.astype(BF16), hoc.astype(BF16),
            lanes_by_group(a1).astype(F32), lanes_by_group(a2).astype(F32))


def _s5_replicate():
    r = lax.broadcasted_iota(jnp.int32, (S5_CHUNK * S5_GROUP, S5_CHUNK * 128), 0)
    c = lax.broadcasted_iota(jnp.int32, (S5_CHUNK * S5_GROUP, S5_CHUNK * 128), 1)
    return ((r // S5_GROUP == c // 128) & (r % S5_GROUP == c % S5_GROUP)).astype(BF16)


def _s5_core_kernel(u_ref, tc_ref, hc_ref, hoc_ref, a1_ref, a2_ref, rep_ref, y_ref, toep_scr, hin_scr, hout_scr,
                    *, n_levels):
    tc = S5_CHUNK
    lanes = u_ref.shape[1]
    state_lanes = hc_ref.shape[1]

    @pl.when(pl.program_id(1) == 0)
    def _():
        kd = toep_scr.shape[1]
        rb = 256
        col_g = (lax.broadcasted_iota(jnp.int32, (rb, kd), 1) // S5_GROUP) % S5_OCT
        row_i = lax.broadcasted_iota(jnp.int32, (rb, kd), 0)
        for i in range(toep_scr.shape[0] // rb):
            full = jnp.dot(tc_ref[i * rb:(i + 1) * rb, :], rep_ref[...], preferred_element_type=F32)
            row_g = ((row_i + i * rb) // S5_GROUP) % S5_OCT
            toep_scr[i * rb:(i + 1) * rb, :] = jnp.where(row_g == col_g, full, 0.0).astype(BF16)
        for i in range(hout_scr.shape[0] // rb):
            full = jnp.dot(hoc_ref[i * rb:(i + 1) * rb, :], rep_ref[...], preferred_element_type=F32)
            row_g = (row_i + i * rb) // state_lanes
            hout_scr[i * rb:(i + 1) * rb, :] = jnp.where(row_g == col_g, full, 0.0).astype(BF16)
        hrow_g = (lax.broadcasted_iota(jnp.int32, hc_ref.shape, 0) // S5_GROUP) % S5_OCT
        for g in range(S5_OCT):
            hin_scr[:, g * state_lanes:(g + 1) * state_lanes] = jnp.where(hrow_g == g, hc_ref[...], 0.0).astype(BF16)

    rows = u_ref.shape[0] // tc
    x = jnp.concatenate([u_ref[pl.ds(t, rows, stride=tc), :] for t in range(tc)], axis=1).astype(BF16)
    s_loc = jnp.dot(x, hin_scr[...], preferred_element_type=F32)
    pos = lax.broadcasted_iota(jnp.int32, (rows, state_lanes), 0)
    s_in = []
    for g in range(S5_OCT):
        ln = slice(g * state_lanes, (g + 1) * state_lanes)
        s = s_loc[:, ln]
        for k in range(n_levels):
            d = 1 << k
            prev = jnp.where(pos >= d, pltpu.roll(s, d, axis=0), 0.0)
            s = s + a1_ref[k:k + 1, ln] * prev + a2_ref[k:k + 1, ln] * pltpu.roll(prev, state_lanes // 2, axis=1)
        s_in.append(jnp.where(pos >= 1, pltpu.roll(s, 1, axis=0), 0.0).astype(BF16))
    y = (jnp.dot(x, toep_scr[...], preferred_element_type=F32)
         + jnp.dot(jnp.concatenate(s_in, axis=1), hout_scr[...], preferred_element_type=F32))
    for t in range(tc):
        y_ref[pl.ds(t, rows, stride=tc), :] = y[:, t * lanes:(t + 1) * lanes]


def _s5_core(proj, u_lane_block, tcomp, hc, hoc, a1, a2, rep, layer, batch, seq):
    t = proj.shape[0]
    _, octs, kdim, tn = tcomp.shape
    sl = hc.shape[-1]
    sw = hoc.shape[2]
    n_levels = a1.shape[2]
    lanes = kdim // S5_CHUNK
    kern = functools.partial(_s5_core_kernel, n_levels=n_levels)
    return pl.pallas_call(
        kern,
        grid=(octs, batch),
        in_specs=[pl.BlockSpec((seq, lanes), lambda o, b: (b, u_lane_block + o)),
                  pl.BlockSpec((None, None, kdim, tn), lambda o, b: (layer, o, 0, 0)),
                  pl.BlockSpec((None, None, kdim, sl), lambda o, b: (layer, o, 0, 0)),
                  pl.BlockSpec((None, None, sw, tn), lambda o, b: (layer, o, 0, 0)),
                  pl.BlockSpec((None, None, n_levels, sw), lambda o, b: (layer, o, 0, 0)),
                  pl.BlockSpec((None, None, n_levels, sw), lambda o, b: (layer, o, 0, 0)),
                  pl.BlockSpec((tn, kdim), lambda o, b: (0, 0))],
        out_specs=pl.BlockSpec((seq, lanes), lambda o, b: (b, o)),
        out_shape=jax.ShapeDtypeStruct((t, octs * lanes), F32),
        scratch_shapes=[pltpu.VMEM((kdim, kdim), BF16), pltpu.VMEM((kdim, sw), BF16), pltpu.VMEM((sw, kdim), BF16)],
        compiler_params=_cparams(("parallel", "arbitrary"), 48),
    )(proj, tcomp, hc, hoc, a1, a2, rep)


def _s5_post_kernel(y_ref, u_ref, d_ref, w_ref, b_ref, o_ref):
    y = jax.nn.gelu(y_ref[...] + d_ref[...] * u_ref[...])
    gate = jnp.dot(y.astype(BF16), w_ref[...], preferred_element_type=F32) + b_ref[...]
    o_ref[...] = (y * jax.nn.sigmoid(gate)).astype(o_ref.dtype)


def _s5_post(y_lin, proj, u_col, d_skip, w_glu, b_glu):
    t, w = y_lin.shape
    tm = min(t, 512)
    return pl.pallas_call(
        _s5_post_kernel,
        grid=(t // tm,),
        in_specs=[pl.BlockSpec((tm, w), lambda i: (i, 0)),
                  pl.BlockSpec((tm, w), lambda i: (i, u_col)),
                  pl.BlockSpec((1, w), lambda i: (0, 0)),
                  pl.BlockSpec((w, w), lambda i: (0, 0)),
                  pl.BlockSpec((1, w), lambda i: (0, 0))],
        out_specs=pl.BlockSpec((tm, w), lambda i: (i, 0)),
        out_shape=jax.ShapeDtypeStruct((t, w), BF16),
        compiler_params=_cparams(("parallel",), 32),
    )(y_lin, proj, d_skip, w_glu, b_glu)


def _gdn_kernel(q_ref, k_ref, v_ref, qh_ref, kh_ref, vh_ref, z_ref, sm_ref, cq_ref, ck_ref, cv_ref,
                hp_ref, nw_ref, tril_ref, cmask_ref, lmask_ref, o_ref, betab, gcb, grow, st_scr, *, heads):
    tt = GDN_TILE
    hd = HEAD_DIM
    nck = tt // GDN_CHUNK
    first = pl.program_id(1) == 0

    @pl.when(first)
    def _():
        st_scr[...] = jnp.zeros_like(st_scr)

    sm = sm_ref[...]
    hp = hp_ref[...]
    beta_all = jax.nn.sigmoid(sm)
    g_all = hp[0:1, :] * _softplus(sm + hp[1:2, :])
    gc_all = jnp.dot(tril_ref[...], g_all, precision=HIGHEST, preferred_element_type=F32)
    gc_t = gc_all.T
    for h in range(heads):
        betab[h] = jnp.broadcast_to(beta_all[:, h:h + 1], (tt, hd))
        gcb[h] = jnp.broadcast_to(gc_all[:, heads + h:heads + h + 1], (tt, hd))
        grow[h:h + 1, :] = gc_t[heads + h:heads + h + 1, :]

    row8 = lax.broadcasted_iota(jnp.int32, (8, hd), 0)
    eye = (lax.broadcasted_iota(jnp.int32, (tt, tt), 0) == lax.broadcasted_iota(jnp.int32, (tt, tt), 1)).astype(F32)
    n_lv = lmask_ref.shape[0]

    def conv_silu(x_ref, halo_ref, cw_ref, hs):
        x = x_ref[:, pl.ds(hs, hd)]
        halo = jnp.where(first, 0.0, halo_ref[:, pl.ds(hs, hd)])
        w = cw_ref[:, pl.ds(hs, hd)]
        acc = x * w[CONV_K - 1:CONV_K, :]
        for s in range(1, CONV_K):
            rolled = pltpu.roll(x, s, axis=0)
            head8 = jnp.where(row8 < s, pltpu.roll(halo, s, axis=0), rolled[:8])
            shifted = jnp.concatenate([head8, rolled[8:]], axis=0)
            acc = acc + shifted * w[CONV_K - 1 - s:CONV_K - s, :]
        return _silu(acc)

    def heads_lockstep(hs_idx):
        each = lambda f, *ls: [f(*a) for a in zip(*ls)]
        hs = [pl.multiple_of(h * hd, hd) for h in hs_idx]
        q = [conv_silu(q_ref, qh_ref, cq_ref, o) for o in hs]
        k = [conv_silu(k_ref, kh_ref, ck_ref, o) for o in hs]
        v = [conv_silu(v_ref, vh_ref, cv_ref, o) for o in hs]
        qn = each(lambda a: a * (lax.rsqrt(jnp.sum(a * a, axis=-1, keepdims=True) + RMS_EPS) * (hd ** -0.5)), q)
        kn = each(lambda a: a * lax.rsqrt(jnp.sum(a * a, axis=-1, keepdims=True) + RMS_EPS), k)
        bb = [betab[h] for h in hs_idx]
        gc = [gcb[h] for h in hs_idx]
        gr = [grow[pl.ds(h, 1), :] for h in hs_idx]
        decay = each(lambda c, r: jnp.exp(jnp.minimum(jnp.concatenate([c, c], axis=1) - r, 0.0)) * cmask_ref[0],
                     gc, gr)
        kb = each(lambda a, b: a * b, kn, bb)
        knb = each(lambda a: a.astype(BF16), kn)
        lmat = each(lambda a, b, dc: (_nt_dot(a.astype(BF16), b) * (dc * cmask_ref[1])).astype(BF16),
                    kb, knb, decay)
        tinv = each(lambda m: eye - (m * lmask_ref[0]).astype(F32), lmat)
        for lv in range(1, n_lv):
            d = 1 << lv
            nb = tt // d
            tb = each(lambda a: a.astype(BF16), tinv)
            bd = each(lambda m: m * lmask_ref[lv], lmat)
            if d % 8 == 0:
                odd = each(lambda a: jnp.concatenate([a[i * d:(i + 1) * d] for i in range(1, nb, 2)], axis=0), tinv)
                tl = each(lambda a, b: jnp.dot(a.astype(BF16), b, preferred_element_type=F32).astype(BF16), odd, bd)
                upd = each(lambda a, b, c: a - jnp.dot(b, c, preferred_element_type=F32), odd, tl, tb)
                tinv = each(lambda a, u: jnp.concatenate(
                    [a[i * d:(i + 1) * d] if i % 2 == 0 else u[(i // 2) * d:(i // 2 + 1) * d] for i in range(nb)],
                    axis=0), tinv, upd)
            else:
                tl = each(lambda a, b: jnp.dot(a, b, preferred_element_type=F32).astype(BF16), tb, bd)
                tinv = each(lambda a, b, c: a - jnp.dot(b, c, preferred_element_type=F32), tinv, tl, tb)
        eg = each(jnp.exp, gc)
        rhs = each(lambda a, e, b, c: jnp.concatenate([a * e, b * c], axis=1).astype(BF16), kb, eg, v, bb)
        wu = each(lambda a, b: jnp.dot(a.astype(BF16), b, preferred_element_type=F32).astype(BF16), tinv, rhs)
        attn = each(lambda a, b, dc: (_nt_dot(a.astype(BF16), b) * dc).astype(BF16), qn, knb, decay)
        awu = each(lambda a, b: jnp.dot(a, b, preferred_element_type=F32), attn, wu)
        qp = each(lambda a, e, b: (a * e - b[:, :hd]).astype(BF16), qn, eg, awu)
        s = [st_scr[h] for h in hs_idx]
        outs = [[] for _ in hs_idx]
        for c in range(nck):
            r0 = c * GDN_CHUNK
            r1 = r0 + GDN_CHUNK
            gl = each(lambda a: a[r1 - 1:r1, :], gc)
            kd = each(lambda a, l, g: (a[r0:r1] * jnp.exp(l - g[r0:r1])).astype(BF16), kn, gl, gc)
            m = each(lambda a, b: _tn_dot(a, b[r0:r1]), kd, wu)
            sb = each(lambda a: a.astype(BF16), s)
            for i, (a, b, y) in enumerate(zip(qp, sb, awu)):
                outs[i].append(jnp.dot(a[r0:r1], b, preferred_element_type=F32) + y[r0:r1, hd:])
            s = each(lambda a, l, mm, b: a * jnp.exp(l) - jnp.dot(mm[:, :hd].astype(BF16), b,
                                                                  preferred_element_type=F32) + mm[:, hd:],
                     s, gl, m, sb)
        for i, h in enumerate(hs_idx):
            st_scr[h] = s[i]
            o = jnp.concatenate(outs[i], axis=0)
            z = z_ref[:, pl.ds(hs[i], hd)]
            o_ref[:, pl.ds(hs[i], hd)] = (o * _rms_scale(o) * nw_ref[...] * _silu(z)).astype(o_ref.dtype)

    def trip(j, carry):
        heads_lockstep([GDN_HEADS_PER_TRIP * j + i for i in range(GDN_HEADS_PER_TRIP)])
        return carry

    lax.fori_loop(0, heads // GDN_HEADS_PER_TRIP, trip, 0)


def _gdn_masks():
    tt = GDN_TILE
    ri = lax.broadcasted_iota(jnp.int32, (tt, tt), 0)
    ci = lax.broadcasted_iota(jnp.int32, (tt, tt), 1)
    same = (ri // GDN_CHUNK) == (ci // GDN_CHUNK)
    causal = same & (ci <= ri)
    tril = causal.astype(F32)
    cmask = jnp.stack([tril, (ri != ci).astype(F32)])
    levels = []
    d = 1
    while d < GDN_CHUNK:
        levels.append((((ri // d) % 2) == 1) & ((ci // d) == (ri // d) - 1))
        d *= 2
    return tril, cmask, jnp.stack(levels).astype(BF16)


def _gdn(proj, small, conv_w, head_params, norm_w, masks, cols, batch, seq, heads):
    t = proj.shape[0]
    tt = GDN_TILE
    width = heads * HEAD_DIM
    tiles = seq // tt
    cq, ck, cv, cz = cols
    tril, cmask, lmask = masks

    def cur(col):
        return pl.BlockSpec((tt, width), lambda b, i: (b * tiles + i, col))

    def halo(col):
        return pl.BlockSpec((8, width), lambda b, i: (jnp.maximum((b * tiles + i) * (tt // 8) - 1, 0), col))

    def cw(col):
        return pl.BlockSpec((CONV_K, width), lambda b, i: (0, col))

    kern = functools.partial(_gdn_kernel, heads=heads)
    return pl.pallas_call(
        kern,
        grid=(batch, tiles),
        in_specs=[cur(cq), cur(ck), cur(cv), halo(cq), halo(ck), halo(cv), cur(cz),
                  pl.BlockSpec((tt, 128), lambda b, i: (b * tiles + i, 0)),
                  cw(0), cw(1), cw(2),
                  pl.BlockSpec((8, 128), lambda b, i: (0, 0)),
                  pl.BlockSpec((1, HEAD_DIM), lambda b, i: (0, 0)),
                  pl.BlockSpec((tt, tt), lambda b, i: (0, 0)),
                  pl.BlockSpec(cmask.shape, lambda b, i: (0, 0, 0)),
                  pl.BlockSpec(lmask.shape, lambda b, i: (0, 0, 0))],
        out_specs=pl.BlockSpec((tt, width), lambda b, i: (b * tiles + i, 0)),
        out_shape=jax.ShapeDtypeStruct((t, width), BF16),
        scratch_shapes=[pltpu.VMEM((heads, tt, HEAD_DIM), F32),
                        pltpu.VMEM((heads, tt, HEAD_DIM), F32),
                        pltpu.VMEM((8, tt), F32),
                        pltpu.VMEM((heads, HEAD_DIM, HEAD_DIM), F32)],
        compiler_params=_cparams(("parallel", "arbitrary"), 40),
    )(proj, proj, proj, proj, proj, proj, proj, small, conv_w, conv_w, conv_w, head_params, norm_w,
      tril, cmask, lmask)


def _route(h, wr_t, rb):
    logits = lax.dot_general(wr_t, h, (((1,), (1,)), ((), ())), precision=HIGHEST,
                             preferred_element_type=F32)
    scores = jax.nn.sigmoid(logits)
    sel = scores + rb
    sel_rows = [sel[e:e + 1, :] for e in range(N_EXPERTS)]
    score_rows = [scores[e:e + 1, :] for e in range(N_EXPERTS)]
    epg = EXPERTS_PER_GROUP
    best_score = None
    best_group = None
    for g in range(N_GROUPS):
        rows = sel_rows[g * epg:(g + 1) * epg]
        gs = None
        for a in range(epg):
            for b in range(a + 1, epg):
                pair = rows[a] + rows[b]
                gs = pair if gs is None else jnp.maximum(gs, pair)
        if g == 0:
            best_score, best_group = gs, jnp.zeros_like(gs)
        else:
            better = gs > best_score
            best_group = jnp.where(better, float(g), best_group)
            best_score = jnp.where(better, gs, best_score)
    in_sel = []
    in_score = []
    for j in range(epg):
        a = sel_rows[j]
        b = score_rows[j]
        for g in range(1, N_GROUPS):
            pick = best_group == float(g)
            a = jnp.where(pick, sel_rows[g * epg + j], a)
            b = jnp.where(pick, score_rows[g * epg + j], b)
        in_sel.append(a)
        in_score.append(b)
    i1, m1, w1 = jnp.zeros_like(in_sel[0]), in_sel[0], in_score[0]
    for j in range(1, epg):
        better = in_sel[j] > m1
        i1 = jnp.where(better, float(j), i1)
        m1 = jnp.where(better, in_sel[j], m1)
        w1 = jnp.where(better, in_score[j], w1)
    i2 = m2 = w2 = None
    for j in range(epg):
        cand = jnp.where(i1 == float(j), -jnp.inf, in_sel[j])
        if j == 0:
            i2, m2, w2 = jnp.zeros_like(cand), cand, in_score[0]
        else:
            better = cand > m2
            i2 = jnp.where(better, float(j), i2)
            m2 = jnp.where(better, cand, m2)
            w2 = jnp.where(better, in_score[j], w2)
    total = w1 + w2
    zero = jnp.zeros_like(w1)
    return jnp.concatenate([best_group * epg + i1, best_group * epg + i2, w1 / total, w2 / total,
                            zero, zero, zero, zero], axis=0)


def _merge_kernel(ya_ref, yb_ref, ga_ref, gb_ref, x_ref, mod_ref, wa_ref, wb_ref, wo_ref, nw_ref, wrt_ref, rb_ref,
                  o_ref, h_ref, r_ref):
    pa = jnp.dot(ya_ref[...], wa_ref[...], preferred_element_type=F32)
    pb = jnp.dot(yb_ref[...], wb_ref[...], preferred_element_type=F32)
    merged = jax.nn.sigmoid(ga_ref[...]) * pa + jax.nn.sigmoid(gb_ref[...]) * pb
    out = jnp.dot(merged.astype(BF16), wo_ref[...], preferred_element_type=F32)
    xn = x_ref[...] + mod_ref[2:3, :] * out
    o_ref[...] = xn
    h = xn * _rms_scale(xn) * nw_ref[...] * (1.0 + mod_ref[4:5, :]) + mod_ref[3:4, :]
    h_ref[...] = h
    r_ref[...] = _route(h, wrt_ref[...], rb_ref[...])


def _merge(ya, yb, proj, x2, modl, wa, wb, wo, nw2, wr_t, rb, seq, ga_col, gb_col):
    t, d = x2.shape
    w = ya.shape[1]
    tm = min(seq, 256)
    per_batch = seq // tm
    const = dict(pipeline_mode=pl.Buffered(1))
    return pl.pallas_call(
        _merge_kernel,
        grid=(t // tm,),
        in_specs=[pl.BlockSpec((tm, w), lambda i: (i, 0)),
                  pl.BlockSpec((tm, w), lambda i: (i, 0)),
                  pl.BlockSpec((tm, d), lambda i: (i, ga_col)),
                  pl.BlockSpec((tm, d), lambda i: (i, gb_col)),
                  pl.BlockSpec((tm, d), lambda i: (i, 0)),
                  pl.BlockSpec((None, 6, d), lambda i: (i // per_batch, 0, 0)),
                  pl.BlockSpec((w, d), lambda i: (0, 0), **const),
                  pl.BlockSpec((w, d), lambda i: (0, 0), **const),
                  pl.BlockSpec((d, d), lambda i: (0, 0), **const),
                  pl.BlockSpec((1, d), lambda i: (0, 0)),
                  pl.BlockSpec((N_EXPERTS, d), lambda i: (0, 0)),
                  pl.BlockSpec((N_EXPERTS, 1), lambda i: (0, 0))],
        out_specs=[pl.BlockSpec((tm, d), lambda i: (i, 0)),
                   pl.BlockSpec((tm, d), lambda i: (i, 0)),
                   pl.BlockSpec((8, tm), lambda i: (0, i))],
        out_shape=[jax.ShapeDtypeStruct((t, d), F32), jax.ShapeDtypeStruct((t, d), F32),
                   jax.ShapeDtypeStruct((8, t), F32)],
        compiler_params=_cparams(("parallel",), 52),
    )(ya, yb, proj, proj, x2, modl, wa, wb, wo, nw2, wr_t, rb)


def _moe_kernel(te_ref, nu_ref, x_ref, wg_ref, wu_ref, wd_ref, o_ref):
    used = pl.program_id(0) < nu_ref[0]

    @pl.when(used)
    def _():
        x = x_ref[...].astype(BF16)
        a = jnp.dot(x, wg_ref[...], preferred_element_type=F32)
        b = jnp.dot(x, wu_ref[...], preferred_element_type=F32)
        hid = (_silu(a) * b).astype(BF16)
        o_ref[...] = jnp.dot(hid, wd_ref[...], preferred_element_type=F32)

    @pl.when(jnp.logical_not(used))
    def _():
        o_ref[...] = jnp.zeros_like(o_ref)


def _moe_experts(tile_expert, n_used, xs, wg, wu, wd, layer, tm):
    p, d = xs.shape
    f = wg.shape[-1]
    grid_spec = pltpu.PrefetchScalarGridSpec(
        num_scalar_prefetch=2,
        grid=(p // tm,),
        in_specs=[pl.BlockSpec((tm, d), lambda i, te, nu: (i, 0)),
                  pl.BlockSpec((None, None, d, f), lambda i, te, nu: (layer, te[i], 0, 0)),
                  pl.BlockSpec((None, None, d, f), lambda i, te, nu: (layer, te[i], 0, 0)),
                  pl.BlockSpec((None, None, f, d), lambda i, te, nu: (layer, te[i], 0, 0))],
        out_specs=pl.BlockSpec((tm, d), lambda i, te, nu: (i, 0)),
    )
    return pl.pallas_call(
        _moe_kernel,
        grid_spec=grid_spec,
        out_shape=jax.ShapeDtypeStruct((p, d), F32),
        compiler_params=_cparams(("arbitrary",), 52),
    )(tile_expert, n_used, xs, wg, wu, wd)


def _dispatch_plan(expert_idx, tm, n_tiles):
    flat_e = expert_idx.reshape(-1)
    onehot = (flat_e[:, None] == jnp.arange(N_EXPERTS, dtype=jnp.int32)[None, :]).astype(jnp.int32)
    csum = jnp.cumsum(onehot, axis=0)
    rank = jnp.sum((csum - onehot) * onehot, axis=1)
    sizes = csum[-1]
    padded = ((sizes + tm - 1) // tm) * tm
    pad_end = jnp.cumsum(padded)
    pad_start = pad_end - padded
    dest = pad_start[flat_e] + rank
    tile_start = jnp.arange(n_tiles, dtype=jnp.int32) * tm
    tile_expert = jnp.minimum(jnp.sum((tile_start[:, None] >= pad_end[None, :]).astype(jnp.int32), axis=1),
                              N_EXPERTS - 1).astype(jnp.int32)
    n_used = (pad_end[-1] // tm).astype(jnp.int32).reshape(1)
    return dest.astype(jnp.int32), tile_expert, n_used


def _combine_kernel(x_ref, y0_ref, y1_ref, gw_ref, mod_ref, fw_ref, o_ref, *, final_norm):
    gw = gw_ref[...]
    moe = gw[:, 0:1] * y0_ref[...] + gw[:, 1:2] * y1_ref[...]
    xn = x_ref[...] + mod_ref[5:6, :] * moe
    if final_norm:
        xn = xn * _rms_scale(xn) * fw_ref[...]
    o_ref[...] = xn


def _combine(x2, y_pairs, gate_w, modl, final_w, seq, final_norm):
    t, d = x2.shape
    tm = min(seq, 512)
    per_batch = seq // tm
    nblk = t // tm
    return pl.pallas_call(
        functools.partial(_combine_kernel, final_norm=final_norm),
        grid=(nblk,),
        in_specs=[pl.BlockSpec((tm, d), lambda i: (i, 0)),
                  pl.BlockSpec((tm, d), lambda i: (i, 0)),
                  pl.BlockSpec((tm, d), lambda i: (i + nblk, 0)),
                  pl.BlockSpec((tm, TOP_K), lambda i: (i, 0)),
                  pl.BlockSpec((None, 6, d), lambda i: (i // per_batch, 0, 0)),
                  pl.BlockSpec((1, d), lambda i: (0, 0))],
        out_specs=pl.BlockSpec((tm, d), lambda i: (i, 0)),
        out_shape=jax.ShapeDtypeStruct((t, d), F32),
        compiler_params=_cparams(("parallel",), 40),
    )(x2, y_pairs, y_pairs, gate_w, modl, final_w)


def kernel(x, c, w_ada, b_ada, norm1_w, norm2_w, w_in, s5_a_re, s5_a_im, s5_log_dt, s5_b_re, s5_b_im, s5_c_re, s5_c_im, s5_d, s5_w_glu, s5_b_glu, gdn_conv_w, gdn_a_log, gdn_dt_bias, gdn_norm_w, w_proj_a, w_proj_b, w_out, w_router, router_bias, w_gate, w_up, w_down, final_norm_w):
    batch, seq, d = x.shape
    depth = w_ada.shape[0]
    t = batch * seq
    s5w = s5_d.shape[1]
    gdw = w_proj_b.shape[1]
    heads = gdn_a_log.shape[1]
    assert gdw == heads * HEAD_DIM and s5w == gdw and d == 2 * gdw
    assert seq % GDN_TILE == 0 and seq % S5_CHUNK == 0
    n_levels = int(math.log2(seq // S5_CHUNK))
    assert S5_CHUNK << n_levels == seq and S5_OCT * S5_GROUP == 128

    c_pad = jnp.zeros((8, d), F32).at[:batch].set(c)
    mod = _ada(c_pad, w_ada, b_ada)

    o_u, o_q, o_k, o_v, o_z = 0, s5w, 2 * s5w, 3 * s5w, 4 * s5w
    o_ba = 5 * s5w
    o_ga = o_ba + 2 * heads
    o_gb = o_ga + d
    col_u, col_q, col_k, col_v, col_z = 4, 5, 6, 7, 8

    gdn_masks = _gdn_masks()
    s5_rep = _s5_replicate()
    s5_ops = jax.vmap(functools.partial(_s5_weights, n_levels=n_levels))(
        s5_a_re, s5_a_im, s5_log_dt, s5_b_re, s5_b_im, s5_c_re, s5_c_im)

    wr_t = w_router.T.astype(F32)
    rb = router_bias.reshape(N_EXPERTS, 1).astype(F32)

    tm_e = min(512, t)
    n_tiles = (TOP_K * t) // tm_e + N_EXPERTS
    p_rows = n_tiles * tm_e
    pad_tok = jnp.arange(p_rows, dtype=jnp.int32) % t

    w_big = jnp.concatenate([w_in[:, :, o_ga:o_ga + d], w_in[:, :, o_gb:o_gb + d], w_in[:, :, o_u:o_ba]],
                            axis=2).astype(BF16)
    w_small = jnp.pad(w_in[:, :, o_ba:o_ga], ((0, 0), (0, 0), (0, 128 - 2 * heads))).astype(BF16)
    w_gate_b = w_gate.astype(BF16)
    w_up_b = w_up.astype(BF16)
    w_down_b = w_down.astype(BF16)

    x2 = x.reshape(t, d)
    for l in range(depth):
        modl = mod[l, :batch].reshape(batch, 6, d)
        proj, small = _inproj(x2, modl, norm1_w[l].reshape(1, d), w_big, w_small, l, seq)

        y_lin = _s5_core(proj, col_u * s5w // 128, *s5_ops, s5_rep, l, batch, seq)
        y_a = _s5_post(y_lin, proj, col_u, s5_d[l].reshape(1, s5w), s5_w_glu[l].astype(BF16),
                       s5_b_glu[l].reshape(1, s5w))

        head_params = jnp.zeros((8, 128), F32)
        head_params = head_params.at[0, heads:2 * heads].set(-jnp.exp(gdn_a_log[l]))
        head_params = head_params.at[1, heads:2 * heads].set(gdn_dt_bias[l])
        y_b = _gdn(proj, small, gdn_conv_w[l], head_params, gdn_norm_w[l].reshape(1, HEAD_DIM), gdn_masks,
                   (col_q, col_k, col_v, col_z), batch, seq, heads)

        x2, h2, route = _merge(y_a, y_b, proj, x2, modl, w_proj_a[l].astype(BF16), w_proj_b[l].astype(BF16),
                               w_out[l].astype(BF16), norm2_w[l].reshape(1, d), wr_t, rb, seq, 0, 1)

        expert_idx = route[0:TOP_K].T.astype(jnp.int32)
        gate_w = route[TOP_K:2 * TOP_K].T
        dest, tile_expert, n_used = _dispatch_plan(expert_idx, tm_e, n_tiles)
        src_tok = pad_tok.at[dest].set(jnp.arange(TOP_K * t, dtype=jnp.int32) // TOP_K)
        xs = jnp.take(h2, src_tok, axis=0, mode='clip')
        ys = _moe_experts(tile_expert, n_used, xs, w_gate_b, w_up_b, w_down_b, l, tm_e)
        y_pairs = jnp.take(ys, dest.reshape(t, TOP_K).T.reshape(-1), axis=0, mode='clip')
        x2 = _combine(x2, y_pairs, gate_w, modl, final_norm_w.reshape(1, d), seq, l == depth - 1)

    return x2.reshape(batch, seq, d)
```

```python
import functools
import math

import jax
import jax.numpy as jnp
from jax import lax
from jax.experimental import pallas as pl
from jax.experimental.pallas import tpu as pltpu

F32 = jnp.float32
BF16 = jnp.bfloat16
HIGHEST = lax.Precision.HIGHEST

RMS_EPS = 1e-6
S5_GROUP = 16
S5_STATE = 64
S5_CHUNK = 16
S5_OCT = 8
HEAD_DIM = 128
GDN_CHUNK = 64
GDN_TILE = 256
GDN_HEADS_PER_TRIP = 4
CONV_K = 4
N_EXPERTS = 16
N_GROUPS = 4
EXPERTS_PER_GROUP = N_EXPERTS // N_GROUPS
TOP_K = 2
MIB = 1024 * 1024


def _cparams(semantics, vmem_mib):
    return pltpu.CompilerParams(dimension_semantics=semantics, vmem_limit_bytes=vmem_mib * MIB)


def _silu(x):
    return x * jax.nn.sigmoid(x)


def _softplus(x):
    return jnp.maximum(x, 0.0) + jnp.log1p(jnp.exp(-jnp.abs(x)))


def _rms_scale(x):
    return lax.rsqrt(jnp.mean(x * x, axis=-1, keepdims=True) + RMS_EPS)


def _nt_dot(a, b):
    return lax.dot_general(a, b, (((1,), (1,)), ((), ())), preferred_element_type=F32)


def _tn_dot(a, b):
    return lax.dot_general(a, b, (((0,), (0,)), ((), ())), preferred_element_type=F32)


def _ada_kernel(c_ref, w_ref, b_ref, o_ref):
    ca = _silu(c_ref[...]).astype(BF16)
    o_ref[...] = jnp.dot(ca, w_ref[...].astype(BF16), preferred_element_type=F32) + b_ref[...]


def _ada(c_pad, w_ada, b_ada):
    depth, d, n = w_ada.shape
    rows = c_pad.shape[0]
    tn = min(n, 1024)
    return pl.pallas_call(
        _ada_kernel,
        grid=(depth, n // tn),
        in_specs=[pl.BlockSpec((rows, d), lambda l, j: (0, 0)),
                  pl.BlockSpec((None, d, tn), lambda l, j: (l, 0, j)),
                  pl.BlockSpec((None, 1, tn), lambda l, j: (l, 0, j))],
        out_specs=pl.BlockSpec((None, rows, tn), lambda l, j: (l, 0, j)),
        out_shape=jax.ShapeDtypeStruct((depth, rows, n), F32),
        compiler_params=_cparams(("parallel", "parallel"), 40),
    )(c_pad, w_ada, b_ada.reshape(depth, 1, n))


def _inproj_kernel(x_ref, mod_ref, nw_ref, w_ref, wsm_ref, o_ref, osm_ref, h_scr):
    @pl.when(pl.program_id(1) == 0)
    def _():
        x = x_ref[...]
        h = x * _rms_scale(x) * nw_ref[...] * (1.0 + mod_ref[1:2, :]) + mod_ref[0:1, :]
        hb = h.astype(BF16)
        h_scr[...] = hb
        osm_ref[...] = jnp.dot(hb, wsm_ref[...], preferred_element_type=F32)

    o_ref[...] = jnp.dot(h_scr[...], w_ref[...], preferred_element_type=F32)


def _inproj(x2, modl, nw, w_big, w_small, layer, seq):
    t, d = x2.shape
    n = w_big.shape[-1]
    tm = min(seq, 1024)
    tn = 1024
    per_batch = seq // tm
    return pl.pallas_call(
        _inproj_kernel,
        grid=(t // tm, n // tn),
        in_specs=[pl.BlockSpec((tm, d), lambda i, j: (i, 0)),
                  pl.BlockSpec((None, 6, d), lambda i, j: (i // per_batch, 0, 0)),
                  pl.BlockSpec((1, d), lambda i, j: (0, 0)),
                  pl.BlockSpec((None, d, tn), lambda i, j: (layer, 0, j)),
                  pl.BlockSpec((None, d, 128), lambda i, j: (layer, 0, 0))],
        out_specs=[pl.BlockSpec((tm, tn), lambda i, j: (i, j)),
                   pl.BlockSpec((tm, 128), lambda i, j: (i, 0))],
        out_shape=[jax.ShapeDtypeStruct((t, n), F32), jax.ShapeDtypeStruct((t, 128), F32)],
        scratch_shapes=[pltpu.VMEM((tm, d), BF16)],
        compiler_params=_cparams(("parallel", "arbitrary"), 48),
    )(x2, modl, nw, w_big, w_small)


def _s5_weights(a_re, a_im, log_dt, b_re, b_im, c_re, c_im, n_levels):
    tc = S5_CHUNK
    groups, p = a_re.shape
    n = b_re.shape[-1]
    oc = S5_OCT
    octs = groups // oc
    lam = lax.complex(a_re, a_im)
    dt = jnp.exp(log_dt)[:, None]
    log_a = lam * dt
    a_bar = jnp.exp(log_a)
    b_bar = ((a_bar - 1.0) / lam)[..., None] * lax.complex(b_re, b_im)
    c_mat = lax.complex(c_re, c_im)
    apow = jnp.exp(log_a[:, :, None] * jnp.arange(tc + 1, dtype=F32)[None, None, :])
    lane = jnp.arange(tc * n)
    e_t = (jnp.arange(tc)[:, None] == (lane // n)[None, :]).astype(F32)
    e_n = (jnp.arange(n)[:, None] == (lane % n)[None, :]).astype(F32)

    def expand(z, e):
        f = lambda r: jnp.einsum('gpk,kj->gpj', r, e, precision=HIGHEST)
        return lax.complex(f(jnp.real(z)), f(jnp.imag(z)))

    def re_negim_rows(z):
        return jnp.concatenate([jnp.real(z), -jnp.imag(z)], axis=1)

    c_rep = expand(jnp.transpose(c_mat, (0, 2, 1)), e_n)
    lag_c = re_negim_rows(expand(apow[:, :, :tc], e_t) * c_rep)
    hoc = re_negim_rows(expand(apow[:, :, 1:], e_t) * c_rep)
    b_t = jnp.transpose(b_bar, (0, 2, 1))
    b_cat = jnp.concatenate([jnp.real(b_t), jnp.imag(b_t)], axis=-1)
    kern = jnp.einsum('gnk,gkj->gnj', b_cat, lag_c, precision=HIGHEST)
    tcomp = jnp.stack([jnp.pad(kern[:, :, :(tc - s) * n], ((0, 0), (0, 0), (s * n, 0))) for s in range(tc)],
                      axis=1)
    tcomp = jnp.transpose(tcomp.reshape(octs, oc, tc, n, tc * n), (0, 2, 1, 3, 4)).reshape(octs, tc * oc * n, tc * n)
    ap_rev = jnp.transpose(apow[:, :, tc - 1 - jnp.arange(tc)], (0, 2, 1))
    hin_c = ap_rev[:, :, None, :] * b_t[:, None, :, :]
    hc = jnp.concatenate([jnp.real(hin_c), jnp.imag(hin_c)], axis=-1)
    hc = jnp.transpose(hc.reshape(octs, oc, tc, n, 2 * p), (0, 2, 1, 3, 4)).reshape(octs, tc * oc * n, 2 * p)
    hoc = hoc.reshape(octs, oc * 2 * p, tc * n)
    steps = (tc * (2.0 ** jnp.arange(n_levels, dtype=F32)))
    amul = jnp.exp(log_a[:, None, :] * steps[None, :, None])
    a1 = jnp.concatenate([jnp.real(amul), jnp.real(amul)], axis=-1)
    a2 = jnp.concatenate([-jnp.imag(amul), jnp.imag(amul)], axis=-1)

    def lanes_by_group(a):
        return jnp.transpose(a.reshape(octs, oc, n_levels, 2 * p), (0, 2, 1, 3)).reshape(octs, n_levels, oc * 2 * p)

    return (tcomp.astype(BF16), hc.astype(BF16), hoc.astype(BF16),
            lanes_by_group(a1).astype(F32), lanes_by_group(a2).astype(F32))


def _s5_replicate():
    r = lax.broadcasted_iota(jnp.int32, (S5_CHUNK * S5_GROUP, S5_CHUNK * 128), 0)
    c = lax.broadcasted_iota(jnp.int32, (S5_CHUNK * S5_GROUP, S5_CHUNK * 128), 1)
    return ((r // S5_GROUP == c // 128) & (r % S5_GROUP == c % S5_GROUP)).astype(BF16)


def _s5_core_kernel(u_ref, tc_ref, hc_ref, hoc_ref, a1_ref, a2_ref, rep_ref, y_ref, toep_scr, hin_scr, hout_scr,
                    *, n_levels):
    tc = S5_CHUNK
    lanes = u_ref.shape[1]
    state_lanes = hc_ref.shape[1]

    @pl.when(pl.program_id(1) == 0)
    def _():
        kd = toep_scr.shape[1]
        rb = 256
        col_g = (lax.broadcasted_iota(jnp.int32, (rb, kd), 1) // S5_GROUP) % S5_OCT
        row_i = lax.broadcasted_iota(jnp.int32, (rb, kd), 0)
        for i in range(toep_scr.shape[0] // rb):
            full = jnp.dot(tc_ref[i * rb:(i + 1) * rb, :], rep_ref[...], preferred_element_type=F32)
            row_g = ((row_i + i * rb) // S5_GROUP) % S5_OCT
            toep_scr[i * rb:(i + 1) * rb, :] = jnp.where(row_g == col_g, full, 0.0).astype(BF16)
        for i in range(hout_scr.shape[0] // rb):
            full = jnp.dot(hoc_ref[i * rb:(i + 1) * rb, :], rep_ref[...], preferred_element_type=F32)
            row_g = (row_i + i * rb) // state_lanes
            hout_scr[i * rb:(i + 1) * rb, :] = jnp.where(row_g == col_g, full, 0.0).astype(BF16)
        hrow_g = (lax.broadcasted_iota(jnp.int32, hc_ref.shape, 0) // S5_GROUP) % S5_OCT
        for g in range(S5_OCT):
            hin_scr[:, g * state_lanes:(g + 1) * state_lanes] = jnp.where(hrow_g == g, hc_ref[...], 0.0).astype(BF16)

    rows = u_ref.shape[0] // tc
    x = jnp.concatenate([u_ref[pl.ds(t, rows, stride=tc), :] for t in range(tc)], axis=1).astype(BF16)
    s_loc = jnp.dot(x, hin_scr[...], preferred_element_type=F32)
    pos = lax.broadcasted_iota(jnp.int32, (rows, state_lanes), 0)
    s_in = []
    for g in range(S5_OCT):
        ln = slice(g * state_lanes, (g + 1) * state_lanes)
        s = s_loc[:, ln]
        for k in range(n_levels):
            d = 1 << k
            prev = jnp.where(pos >= d, pltpu.roll(s, d, axis=0), 0.0)
            s = s + a1_ref[k:k + 1, ln] * prev + a2_ref[k:k + 1, ln] * pltpu.roll(prev, state_lanes // 2, axis=1)
        s_in.append(jnp.where(pos >= 1, pltpu.roll(s, 1, axis=0), 0.0).astype(BF16))
    y = (jnp.dot(x, toep_scr[...], preferred_element_type=F32)
         + jnp.dot(jnp.concatenate(s_in, axis=1), hout_scr[...], preferred_element_type=F32))
    for t in range(tc):
        y_ref[pl.ds(t, rows, stride=tc), :] = y[:, t * lanes:(t + 1) * lanes]


def _s5_core(proj, u_lane_block, tcomp, hc, hoc, a1, a2, rep, layer, batch, seq):
    t = proj.shape[0]
    _, octs, kdim, tn = tcomp.shape
    sl = hc.shape[-1]
    sw = hoc.shape[2]
    n_levels = a1.shape[2]
    lanes = kdim // S5_CHUNK
    kern = functools.partial(_s5_core_kernel, n_levels=n_levels)
    return pl.pallas_call(
        kern,
        grid=(octs, batch),
        in_specs=[pl.BlockSpec((seq, lanes), lambda o, b: (b, u_lane_block + o)),
                  pl.BlockSpec((None, None, kdim, tn), lambda o, b: (layer, o, 0, 0)),
                  pl.BlockSpec((None, None, kdim, sl), lambda o, b: (layer, o, 0, 0)),
                  pl.BlockSpec((None, None, sw, tn), lambda o, b: (layer, o, 0, 0)),
                  pl.BlockSpec((None, None, n_levels, sw), lambda o, b: (layer, o, 0, 0)),
                  pl.BlockSpec((None, None, n_levels, sw), lambda o, b: (layer, o, 0, 0)),
                  pl.BlockSpec((tn, kdim), lambda o, b: (0, 0))],
        out_specs=pl.BlockSpec((seq, lanes), lambda o, b: (b, o)),
        out_shape=jax.ShapeDtypeStruct((t, octs * lanes), F32),
        scratch_shapes=[pltpu.VMEM((kdim, kdim), BF16), pltpu.VMEM((kdim, sw), BF16), pltpu.VMEM((sw, kdim), BF16)],
        compiler_params=_cparams(("parallel", "arbitrary"), 48),
    )(proj, tcomp, hc, hoc, a1, a2, rep)


def _s5_post_kernel(y_ref, u_ref, d_ref, w_ref, b_ref, o_ref):
    y = jax.nn.gelu(y_ref[...] + d_ref[...] * u_ref[...])
    gate = jnp.dot(y.astype(BF16), w_ref[...], preferred_element_type=F32) + b_ref[...]
    o_ref[...] = (y * jax.nn.sigmoid(gate)).astype(o_ref.dtype)


def _s5_post(y_lin, proj, u_col, d_skip, w_glu, b_glu):
    t, w = y_lin.shape
    tm = min(t, 512)
    return pl.pallas_call(
        _s5_post_kernel,
        grid=(t // tm,),
        in_specs=[pl.BlockSpec((tm, w), lambda i: (i, 0)),
                  pl.BlockSpec((tm, w), lambda i: (i, u_col)),
                  pl.BlockSpec((1, w), lambda i: (0, 0)),
                  pl.BlockSpec((w, w), lambda i: (0, 0)),
                  pl.BlockSpec((1, w), lambda i: (0, 0))],
        out_specs=pl.BlockSpec((tm, w), lambda i: (i, 0)),
        out_shape=jax.ShapeDtypeStruct((t, w), BF16),
        compiler_params=_cparams(("parallel",), 32),
    )(y_lin, proj, d_skip, w_glu, b_glu)


def _gdn_kernel(q_ref, k_ref, v_ref, qh_ref, kh_ref, vh_ref, z_ref, sm_ref, cq_ref, ck_ref, cv_ref,
                hp_ref, nw_ref, tril_ref, cmask_ref, lmask_ref, o_ref, betab, gcb, grow, st_scr,
                cat_q, cat_k, cat_v, *, heads):
    tt = GDN_TILE
    hd = HEAD_DIM
    nck = tt // GDN_CHUNK
    first = pl.program_id(1) == 0

    @pl.when(first)
    def _():
        st_scr[...] = jnp.zeros_like(st_scr)

    sm = sm_ref[...]
    hp = hp_ref[...]
    beta_all = jax.nn.sigmoid(sm)
    g_all = hp[0:1, :] * _softplus(sm + hp[1:2, :])
    gc_all = jnp.dot(tril_ref[...], g_all, precision=HIGHEST, preferred_element_type=F32)
    gc_t = gc_all.T
    for h in range(heads):
        betab[h] = jnp.broadcast_to(beta_all[:, h:h + 1], (tt, hd))
        gcb[h] = jnp.broadcast_to(gc_all[:, heads + h:heads + h + 1], (tt, hd))
        grow[h:h + 1, :] = gc_t[heads + h:heads + h + 1, :]

    eye = (lax.broadcasted_iota(jnp.int32, (tt, tt), 0) == lax.broadcasted_iota(jnp.int32, (tt, tt), 1)).astype(F32)
    n_lv = lmask_ref.shape[0]

    for x_ref, halo_ref, cat in ((q_ref, qh_ref, cat_q), (k_ref, kh_ref, cat_k), (v_ref, vh_ref, cat_v)):
        cat[0:8, :] = jnp.where(first, 0.0, halo_ref[...])
        cat[8:8 + tt, :] = x_ref[...]

    def conv_silu(cat, cw_ref, hs):
        w = cw_ref[:, pl.ds(hs, hd)]
        acc = cat[pl.ds(8, tt), pl.ds(hs, hd)] * w[CONV_K - 1:CONV_K, :]
        for s in range(1, CONV_K):
            acc = acc + cat[pl.ds(8 - s, tt), pl.ds(hs, hd)] * w[CONV_K - 1 - s:CONV_K - s, :]
        return _silu(acc)

    def heads_lockstep(hs_idx):
        each = lambda f, *ls: [f(*a) for a in zip(*ls)]
        hs = [pl.multiple_of(h * hd, hd) for h in hs_idx]
        q = [conv_silu(cat_q, cq_ref, o) for o in hs]
        k = [conv_silu(cat_k, ck_ref, o) for o in hs]
        v = [conv_silu(cat_v, cv_ref, o) for o in hs]
        qn = each(lambda a: a * (lax.rsqrt(jnp.sum(a * a, axis=-1, keepdims=True) + RMS_EPS) * (hd ** -0.5)), q)
        kn = each(lambda a: a * lax.rsqrt(jnp.sum(a * a, axis=-1, keepdims=True) + RMS_EPS), k)
        bb = [betab[h] for h in hs_idx]
        gc = [gcb[h] for h in hs_idx]
        gr = [grow[pl.ds(h, 1), :] for h in hs_idx]
        decay = each(lambda c, r: jnp.exp(jnp.where(cmask_ref[0] > 0.0, jnp.concatenate([c, c], axis=1) - r, -1e30)),
                     gc, gr)
        kb = each(lambda a, b: a * b, kn, bb)
        knb = each(lambda a: a.astype(BF16), kn)
        lfull = each(lambda a, b, dc: _nt_dot(a.astype(BF16), b) * dc, kb, knb, decay)
        lmat = each(lambda a: a.astype(BF16), lfull)
        tinv = each(lambda a: eye - a * cmask_ref[1], lfull)
        for lv in range(1, n_lv):
            d = 1 << lv
            nb = tt // d
            tb = each(lambda a: a.astype(BF16), tinv)
            bd = each(lambda m: m * lmask_ref[lv], lmat)
            if d % 8 == 0:
                odd = each(lambda a: jnp.concatenate([a[i * d:(i + 1) * d] for i in range(1, nb, 2)], axis=0), tinv)
                tl = each(lambda a, b: jnp.dot(a.astype(BF16), b, preferred_element_type=F32).astype(BF16), odd, bd)
                upd = each(lambda a, b, c: a - jnp.dot(b, c, preferred_element_type=F32), odd, tl, tb)
                tinv = each(lambda a, u: jnp.concatenate(
                    [a[i * d:(i + 1) * d] if i % 2 == 0 else u[(i // 2) * d:(i // 2 + 1) * d] for i in range(nb)],
                    axis=0), tinv, upd)
            else:
                tl = each(lambda a, b: jnp.dot(a, b, preferred_element_type=F32).astype(BF16), tb, bd)
                tinv = each(lambda a, b, c: a - jnp.dot(b, c, preferred_element_type=F32), tinv, tl, tb)
        eg = each(jnp.exp, gc)
        rhs = each(lambda a, e, b, c: jnp.concatenate([a * e, b * c], axis=1).astype(BF16), kb, eg, v, bb)
        wu = each(lambda a, b: jnp.dot(a.astype(BF16), b, preferred_element_type=F32).astype(BF16), tinv, rhs)
        attn = each(lambda a, b, dc: (_nt_dot(a.astype(BF16), b) * dc).astype(BF16), qn, knb, decay)
        awu = each(lambda a, b: jnp.dot(a, b, preferred_element_type=F32), attn, wu)
        qp = each(lambda a, e, b: (a * e - b[:, :hd]).astype(BF16), qn, eg, awu)
        s = [st_scr[h] for h in hs_idx]
        outs = [[] for _ in hs_idx]
        for c in range(nck):
            r0 = c * GDN_CHUNK
            r1 = r0 + GDN_CHUNK
            gl = each(lambda a: a[r1 - 1:r1, :], gc)
            kd = each(lambda a, l, g: (a[r0:r1] * jnp.exp(l - g[r0:r1])).astype(BF16), kn, gl, gc)
            m = each(lambda a, b: _tn_dot(a, b[r0:r1]), kd, wu)
            sb = each(lambda a: a.astype(BF16), s)
            for i, (a, b, y) in enumerate(zip(qp, sb, awu)):
                outs[i].append(jnp.dot(a[r0:r1], b, preferred_element_type=F32) + y[r0:r1, hd:])
            s = each(lambda a, l, mm, b: a * jnp.exp(l) - jnp.dot(mm[:, :hd].astype(BF16), b,
                                                                  preferred_element_type=F32) + mm[:, hd:],
                     s, gl, m, sb)
        for i, h in enumerate(hs_idx):
            st_scr[h] = s[i]
            o = jnp.concatenate(outs[i], axis=0)
            z = z_ref[:, pl.ds(hs[i], hd)]
            o_ref[:, pl.ds(hs[i], hd)] = (o * _rms_scale(o) * nw_ref[...] * _silu(z)).astype(o_ref.dtype)

    def trip(j, carry):
        heads_lockstep([GDN_HEADS_PER_TRIP * j + i for i in range(GDN_HEADS_PER_TRIP)])
        return carry

    lax.fori_loop(0, heads // GDN_HEADS_PER_TRIP, trip, 0)


def _gdn_masks():
    tt = GDN_TILE
    ri = lax.broadcasted_iota(jnp.int32, (tt, tt), 0)
    ci = lax.broadcasted_iota(jnp.int32, (tt, tt), 1)
    same = (ri // GDN_CHUNK) == (ci // GDN_CHUNK)
    causal = same & (ci <= ri)
    tril = causal.astype(F32)
    cmask = jnp.stack([tril, (((ri % 2) == 1) & (ci == ri - 1)).astype(F32)])
    levels = []
    d = 1
    while d < GDN_CHUNK:
        levels.append((((ri // d) % 2) == 1) & ((ci // d) == (ri // d) - 1))
        d *= 2
    return tril, cmask, jnp.stack(levels).astype(BF16)


def _gdn(proj, small, conv_w, head_params, norm_w, masks, cols, batch, seq, heads):
    t = proj.shape[0]
    tt = GDN_TILE
    width = heads * HEAD_DIM
    tiles = seq // tt
    cq, ck, cv, cz = cols
    tril, cmask, lmask = masks

    def cur(col):
        return pl.BlockSpec((tt, width), lambda b, i: (b * tiles + i, col))

    def halo(col):
        return pl.BlockSpec((8, width), lambda b, i: (jnp.maximum((b * tiles + i) * (tt // 8) - 1, 0), col))

    def cw(col):
        return pl.BlockSpec((CONV_K, width), lambda b, i: (0, col))

    kern = functools.partial(_gdn_kernel, heads=heads)
    return pl.pallas_call(
        kern,
        grid=(batch, tiles),
        in_specs=[cur(cq), cur(ck), cur(cv), halo(cq), halo(ck), halo(cv), cur(cz),
                  pl.BlockSpec((tt, 128), lambda b, i: (b * tiles + i, 0)),
                  cw(0), cw(1), cw(2),
                  pl.BlockSpec((8, 128), lambda b, i: (0, 0)),
                  pl.BlockSpec((1, HEAD_DIM), lambda b, i: (0, 0)),
                  pl.BlockSpec((tt, tt), lambda b, i: (0, 0)),
                  pl.BlockSpec(cmask.shape, lambda b, i: (0, 0, 0)),
                  pl.BlockSpec(lmask.shape, lambda b, i: (0, 0, 0))],
        out_specs=pl.BlockSpec((tt, width), lambda b, i: (b * tiles + i, 0)),
        out_shape=jax.ShapeDtypeStruct((t, width), BF16),
        scratch_shapes=[pltpu.VMEM((heads, tt, HEAD_DIM), F32),
                        pltpu.VMEM((heads, tt, HEAD_DIM), F32),
                        pltpu.VMEM((8, tt), F32),
                        pltpu.VMEM((heads, HEAD_DIM, HEAD_DIM), F32),
                        pltpu.VMEM((tt + 8, width), F32),
                        pltpu.VMEM((tt + 8, width), F32),
                        pltpu.VMEM((tt + 8, width), F32)],
        compiler_params=_cparams(("parallel", "arbitrary"), 40),
    )(proj, proj, proj, proj, proj, proj, proj, small, conv_w, conv_w, conv_w, head_params, norm_w,
      tril, cmask, lmask)


def _route(h, wr_hl, rb):
    h_hi = h.astype(BF16)
    h_lo = (h - h_hi.astype(F32)).astype(BF16)
    first = jnp.dot(h_hi, wr_hl[...], preferred_element_type=F32)
    logits = first[:, :128] + first[:, 128:] + jnp.dot(h_lo, wr_hl[:, :128], preferred_element_type=F32)
    logits = logits.T[:N_EXPERTS]
    scores = jax.nn.sigmoid(logits)
    sel = scores + rb
    sel_rows = [sel[e:e + 1, :] for e in range(N_EXPERTS)]
    score_rows = [scores[e:e + 1, :] for e in range(N_EXPERTS)]
    epg = EXPERTS_PER_GROUP
    best_score = None
    best_group = None
    for g in range(N_GROUPS):
        rows = sel_rows[g * epg:(g + 1) * epg]
        gs = None
        for a in range(epg):
            for b in range(a + 1, epg):
                pair = rows[a] + rows[b]
                gs = pair if gs is None else jnp.maximum(gs, pair)
        if g == 0:
            best_score, best_group = gs, jnp.zeros_like(gs)
        else:
            better = gs > best_score
            best_group = jnp.where(better, float(g), best_group)
            best_score = jnp.where(better, gs, best_score)
    in_sel = []
    in_score = []
    for j in range(epg):
        a = sel_rows[j]
        b = score_rows[j]
        for g in range(1, N_GROUPS):
            pick = best_group == float(g)
            a = jnp.where(pick, sel_rows[g * epg + j], a)
            b = jnp.where(pick, score_rows[g * epg + j], b)
        in_sel.append(a)
        in_score.append(b)
    i1, m1, w1 = jnp.zeros_like(in_sel[0]), in_sel[0], in_score[0]
    for j in range(1, epg):
        better = in_sel[j] > m1
        i1 = jnp.where(better, float(j), i1)
        m1 = jnp.where(better, in_sel[j], m1)
        w1 = jnp.where(better, in_score[j], w1)
    i2 = m2 = w2 = None
    for j in range(epg):
        cand = jnp.where(i1 == float(j), -jnp.inf, in_sel[j])
        if j == 0:
            i2, m2, w2 = jnp.zeros_like(cand), cand, in_score[0]
        else:
            better = cand > m2
            i2 = jnp.where(better, float(j), i2)
            m2 = jnp.where(better, cand, m2)
            w2 = jnp.where(better, in_score[j], w2)
    total = w1 + w2
    zero = jnp.zeros_like(w1)
    return jnp.concatenate([best_group * epg + i1, best_group * epg + i2, w1 / total, w2 / total,
                            zero, zero, zero, zero], axis=0)


def _merge_kernel(ya_ref, yb_ref, ga_ref, gb_ref, x_ref, mod_ref, wa_ref, wb_ref, wo_ref, nw_ref, wrt_ref, rb_ref,
                  o_ref, h_ref, r_ref):
    pa = jnp.dot(ya_ref[...], wa_ref[...], preferred_element_type=F32)
    pb = jnp.dot(yb_ref[...], wb_ref[...], preferred_element_type=F32)
    merged = jax.nn.sigmoid(ga_ref[...]) * pa + jax.nn.sigmoid(gb_ref[...]) * pb
    out = jnp.dot(merged.astype(BF16), wo_ref[...], preferred_element_type=F32)
    xn = x_ref[...] + mod_ref[2:3, :] * out
    o_ref[...] = xn
    h = xn * _rms_scale(xn) * nw_ref[...] * (1.0 + mod_ref[4:5, :]) + mod_ref[3:4, :]
    h_ref[...] = h
    r_ref[...] = _route(h, wrt_ref, rb_ref[...])


def _merge(ya, yb, proj, x2, modl, wa, wb, wo, nw2, wr_t, rb, seq, ga_col, gb_col):
    t, d = x2.shape
    w = ya.shape[1]
    tm = min(seq, 256)
    per_batch = seq // tm
    const = dict(pipeline_mode=pl.Buffered(1))
    return pl.pallas_call(
        _merge_kernel,
        grid=(t // tm,),
        in_specs=[pl.BlockSpec((tm, w), lambda i: (i, 0)),
                  pl.BlockSpec((tm, w), lambda i: (i, 0)),
                  pl.BlockSpec((tm, d), lambda i: (i, ga_col)),
                  pl.BlockSpec((tm, d), lambda i: (i, gb_col)),
                  pl.BlockSpec((tm, d), lambda i: (i, 0)),
                  pl.BlockSpec((None, 6, d), lambda i: (i // per_batch, 0, 0)),
                  pl.BlockSpec((w, d), lambda i: (0, 0), **const),
                  pl.BlockSpec((w, d), lambda i: (0, 0), **const),
                  pl.BlockSpec((d, d), lambda i: (0, 0), **const),
                  pl.BlockSpec((1, d), lambda i: (0, 0)),
                  pl.BlockSpec((d, 256), lambda i: (0, 0)),
                  pl.BlockSpec((N_EXPERTS, 1), lambda i: (0, 0))],
        out_specs=[pl.BlockSpec((tm, d), lambda i: (i, 0)),
                   pl.BlockSpec((tm, d), lambda i: (i, 0)),
                   pl.BlockSpec((8, tm), lambda i: (0, i))],
        out_shape=[jax.ShapeDtypeStruct((t, d), F32), jax.ShapeDtypeStruct((t, d), F32),
                   jax.ShapeDtypeStruct((8, t), F32)],
        compiler_params=_cparams(("parallel",), 52),
    )(ya, yb, proj, proj, x2, modl, wa, wb, wo, nw2, wr_t, rb)


def _moe_kernel(te_ref, nu_ref, x_ref, wg_ref, wu_ref, wd_ref, o_ref):
    used = pl.program_id(0) < nu_ref[0]

    @pl.when(used)
    def _():
        x = x_ref[...].astype(BF16)
        a = jnp.dot(x, wg_ref[...], preferred_element_type=F32)
        b = jnp.dot(x, wu_ref[...], preferred_element_type=F32)
        hid = (_silu(a) * b).astype(BF16)
        o_ref[...] = jnp.dot(hid, wd_ref[...], preferred_element_type=F32)

    @pl.when(jnp.logical_not(used))
    def _():
        o_ref[...] = jnp.zeros_like(o_ref)


def _moe_experts(tile_expert, n_used, xs, wg, wu, wd, layer, tm):
    p, d = xs.shape
    f = wg.shape[-1]
    grid_spec = pltpu.PrefetchScalarGridSpec(
        num_scalar_prefetch=2,
        grid=(p // tm,),
        in_specs=[pl.BlockSpec((tm, d), lambda i, te, nu: (i, 0)),
                  pl.BlockSpec((None, None, d, f), lambda i, te, nu: (layer, te[i], 0, 0)),
                  pl.BlockSpec((None, None, d, f), lambda i, te, nu: (layer, te[i], 0, 0)),
                  pl.BlockSpec((None, None, f, d), lambda i, te, nu: (layer, te[i], 0, 0))],
        out_specs=pl.BlockSpec((tm, d), lambda i, te, nu: (i, 0)),
    )
    return pl.pallas_call(
        _moe_kernel,
        grid_spec=grid_spec,
        out_shape=jax.ShapeDtypeStruct((p, d), F32),
        compiler_params=_cparams(("arbitrary",), 52),
    )(tile_expert, n_used, xs, wg, wu, wd)


def _dispatch_plan(expert_idx, tm, n_tiles):
    flat_e = expert_idx.reshape(-1)
    onehot = (flat_e[:, None] == jnp.arange(N_EXPERTS, dtype=jnp.int32)[None, :]).astype(jnp.int32)
    csum = jnp.cumsum(onehot, axis=0)
    rank = jnp.sum((csum - onehot) * onehot, axis=1)
    sizes = csum[-1]
    padded = ((sizes + tm - 1) // tm) * tm
    pad_end = jnp.cumsum(padded)
    pad_start = pad_end - padded
    dest = pad_start[flat_e] + rank
    tile_start = jnp.arange(n_tiles, dtype=jnp.int32) * tm
    tile_expert = jnp.minimum(jnp.sum((tile_start[:, None] >= pad_end[None, :]).astype(jnp.int32), axis=1),
                              N_EXPERTS - 1).astype(jnp.int32)
    n_used = (pad_end[-1] // tm).astype(jnp.int32).reshape(1)
    return dest.astype(jnp.int32), tile_expert, n_used


def _combine_kernel(x_ref, y0_ref, y1_ref, gw_ref, mod_ref, fw_ref, o_ref, *, final_norm):
    gw = gw_ref[...]
    moe = gw[:, 0:1] * y0_ref[...] + gw[:, 1:2] * y1_ref[...]
    xn = x_ref[...] + mod_ref[5:6, :] * moe
    if final_norm:
        xn = xn * _rms_scale(xn) * fw_ref[...]
    o_ref[...] = xn


def _combine(x2, y_pairs, gate_w, modl, final_w, seq, final_norm):
    t, d = x2.shape
    tm = min(seq, 512)
    per_batch = seq // tm
    nblk = t // tm
    return pl.pallas_call(
        functools.partial(_combine_kernel, final_norm=final_norm),
        grid=(nblk,),
        in_specs=[pl.BlockSpec((tm, d), lambda i: (i, 0)),
                  pl.BlockSpec((tm, d), lambda i: (i, 0)),
                  pl.BlockSpec((tm, d), lambda i: (i + nblk, 0)),
                  pl.BlockSpec((tm, TOP_K), lambda i: (i, 0)),
                  pl.BlockSpec((None, 6, d), lambda i: (i // per_batch, 0, 0)),
                  pl.BlockSpec((1, d), lambda i: (0, 0))],
        out_specs=pl.BlockSpec((tm, d), lambda i: (i, 0)),
        out_shape=jax.ShapeDtypeStruct((t, d), F32),
        compiler_params=_cparams(("parallel",), 40),
    )(x2, y_pairs, y_pairs, gate_w, modl, final_w)


def kernel(x, c, w_ada, b_ada, norm1_w, norm2_w, w_in, s5_a_re, s5_a_im, s5_log_dt, s5_b_re, s5_b_im, s5_c_re, s5_c_im, s5_d, s5_w_glu, s5_b_glu, gdn_conv_w, gdn_a_log, gdn_dt_bias, gdn_norm_w, w_proj_a, w_proj_b, w_out, w_router, router_bias, w_gate, w_up, w_down, final_norm_w):
    batch, seq, d = x.shape
    depth = w_ada.shape[0]
    t = batch * seq
    s5w = s5_d.shape[1]
    gdw = w_proj_b.shape[1]
    heads = gdn_a_log.shape[1]
    assert gdw == heads * HEAD_DIM and s5w == gdw and d == 2 * gdw
    assert seq % GDN_TILE == 0 and seq % S5_CHUNK == 0
    n_levels = int(math.log2(seq // S5_CHUNK))
    assert S5_CHUNK << n_levels == seq and S5_OCT * S5_GROUP == 128

    c_pad = jnp.zeros((8, d), F32).at[:batch].set(c)
    mod = _ada(c_pad, w_ada, b_ada)

    o_u, o_q, o_k, o_v, o_z = 0, s5w, 2 * s5w, 3 * s5w, 4 * s5w
    o_ba = 5 * s5w
    o_ga = o_ba + 2 * heads
    o_gb = o_ga + d
    col_u, col_q, col_k, col_v, col_z = 4, 5, 6, 7, 8

    gdn_masks = _gdn_masks()
    s5_rep = _s5_replicate()
    s5_ops = jax.vmap(functools.partial(_s5_weights, n_levels=n_levels))(
        s5_a_re, s5_a_im, s5_log_dt, s5_b_re, s5_b_im, s5_c_re, s5_c_im)

    wr_hi = w_router.astype(BF16)
    wr_lo = (w_router - wr_hi.astype(F32)).astype(BF16)
    pad_e = ((0, 0), (0, 128 - N_EXPERTS))
    wr_t = jnp.concatenate([jnp.pad(wr_hi, pad_e), jnp.pad(wr_lo, pad_e)], axis=1)
    rb = router_bias.reshape(N_EXPERTS, 1).astype(F32)

    tm_e = min(512, t)
    n_tiles = (TOP_K * t) // tm_e + N_EXPERTS
    p_rows = n_tiles * tm_e
    pad_tok = jnp.arange(p_rows, dtype=jnp.int32) % t

    w_big = jnp.concatenate([w_in[:, :, o_ga:o_ga + d], w_in[:, :, o_gb:o_gb + d], w_in[:, :, o_u:o_ba]],
                            axis=2).astype(BF16)
    w_small = jnp.pad(w_in[:, :, o_ba:o_ga], ((0, 0), (0, 0), (0, 128 - 2 * heads))).astype(BF16)
    w_gate_b = w_gate.astype(BF16)
    w_up_b = w_up.astype(BF16)
    w_down_b = w_down.astype(BF16)

    x2 = x.reshape(t, d)
    for l in range(depth):
        modl = mod[l, :batch].reshape(batch, 6, d)
        proj, small = _inproj(x2, modl, norm1_w[l].reshape(1, d), w_big, w_small, l, seq)

        y_lin = _s5_core(proj, col_u * s5w // 128, *s5_ops, s5_rep, l, batch, seq)
        y_a = _s5_post(y_lin, proj, col_u, s5_d[l].reshape(1, s5w), s5_w_glu[l].astype(BF16),
                       s5_b_glu[l].reshape(1, s5w))

        head_params = jnp.zeros((8, 128), F32)
        head_params = head_params.at[0, heads:2 * heads].set(-jnp.exp(gdn_a_log[l]))
        head_params = head_params.at[1, heads:2 * heads].set(gdn_dt_bias[l])
        y_b = _gdn(proj, small, gdn_conv_w[l], head_params, gdn_norm_w[l].reshape(1, HEAD_DIM), gdn_masks,
                   (col_q, col_k, col_v, col_z), batch, seq, heads)

        x2, h2, route = _merge(y_a, y_b, proj, x2, modl, w_proj_a[l].astype(BF16), w_proj_b[l].astype(BF16),
                               w_out[l].astype(BF16), norm2_w[l].reshape(1, d), wr_t, rb, seq, 0, 1)

        expert_idx = route[0:TOP_K].T.astype(jnp.int32)
        gate_w = route[TOP_K:2 * TOP_K].T
        dest, tile_expert, n_used = _dispatch_plan(expert_idx, tm_e, n_tiles)
        src_tok = pad_tok.at[dest].set(jnp.arange(TOP_K * t, dtype=jnp.int32) // TOP_K)
        xs = jnp.take(h2, src_tok, axis=0, mode='clip')
        ys = _moe_experts(tile_expert, n_used, xs, w_gate_b, w_up_b, w_down_b, l, tm_e)
        y_pairs = jnp.take(ys, dest.reshape(t, TOP_K).T.reshape(-1), axis=0, mode='clip')
        x2 = _combine(x2, y_pairs, gate_w, modl, final_norm_w.reshape(1, d), seq, l == depth - 1)

    return x2.reshape(batch, seq, d)
```

```python
import functools
import math

import jax
import jax.numpy as jnp
from jax import lax
from jax.experimental import pallas as pl
from jax.experimental.pallas import tpu as pltpu

F32 = jnp.float32
BF16 = jnp.bfloat16
HIGHEST = lax.Precision.HIGHEST

RMS_EPS = 1e-6
S5_GROUP = 16
S5_STATE = 64
S5_CHUNK = 16
S5_OCT = 8
HEAD_DIM = 128
GDN_CHUNK = 64
GDN_TILE = 256
GDN_HEADS_PER_TRIP = 4
CONV_K = 4
N_EXPERTS = 16
N_GROUPS = 4
EXPERTS_PER_GROUP = N_EXPERTS // N_GROUPS
TOP_K = 2
MIB = 1024 * 1024


def _cparams(semantics, vmem_mib):
    return pltpu.CompilerParams(dimension_semantics=semantics, vmem_limit_bytes=vmem_mib * MIB)


def _silu(x):
    return x * jax.nn.sigmoid(x)


def _softplus(x):
    return jnp.maximum(x, 0.0) + jnp.log1p(jnp.exp(-jnp.abs(x)))


def _rms_scale(x):
    return lax.rsqrt(jnp.mean(x * x, axis=-1, keepdims=True) + RMS_EPS)


def _pack_halves(x):
    n = x.shape[1] // 2
    lo = lax.bitcast_convert_type(x[:, :n].astype(BF16).astype(F32), jnp.uint32)
    hi = lax.bitcast_convert_type(x[:, n:].astype(BF16).astype(F32), jnp.uint32)
    return (lo >> 16) | hi


def _unpack_halves(p):
    lo = lax.bitcast_convert_type(p << 16, F32)
    hi = lax.bitcast_convert_type(p & jnp.uint32(0xFFFF0000), F32)
    return lo, hi


def _nt_dot(a, b):
    return lax.dot_general(a, b, (((1,), (1,)), ((), ())), preferred_element_type=F32)


def _tn_dot(a, b):
    return lax.dot_general(a, b, (((0,), (0,)), ((), ())), preferred_element_type=F32)


def _ada_kernel(c_ref, w_ref, b_ref, o_ref):
    ca = _silu(c_ref[...]).astype(BF16)
    o_ref[...] = jnp.dot(ca, w_ref[...].astype(BF16), preferred_element_type=F32) + b_ref[...]


def _ada(c_pad, w_ada, b_ada):
    depth, d, n = w_ada.shape
    rows = c_pad.shape[0]
    tn = min(n, 1024)
    return pl.pallas_call(
        _ada_kernel,
        grid=(depth, n // tn),
        in_specs=[pl.BlockSpec((rows, d), lambda l, j: (0, 0)),
                  pl.BlockSpec((None, d, tn), lambda l, j: (l, 0, j)),
                  pl.BlockSpec((None, 1, tn), lambda l, j: (l, 0, j))],
        out_specs=pl.BlockSpec((None, rows, tn), lambda l, j: (l, 0, j)),
        out_shape=jax.ShapeDtypeStruct((depth, rows, n), F32),
        compiler_params=_cparams(("parallel", "parallel"), 40),
    )(c_pad, w_ada, b_ada.reshape(depth, 1, n))


def _inproj_kernel(x_ref, mod_ref, nw_ref, w_ref, wsm_ref, o_ref, osm_ref, h_scr):
    @pl.when(pl.program_id(1) == 0)
    def _():
        x = x_ref[...]
        h = x * _rms_scale(x) * nw_ref[...] * (1.0 + mod_ref[1:2, :]) + mod_ref[0:1, :]
        hb = h.astype(BF16)
        h_scr[...] = hb
        osm_ref[...] = jnp.dot(hb, wsm_ref[...], preferred_element_type=F32)

    o_ref[...] = jnp.dot(h_scr[...], w_ref[...], preferred_element_type=F32)


def _inproj(x2, modl, nw, w_big, w_small, layer, seq):
    t, d = x2.shape
    n = w_big.shape[-1]
    tm = min(seq, 1024)
    tn = 1024
    per_batch = seq // tm
    return pl.pallas_call(
        _inproj_kernel,
        grid=(t // tm, n // tn),
        in_specs=[pl.BlockSpec((tm, d), lambda i, j: (i, 0)),
                  pl.BlockSpec((None, 6, d), lambda i, j: (i // per_batch, 0, 0)),
                  pl.BlockSpec((1, d), lambda i, j: (0, 0)),
                  pl.BlockSpec((None, d, tn), lambda i, j: (layer, 0, j)),
                  pl.BlockSpec((None, d, 128), lambda i, j: (layer, 0, 0))],
        out_specs=[pl.BlockSpec((tm, tn), lambda i, j: (i, j)),
                   pl.BlockSpec((tm, 128), lambda i, j: (i, 0))],
        out_shape=[jax.ShapeDtypeStruct((t, n), F32), jax.ShapeDtypeStruct((t, 128), F32)],
        scratch_shapes=[pltpu.VMEM((tm, d), BF16)],
        compiler_params=_cparams(("parallel", "arbitrary"), 48),
    )(x2, modl, nw, w_big, w_small)


def _s5_weights(a_re, a_im, log_dt, b_re, b_im, c_re, c_im, n_levels):
    tc = S5_CHUNK
    groups, p = a_re.shape
    n = b_re.shape[-1]
    oc = S5_OCT
    octs = groups // oc
    lam = lax.complex(a_re, a_im)
    dt = jnp.exp(log_dt)[:, None]
    log_a = lam * dt
    a_bar = jnp.exp(log_a)
    b_bar = ((a_bar - 1.0) / lam)[..., None] * lax.complex(b_re, b_im)
    c_mat = lax.complex(c_re, c_im)
    apow = jnp.exp(log_a[:, :, None] * jnp.arange(tc + 1, dtype=F32)[None, None, :])
    lane = jnp.arange(tc * n)
    e_t = (jnp.arange(tc)[:, None] == (lane // n)[None, :]).astype(F32)
    e_n = (jnp.arange(n)[:, None] == (lane % n)[None, :]).astype(F32)

    def expand(z, e):
        f = lambda r: jnp.einsum('gpk,kj->gpj', r, e, precision=HIGHEST)
        return lax.complex(f(jnp.real(z)), f(jnp.imag(z)))

    def re_negim_rows(z):
        return jnp.concatenate([jnp.real(z), -jnp.imag(z)], axis=1)

    c_rep = expand(jnp.transpose(c_mat, (0, 2, 1)), e_n)
    lag_c = re_negim_rows(expand(apow[:, :, :tc], e_t) * c_rep)
    hoc = re_negim_rows(expand(apow[:, :, 1:], e_t) * c_rep)
    b_t = jnp.transpose(b_bar, (0, 2, 1))
    b_cat = jnp.concatenate([jnp.real(b_t), jnp.imag(b_t)], axis=-1)
    kern = jnp.einsum('gnk,gkj->gnj', b_cat, lag_c, precision=HIGHEST)
    tcomp = jnp.stack([jnp.pad(kern[:, :, :(tc - s) * n], ((0, 0), (0, 0), (s * n, 0))) for s in range(tc)],
                      axis=1)
    tcomp = jnp.transpose(tcomp.reshape(octs, oc, tc, n, tc * n), (0, 2, 1, 3, 4)).reshape(octs, tc * oc * n, tc * n)
    ap_rev = jnp.transpose(apow[:, :, tc - 1 - jnp.arange(tc)], (0, 2, 1))
    hin_c = ap_rev[:, :, None, :] * b_t[:, None, :, :]
    hc = jnp.concatenate([jnp.real(hin_c), jnp.imag(hin_c)], axis=-1)
    hc = jnp.transpose(hc.reshape(octs, oc, tc, n, 2 * p), (0, 2, 1, 3, 4)).reshape(octs, tc * oc * n, 2 * p)
    hoc = hoc.reshape(octs, oc * 2 * p, tc * n)
    steps = (tc * (2.0 ** jnp.arange(n_levels, dtype=F32)))
    amul = jnp.exp(log_a[:, None, :] * steps[None, :, None])
    a1 = jnp.concatenate([jnp.real(amul), jnp.real(amul)], axis=-1)
    a2 = jnp.concatenate([-jnp.imag(amul), jnp.imag(amul)], axis=-1)

    def lanes_by_group(a):
        return jnp.transpose(a.reshape(octs, oc, n_levels, 2 * p), (0, 2, 1, 3)).reshape(octs, n_levels, oc * 2 * p)

    return (tcomp.astype(BF16), hc.astype(BF16), hoc.astype(BF16),
            lanes_by_group(a1).astype(F32), lanes_by_group(a2).astype(F32))


def _s5_replicate():
    r = lax.broadcasted_iota(jnp.int32, (S5_CHUNK * S5_GROUP, S5_CHUNK * 128), 0)
    c = lax.broadcasted_iota(jnp.int32, (S5_CHUNK * S5_GROUP, S5_CHUNK * 128), 1)
    return ((r // S5_GROUP == c // 128) & (r % S5_GROUP == c % S5_GROUP)).astype(BF16)


def _s5_core_kernel(u_ref, tc_ref, hc_ref, hoc_ref, a1_ref, a2_ref, rep_ref, y_ref, toep_scr, hin_scr, hout_scr,
                    *, n_levels):
    tc = S5_CHUNK
    lanes = u_ref.shape[1]
    state_lanes = hc_ref.shape[1]

    @pl.when(pl.program_id(1) == 0)
    def _():
        kd = toep_scr.shape[1]
        rb = 256
        col_g = (lax.broadcasted_iota(jnp.int32, (rb, kd), 1) // S5_GROUP) % S5_OCT
        row_i = lax.broadcasted_iota(jnp.int32, (rb, kd), 0)
        for i in range(toep_scr.shape[0] // rb):
            full = jnp.dot(tc_ref[i * rb:(i + 1) * rb, :], rep_ref[...], preferred_element_type=F32)
            row_g = ((row_i + i * rb) // S5_GROUP) % S5_OCT
            toep_scr[i * rb:(i + 1) * rb, :] = jnp.where(row_g == col_g, full, 0.0).astype(BF16)
        for i in range(hout_scr.shape[0] // rb):
            full = jnp.dot(hoc_ref[i * rb:(i + 1) * rb, :], rep_ref[...], preferred_element_type=F32)
            row_g = (row_i + i * rb) // state_lanes
            hout_scr[i * rb:(i + 1) * rb, :] = jnp.where(row_g == col_g, full, 0.0).astype(BF16)
        hrow_g = (lax.broadcasted_iota(jnp.int32, hc_ref.shape, 0) // S5_GROUP) % S5_OCT
        for g in range(S5_OCT):
            hin_scr[:, g * state_lanes:(g + 1) * state_lanes] = jnp.where(hrow_g == g, hc_ref[...], 0.0).astype(BF16)

    rows = u_ref.shape[0] // tc
    x = jnp.concatenate([u_ref[pl.ds(t, rows, stride=tc), :] for t in range(tc)], axis=1).astype(BF16)
    s_loc = jnp.dot(x, hin_scr[...], preferred_element_type=F32)
    pos = lax.broadcasted_iota(jnp.int32, (rows, state_lanes), 0)
    s_in = []
    for g in range(S5_OCT):
        ln = slice(g * state_lanes, (g + 1) * state_lanes)
        s = s_loc[:, ln]
        for k in range(n_levels):
            d = 1 << k
            prev = jnp.where(pos >= d, pltpu.roll(s, d, axis=0), 0.0)
            s = s + a1_ref[k:k + 1, ln] * prev + a2_ref[k:k + 1, ln] * pltpu.roll(prev, state_lanes // 2, axis=1)
        s_in.append(jnp.where(pos >= 1, pltpu.roll(s, 1, axis=0), 0.0).astype(BF16))
    y = (jnp.dot(x, toep_scr[...], preferred_element_type=F32)
         + jnp.dot(jnp.concatenate(s_in, axis=1), hout_scr[...], preferred_element_type=F32))
    for t in range(tc):
        y_ref[pl.ds(t, rows, stride=tc), :] = y[:, t * lanes:(t + 1) * lanes]


def _s5_core(proj, u_lane_block, tcomp, hc, hoc, a1, a2, rep, layer, batch, seq):
    t = proj.shape[0]
    _, octs, kdim, tn = tcomp.shape
    sl = hc.shape[-1]
    sw = hoc.shape[2]
    n_levels = a1.shape[2]
    lanes = kdim // S5_CHUNK
    kern = functools.partial(_s5_core_kernel, n_levels=n_levels)
    return pl.pallas_call(
        kern,
        grid=(octs, batch),
        in_specs=[pl.BlockSpec((seq, lanes), lambda o, b: (b, u_lane_block + o)),
                  pl.BlockSpec((None, None, kdim, tn), lambda o, b: (layer, o, 0, 0)),
                  pl.BlockSpec((None, None, kdim, sl), lambda o, b: (layer, o, 0, 0)),
                  pl.BlockSpec((None, None, sw, tn), lambda o, b: (layer, o, 0, 0)),
                  pl.BlockSpec((None, None, n_levels, sw), lambda o, b: (layer, o, 0, 0)),
                  pl.BlockSpec((None, None, n_levels, sw), lambda o, b: (layer, o, 0, 0)),
                  pl.BlockSpec((tn, kdim), lambda o, b: (0, 0))],
        out_specs=pl.BlockSpec((seq, lanes), lambda o, b: (b, o)),
        out_shape=jax.ShapeDtypeStruct((t, octs * lanes), F32),
        scratch_shapes=[pltpu.VMEM((kdim, kdim), BF16), pltpu.VMEM((kdim, sw), BF16), pltpu.VMEM((sw, kdim), BF16)],
        compiler_params=_cparams(("parallel", "arbitrary"), 48),
    )(proj, tcomp, hc, hoc, a1, a2, rep)


def _s5_post_kernel(y_ref, u_ref, d_ref, w_ref, b_ref, o_ref):
    y = jax.nn.gelu(y_ref[...] + d_ref[...] * u_ref[...])
    gate = jnp.dot(y.astype(BF16), w_ref[...], preferred_element_type=F32) + b_ref[...]
    o_ref[...] = (y * jax.nn.sigmoid(gate)).astype(o_ref.dtype)


def _s5_post(y_lin, proj, u_col, d_skip, w_glu, b_glu):
    t, w = y_lin.shape
    tm = min(t, 512)
    return pl.pallas_call(
        _s5_post_kernel,
        grid=(t // tm,),
        in_specs=[pl.BlockSpec((tm, w), lambda i: (i, 0)),
                  pl.BlockSpec((tm, w), lambda i: (i, u_col)),
                  pl.BlockSpec((1, w), lambda i: (0, 0)),
                  pl.BlockSpec((w, w), lambda i: (0, 0)),
                  pl.BlockSpec((1, w), lambda i: (0, 0))],
        out_specs=pl.BlockSpec((tm, w), lambda i: (i, 0)),
        out_shape=jax.ShapeDtypeStruct((t, w), BF16),
        compiler_params=_cparams(("parallel",), 32),
    )(y_lin, proj, d_skip, w_glu, b_glu)


def _gdn_kernel(q_ref, k_ref, v_ref, qh_ref, kh_ref, vh_ref, z_ref, sm_ref, cq_ref, ck_ref, cv_ref,
                hp_ref, nw_ref, tril_ref, cmask_ref, lmask_ref, o_ref, betab, gcb, grow, st_scr,
                cat_q, cat_k, cat_v, *, heads):
    tt = GDN_TILE
    hd = HEAD_DIM
    nck = tt // GDN_CHUNK
    first = pl.program_id(1) == 0

    @pl.when(first)
    def _():
        st_scr[...] = jnp.zeros_like(st_scr)

    sm = sm_ref[...]
    hp = hp_ref[...]
    beta_all = jax.nn.sigmoid(sm)
    g_all = hp[0:1, :] * _softplus(sm + hp[1:2, :])
    gc_all = jnp.dot(tril_ref[...], g_all, precision=HIGHEST, preferred_element_type=F32)
    gc_t = gc_all.T
    for h in range(heads):
        betab[h] = jnp.broadcast_to(beta_all[:, h:h + 1], (tt, hd))
        gcb[h] = jnp.broadcast_to(gc_all[:, heads + h:heads + h + 1], (tt, hd))
        grow[h:h + 1, :] = gc_t[heads + h:heads + h + 1, :]

    eye = (lax.broadcasted_iota(jnp.int32, (tt, tt), 0) == lax.broadcasted_iota(jnp.int32, (tt, tt), 1)).astype(F32)
    n_lv = lmask_ref.shape[0]

    for x_ref, halo_ref, cat in ((q_ref, qh_ref, cat_q), (k_ref, kh_ref, cat_k), (v_ref, vh_ref, cat_v)):
        cat[0:8, :] = jnp.where(first, 0.0, halo_ref[...])
        cat[8:8 + tt, :] = x_ref[...]

    def conv_silu(cat, cw_ref, hs):
        w = cw_ref[:, pl.ds(hs, hd)]
        acc = cat[pl.ds(8, tt), pl.ds(hs, hd)] * w[CONV_K - 1:CONV_K, :]
        for s in range(1, CONV_K):
            acc = acc + cat[pl.ds(8 - s, tt), pl.ds(hs, hd)] * w[CONV_K - 1 - s:CONV_K - s, :]
        return _silu(acc)

    def heads_lockstep(hs_idx):
        each = lambda f, *ls: [f(*a) for a in zip(*ls)]
        hs = [pl.multiple_of(h * hd, hd) for h in hs_idx]
        q = [conv_silu(cat_q, cq_ref, o) for o in hs]
        k = [conv_silu(cat_k, ck_ref, o) for o in hs]
        v = [conv_silu(cat_v, cv_ref, o) for o in hs]
        qn = each(lambda a: a * (lax.rsqrt(jnp.sum(a * a, axis=-1, keepdims=True) + RMS_EPS) * (hd ** -0.5)), q)
        kn = each(lambda a: a * lax.rsqrt(jnp.sum(a * a, axis=-1, keepdims=True) + RMS_EPS), k)
        bb = [betab[h] for h in hs_idx]
        gc = [gcb[h] for h in hs_idx]
        gr = [grow[pl.ds(h, 1), :] for h in hs_idx]
        decay = each(lambda c, r: jnp.exp(jnp.where(cmask_ref[0] > 0.0, jnp.concatenate([c, c], axis=1) - r, -1e30)),
                     gc, gr)
        kb = each(lambda a, b: a * b, kn, bb)
        knb = each(lambda a: a.astype(BF16), kn)
        lfull = each(lambda a, b, dc: _nt_dot(a.astype(BF16), b) * dc, kb, knb, decay)
        lmat = each(lambda a: a.astype(BF16), lfull)
        tinv = each(lambda a: eye - a * cmask_ref[1], lfull)
        for lv in range(1, n_lv):
            d = 1 << lv
            nb = tt // d
            tb = each(lambda a: a.astype(BF16), tinv)
            bd = each(lambda m: m * lmask_ref[lv], lmat)
            if d % 8 == 0:
                odd = each(lambda a: jnp.concatenate([a[i * d:(i + 1) * d] for i in range(1, nb, 2)], axis=0), tinv)
                tl = each(lambda a, b: jnp.dot(a.astype(BF16), b, preferred_element_type=F32).astype(BF16), odd, bd)
                upd = each(lambda a, b, c: a - jnp.dot(b, c, preferred_element_type=F32), odd, tl, tb)
                tinv = each(lambda a, u: jnp.concatenate(
                    [a[i * d:(i + 1) * d] if i % 2 == 0 else u[(i // 2) * d:(i // 2 + 1) * d] for i in range(nb)],
                    axis=0), tinv, upd)
            else:
                tl = each(lambda a, b: jnp.dot(a, b, preferred_element_type=F32).astype(BF16), tb, bd)
                tinv = each(lambda a, b, c: a - jnp.dot(b, c, preferred_element_type=F32), tinv, tl, tb)
        eg = each(jnp.exp, gc)
        rhs = each(lambda a, e, b, c: jnp.concatenate([a * e, b * c], axis=1).astype(BF16), kb, eg, v, bb)
        wu = each(lambda a, b: jnp.dot(a.astype(BF16), b, preferred_element_type=F32).astype(BF16), tinv, rhs)
        attn = each(lambda a, b, dc: (_nt_dot(a.astype(BF16), b) * dc).astype(BF16), qn, knb, decay)
        awu = each(lambda a, b: jnp.dot(a, b, preferred_element_type=F32), attn, wu)
        qp = each(lambda a, e, b: (a * e - b[:, :hd]).astype(BF16), qn, eg, awu)
        s = [st_scr[h] for h in hs_idx]
        outs = [[] for _ in hs_idx]
        for c in range(nck):
            r0 = c * GDN_CHUNK
            r1 = r0 + GDN_CHUNK
            gl = each(lambda a: a[r1 - 1:r1, :], gc)
            kd = each(lambda a, l, g: (a[r0:r1] * jnp.exp(l - g[r0:r1])).astype(BF16), kn, gl, gc)
            m = each(lambda a, b: _tn_dot(a, b[r0:r1]), kd, wu)
            sb = each(lambda a: a.astype(BF16), s)
            for i, (a, b, y) in enumerate(zip(qp, sb, awu)):
                outs[i].append(jnp.dot(a[r0:r1], b, preferred_element_type=F32) + y[r0:r1, hd:])
            s = each(lambda a, l, mm, b: a * jnp.exp(l) - jnp.dot(mm[:, :hd].astype(BF16), b,
                                                                  preferred_element_type=F32) + mm[:, hd:],
                     s, gl, m, sb)
        for i, h in enumerate(hs_idx):
            st_scr[h] = s[i]
            o = jnp.concatenate(outs[i], axis=0)
            z = z_ref[:, pl.ds(hs[i], hd)]
            o_ref[:, pl.ds(hs[i], hd)] = (o * _rms_scale(o) * nw_ref[...] * _silu(z)).astype(o_ref.dtype)

    def trip(j, carry):
        heads_lockstep([GDN_HEADS_PER_TRIP * j + i for i in range(GDN_HEADS_PER_TRIP)])
        return carry

    lax.fori_loop(0, heads // GDN_HEADS_PER_TRIP, trip, 0)


def _gdn_masks():
    tt = GDN_TILE
    ri = lax.broadcasted_iota(jnp.int32, (tt, tt), 0)
    ci = lax.broadcasted_iota(jnp.int32, (tt, tt), 1)
    same = (ri // GDN_CHUNK) == (ci // GDN_CHUNK)
    causal = same & (ci <= ri)
    tril = causal.astype(F32)
    cmask = jnp.stack([tril, (((ri % 2) == 1) & (ci == ri - 1)).astype(F32)])
    levels = []
    d = 1
    while d < GDN_CHUNK:
        levels.append((((ri // d) % 2) == 1) & ((ci // d) == (ri // d) - 1))
        d *= 2
    return tril, cmask, jnp.stack(levels).astype(BF16)


def _gdn(proj, small, conv_w, head_params, norm_w, masks, cols, batch, seq, heads):
    t = proj.shape[0]
    tt = GDN_TILE
    width = heads * HEAD_DIM
    tiles = seq // tt
    cq, ck, cv, cz = cols
    tril, cmask, lmask = masks

    def cur(col):
        return pl.BlockSpec((tt, width), lambda b, i: (b * tiles + i, col))

    def halo(col):
        return pl.BlockSpec((8, width), lambda b, i: (jnp.maximum((b * tiles + i) * (tt // 8) - 1, 0), col))

    def cw(col):
        return pl.BlockSpec((CONV_K, width), lambda b, i: (0, col))

    kern = functools.partial(_gdn_kernel, heads=heads)
    return pl.pallas_call(
        kern,
        grid=(batch, tiles),
        in_specs=[cur(cq), cur(ck), cur(cv), halo(cq), halo(ck), halo(cv), cur(cz),
                  pl.BlockSpec((tt, 128), lambda b, i: (b * tiles + i, 0)),
                  cw(0), cw(1), cw(2),
                  pl.BlockSpec((8, 128), lambda b, i: (0, 0)),
                  pl.BlockSpec((1, HEAD_DIM), lambda b, i: (0, 0)),
                  pl.BlockSpec((tt, tt), lambda b, i: (0, 0)),
                  pl.BlockSpec(cmask.shape, lambda b, i: (0, 0, 0)),
                  pl.BlockSpec(lmask.shape, lambda b, i: (0, 0, 0))],
        out_specs=pl.BlockSpec((tt, width), lambda b, i: (b * tiles + i, 0)),
        out_shape=jax.ShapeDtypeStruct((t, width), BF16),
        scratch_shapes=[pltpu.VMEM((heads, tt, HEAD_DIM), F32),
                        pltpu.VMEM((heads, tt, HEAD_DIM), F32),
                        pltpu.VMEM((8, tt), F32),
                        pltpu.VMEM((heads, HEAD_DIM, HEAD_DIM), F32),
                        pltpu.VMEM((tt + 8, width), F32),
                        pltpu.VMEM((tt + 8, width), F32),
                        pltpu.VMEM((tt + 8, width), F32)],
        compiler_params=_cparams(("parallel", "arbitrary"), 40),
    )(proj, proj, proj, proj, proj, proj, proj, small, conv_w, conv_w, conv_w, head_params, norm_w,
      tril, cmask, lmask)


def _route(h, wr_hl, rb):
    h_hi = h.astype(BF16)
    h_lo = (h - h_hi.astype(F32)).astype(BF16)
    first = jnp.dot(h_hi, wr_hl[...], preferred_element_type=F32)
    logits = first[:, :128] + first[:, 128:] + jnp.dot(h_lo, wr_hl[:, :128], preferred_element_type=F32)
    logits = logits.T[:N_EXPERTS]
    scores = jax.nn.sigmoid(logits)
    sel = scores + rb
    sel_rows = [sel[e:e + 1, :] for e in range(N_EXPERTS)]
    score_rows = [scores[e:e + 1, :] for e in range(N_EXPERTS)]
    epg = EXPERTS_PER_GROUP
    best_score = None
    best_group = None
    for g in range(N_GROUPS):
        rows = sel_rows[g * epg:(g + 1) * epg]
        gs = None
        for a in range(epg):
            for b in range(a + 1, epg):
                pair = rows[a] + rows[b]
                gs = pair if gs is None else jnp.maximum(gs, pair)
        if g == 0:
            best_score, best_group = gs, jnp.zeros_like(gs)
        else:
            better = gs > best_score
            best_group = jnp.where(better, float(g), best_group)
            best_score = jnp.where(better, gs, best_score)
    in_sel = []
    in_score = []
    for j in range(epg):
        a = sel_rows[j]
        b = score_rows[j]
        for g in range(1, N_GROUPS):
            pick = best_group == float(g)
            a = jnp.where(pick, sel_rows[g * epg + j], a)
            b = jnp.where(pick, score_rows[g * epg + j], b)
        in_sel.append(a)
        in_score.append(b)
    i1, m1, w1 = jnp.zeros_like(in_sel[0]), in_sel[0], in_score[0]
    for j in range(1, epg):
        better = in_sel[j] > m1
        i1 = jnp.where(better, float(j), i1)
        m1 = jnp.where(better, in_sel[j], m1)
        w1 = jnp.where(better, in_score[j], w1)
    i2 = m2 = w2 = None
    for j in range(epg):
        cand = jnp.where(i1 == float(j), -jnp.inf, in_sel[j])
        if j == 0:
            i2, m2, w2 = jnp.zeros_like(cand), cand, in_score[0]
        else:
            better = cand > m2
            i2 = jnp.where(better, float(j), i2)
            m2 = jnp.where(better, cand, m2)
            w2 = jnp.where(better, in_score[j], w2)
    total = w1 + w2
    zero = jnp.zeros_like(w1)
    return jnp.concatenate([best_group * epg + i1, best_group * epg + i2, w1 / total, w2 / total,
                            zero, zero, zero, zero], axis=0)


def _merge_kernel(ya_ref, yb_ref, ga_ref, gb_ref, x_ref, mod_ref, wa_ref, wb_ref, wo_ref, nw_ref, wrt_ref, rb_ref,
                  o_ref, h_ref, r_ref):
    pa = jnp.dot(ya_ref[...], wa_ref[...], preferred_element_type=F32)
    pb = jnp.dot(yb_ref[...], wb_ref[...], preferred_element_type=F32)
    merged = jax.nn.sigmoid(ga_ref[...]) * pa + jax.nn.sigmoid(gb_ref[...]) * pb
    out = jnp.dot(merged.astype(BF16), wo_ref[...], preferred_element_type=F32)
    xn = x_ref[...] + mod_ref[2:3, :] * out
    o_ref[...] = xn
    h = xn * _rms_scale(xn) * nw_ref[...] * (1.0 + mod_ref[4:5, :]) + mod_ref[3:4, :]
    h_ref[...] = _pack_halves(h)
    r_ref[...] = _route(h, wrt_ref, rb_ref[...])


def _merge(ya, yb, proj, x2, modl, wa, wb, wo, nw2, wr_t, rb, seq, ga_col, gb_col):
    t, d = x2.shape
    w = ya.shape[1]
    tm = min(seq, 256)
    per_batch = seq // tm
    const = dict(pipeline_mode=pl.Buffered(1))
    return pl.pallas_call(
        _merge_kernel,
        grid=(t // tm,),
        in_specs=[pl.BlockSpec((tm, w), lambda i: (i, 0)),
                  pl.BlockSpec((tm, w), lambda i: (i, 0)),
                  pl.BlockSpec((tm, d), lambda i: (i, ga_col)),
                  pl.BlockSpec((tm, d), lambda i: (i, gb_col)),
                  pl.BlockSpec((tm, d), lambda i: (i, 0)),
                  pl.BlockSpec((None, 6, d), lambda i: (i // per_batch, 0, 0)),
                  pl.BlockSpec((w, d), lambda i: (0, 0), **const),
                  pl.BlockSpec((w, d), lambda i: (0, 0), **const),
                  pl.BlockSpec((d, d), lambda i: (0, 0), **const),
                  pl.BlockSpec((1, d), lambda i: (0, 0)),
                  pl.BlockSpec((d, 256), lambda i: (0, 0)),
                  pl.BlockSpec((N_EXPERTS, 1), lambda i: (0, 0))],
        out_specs=[pl.BlockSpec((tm, d), lambda i: (i, 0)),
                   pl.BlockSpec((tm, d // 2), lambda i: (i, 0)),
                   pl.BlockSpec((8, tm), lambda i: (0, i))],
        out_shape=[jax.ShapeDtypeStruct((t, d), F32), jax.ShapeDtypeStruct((t, d // 2), jnp.uint32),
                   jax.ShapeDtypeStruct((8, t), F32)],
        compiler_params=_cparams(("parallel",), 52),
    )(ya, yb, proj, proj, x2, modl, wa, wb, wo, nw2, wr_t, rb)


def _moe_kernel(te_ref, nu_ref, x_ref, wg_ref, wu_ref, wd_ref, o_ref):
    used = pl.program_id(0) < nu_ref[0]

    @pl.when(used)
    def _():
        x_lo, x_hi = _unpack_halves(x_ref[...])
        x_lo = x_lo.astype(BF16)
        x_hi = x_hi.astype(BF16)
        half = x_lo.shape[1]

        def up(w_ref):
            return (jnp.dot(x_lo, w_ref[:half, :], preferred_element_type=F32)
                    + jnp.dot(x_hi, w_ref[half:, :], preferred_element_type=F32))

        hid = (_silu(up(wg_ref)) * up(wu_ref)).astype(BF16)
        o_ref[...] = _pack_halves(jnp.dot(hid, wd_ref[...], preferred_element_type=F32))

    @pl.when(jnp.logical_not(used))
    def _():
        o_ref[...] = jnp.zeros_like(o_ref)


def _moe_experts(tile_expert, n_used, xs, wg, wu, wd, layer, tm):
    p, dh = xs.shape
    d = 2 * dh
    f = wg.shape[-1]
    grid_spec = pltpu.PrefetchScalarGridSpec(
        num_scalar_prefetch=2,
        grid=(p // tm,),
        in_specs=[pl.BlockSpec((tm, dh), lambda i, te, nu: (i, 0)),
                  pl.BlockSpec((None, None, d, f), lambda i, te, nu: (layer, te[i], 0, 0)),
                  pl.BlockSpec((None, None, d, f), lambda i, te, nu: (layer, te[i], 0, 0)),
                  pl.BlockSpec((None, None, f, d), lambda i, te, nu: (layer, te[i], 0, 0))],
        out_specs=pl.BlockSpec((tm, dh), lambda i, te, nu: (i, 0)),
    )
    return pl.pallas_call(
        _moe_kernel,
        grid_spec=grid_spec,
        out_shape=jax.ShapeDtypeStruct((p, dh), jnp.uint32),
        compiler_params=_cparams(("arbitrary",), 52),
    )(tile_expert, n_used, xs, wg, wu, wd)


def _dispatch_plan(expert_idx, tm, n_tiles):
    flat_e = expert_idx.reshape(-1)
    onehot = (flat_e[:, None] == jnp.arange(N_EXPERTS, dtype=jnp.int32)[None, :]).astype(jnp.int32)
    csum = jnp.cumsum(onehot, axis=0)
    rank = jnp.sum((csum - onehot) * onehot, axis=1)
    sizes = csum[-1]
    padded = ((sizes + tm - 1) // tm) * tm
    pad_end = jnp.cumsum(padded)
    pad_start = pad_end - padded
    dest = pad_start[flat_e] + rank
    tile_start = jnp.arange(n_tiles, dtype=jnp.int32) * tm
    tile_expert = jnp.minimum(jnp.sum((tile_start[:, None] >= pad_end[None, :]).astype(jnp.int32), axis=1),
                              N_EXPERTS - 1).astype(jnp.int32)
    n_used = (pad_end[-1] // tm).astype(jnp.int32).reshape(1)
    return dest.astype(jnp.int32), tile_expert, n_used


def _combine_kernel(x_ref, y0_ref, y1_ref, gw_ref, mod_ref, fw_ref, o_ref, *, final_norm):
    gw = gw_ref[...]
    y0_lo, y0_hi = _unpack_halves(y0_ref[...])
    y1_lo, y1_hi = _unpack_halves(y1_ref[...])
    moe = jnp.concatenate([gw[:, 0:1] * y0_lo + gw[:, 1:2] * y1_lo, gw[:, 0:1] * y0_hi + gw[:, 1:2] * y1_hi], axis=1)
    xn = x_ref[...] + mod_ref[5:6, :] * moe
    if final_norm:
        xn = xn * _rms_scale(xn) * fw_ref[...]
    o_ref[...] = xn


def _combine(x2, y_pairs, gate_w, modl, final_w, seq, final_norm):
    t, d = x2.shape
    tm = min(seq, 512)
    per_batch = seq // tm
    nblk = t // tm
    return pl.pallas_call(
        functools.partial(_combine_kernel, final_norm=final_norm),
        grid=(nblk,),
        in_specs=[pl.BlockSpec((tm, d), lambda i: (i, 0)),
                  pl.BlockSpec((tm, d // 2), lambda i: (i, 0)),
                  pl.BlockSpec((tm, d // 2), lambda i: (i + nblk, 0)),
                  pl.BlockSpec((tm, TOP_K), lambda i: (i, 0)),
                  pl.BlockSpec((None, 6, d), lambda i: (i // per_batch, 0, 0)),
                  pl.BlockSpec((1, d), lambda i: (0, 0))],
        out_specs=pl.BlockSpec((tm, d), lambda i: (i, 0)),
        out_shape=jax.ShapeDtypeStruct((t, d), F32),
        compiler_params=_cparams(("parallel",), 40),
    )(x2, y_pairs, y_pairs, gate_w, modl, final_w)


def kernel(x, c, w_ada, b_ada, norm1_w, norm2_w, w_in, s5_a_re, s5_a_im, s5_log_dt, s5_b_re, s5_b_im, s5_c_re, s5_c_im, s5_d, s5_w_glu, s5_b_glu, gdn_conv_w, gdn_a_log, gdn_dt_bias, gdn_norm_w, w_proj_a, w_proj_b, w_out, w_router, router_bias, w_gate, w_up, w_down, final_norm_w):
    batch, seq, d = x.shape
    depth = w_ada.shape[0]
    t = batch * seq
    s5w = s5_d.shape[1]
    gdw = w_proj_b.shape[1]
    heads = gdn_a_log.shape[1]
    assert gdw == heads * HEAD_DIM and s5w == gdw and d == 2 * gdw
    assert seq % GDN_TILE == 0 and seq % S5_CHUNK == 0
    n_levels = int(math.log2(seq // S5_CHUNK))
    assert S5_CHUNK << n_levels == seq and S5_OCT * S5_GROUP == 128

    c_pad = jnp.zeros((8, d), F32).at[:batch].set(c)
    mod = _ada(c_pad, w_ada, b_ada)

    o_u, o_q, o_k, o_v, o_z = 0, s5w, 2 * s5w, 3 * s5w, 4 * s5w
    o_ba = 5 * s5w
    o_ga = o_ba + 2 * heads
    o_gb = o_ga + d
    col_u, col_q, col_k, col_v, col_z = 4, 5, 6, 7, 8

    gdn_masks = _gdn_masks()
    s5_rep = _s5_replicate()
    s5_ops = jax.vmap(functools.partial(_s5_weights, n_levels=n_levels))(
        s5_a_re, s5_a_im, s5_log_dt, s5_b_re, s5_b_im, s5_c_re, s5_c_im)

    wr_hi = w_router.astype(BF16)
    wr_lo = (w_router - wr_hi.astype(F32)).astype(BF16)
    pad_e = ((0, 0), (0, 128 - N_EXPERTS))
    wr_t = jnp.concatenate([jnp.pad(wr_hi, pad_e), jnp.pad(wr_lo, pad_e)], axis=1)
    rb = router_bias.reshape(N_EXPERTS, 1).astype(F32)

    tm_e = min(512, t)
    n_tiles = (TOP_K * t) // tm_e + N_EXPERTS
    p_rows = n_tiles * tm_e
    pad_tok = jnp.arange(p_rows, dtype=jnp.int32) % t

    w_big = jnp.concatenate([w_in[:, :, o_ga:o_ga + d], w_in[:, :, o_gb:o_gb + d], w_in[:, :, o_u:o_ba]],
                            axis=2).astype(BF16)
    w_small = jnp.pad(w_in[:, :, o_ba:o_ga], ((0, 0), (0, 0), (0, 128 - 2 * heads))).astype(BF16)
    w_gate_b = w_gate.astype(BF16)
    w_up_b = w_up.astype(BF16)
    w_down_b = w_down.astype(BF16)

    x2 = x.reshape(t, d)
    for l in range(depth):
        modl = mod[l, :batch].reshape(batch, 6, d)
        proj, small = _inproj(x2, modl, norm1_w[l].reshape(1, d), w_big, w_small, l, seq)

        y_lin = _s5_core(proj, col_u * s5w // 128, *s5_ops, s5_rep, l, batch, seq)
        y_a = _s5_post(y_lin, proj, col_u, s5_d[l].reshape(1, s5w), s5_w_glu[l].astype(BF16),
                       s5_b_glu[l].reshape(1, s5w))

        head_params = jnp.zeros((8, 128), F32)
        head_params = head_params.at[0, heads:2 * heads].set(-jnp.exp(gdn_a_log[l]))
        head_params = head_params.at[1, heads:2 * heads].set(gdn_dt_bias[l])
        y_b = _gdn(proj, small, gdn_conv_w[l], head_params, gdn_norm_w[l].reshape(1, HEAD_DIM), gdn_masks,
                   (col_q, col_k, col_v, col_z), batch, seq, heads)

        x2, h2, route = _merge(y_a, y_b, proj, x2, modl, w_proj_a[l].astype(BF16), w_proj_b[l].astype(BF16),
                               w_out[l].astype(BF16), norm2_w[l].reshape(1, d), wr_t, rb, seq, 0, 1)

        expert_idx = route[0:TOP_K].T.astype(jnp.int32)
        gate_w = route[TOP_K:2 * TOP_K].T
        dest, tile_expert, n_used = _dispatch_plan(expert_idx, tm_e, n_tiles)
        src_tok = pad_tok.at[dest].set(jnp.arange(TOP_K * t, dtype=jnp.int32) // TOP_K)
        xs = jnp.take(h2, src_tok, axis=0, mode='clip')
        ys = _moe_experts(tile_expert, n_used, xs, w_gate_b, w_up_b, w_down_b, l, tm_e)
        y_pairs = jnp.take(ys, dest.reshape(t, TOP_K).T.reshape(-1), axis=0, mode='clip')
        x2 = _combine(x2, y_pairs, gate_w, modl, final_norm_w.reshape(1, d), seq, l == depth - 1)

    return x2.reshape(batch, seq, d)
```

```python
import functools
import math

import jax
import jax.numpy as jnp
from jax import lax
from jax.experimental import pallas as pl
from jax.experimental.pallas import tpu as pltpu

F32 = jnp.float32
BF16 = jnp.bfloat16
HIGHEST = lax.Precision.HIGHEST

RMS_EPS = 1e-6
S5_GROUP = 16
S5_STATE = 64
S5_CHUNK = 16
S5_OCT = 8
HEAD_DIM = 128
GDN_CHUNK = 64
GDN_TILE = 256
GDN_HEADS_PER_TRIP = 4
CONV_K = 4
N_EXPERTS = 16
N_GROUPS = 4
EXPERTS_PER_GROUP = N_EXPERTS // N_GROUPS
TOP_K = 2
MIB = 1024 * 1024


def _cparams(semantics, vmem_mib):
    return pltpu.CompilerParams(dimension_semantics=semantics, vmem_limit_bytes=vmem_mib * MIB)


def _silu(x):
    return x * jax.nn.sigmoid(x)


def _softplus(x):
    return jnp.maximum(x, 0.0) + jnp.log1p(jnp.exp(-jnp.abs(x)))


def _rms_scale(x):
    return lax.rsqrt(jnp.mean(x * x, axis=-1, keepdims=True) + RMS_EPS)


def _pack_halves(x):
    n = x.shape[1] // 2
    lo = lax.bitcast_convert_type(x[:, :n].astype(BF16).astype(F32), jnp.uint32)
    hi = lax.bitcast_convert_type(x[:, n:].astype(BF16).astype(F32), jnp.uint32)
    return (lo >> 16) | hi


def _unpack_halves(p):
    lo = lax.bitcast_convert_type(p << 16, F32)
    hi = lax.bitcast_convert_type(p & jnp.uint32(0xFFFF0000), F32)
    return lo, hi


def _nt_dot(a, b):
    return lax.dot_general(a, b, (((1,), (1,)), ((), ())), preferred_element_type=F32)


def _tn_dot(a, b):
    return lax.dot_general(a, b, (((0,), (0,)), ((), ())), preferred_element_type=F32)


def _ada_kernel(c_ref, w_ref, b_ref, o_ref):
    ca = _silu(c_ref[...]).astype(BF16)
    o_ref[...] = jnp.dot(ca, w_ref[...].astype(BF16), preferred_element_type=F32) + b_ref[...]


def _ada(c_pad, w_ada, b_ada):
    depth, d, n = w_ada.shape
    rows = c_pad.shape[0]
    tn = min(n, 1024)
    return pl.pallas_call(
        _ada_kernel,
        grid=(depth, n // tn),
        in_specs=[pl.BlockSpec((rows, d), lambda l, j: (0, 0)),
                  pl.BlockSpec((None, d, tn), lambda l, j: (l, 0, j)),
                  pl.BlockSpec((None, 1, tn), lambda l, j: (l, 0, j))],
        out_specs=pl.BlockSpec((None, rows, tn), lambda l, j: (l, 0, j)),
        out_shape=jax.ShapeDtypeStruct((depth, rows, n), F32),
        compiler_params=_cparams(("parallel", "parallel"), 40),
    )(c_pad, w_ada, b_ada.reshape(depth, 1, n))


def _inproj_kernel(x_ref, mod_ref, nw_ref, w_ref, wsm_ref, o_ref, osm_ref, h_scr):
    @pl.when(pl.program_id(1) == 0)
    def _():
        x = x_ref[...]
        h = x * _rms_scale(x) * nw_ref[...] * (1.0 + mod_ref[1:2, :]) + mod_ref[0:1, :]
        hb = h.astype(BF16)
        h_scr[...] = hb
        osm_ref[...] = jnp.dot(hb, wsm_ref[...], preferred_element_type=F32)

    o_ref[...] = _nt_dot(h_scr[...], w_ref[...])


def _inproj(x2, modl, nw, w_big, w_small, layer, seq):
    t, d = x2.shape
    n = w_big.shape[1]
    tm = min(seq, 1024)
    tn = 1024
    per_batch = seq // tm
    return pl.pallas_call(
        _inproj_kernel,
        grid=(t // tm, n // tn),
        in_specs=[pl.BlockSpec((tm, d), lambda i, j: (i, 0)),
                  pl.BlockSpec((None, 6, d), lambda i, j: (i // per_batch, 0, 0)),
                  pl.BlockSpec((1, d), lambda i, j: (0, 0)),
                  pl.BlockSpec((None, tn, d), lambda i, j: (layer, j, 0)),
                  pl.BlockSpec((None, d, 128), lambda i, j: (layer, 0, 0))],
        out_specs=[pl.BlockSpec((tm, tn), lambda i, j: (i, j)),
                   pl.BlockSpec((tm, 128), lambda i, j: (i, 0))],
        out_shape=[jax.ShapeDtypeStruct((t, n), F32), jax.ShapeDtypeStruct((t, 128), F32)],
        scratch_shapes=[pltpu.VMEM((tm, d), BF16)],
        compiler_params=_cparams(("parallel", "arbitrary"), 48),
    )(x2, modl, nw, w_big, w_small)


def _s5_weights(a_re, a_im, log_dt, b_re, b_im, c_re, c_im, n_levels):
    tc = S5_CHUNK
    groups, p = a_re.shape
    n = b_re.shape[-1]
    oc = S5_OCT
    octs = groups // oc
    lam = lax.complex(a_re, a_im)
    dt = jnp.exp(log_dt)[:, None]
    log_a = lam * dt
    a_bar = jnp.exp(log_a)
    b_bar = ((a_bar - 1.0) / lam)[..., None] * lax.complex(b_re, b_im)
    c_mat = lax.complex(c_re, c_im)
    apow = jnp.exp(log_a[:, :, None] * jnp.arange(tc + 1, dtype=F32)[None, None, :])
    lane = jnp.arange(tc * n)
    e_t = (jnp.arange(tc)[:, None] == (lane // n)[None, :]).astype(F32)
    e_n = (jnp.arange(n)[:, None] == (lane % n)[None, :]).astype(F32)

    def expand(z, e):
        f = lambda r: jnp.einsum('gpk,kj->gpj', r, e, precision=HIGHEST)
        return lax.complex(f(jnp.real(z)), f(jnp.imag(z)))

    def re_negim_rows(z):
        return jnp.concatenate([jnp.real(z), -jnp.imag(z)], axis=1)

    c_rep = expand(jnp.transpose(c_mat, (0, 2, 1)), e_n)
    lag_c = re_negim_rows(expand(apow[:, :, :tc], e_t) * c_rep)
    hoc = re_negim_rows(expand(apow[:, :, 1:], e_t) * c_rep)
    b_t = jnp.transpose(b_bar, (0, 2, 1))
    b_cat = jnp.concatenate([jnp.real(b_t), jnp.imag(b_t)], axis=-1)
    kern = jnp.einsum('gnk,gkj->gnj', b_cat, lag_c, precision=HIGHEST)
    tcomp = jnp.stack([jnp.pad(kern[:, :, :(tc - s) * n], ((0, 0), (0, 0), (s * n, 0))) for s in range(tc)],
                      axis=1)
    tcomp = jnp.transpose(tcomp.reshape(octs, oc, tc, n, tc * n), (0, 2, 1, 3, 4)).reshape(octs, tc * oc * n, tc * n)
    ap_rev = jnp.transpose(apow[:, :, tc - 1 - jnp.arange(tc)], (0, 2, 1))
    hin_c = ap_rev[:, :, None, :] * b_t[:, None, :, :]
    hc = jnp.concatenate([jnp.real(hin_c), jnp.imag(hin_c)], axis=-1)
    hc = jnp.transpose(hc.reshape(octs, oc, tc, n, 2 * p), (0, 2, 1, 3, 4)).reshape(octs, tc * oc * n, 2 * p)
    hoc = hoc.reshape(octs, oc * 2 * p, tc * n)
    steps = (tc * (2.0 ** jnp.arange(n_levels, dtype=F32)))
    amul = jnp.exp(log_a[:, None, :] * steps[None, :, None])
    a1 = jnp.concatenate([jnp.real(amul), jnp.real(amul)], axis=-1)
    a2 = jnp.concatenate([-jnp.imag(amul), jnp.imag(amul)], axis=-1)

    def lanes_by_group(a):
        return jnp.transpose(a.reshape(octs, oc, n_levels, 2 * p), (0, 2, 1, 3)).reshape(octs, n_levels, oc * 2 * p)

    return (tcomp.astype(BF16), hc.astype(BF16), hoc.astype(BF16),
            lanes_by_group(a1).astype(F32), lanes_by_group(a2).astype(F32))


def _s5_replicate():
    r = lax.broadcasted_iota(jnp.int32, (S5_CHUNK * S5_GROUP, S5_CHUNK * 128), 0)
    c = lax.broadcasted_iota(jnp.int32, (S5_CHUNK * S5_GROUP, S5_CHUNK * 128), 1)
    return ((r // S5_GROUP == c // 128) & (r % S5_GROUP == c % S5_GROUP)).astype(BF16)


def _s5_core_kernel(u_ref, tc_ref, hc_ref, hoc_ref, a1_ref, a2_ref, rep_ref, y_ref, toep_scr, hin_scr, hout_scr,
                    *, n_levels):
    tc = S5_CHUNK
    lanes = u_ref.shape[1]
    state_lanes = hc_ref.shape[1]

    @pl.when(pl.program_id(1) == 0)
    def _():
        kd = toep_scr.shape[1]
        rb = 256
        col_g = (lax.broadcasted_iota(jnp.int32, (rb, kd), 1) // S5_GROUP) % S5_OCT
        row_i = lax.broadcasted_iota(jnp.int32, (rb, kd), 0)
        for i in range(toep_scr.shape[0] // rb):
            full = jnp.dot(tc_ref[i * rb:(i + 1) * rb, :], rep_ref[...], preferred_element_type=F32)
            row_g = ((row_i + i * rb) // S5_GROUP) % S5_OCT
            toep_scr[i * rb:(i + 1) * rb, :] = jnp.where(row_g == col_g, full, 0.0).astype(BF16)
        for i in range(hout_scr.shape[0] // rb):
            full = jnp.dot(hoc_ref[i * rb:(i + 1) * rb, :], rep_ref[...], preferred_element_type=F32)
            row_g = (row_i + i * rb) // state_lanes
            hout_scr[i * rb:(i + 1) * rb, :] = jnp.where(row_g == col_g, full, 0.0).astype(BF16)
        hrow_g = (lax.broadcasted_iota(jnp.int32, hc_ref.shape, 0) // S5_GROUP) % S5_OCT
        for g in range(S5_OCT):
            hin_scr[:, g * state_lanes:(g + 1) * state_lanes] = jnp.where(hrow_g == g, hc_ref[...], 0.0).astype(BF16)

    rows = u_ref.shape[0] // tc
    x = jnp.concatenate([u_ref[pl.ds(t, rows, stride=tc), :] for t in range(tc)], axis=1).astype(BF16)
    s_loc = jnp.dot(x, hin_scr[...], preferred_element_type=F32)
    pos = lax.broadcasted_iota(jnp.int32, (rows, state_lanes), 0)
    s_in = []
    for g in range(S5_OCT):
        ln = slice(g * state_lanes, (g + 1) * state_lanes)
        s = s_loc[:, ln]
        for k in range(n_levels):
            d = 1 << k
            prev = jnp.where(pos >= d, pltpu.roll(s, d, axis=0), 0.0)
            s = s + a1_ref[k:k + 1, ln] * prev + a2_ref[k:k + 1, ln] * pltpu.roll(prev, state_lanes // 2, axis=1)
        s_in.append(jnp.where(pos >= 1, pltpu.roll(s, 1, axis=0), 0.0).astype(BF16))
    y = (jnp.dot(x, toep_scr[...], preferred_element_type=F32)
         + jnp.dot(jnp.concatenate(s_in, axis=1), hout_scr[...], preferred_element_type=F32))
    for t in range(tc):
        y_ref[pl.ds(t, rows, stride=tc), :] = y[:, t * lanes:(t + 1) * lanes]


def _s5_core(proj, u_lane_block, tcomp, hc, hoc, a1, a2, rep, layer, batch, seq):
    t = proj.shape[0]
    _, octs, kdim, tn = tcomp.shape
    sl = hc.shape[-1]
    sw = hoc.shape[2]
    n_levels = a1.shape[2]
    lanes = kdim // S5_CHUNK
    kern = functools.partial(_s5_core_kernel, n_levels=n_levels)
    return pl.pallas_call(
        kern,
        grid=(octs, batch),
        in_specs=[pl.BlockSpec((seq, lanes), lambda o, b: (b, u_lane_block + o)),
                  pl.BlockSpec((None, None, kdim, tn), lambda o, b: (layer, o, 0, 0)),
                  pl.BlockSpec((None, None, kdim, sl), lambda o, b: (layer, o, 0, 0)),
                  pl.BlockSpec((None, None, sw, tn), lambda o, b: (layer, o, 0, 0)),
                  pl.BlockSpec((None, None, n_levels, sw), lambda o, b: (layer, o, 0, 0)),
                  pl.BlockSpec((None, None, n_levels, sw), lambda o, b: (layer, o, 0, 0)),
                  pl.BlockSpec((tn, kdim), lambda o, b: (0, 0))],
        out_specs=pl.BlockSpec((seq, lanes), lambda o, b: (b, o)),
        out_shape=jax.ShapeDtypeStruct((t, octs * lanes), F32),
        scratch_shapes=[pltpu.VMEM((kdim, kdim), BF16), pltpu.VMEM((kdim, sw), BF16), pltpu.VMEM((sw, kdim), BF16)],
        compiler_params=_cparams(("parallel", "arbitrary"), 48),
    )(proj, tcomp, hc, hoc, a1, a2, rep)


def _s5_post_kernel(y_ref, u_ref, d_ref, w_ref, b_ref, o_ref):
    y = jax.nn.gelu(y_ref[...] + d_ref[...] * u_ref[...])
    gate = jnp.dot(y.astype(BF16), w_ref[...], preferred_element_type=F32) + b_ref[...]
    o_ref[...] = (y * jax.nn.sigmoid(gate)).astype(o_ref.dtype)


def _s5_post(y_lin, proj, u_col, d_skip, w_glu, b_glu):
    t, w = y_lin.shape
    tm = min(t, 512)
    return pl.pallas_call(
        _s5_post_kernel,
        grid=(t // tm,),
        in_specs=[pl.BlockSpec((tm, w), lambda i: (i, 0)),
                  pl.BlockSpec((tm, w), lambda i: (i, u_col)),
                  pl.BlockSpec((1, w), lambda i: (0, 0)),
                  pl.BlockSpec((w, w), lambda i: (0, 0)),
                  pl.BlockSpec((1, w), lambda i: (0, 0))],
        out_specs=pl.BlockSpec((tm, w), lambda i: (i, 0)),
        out_shape=jax.ShapeDtypeStruct((t, w), BF16),
        compiler_params=_cparams(("parallel",), 32),
    )(y_lin, proj, d_skip, w_glu, b_glu)


def _gdn_kernel(q_ref, k_ref, v_ref, qh_ref, kh_ref, vh_ref, z_ref, sm_ref, cq_ref, ck_ref, cv_ref,
                hp_ref, nw_ref, tril_ref, cmask_ref, lmask_ref, o_ref, betab, gcb, grow, st_scr,
                cat_q, cat_k, cat_v, *, heads):
    tt = GDN_TILE
    hd = HEAD_DIM
    nck = tt // GDN_CHUNK
    first = pl.program_id(1) == 0

    @pl.when(first)
    def _():
        st_scr[...] = jnp.zeros_like(st_scr)

    sm = sm_ref[...]
    hp = hp_ref[...]
    beta_all = jax.nn.sigmoid(sm)
    g_all = hp[0:1, :] * _softplus(sm + hp[1:2, :])
    gc_all = jnp.dot(tril_ref[...], g_all, precision=HIGHEST, preferred_element_type=F32)
    gc_t = gc_all.T
    for h in range(heads):
        betab[h] = jnp.broadcast_to(beta_all[:, h:h + 1], (tt, hd))
        gcb[h] = jnp.broadcast_to(gc_all[:, heads + h:heads + h + 1], (tt, hd))
        grow[h:h + 1, :] = gc_t[heads + h:heads + h + 1, :]

    eye = (lax.broadcasted_iota(jnp.int32, (tt, tt), 0) == lax.broadcasted_iota(jnp.int32, (tt, tt), 1)).astype(F32)
    n_lv = lmask_ref.shape[0]

    for x_ref, halo_ref, cat in ((q_ref, qh_ref, cat_q), (k_ref, kh_ref, cat_k), (v_ref, vh_ref, cat_v)):
        cat[0:8, :] = jnp.where(first, 0.0, halo_ref[...])
        cat[8:8 + tt, :] = x_ref[...]

    def conv_silu(cat, cw_ref, hs):
        w = cw_ref[:, pl.ds(hs, hd)]
        acc = cat[pl.ds(8, tt), pl.ds(hs, hd)] * w[CONV_K - 1:CONV_K, :]
        for s in range(1, CONV_K):
            acc = acc + cat[pl.ds(8 - s, tt), pl.ds(hs, hd)] * w[CONV_K - 1 - s:CONV_K - s, :]
        return _silu(acc)

    def heads_lockstep(hs_idx):
        each = lambda f, *ls: [f(*a) for a in zip(*ls)]
        hs = [pl.multiple_of(h * hd, hd) for h in hs_idx]
        q = [conv_silu(cat_q, cq_ref, o) for o in hs]
        k = [conv_silu(cat_k, ck_ref, o) for o in hs]
        v = [conv_silu(cat_v, cv_ref, o) for o in hs]
        qn = each(lambda a: a * (lax.rsqrt(jnp.sum(a * a, axis=-1, keepdims=True) + RMS_EPS) * (hd ** -0.5)), q)
        kn = each(lambda a: a * lax.rsqrt(jnp.sum(a * a, axis=-1, keepdims=True) + RMS_EPS), k)
        bb = [betab[h] for h in hs_idx]
        gc = [gcb[h] for h in hs_idx]
        gr = [grow[pl.ds(h, 1), :] for h in hs_idx]
        decay = each(lambda c, r: jnp.exp(jnp.where(cmask_ref[0] > 0.0, jnp.concatenate([c, c], axis=1) - r, -1e30)),
                     gc, gr)
        kb = each(lambda a, b: a * b, kn, bb)
        knb = each(lambda a: a.astype(BF16), kn)
        lfull = each(lambda a, b, dc: _nt_dot(a.astype(BF16), b) * dc, kb, knb, decay)
        lmat = each(lambda a: a.astype(BF16), lfull)
        tinv = each(lambda a: eye - a * cmask_ref[1], lfull)
        for lv in range(1, n_lv):
            d = 1 << lv
            nb = tt // d
            tb = each(lambda a: a.astype(BF16), tinv)
            bd = each(lambda m: m * lmask_ref[lv], lmat)
            if d % 8 == 0:
                odd = each(lambda a: jnp.concatenate([a[i * d:(i + 1) * d] for i in range(1, nb, 2)], axis=0), tinv)
                tl = each(lambda a, b: jnp.dot(a.astype(BF16), b, preferred_element_type=F32).astype(BF16), odd, bd)
                upd = each(lambda a, b, c: a - jnp.dot(b, c, preferred_element_type=F32), odd, tl, tb)
                tinv = each(lambda a, u: jnp.concatenate(
                    [a[i * d:(i + 1) * d] if i % 2 == 0 else u[(i // 2) * d:(i // 2 + 1) * d] for i in range(nb)],
                    axis=0), tinv, upd)
            else:
                tl = each(lambda a, b: jnp.dot(a, b, preferred_element_type=F32).astype(BF16), tb, bd)
                tinv = each(lambda a, b, c: a - jnp.dot(b, c, preferred_element_type=F32), tinv, tl, tb)
        eg = each(jnp.exp, gc)
        rhs = each(lambda a, e, b, c: jnp.concatenate([a * e, b * c], axis=1).astype(BF16), kb, eg, v, bb)
        wu = each(lambda a, b: jnp.dot(a.astype(BF16), b, preferred_element_type=F32).astype(BF16), tinv, rhs)
        attn = each(lambda a, b, dc: (_nt_dot(a.astype(BF16), b) * dc).astype(BF16), qn, knb, decay)
        awu = each(lambda a, b: jnp.dot(a, b, preferred_element_type=F32), attn, wu)
        qp = each(lambda a, e, b: (a * e - b[:, :hd]).astype(BF16), qn, eg, awu)
        s = [st_scr[h] for h in hs_idx]
        outs = [[] for _ in hs_idx]
        for c in range(nck):
            r0 = c * GDN_CHUNK
            r1 = r0 + GDN_CHUNK
            gl = each(lambda a: a[r1 - 1:r1, :], gc)
            kd = each(lambda a, l, g: (a[r0:r1] * jnp.exp(l - g[r0:r1])).astype(BF16), kn, gl, gc)
            m = each(lambda a, b: _tn_dot(a, b[r0:r1]), kd, wu)
            sb = each(lambda a: a.astype(BF16), s)
            for i, (a, b, y) in enumerate(zip(qp, sb, awu)):
                outs[i].append(jnp.dot(a[r0:r1], b, preferred_element_type=F32) + y[r0:r1, hd:])
            s = each(lambda a, l, mm, b: a * jnp.exp(l) - jnp.dot(mm[:, :hd].astype(BF16), b,
                                                                  preferred_element_type=F32) + mm[:, hd:],
                     s, gl, m, sb)
        for i, h in enumerate(hs_idx):
            st_scr[h] = s[i]
            o = jnp.concatenate(outs[i], axis=0)
            z = z_ref[:, pl.ds(hs[i], hd)]
            o_ref[:, pl.ds(hs[i], hd)] = (o * _rms_scale(o) * nw_ref[...] * _silu(z)).astype(o_ref.dtype)

    def trip(j, carry):
        heads_lockstep([GDN_HEADS_PER_TRIP * j + i for i in range(GDN_HEADS_PER_TRIP)])
        return carry

    lax.fori_loop(0, heads // GDN_HEADS_PER_TRIP, trip, 0)


def _gdn_masks():
    tt = GDN_TILE
    ri = lax.broadcasted_iota(jnp.int32, (tt, tt), 0)
    ci = lax.broadcasted_iota(jnp.int32, (tt, tt), 1)
    same = (ri // GDN_CHUNK) == (ci // GDN_CHUNK)
    causal = same & (ci <= ri)
    tril = causal.astype(F32)
    cmask = jnp.stack([tril, (((ri % 2) == 1) & (ci == ri - 1)).astype(F32)])
    levels = []
    d = 1
    while d < GDN_CHUNK:
        levels.append((((ri // d) % 2) == 1) & ((ci // d) == (ri // d) - 1))
        d *= 2
    return tril, cmask, jnp.stack(levels).astype(BF16)


def _gdn(proj, small, conv_w, head_params, norm_w, masks, cols, batch, seq, heads):
    t = proj.shape[0]
    tt = GDN_TILE
    width = heads * HEAD_DIM
    tiles = seq // tt
    cq, ck, cv, cz = cols
    tril, cmask, lmask = masks

    def cur(col):
        return pl.BlockSpec((tt, width), lambda b, i: (b * tiles + i, col))

    def halo(col):
        return pl.BlockSpec((8, width), lambda b, i: (jnp.maximum((b * tiles + i) * (tt // 8) - 1, 0), col))

    def cw(col):
        return pl.BlockSpec((CONV_K, width), lambda b, i: (0, col))

    kern = functools.partial(_gdn_kernel, heads=heads)
    return pl.pallas_call(
        kern,
        grid=(batch, tiles),
        in_specs=[cur(cq), cur(ck), cur(cv), halo(cq), halo(ck), halo(cv), cur(cz),
                  pl.BlockSpec((tt, 128), lambda b, i: (b * tiles + i, 0)),
                  cw(0), cw(1), cw(2),
                  pl.BlockSpec((8, 128), lambda b, i: (0, 0)),
                  pl.BlockSpec((1, HEAD_DIM), lambda b, i: (0, 0)),
                  pl.BlockSpec((tt, tt), lambda b, i: (0, 0)),
                  pl.BlockSpec(cmask.shape, lambda b, i: (0, 0, 0)),
                  pl.BlockSpec(lmask.shape, lambda b, i: (0, 0, 0))],
        out_specs=pl.BlockSpec((tt, width), lambda b, i: (b * tiles + i, 0)),
        out_shape=jax.ShapeDtypeStruct((t, width), BF16),
        scratch_shapes=[pltpu.VMEM((heads, tt, HEAD_DIM), F32),
                        pltpu.VMEM((heads, tt, HEAD_DIM), F32),
                        pltpu.VMEM((8, tt), F32),
                        pltpu.VMEM((heads, HEAD_DIM, HEAD_DIM), F32),
                        pltpu.VMEM((tt + 8, width), F32),
                        pltpu.VMEM((tt + 8, width), F32),
                        pltpu.VMEM((tt + 8, width), F32)],
        compiler_params=_cparams(("parallel", "arbitrary"), 40),
    )(proj, proj, proj, proj, proj, proj, proj, small, conv_w, conv_w, conv_w, head_params, norm_w,
      tril, cmask, lmask)


def _route(h, wr_hl, rb):
    h_hi = h.astype(BF16)
    h_lo = (h - h_hi.astype(F32)).astype(BF16)
    first = jnp.dot(h_hi, wr_hl[...], preferred_element_type=F32)
    logits = first[:, :128] + first[:, 128:] + jnp.dot(h_lo, wr_hl[:, :128], preferred_element_type=F32)
    logits = logits.T[:N_EXPERTS]
    scores = jax.nn.sigmoid(logits)
    sel = scores + rb
    sel_rows = [sel[e:e + 1, :] for e in range(N_EXPERTS)]
    score_rows = [scores[e:e + 1, :] for e in range(N_EXPERTS)]
    epg = EXPERTS_PER_GROUP
    best_score = None
    best_group = None
    for g in range(N_GROUPS):
        rows = sel_rows[g * epg:(g + 1) * epg]
        gs = None
        for a in range(epg):
            for b in range(a + 1, epg):
                pair = rows[a] + rows[b]
                gs = pair if gs is None else jnp.maximum(gs, pair)
        if g == 0:
            best_score, best_group = gs, jnp.zeros_like(gs)
        else:
            better = gs > best_score
            best_group = jnp.where(better, float(g), best_group)
            best_score = jnp.where(better, gs, best_score)
    in_sel = []
    in_score = []
    for j in range(epg):
        a = sel_rows[j]
        b = score_rows[j]
        for g in range(1, N_GROUPS):
            pick = best_group == float(g)
            a = jnp.where(pick, sel_rows[g * epg + j], a)
            b = jnp.where(pick, score_rows[g * epg + j], b)
        in_sel.append(a)
        in_score.append(b)
    i1, m1, w1 = jnp.zeros_like(in_sel[0]), in_sel[0], in_score[0]
    for j in range(1, epg):
        better = in_sel[j] > m1
        i1 = jnp.where(better, float(j), i1)
        m1 = jnp.where(better, in_sel[j], m1)
        w1 = jnp.where(better, in_score[j], w1)
    i2 = m2 = w2 = None
    for j in range(epg):
        cand = jnp.where(i1 == float(j), -jnp.inf, in_sel[j])
        if j == 0:
            i2, m2, w2 = jnp.zeros_like(cand), cand, in_score[0]
        else:
            better = cand > m2
            i2 = jnp.where(better, float(j), i2)
            m2 = jnp.where(better, cand, m2)
            w2 = jnp.where(better, in_score[j], w2)
    total = w1 + w2
    zero = jnp.zeros_like(w1)
    return jnp.concatenate([best_group * epg + i1, best_group * epg + i2, w1 / total, w2 / total,
                            zero, zero, zero, zero], axis=0)


def _merge_kernel(ya_ref, yb_ref, ga_ref, gb_ref, x_ref, mod_ref, wa_ref, wb_ref, wo_ref, nw_ref, wrt_ref, rb_ref,
                  o_ref, h_ref, r_ref):
    pa = jnp.dot(ya_ref[...], wa_ref[...], preferred_element_type=F32)
    pb = jnp.dot(yb_ref[...], wb_ref[...], preferred_element_type=F32)
    merged = jax.nn.sigmoid(ga_ref[...]) * pa + jax.nn.sigmoid(gb_ref[...]) * pb
    out = jnp.dot(merged.astype(BF16), wo_ref[...], preferred_element_type=F32)
    xn = x_ref[...] + mod_ref[2:3, :] * out
    o_ref[...] = xn
    h = xn * _rms_scale(xn) * nw_ref[...] * (1.0 + mod_ref[4:5, :]) + mod_ref[3:4, :]
    h_ref[...] = _pack_halves(h)
    r_ref[...] = _route(h, wrt_ref, rb_ref[...])


def _merge(ya, yb, proj, x2, modl, wa, wb, wo, nw2, wr_t, rb, seq, ga_col, gb_col):
    t, d = x2.shape
    w = ya.shape[1]
    tm = min(seq, 256)
    per_batch = seq // tm
    const = dict(pipeline_mode=pl.Buffered(1))
    return pl.pallas_call(
        _merge_kernel,
        grid=(t // tm,),
        in_specs=[pl.BlockSpec((tm, w), lambda i: (i, 0)),
                  pl.BlockSpec((tm, w), lambda i: (i, 0)),
                  pl.BlockSpec((tm, d), lambda i: (i, ga_col)),
                  pl.BlockSpec((tm, d), lambda i: (i, gb_col)),
                  pl.BlockSpec((tm, d), lambda i: (i, 0)),
                  pl.BlockSpec((None, 6, d), lambda i: (i // per_batch, 0, 0)),
                  pl.BlockSpec((w, d), lambda i: (0, 0), **const),
                  pl.BlockSpec((w, d), lambda i: (0, 0), **const),
                  pl.BlockSpec((d, d), lambda i: (0, 0), **const),
                  pl.BlockSpec((1, d), lambda i: (0, 0)),
                  pl.BlockSpec((d, 256), lambda i: (0, 0)),
                  pl.BlockSpec((N_EXPERTS, 1), lambda i: (0, 0))],
        out_specs=[pl.BlockSpec((tm, d), lambda i: (i, 0)),
                   pl.BlockSpec((tm, d // 2), lambda i: (i, 0)),
                   pl.BlockSpec((8, tm), lambda i: (0, i))],
        out_shape=[jax.ShapeDtypeStruct((t, d), F32), jax.ShapeDtypeStruct((t, d // 2), jnp.uint32),
                   jax.ShapeDtypeStruct((8, t), F32)],
        compiler_params=_cparams(("parallel",), 52),
    )(ya, yb, proj, proj, x2, modl, wa, wb, wo, nw2, wr_t, rb)


def _moe_kernel(te_ref, nu_ref, x_ref, wg_ref, wu_ref, wd_ref, o_ref):
    used = pl.program_id(0) < nu_ref[0]

    @pl.when(used)
    def _():
        x_lo, x_hi = _unpack_halves(x_ref[...])
        x_lo = x_lo.astype(BF16)
        x_hi = x_hi.astype(BF16)
        half = x_lo.shape[1]

        def up(w_ref):
            return (jnp.dot(x_lo, w_ref[:half, :], preferred_element_type=F32)
                    + jnp.dot(x_hi, w_ref[half:, :], preferred_element_type=F32))

        hid = (_silu(up(wg_ref)) * up(wu_ref)).astype(BF16)
        o_ref[...] = _pack_halves(jnp.dot(hid, wd_ref[...], preferred_element_type=F32))

    @pl.when(jnp.logical_not(used))
    def _():
        o_ref[...] = jnp.zeros_like(o_ref)


def _moe_experts(tile_expert, n_used, xs, wg, wu, wd, layer, tm):
    p, dh = xs.shape
    d = 2 * dh
    f = wg.shape[-1]
    grid_spec = pltpu.PrefetchScalarGridSpec(
        num_scalar_prefetch=2,
        grid=(p // tm,),
        in_specs=[pl.BlockSpec((tm, dh), lambda i, te, nu: (i, 0)),
                  pl.BlockSpec((None, None, d, f), lambda i, te, nu: (layer, te[i], 0, 0)),
                  pl.BlockSpec((None, None, d, f), lambda i, te, nu: (layer, te[i], 0, 0)),
                  pl.BlockSpec((None, None, f, d), lambda i, te, nu: (layer, te[i], 0, 0))],
        out_specs=pl.BlockSpec((tm, dh), lambda i, te, nu: (i, 0)),
    )
    return pl.pallas_call(
        _moe_kernel,
        grid_spec=grid_spec,
        out_shape=jax.ShapeDtypeStruct((p, dh), jnp.uint32),
        compiler_params=_cparams(("arbitrary",), 52),
    )(tile_expert, n_used, xs, wg, wu, wd)


def _dispatch_plan(expert_idx, tm, n_tiles):
    flat_e = expert_idx.reshape(-1)
    onehot = (flat_e[:, None] == jnp.arange(N_EXPERTS, dtype=jnp.int32)[None, :]).astype(jnp.int32)
    csum = jnp.cumsum(onehot, axis=0)
    rank = jnp.sum((csum - onehot) * onehot, axis=1)
    sizes = csum[-1]
    padded = ((sizes + tm - 1) // tm) * tm
    pad_end = jnp.cumsum(padded)
    pad_start = pad_end - padded
    dest = pad_start[flat_e] + rank
    tile_start = jnp.arange(n_tiles, dtype=jnp.int32) * tm
    tile_expert = jnp.minimum(jnp.sum((tile_start[:, None] >= pad_end[None, :]).astype(jnp.int32), axis=1),
                              N_EXPERTS - 1).astype(jnp.int32)
    n_used = (pad_end[-1] // tm).astype(jnp.int32).reshape(1)
    return dest.astype(jnp.int32), tile_expert, n_used


def _combine_kernel(x_ref, y0_ref, y1_ref, gw_ref, mod_ref, fw_ref, o_ref, *, final_norm):
    gw = gw_ref[...]
    y0_lo, y0_hi = _unpack_halves(y0_ref[...])
    y1_lo, y1_hi = _unpack_halves(y1_ref[...])
    moe = jnp.concatenate([gw[:, 0:1] * y0_lo + gw[:, 1:2] * y1_lo, gw[:, 0:1] * y0_hi + gw[:, 1:2] * y1_hi], axis=1)
    xn = x_ref[...] + mod_ref[5:6, :] * moe
    if final_norm:
        xn = xn * _rms_scale(xn) * fw_ref[...]
    o_ref[...] = xn


def _combine(x2, y_pairs, gate_w, modl, final_w, seq, final_norm):
    t, d = x2.shape
    tm = min(seq, 512)
    per_batch = seq // tm
    nblk = t // tm
    return pl.pallas_call(
        functools.partial(_combine_kernel, final_norm=final_norm),
        grid=(nblk,),
        in_specs=[pl.BlockSpec((tm, d), lambda i: (i, 0)),
                  pl.BlockSpec((tm, d // 2), lambda i: (i, 0)),
                  pl.BlockSpec((tm, d // 2), lambda i: (i + nblk, 0)),
                  pl.BlockSpec((tm, TOP_K), lambda i: (i, 0)),
                  pl.BlockSpec((None, 6, d), lambda i: (i // per_batch, 0, 0)),
                  pl.BlockSpec((1, d), lambda i: (0, 0))],
        out_specs=pl.BlockSpec((tm, d), lambda i: (i, 0)),
        out_shape=jax.ShapeDtypeStruct((t, d), F32),
        compiler_params=_cparams(("parallel",), 40),
    )(x2, y_pairs, y_pairs, gate_w, modl, final_w)


def kernel(x, c, w_ada, b_ada, norm1_w, norm2_w, w_in, s5_a_re, s5_a_im, s5_log_dt, s5_b_re, s5_b_im, s5_c_re, s5_c_im, s5_d, s5_w_glu, s5_b_glu, gdn_conv_w, gdn_a_log, gdn_dt_bias, gdn_norm_w, w_proj_a, w_proj_b, w_out, w_router, router_bias, w_gate, w_up, w_down, final_norm_w):
    batch, seq, d = x.shape
    depth = w_ada.shape[0]
    t = batch * seq
    s5w = s5_d.shape[1]
    gdw = w_proj_b.shape[1]
    heads = gdn_a_log.shape[1]
    assert gdw == heads * HEAD_DIM and s5w == gdw and d == 2 * gdw
    assert seq % GDN_TILE == 0 and seq % S5_CHUNK == 0
    n_levels = int(math.log2(seq // S5_CHUNK))
    assert S5_CHUNK << n_levels == seq and S5_OCT * S5_GROUP == 128

    c_pad = jnp.zeros((8, d), F32).at[:batch].set(c)
    mod = _ada(c_pad, w_ada, b_ada)

    o_u, o_q, o_k, o_v, o_z = 0, s5w, 2 * s5w, 3 * s5w, 4 * s5w
    o_ba = 5 * s5w
    o_ga = o_ba + 2 * heads
    o_gb = o_ga + d
    col_u, col_q, col_k, col_v, col_z = 4, 5, 6, 7, 8

    gdn_masks = _gdn_masks()
    s5_rep = _s5_replicate()
    s5_ops = jax.vmap(functools.partial(_s5_weights, n_levels=n_levels))(
        s5_a_re, s5_a_im, s5_log_dt, s5_b_re, s5_b_im, s5_c_re, s5_c_im)

    wr_hi = w_router.astype(BF16)
    wr_lo = (w_router - wr_hi.astype(F32)).astype(BF16)
    pad_e = ((0, 0), (0, 128 - N_EXPERTS))
    wr_t = jnp.concatenate([jnp.pad(wr_hi, pad_e), jnp.pad(wr_lo, pad_e)], axis=1)
    rb = router_bias.reshape(N_EXPERTS, 1).astype(F32)

    tm_e = min(512, t)
    n_tiles = (TOP_K * t) // tm_e + N_EXPERTS
    p_rows = n_tiles * tm_e
    pad_tok = jnp.arange(p_rows, dtype=jnp.int32) % t

    w_in_t = jnp.transpose(w_in, (0, 2, 1))
    w_big = jnp.concatenate([w_in_t[:, o_ga:o_ga + d], w_in_t[:, o_gb:o_gb + d], w_in_t[:, o_u:o_ba]],
                            axis=1).astype(BF16)
    w_small = jnp.pad(w_in[:, :, o_ba:o_ga], ((0, 0), (0, 0), (0, 128 - 2 * heads))).astype(BF16)
    w_gate_b = w_gate.astype(BF16)
    w_up_b = w_up.astype(BF16)
    w_down_b = w_down.astype(BF16)

    x2 = x.reshape(t, d)
    for l in range(depth):
        modl = mod[l, :batch].reshape(batch, 6, d)
        proj, small = _inproj(x2, modl, norm1_w[l].reshape(1, d), w_big, w_small, l, seq)

        y_lin = _s5_core(proj, col_u * s5w // 128, *s5_ops, s5_rep, l, batch, seq)
        y_a = _s5_post(y_lin, proj, col_u, s5_d[l].reshape(1, s5w), s5_w_glu[l].astype(BF16),
                       s5_b_glu[l].reshape(1, s5w))

        head_params = jnp.zeros((8, 128), F32)
        head_params = head_params.at[0, heads:2 * heads].set(-jnp.exp(gdn_a_log[l]))
        head_params = head_params.at[1, heads:2 * heads].set(gdn_dt_bias[l])
        y_b = _gdn(proj, small, gdn_conv_w[l], head_params, gdn_norm_w[l].reshape(1, HEAD_DIM), gdn_masks,
                   (col_q, col_k, col_v, col_z), batch, seq, heads)

        x2, h2, route = _merge(y_a, y_b, proj, x2, modl, w_proj_a[l].astype(BF16), w_proj_b[l].astype(BF16),
                               w_out[l].astype(BF16), norm2_w[l].reshape(1, d), wr_t, rb, seq, 0, 1)

        expert_idx = route[0:TOP_K].T.astype(jnp.int32)
        gate_w = route[TOP_K:2 * TOP_K].T
        dest, tile_expert, n_used = _dispatch_plan(expert_idx, tm_e, n_tiles)
        src_tok = pad_tok.at[dest].set(jnp.arange(TOP_K * t, dtype=jnp.int32) // TOP_K, unique_indices=True,
                                       mode='promise_in_bounds')
        xs = jnp.take(h2, src_tok, axis=0, mode='clip')
        ys = _moe_experts(tile_expert, n_used, xs, w_gate_b, w_up_b, w_down_b, l, tm_e)
        y_pairs = jnp.take(ys, dest.reshape(t, TOP_K).T.reshape(-1), axis=0, mode='clip')
        x2 = _combine(x2, y_pairs, gate_w, modl, final_norm_w.reshape(1, d), seq, l == depth - 1)

    return x2.reshape(batch, seq, d)
```

```python
import functools
import math

import jax
import jax.numpy as jnp
from jax import lax
from jax.experimental import pallas as pl
from jax.experimental.pallas import tpu as pltpu

F32 = jnp.float32
BF16 = jnp.bfloat16
HIGHEST = lax.Precision.HIGHEST

RMS_EPS = 1e-6
S5_GROUP = 16
S5_STATE = 64
S5_CHUNK = 16
S5_OCT = 8
HEAD_DIM = 128
GDN_CHUNK = 64
GDN_TILE = 256
GDN_HEADS_PER_TRIP = 8
CONV_K = 4
N_EXPERTS = 16
N_GROUPS = 4
EXPERTS_PER_GROUP = N_EXPERTS // N_GROUPS
TOP_K = 2
MIB = 1024 * 1024


def _cparams(semantics, vmem_mib):
    return pltpu.CompilerParams(dimension_semantics=semantics, vmem_limit_bytes=vmem_mib * MIB)


def _silu(x):
    return x * jax.nn.sigmoid(x)


def _softplus(x):
    return jnp.maximum(x, 0.0) + jnp.log1p(jnp.exp(-jnp.abs(x)))


def _rms_scale(x):
    return lax.rsqrt(jnp.mean(x * x, axis=-1, keepdims=True) + RMS_EPS)


def _pack_halves(x):
    n = x.shape[1] // 2
    lo = lax.bitcast_convert_type(x[:, :n].astype(BF16).astype(F32), jnp.uint32)
    hi = lax.bitcast_convert_type(x[:, n:].astype(BF16).astype(F32), jnp.uint32)
    return (lo >> 16) | hi


def _unpack_halves(p):
    lo = lax.bitcast_convert_type(p << 16, F32)
    hi = lax.bitcast_convert_type(p & jnp.uint32(0xFFFF0000), F32)
    return lo, hi


def _nt_dot(a, b):
    return lax.dot_general(a, b, (((1,), (1,)), ((), ())), preferred_element_type=F32)


def _tn_dot(a, b):
    return lax.dot_general(a, b, (((0,), (0,)), ((), ())), preferred_element_type=F32)


def _ada_kernel(c_ref, w_ref, b_ref, o_ref):
    ca = _silu(c_ref[...]).astype(BF16)
    o_ref[...] = jnp.dot(ca, w_ref[...].astype(BF16), preferred_element_type=F32) + b_ref[...]


def _ada(c_pad, w_ada, b_ada):
    depth, d, n = w_ada.shape
    rows = c_pad.shape[0]
    tn = min(n, 1024)
    return pl.pallas_call(
        _ada_kernel,
        grid=(depth, n // tn),
        in_specs=[pl.BlockSpec((rows, d), lambda l, j: (0, 0)),
                  pl.BlockSpec((None, d, tn), lambda l, j: (l, 0, j)),
                  pl.BlockSpec((None, 1, tn), lambda l, j: (l, 0, j))],
        out_specs=pl.BlockSpec((None, rows, tn), lambda l, j: (l, 0, j)),
        out_shape=jax.ShapeDtypeStruct((depth, rows, n), F32),
        compiler_params=_cparams(("parallel", "parallel"), 40),
    )(c_pad, w_ada, b_ada.reshape(depth, 1, n))


def _inproj_kernel(x_ref, mod_ref, nw_ref, w_ref, wsm_ref, o_ref, osm_ref, h_scr):
    @pl.when(pl.program_id(1) == 0)
    def _():
        x = x_ref[...]
        h = x * _rms_scale(x) * nw_ref[...] * (1.0 + mod_ref[1:2, :]) + mod_ref[0:1, :]
        hb = h.astype(BF16)
        h_scr[...] = hb
        osm_ref[...] = jnp.dot(hb, wsm_ref[...], preferred_element_type=F32)

    o_ref[...] = jnp.dot(h_scr[...], w_ref[...], preferred_element_type=F32)


def _inproj(x2, modl, nw, w_big, w_small, layer, seq):
    t, d = x2.shape
    n = w_big.shape[-1]
    tm = min(seq, 1024)
    tn = 1024
    per_batch = seq // tm
    return pl.pallas_call(
        _inproj_kernel,
        grid=(t // tm, n // tn),
        in_specs=[pl.BlockSpec((tm, d), lambda i, j: (i, 0)),
                  pl.BlockSpec((None, 6, d), lambda i, j: (i // per_batch, 0, 0)),
                  pl.BlockSpec((1, d), lambda i, j: (0, 0)),
                  pl.BlockSpec((None, d, tn), lambda i, j: (layer, 0, j)),
                  pl.BlockSpec((None, d, 128), lambda i, j: (layer, 0, 0))],
        out_specs=[pl.BlockSpec((tm, tn), lambda i, j: (i, j)),
                   pl.BlockSpec((tm, 128), lambda i, j: (i, 0))],
        out_shape=[jax.ShapeDtypeStruct((t, n), F32), jax.ShapeDtypeStruct((t, 128), F32)],
        scratch_shapes=[pltpu.VMEM((tm, d), BF16)],
        compiler_params=_cparams(("parallel", "arbitrary"), 48),
    )(x2, modl, nw, w_big, w_small)


def _s5_weights(a_re, a_im, log_dt, b_re, b_im, c_re, c_im, n_levels):
    tc = S5_CHUNK
    groups, p = a_re.shape
    n = b_re.shape[-1]
    oc = S5_OCT
    octs = groups // oc
    lam = lax.complex(a_re, a_im)
    dt = jnp.exp(log_dt)[:, None]
    log_a = lam * dt
    a_bar = jnp.exp(log_a)
    b_bar = ((a_bar - 1.0) / lam)[..., None] * lax.complex(b_re, b_im)
    c_mat = lax.complex(c_re, c_im)
    apow = jnp.exp(log_a[:, :, None] * jnp.arange(tc + 1, dtype=F32)[None, None, :])
    lane = jnp.arange(tc * n)
    e_t = (jnp.arange(tc)[:, None] == (lane // n)[None, :]).astype(F32)
    e_n = (jnp.arange(n)[:, None] == (lane % n)[None, :]).astype(F32)

    def expand(z, e):
        f = lambda r: jnp.einsum('gpk,kj->gpj', r, e, precision=HIGHEST)
        return lax.complex(f(jnp.real(z)), f(jnp.imag(z)))

    def re_negim_rows(z):
        return jnp.concatenate([jnp.real(z), -jnp.imag(z)], axis=1)

    c_rep = expand(jnp.transpose(c_mat, (0, 2, 1)), e_n)
    lag_c = re_negim_rows(expand(apow[:, :, :tc], e_t) * c_rep)
    hoc = re_negim_rows(expand(apow[:, :, 1:], e_t) * c_rep)
    b_t = jnp.transpose(b_bar, (0, 2, 1))
    b_cat = jnp.concatenate([jnp.real(b_t), jnp.imag(b_t)], axis=-1)
    kern = jnp.einsum('gnk,gkj->gnj', b_cat, lag_c, precision=HIGHEST)
    tcomp = jnp.stack([jnp.pad(kern[:, :, :(tc - s) * n], ((0, 0), (0, 0), (s * n, 0))) for s in range(tc)],
                      axis=1)
    tcomp = jnp.transpose(tcomp.reshape(octs, oc, tc, n, tc * n), (0, 2, 1, 3, 4)).reshape(octs, tc * oc * n, tc * n)
    ap_rev = jnp.transpose(apow[:, :, tc - 1 - jnp.arange(tc)], (0, 2, 1))
    hin_c = ap_rev[:, :, None, :] * b_t[:, None, :, :]
    hc = jnp.concatenate([jnp.real(hin_c), jnp.imag(hin_c)], axis=-1)
    hc = jnp.transpose(hc.reshape(octs, oc, tc, n, 2 * p), (0, 2, 1, 3, 4)).reshape(octs, tc * oc * n, 2 * p)
    hoc = hoc.reshape(octs, oc * 2 * p, tc * n)
    steps = (tc * (2.0 ** jnp.arange(n_levels, dtype=F32)))
    amul = jnp.exp(log_a[:, None, :] * steps[None, :, None])
    a1 = jnp.concatenate([jnp.real(amul), jnp.real(amul)], axis=-1)
    a2 = jnp.concatenate([-jnp.imag(amul), jnp.imag(amul)], axis=-1)

    def lanes_by_group(a):
        return jnp.transpose(a.reshape(octs, oc, n_levels, 2 * p), (0, 2, 1, 3)).reshape(octs, n_levels, oc * 2 * p)

    return (tcomp.astype(BF16), hc.astype(BF16), hoc.astype(BF16),
            lanes_by_group(a1).astype(F32), lanes_by_group(a2).astype(F32))


def _s5_replicate():
    r = lax.broadcasted_iota(jnp.int32, (S5_CHUNK * S5_GROUP, S5_CHUNK * 128), 0)
    c = lax.broadcasted_iota(jnp.int32, (S5_CHUNK * S5_GROUP, S5_CHUNK * 128), 1)
    return ((r // S5_GROUP == c // 128) & (r % S5_GROUP == c % S5_GROUP)).astype(BF16)


def _s5_core_kernel(u_ref, tc_ref, hc_ref, hoc_ref, a1_ref, a2_ref, rep_ref, y_ref, toep_scr, hin_scr, hout_scr,
                    *, n_levels):
    tc = S5_CHUNK
    lanes = u_ref.shape[1]
    state_lanes = hc_ref.shape[1]

    @pl.when(pl.program_id(1) == 0)
    def _():
        kd = toep_scr.shape[1]
        rb = 256
        col_g = (lax.broadcasted_iota(jnp.int32, (rb, kd), 1) // S5_GROUP) % S5_OCT
        row_i = lax.broadcasted_iota(jnp.int32, (rb, kd), 0)
        for i in range(toep_scr.shape[0] // rb):
            full = jnp.dot(tc_ref[i * rb:(i + 1) * rb, :], rep_ref[...], preferred_element_type=F32)
            row_g = ((row_i + i * rb) // S5_GROUP) % S5_OCT
            toep_scr[i * rb:(i + 1) * rb, :] = jnp.where(row_g == col_g, full, 0.0).astype(BF16)
        for i in range(hout_scr.shape[0] // rb):
            full = jnp.dot(hoc_ref[i * rb:(i + 1) * rb, :], rep_ref[...], preferred_element_type=F32)
            row_g = (row_i + i * rb) // state_lanes
            hout_scr[i * rb:(i + 1) * rb, :] = jnp.where(row_g == col_g, full, 0.0).astype(BF16)
        hrow_g = (lax.broadcasted_iota(jnp.int32, hc_ref.shape, 0) // S5_GROUP) % S5_OCT
        for g in range(S5_OCT):
            hin_scr[:, g * state_lanes:(g + 1) * state_lanes] = jnp.where(hrow_g == g, hc_ref[...], 0.0).astype(BF16)

    rows = u_ref.shape[0] // tc
    x = jnp.concatenate([u_ref[pl.ds(t, rows, stride=tc), :] for t in range(tc)], axis=1).astype(BF16)
    s_loc = jnp.dot(x, hin_scr[...], preferred_element_type=F32)
    pos = lax.broadcasted_iota(jnp.int32, (rows, state_lanes), 0)
    s_in = []
    for g in range(S5_OCT):
        ln = slice(g * state_lanes, (g + 1) * state_lanes)
        s = s_loc[:, ln]
        for k in range(n_levels):
            d = 1 << k
            prev = jnp.where(pos >= d, pltpu.roll(s, d, axis=0), 0.0)
            s = s + a1_ref[k:k + 1, ln] * prev + a2_ref[k:k + 1, ln] * pltpu.roll(prev, state_lanes // 2, axis=1)
        s_in.append(jnp.where(pos >= 1, pltpu.roll(s, 1, axis=0), 0.0).astype(BF16))
    y = (jnp.dot(x, toep_scr[...], preferred_element_type=F32)
         + jnp.dot(jnp.concatenate(s_in, axis=1), hout_scr[...], preferred_element_type=F32))
    for t in range(tc):
        y_ref[pl.ds(t, rows, stride=tc), :] = y[:, t * lanes:(t + 1) * lanes]


def _s5_core(proj, u_lane_block, tcomp, hc, hoc, a1, a2, rep, layer, batch, seq):
    t = proj.shape[0]
    _, octs, kdim, tn = tcomp.shape
    sl = hc.shape[-1]
    sw = hoc.shape[2]
    n_levels = a1.shape[2]
    lanes = kdim // S5_CHUNK
    kern = functools.partial(_s5_core_kernel, n_levels=n_levels)
    return pl.pallas_call(
        kern,
        grid=(octs, batch),
        in_specs=[pl.BlockSpec((seq, lanes), lambda o, b: (b, u_lane_block + o)),
                  pl.BlockSpec((None, None, kdim, tn), lambda o, b: (layer, o, 0, 0)),
                  pl.BlockSpec((None, None, kdim, sl), lambda o, b: (layer, o, 0, 0)),
                  pl.BlockSpec((None, None, sw, tn), lambda o, b: (layer, o, 0, 0)),
                  pl.BlockSpec((None, None, n_levels, sw), lambda o, b: (layer, o, 0, 0)),
                  pl.BlockSpec((None, None, n_levels, sw), lambda o, b: (layer, o, 0, 0)),
                  pl.BlockSpec((tn, kdim), lambda o, b: (0, 0))],
        out_specs=pl.BlockSpec((seq, lanes), lambda o, b: (b, o)),
        out_shape=jax.ShapeDtypeStruct((t, octs * lanes), F32),
        scratch_shapes=[pltpu.VMEM((kdim, kdim), BF16), pltpu.VMEM((kdim, sw), BF16), pltpu.VMEM((sw, kdim), BF16)],
        compiler_params=_cparams(("parallel", "arbitrary"), 48),
    )(proj, tcomp, hc, hoc, a1, a2, rep)


def _s5_post_kernel(y_ref, u_ref, d_ref, w_ref, b_ref, o_ref):
    y = jax.nn.gelu(y_ref[...] + d_ref[...] * u_ref[...])
    gate = jnp.dot(y.astype(BF16), w_ref[...], preferred_element_type=F32) + b_ref[...]
    o_ref[...] = (y * jax.nn.sigmoid(gate)).astype(o_ref.dtype)


def _s5_post(y_lin, proj, u_col, d_skip, w_glu, b_glu):
    t, w = y_lin.shape
    tm = min(t, 512)
    return pl.pallas_call(
        _s5_post_kernel,
        grid=(t // tm,),
        in_specs=[pl.BlockSpec((tm, w), lambda i: (i, 0)),
                  pl.BlockSpec((tm, w), lambda i: (i, u_col)),
                  pl.BlockSpec((1, w), lambda i: (0, 0)),
                  pl.BlockSpec((w, w), lambda i: (0, 0)),
                  pl.BlockSpec((1, w), lambda i: (0, 0))],
        out_specs=pl.BlockSpec((tm, w), lambda i: (i, 0)),
        out_shape=jax.ShapeDtypeStruct((t, w), BF16),
        compiler_params=_cparams(("parallel",), 32),
    )(y_lin, proj, d_skip, w_glu, b_glu)


def _gdn_kernel(q_ref, k_ref, v_ref, qh_ref, kh_ref, vh_ref, z_ref, sm_ref, cq_ref, ck_ref, cv_ref,
                hp_ref, nw_ref, tril_ref, cmask_ref, lmask_ref, o_ref, betab, gcb, grow, st_scr,
                cat_q, cat_k, cat_v, *, heads):
    tt = GDN_TILE
    hd = HEAD_DIM
    nck = tt // GDN_CHUNK
    first = pl.program_id(1) == 0

    @pl.when(first)
    def _():
        st_scr[...] = jnp.zeros_like(st_scr)

    sm = sm_ref[...]
    hp = hp_ref[...]
    beta_all = jax.nn.sigmoid(sm)
    g_all = hp[0:1, :] * _softplus(sm + hp[1:2, :])
    gc_all = jnp.dot(tril_ref[...], g_all, precision=HIGHEST, preferred_element_type=F32)
    gc_t = gc_all.T
    for h in range(heads):
        betab[h] = jnp.broadcast_to(beta_all[:, h:h + 1], (tt, hd))
        gcb[h] = jnp.broadcast_to(gc_all[:, heads + h:heads + h + 1], (tt, hd))
        grow[h:h + 1, :] = gc_t[heads + h:heads + h + 1, :]

    eye = (lax.broadcasted_iota(jnp.int32, (tt, tt), 0) == lax.broadcasted_iota(jnp.int32, (tt, tt), 1)).astype(F32)
    n_lv = lmask_ref.shape[0]

    for x_ref, halo_ref, cat in ((q_ref, qh_ref, cat_q), (k_ref, kh_ref, cat_k), (v_ref, vh_ref, cat_v)):
        cat[0:8, :] = jnp.where(first, 0.0, halo_ref[...])
        cat[8:8 + tt, :] = x_ref[...]

    def conv_silu(cat, cw_ref, hs):
        w = cw_ref[:, pl.ds(hs, hd)]
        acc = cat[pl.ds(8, tt), pl.ds(hs, hd)] * w[CONV_K - 1:CONV_K, :]
        for s in range(1, CONV_K):
            acc = acc + cat[pl.ds(8 - s, tt), pl.ds(hs, hd)] * w[CONV_K - 1 - s:CONV_K - s, :]
        return _silu(acc)

    def heads_lockstep(hs_idx):
        each = lambda f, *ls: [f(*a) for a in zip(*ls)]
        hs = [pl.multiple_of(h * hd, hd) for h in hs_idx]
        q = [conv_silu(cat_q, cq_ref, o) for o in hs]
        k = [conv_silu(cat_k, ck_ref, o) for o in hs]
        v = [conv_silu(cat_v, cv_ref, o) for o in hs]
        qn = each(lambda a: a * (lax.rsqrt(jnp.sum(a * a, axis=-1, keepdims=True) + RMS_EPS) * (hd ** -0.5)), q)
        kn = each(lambda a: a * lax.rsqrt(jnp.sum(a * a, axis=-1, keepdims=True) + RMS_EPS), k)
        bb = [betab[h] for h in hs_idx]
        gc = [gcb[h] for h in hs_idx]
        gr = [grow[pl.ds(h, 1), :] for h in hs_idx]
        decay = each(lambda c, r: jnp.exp(jnp.where(cmask_ref[0] > 0.0, jnp.concatenate([c, c], axis=1) - r, -1e30)),
                     gc, gr)
        kb = each(lambda a, b: a * b, kn, bb)
        knb = each(lambda a: a.astype(BF16), kn)
        lfull = each(lambda a, b, dc: _nt_dot(a.astype(BF16), b) * dc, kb, knb, decay)
        lmat = each(lambda a: a.astype(BF16), lfull)
        tinv = each(lambda a: eye - a * cmask_ref[1], lfull)
        for lv in range(1, n_lv):
            d = 1 << lv
            nb = tt // d
            tb = each(lambda a: a.astype(BF16), tinv)
            bd = each(lambda m: m * lmask_ref[lv], lmat)
            if d % 8 == 0:
                odd = each(lambda a: jnp.concatenate([a[i * d:(i + 1) * d] for i in range(1, nb, 2)], axis=0), tinv)
                tl = each(lambda a, b: jnp.dot(a.astype(BF16), b, preferred_element_type=F32).astype(BF16), odd, bd)
                upd = each(lambda a, b, c: a - jnp.dot(b, c, preferred_element_type=F32), odd, tl, tb)
                tinv = each(lambda a, u: jnp.concatenate(
                    [a[i * d:(i + 1) * d] if i % 2 == 0 else u[(i // 2) * d:(i // 2 + 1) * d] for i in range(nb)],
                    axis=0), tinv, upd)
            else:
                tl = each(lambda a, b: jnp.dot(a, b, preferred_element_type=F32).astype(BF16), tb, bd)
                tinv = each(lambda a, b, c: a - jnp.dot(b, c, preferred_element_type=F32), tinv, tl, tb)
        eg = each(jnp.exp, gc)
        rhs = each(lambda a, e, b, c: jnp.concatenate([a * e, b * c], axis=1).astype(BF16), kb, eg, v, bb)
        wu = each(lambda a, b: jnp.dot(a.astype(BF16), b, preferred_element_type=F32).astype(BF16), tinv, rhs)
        attn = each(lambda a, b, dc: (_nt_dot(a.astype(BF16), b) * dc).astype(BF16), qn, knb, decay)
        awu = each(lambda a, b: jnp.dot(a, b, preferred_element_type=F32), attn, wu)
        qp = each(lambda a, e, b: (a * e - b[:, :hd]).astype(BF16), qn, eg, awu)
        s = [st_scr[h] for h in hs_idx]
        outs = [[] for _ in hs_idx]
        for c in range(nck):
            r0 = c * GDN_CHUNK
            r1 = r0 + GDN_CHUNK
            gl = each(lambda a: a[r1 - 1:r1, :], gc)
            kd = each(lambda a, l, g: (a[r0:r1] * jnp.exp(l - g[r0:r1])).astype(BF16), kn, gl, gc)
            m = each(lambda a, b: _tn_dot(a, b[r0:r1]), kd, wu)
            sb = each(lambda a: a.astype(BF16), s)
            for i, (a, b, y) in enumerate(zip(qp, sb, awu)):
                outs[i].append(jnp.dot(a[r0:r1], b, preferred_element_type=F32) + y[r0:r1, hd:])
            s = each(lambda a, l, mm, b: a * jnp.exp(l) - jnp.dot(mm[:, :hd].astype(BF16), b,
                                                                  preferred_element_type=F32) + mm[:, hd:],
                     s, gl, m, sb)
        for i, h in enumerate(hs_idx):
            st_scr[h] = s[i]
            o = jnp.concatenate(outs[i], axis=0)
            z = z_ref[:, pl.ds(hs[i], hd)]
            o_ref[:, pl.ds(hs[i], hd)] = (o * _rms_scale(o) * nw_ref[...] * _silu(z)).astype(o_ref.dtype)

    def trip(j, carry):
        heads_lockstep([GDN_HEADS_PER_TRIP * j + i for i in range(GDN_HEADS_PER_TRIP)])
        return carry

    lax.fori_loop(0, heads // GDN_HEADS_PER_TRIP, trip, 0)


def _gdn_masks():
    tt = GDN_TILE
    ri = lax.broadcasted_iota(jnp.int32, (tt, tt), 0)
    ci = lax.broadcasted_iota(jnp.int32, (tt, tt), 1)
    same = (ri // GDN_CHUNK) == (ci // GDN_CHUNK)
    causal = same & (ci <= ri)
    tril = causal.astype(F32)
    cmask = jnp.stack([tril, (((ri % 2) == 1) & (ci == ri - 1)).astype(F32)])
    levels = []
    d = 1
    while d < GDN_CHUNK:
        levels.append((((ri // d) % 2) == 1) & ((ci // d) == (ri // d) - 1))
        d *= 2
    return tril, cmask, jnp.stack(levels).astype(BF16)


def _gdn(proj, small, conv_w, head_params, norm_w, masks, cols, batch, seq, heads):
    t = proj.shape[0]
    tt = GDN_TILE
    width = heads * HEAD_DIM
    tiles = seq // tt
    cq, ck, cv, cz = cols
    tril, cmask, lmask = masks

    def cur(col):
        return pl.BlockSpec((tt, width), lambda b, i: (b * tiles + i, col))

    def halo(col):
        return pl.BlockSpec((8, width), lambda b, i: (jnp.maximum((b * tiles + i) * (tt // 8) - 1, 0), col))

    def cw(col):
        return pl.BlockSpec((CONV_K, width), lambda b, i: (0, col))

    kern = functools.partial(_gdn_kernel, heads=heads)
    return pl.pallas_call(
        kern,
        grid=(batch, tiles),
        in_specs=[cur(cq), cur(ck), cur(cv), halo(cq), halo(ck), halo(cv), cur(cz),
                  pl.BlockSpec((tt, 128), lambda b, i: (b * tiles + i, 0)),
                  cw(0), cw(1), cw(2),
                  pl.BlockSpec((8, 128), lambda b, i: (0, 0)),
                  pl.BlockSpec((1, HEAD_DIM), lambda b, i: (0, 0)),
                  pl.BlockSpec((tt, tt), lambda b, i: (0, 0)),
                  pl.BlockSpec(cmask.shape, lambda b, i: (0, 0, 0)),
                  pl.BlockSpec(lmask.shape, lambda b, i: (0, 0, 0))],
        out_specs=pl.BlockSpec((tt, width), lambda b, i: (b * tiles + i, 0)),
        out_shape=jax.ShapeDtypeStruct((t, width), BF16),
        scratch_shapes=[pltpu.VMEM((heads, tt, HEAD_DIM), F32),
                        pltpu.VMEM((heads, tt, HEAD_DIM), F32),
                        pltpu.VMEM((8, tt), F32),
                        pltpu.VMEM((heads, HEAD_DIM, HEAD_DIM), F32),
                        pltpu.VMEM((tt + 8, width), F32),
                        pltpu.VMEM((tt + 8, width), F32),
                        pltpu.VMEM((tt + 8, width), F32)],
        compiler_params=_cparams(("parallel", "arbitrary"), 40),
    )(proj, proj, proj, proj, proj, proj, proj, small, conv_w, conv_w, conv_w, head_params, norm_w,
      tril, cmask, lmask)


def _route(h, wr_hl, rb):
    h_hi = h.astype(BF16)
    h_lo = (h - h_hi.astype(F32)).astype(BF16)
    first = jnp.dot(h_hi, wr_hl[...], preferred_element_type=F32)
    logits = first[:, :128] + first[:, 128:] + jnp.dot(h_lo, wr_hl[:, :128], preferred_element_type=F32)
    logits = logits.T[:N_EXPERTS]
    scores = jax.nn.sigmoid(logits)
    sel = scores + rb
    sel_rows = [sel[e:e + 1, :] for e in range(N_EXPERTS)]
    score_rows = [scores[e:e + 1, :] for e in range(N_EXPERTS)]
    epg = EXPERTS_PER_GROUP
    best_score = None
    best_group = None
    for g in range(N_GROUPS):
        rows = sel_rows[g * epg:(g + 1) * epg]
        gs = None
        for a in range(epg):
            for b in range(a + 1, epg):
                pair = rows[a] + rows[b]
                gs = pair if gs is None else jnp.maximum(gs, pair)
        if g == 0:
            best_score, best_group = gs, jnp.zeros_like(gs)
        else:
            better = gs > best_score
            best_group = jnp.where(better, float(g), best_group)
            best_score = jnp.where(better, gs, best_score)
    in_sel = []
    in_score = []
    for j in range(epg):
        a = sel_rows[j]
        b = score_rows[j]
        for g in range(1, N_GROUPS):
            pick = best_group == float(g)
            a = jnp.where(pick, sel_rows[g * epg + j], a)
            b = jnp.where(pick, score_rows[g * epg + j], b)
        in_sel.append(a)
        in_score.append(b)
    i1, m1, w1 = jnp.zeros_like(in_sel[0]), in_sel[0], in_score[0]
    for j in range(1, epg):
        better = in_sel[j] > m1
        i1 = jnp.where(better, float(j), i1)
        m1 = jnp.where(better, in_sel[j], m1)
        w1 = jnp.where(better, in_score[j], w1)
    i2 = m2 = w2 = None
    for j in range(epg):
        cand = jnp.where(i1 == float(j), -jnp.inf, in_sel[j])
        if j == 0:
            i2, m2, w2 = jnp.zeros_like(cand), cand, in_score[0]
        else:
            better = cand > m2
            i2 = jnp.where(better, float(j), i2)
            m2 = jnp.where(better, cand, m2)
            w2 = jnp.where(better, in_score[j], w2)
    total = w1 + w2
    zero = jnp.zeros_like(w1)
    return jnp.concatenate([best_group * epg + i1, best_group * epg + i2, w1 / total, w2 / total,
                            zero, zero, zero, zero], axis=0)


def _merge_kernel(ya_ref, yb_ref, ga_ref, gb_ref, x_ref, mod_ref, wa_ref, wb_ref, wo_ref, nw_ref, wrt_ref, rb_ref,
                  o_ref, h_ref, r_ref):
    pa = jnp.dot(ya_ref[...], wa_ref[...], preferred_element_type=F32)
    pb = jnp.dot(yb_ref[...], wb_ref[...], preferred_element_type=F32)
    merged = jax.nn.sigmoid(ga_ref[...]) * pa + jax.nn.sigmoid(gb_ref[...]) * pb
    out = jnp.dot(merged.astype(BF16), wo_ref[...], preferred_element_type=F32)
    xn = x_ref[...] + mod_ref[2:3, :] * out
    o_ref[...] = xn
    h = xn * _rms_scale(xn) * nw_ref[...] * (1.0 + mod_ref[4:5, :]) + mod_ref[3:4, :]
    h_ref[...] = _pack_halves(h)
    r_ref[...] = _route(h, wrt_ref, rb_ref[...])


def _merge(ya, yb, proj, x2, modl, wa, wb, wo, nw2, wr_t, rb, seq, ga_col, gb_col):
    t, d = x2.shape
    w = ya.shape[1]
    tm = min(seq, 256)
    per_batch = seq // tm
    const = dict(pipeline_mode=pl.Buffered(1))
    return pl.pallas_call(
        _merge_kernel,
        grid=(t // tm,),
        in_specs=[pl.BlockSpec((tm, w), lambda i: (i, 0)),
                  pl.BlockSpec((tm, w), lambda i: (i, 0)),
                  pl.BlockSpec((tm, d), lambda i: (i, ga_col)),
                  pl.BlockSpec((tm, d), lambda i: (i, gb_col)),
                  pl.BlockSpec((tm, d), lambda i: (i, 0)),
                  pl.BlockSpec((None, 6, d), lambda i: (i // per_batch, 0, 0)),
                  pl.BlockSpec((w, d), lambda i: (0, 0), **const),
                  pl.BlockSpec((w, d), lambda i: (0, 0), **const),
                  pl.BlockSpec((d, d), lambda i: (0, 0), **const),
                  pl.BlockSpec((1, d), lambda i: (0, 0)),
                  pl.BlockSpec((d, 256), lambda i: (0, 0)),
                  pl.BlockSpec((N_EXPERTS, 1), lambda i: (0, 0))],
        out_specs=[pl.BlockSpec((tm, d), lambda i: (i, 0)),
                   pl.BlockSpec((tm, d // 2), lambda i: (i, 0)),
                   pl.BlockSpec((8, tm), lambda i: (0, i))],
        out_shape=[jax.ShapeDtypeStruct((t, d), F32), jax.ShapeDtypeStruct((t, d // 2), jnp.uint32),
                   jax.ShapeDtypeStruct((8, t), F32)],
        compiler_params=_cparams(("parallel",), 52),
    )(ya, yb, proj, proj, x2, modl, wa, wb, wo, nw2, wr_t, rb)


def _moe_kernel(te_ref, nu_ref, x_ref, wg_ref, wu_ref, wd_ref, o_ref):
    used = pl.program_id(0) < nu_ref[0]

    @pl.when(used)
    def _():
        x_lo, x_hi = _unpack_halves(x_ref[...])
        x_lo = x_lo.astype(BF16)
        x_hi = x_hi.astype(BF16)
        half = x_lo.shape[1]

        def up(w_ref):
            return (jnp.dot(x_lo, w_ref[:half, :], preferred_element_type=F32)
                    + jnp.dot(x_hi, w_ref[half:, :], preferred_element_type=F32))

        hid = (_silu(up(wg_ref)) * up(wu_ref)).astype(BF16)
        o_ref[...] = _pack_halves(jnp.dot(hid, wd_ref[...], preferred_element_type=F32))

    @pl.when(jnp.logical_not(used))
    def _():
        o_ref[...] = jnp.zeros_like(o_ref)


def _cast_kernel(w_ref, o_ref):
    o_ref[...] = w_ref[...].astype(o_ref.dtype)


def _expert_weights_bf16(w, layer):
    _, e, a, b = w.shape
    ta = min(a, (1 << 20) // b)
    return pl.pallas_call(
        _cast_kernel,
        grid=(e, a // ta),
        in_specs=[pl.BlockSpec((None, None, ta, b), lambda i, j: (layer, i, j, 0))],
        out_specs=pl.BlockSpec((None, ta, b), lambda i, j: (i, j, 0)),
        out_shape=jax.ShapeDtypeStruct((e, a, b), BF16),
        compiler_params=_cparams(("parallel", "parallel"), 32),
    )(w)


def _moe_experts(tile_expert, n_used, xs, wg, wu, wd, tm):
    p, dh = xs.shape
    d = 2 * dh
    f = wg.shape[-1]
    grid_spec = pltpu.PrefetchScalarGridSpec(
        num_scalar_prefetch=2,
        grid=(p // tm,),
        in_specs=[pl.BlockSpec((tm, dh), lambda i, te, nu: (i, 0)),
                  pl.BlockSpec((None, d, f), lambda i, te, nu: (te[i], 0, 0)),
                  pl.BlockSpec((None, d, f), lambda i, te, nu: (te[i], 0, 0)),
                  pl.BlockSpec((None, f, d), lambda i, te, nu: (te[i], 0, 0))],
        out_specs=pl.BlockSpec((tm, dh), lambda i, te, nu: (i, 0)),
    )
    return pl.pallas_call(
        _moe_kernel,
        grid_spec=grid_spec,
        out_shape=jax.ShapeDtypeStruct((p, dh), jnp.uint32),
        compiler_params=_cparams(("arbitrary",), 52),
    )(tile_expert, n_used, xs, wg, wu, wd)


def _dispatch_plan(expert_idx, tm, n_tiles):
    flat_e = expert_idx.reshape(-1)
    onehot = (flat_e[:, None] == jnp.arange(N_EXPERTS, dtype=jnp.int32)[None, :]).astype(jnp.int32)
    csum = jnp.cumsum(onehot, axis=0)
    rank = jnp.sum((csum - onehot) * onehot, axis=1)
    sizes = csum[-1]
    padded = ((sizes + tm - 1) // tm) * tm
    pad_end = jnp.cumsum(padded)
    pad_start = pad_end - padded
    dest = pad_start[flat_e] + rank
    tile_start = jnp.arange(n_tiles, dtype=jnp.int32) * tm
    tile_expert = jnp.minimum(jnp.sum((tile_start[:, None] >= pad_end[None, :]).astype(jnp.int32), axis=1),
                              N_EXPERTS - 1).astype(jnp.int32)
    n_used = (pad_end[-1] // tm).astype(jnp.int32).reshape(1)
    return dest.astype(jnp.int32), tile_expert, n_used


def _combine_kernel(x_ref, y0_ref, y1_ref, gw_ref, mod_ref, fw_ref, o_ref, *, final_norm):
    gw = gw_ref[...]
    y0_lo, y0_hi = _unpack_halves(y0_ref[...])
    y1_lo, y1_hi = _unpack_halves(y1_ref[...])
    moe = jnp.concatenate([gw[:, 0:1] * y0_lo + gw[:, 1:2] * y1_lo, gw[:, 0:1] * y0_hi + gw[:, 1:2] * y1_hi], axis=1)
    xn = x_ref[...] + mod_ref[5:6, :] * moe
    if final_norm:
        xn = xn * _rms_scale(xn) * fw_ref[...]
    o_ref[...] = xn


def _combine(x2, y_pairs, gate_w, modl, final_w, seq, final_norm):
    t, d = x2.shape
    tm = min(seq, 512)
    per_batch = seq // tm
    nblk = t // tm
    return pl.pallas_call(
        functools.partial(_combine_kernel, final_norm=final_norm),
        grid=(nblk,),
        in_specs=[pl.BlockSpec((tm, d), lambda i: (i, 0)),
                  pl.BlockSpec((tm, d // 2), lambda i: (i, 0)),
                  pl.BlockSpec((tm, d // 2), lambda i: (i + nblk, 0)),
                  pl.BlockSpec((tm, TOP_K), lambda i: (i, 0)),
                  pl.BlockSpec((None, 6, d), lambda i: (i // per_batch, 0, 0)),
                  pl.BlockSpec((1, d), lambda i: (0, 0))],
        out_specs=pl.BlockSpec((tm, d), lambda i: (i, 0)),
        out_shape=jax.ShapeDtypeStruct((t, d), F32),
        compiler_params=_cparams(("parallel",), 40),
    )(x2, y_pairs, y_pairs, gate_w, modl, final_w)


def kernel(x, c, w_ada, b_ada, norm1_w, norm2_w, w_in, s5_a_re, s5_a_im, s5_log_dt, s5_b_re, s5_b_im, s5_c_re, s5_c_im, s5_d, s5_w_glu, s5_b_glu, gdn_conv_w, gdn_a_log, gdn_dt_bias, gdn_norm_w, w_proj_a, w_proj_b, w_out, w_router, router_bias, w_gate, w_up, w_down, final_norm_w):
    batch, seq, d = x.shape
    depth = w_ada.shape[0]
    t = batch * seq
    s5w = s5_d.shape[1]
    gdw = w_proj_b.shape[1]
    heads = gdn_a_log.shape[1]
    assert gdw == heads * HEAD_DIM and s5w == gdw and d == 2 * gdw
    assert seq % GDN_TILE == 0 and seq % S5_CHUNK == 0
    n_levels = int(math.log2(seq // S5_CHUNK))
    assert S5_CHUNK << n_levels == seq and S5_OCT * S5_GROUP == 128

    c_pad = jnp.zeros((8, d), F32).at[:batch].set(c)
    mod = _ada(c_pad, w_ada, b_ada)

    o_u, o_q, o_k, o_v, o_z = 0, s5w, 2 * s5w, 3 * s5w, 4 * s5w
    o_ba = 5 * s5w
    o_ga = o_ba + 2 * heads
    o_gb = o_ga + d
    col_u, col_q, col_k, col_v, col_z = 4, 5, 6, 7, 8

    gdn_masks = _gdn_masks()
    s5_rep = _s5_replicate()
    s5_ops = jax.vmap(functools.partial(_s5_weights, n_levels=n_levels))(
        s5_a_re, s5_a_im, s5_log_dt, s5_b_re, s5_b_im, s5_c_re, s5_c_im)

    wr_hi = w_router.astype(BF16)
    wr_lo = (w_router - wr_hi.astype(F32)).astype(BF16)
    pad_e = ((0, 0), (0, 128 - N_EXPERTS))
    wr_t = jnp.concatenate([jnp.pad(wr_hi, pad_e), jnp.pad(wr_lo, pad_e)], axis=1)
    rb = router_bias.reshape(N_EXPERTS, 1).astype(F32)

    tm_e = min(512, t)
    n_tiles = (TOP_K * t) // tm_e + N_EXPERTS
    p_rows = n_tiles * tm_e
    pad_tok = jnp.arange(p_rows, dtype=jnp.int32) % t

    w_big = jnp.concatenate([w_in[:, :, o_ga:o_ga + d], w_in[:, :, o_gb:o_gb + d], w_in[:, :, o_u:o_ba]],
                            axis=2).astype(BF16)
    w_small = jnp.pad(w_in[:, :, o_ba:o_ga], ((0, 0), (0, 0), (0, 128 - 2 * heads))).astype(BF16)

    x2 = x.reshape(t, d)
    for l in range(depth):
        modl = mod[l, :batch].reshape(batch, 6, d)
        proj, small = _inproj(x2, modl, norm1_w[l].reshape(1, d), w_big, w_small, l, seq)

        y_lin = _s5_core(proj, col_u * s5w // 128, *s5_ops, s5_rep, l, batch, seq)
        y_a = _s5_post(y_lin, proj, col_u, s5_d[l].reshape(1, s5w), s5_w_glu[l].astype(BF16),
                       s5_b_glu[l].reshape(1, s5w))

        head_params = jnp.zeros((8, 128), F32)
        head_params = head_params.at[0, heads:2 * heads].set(-jnp.exp(gdn_a_log[l]))
        head_params = head_params.at[1, heads:2 * heads].set(gdn_dt_bias[l])
        y_b = _gdn(proj, small, gdn_conv_w[l], head_params, gdn_norm_w[l].reshape(1, HEAD_DIM), gdn_masks,
                   (col_q, col_k, col_v, col_z), batch, seq, heads)

        x2, h2, route = _merge(y_a, y_b, proj, x2, modl, w_proj_a[l].astype(BF16), w_proj_b[l].astype(BF16),
                               w_out[l].astype(BF16), norm2_w[l].reshape(1, d), wr_t, rb, seq, 0, 1)

        expert_idx = route[0:TOP_K].T.astype(jnp.int32)
        gate_w = route[TOP_K:2 * TOP_K].T
        dest, tile_expert, n_used = _dispatch_plan(expert_idx, tm_e, n_tiles)
        src_tok = pad_tok.at[dest].set(jnp.arange(TOP_K * t, dtype=jnp.int32) // TOP_K)
        xs = jnp.take(h2, src_tok, axis=0, mode='clip')
        wg_l, wu_l, wd_l = (_expert_weights_bf16(w, l) for w in (w_gate, w_up, w_down))
        ys = _moe_experts(tile_expert, n_used, xs, wg_l, wu_l, wd_l, tm_e)
        y_pairs = jnp.take(ys, dest.reshape(t, TOP_K).T.reshape(-1), axis=0, mode='clip')
        x2 = _combine(x2, y_pairs, gate_w, modl, final_norm_w.reshape(1, d), seq, l == depth - 1)

    return x2.reshape(batch, seq, d)
```

```python
import functools
import math

import jax
import jax.numpy as jnp
from jax import lax
from jax.experimental import pallas as pl
from jax.experimental.pallas import tpu as pltpu

F32 = jnp.float32
BF16 = jnp.bfloat16
HIGHEST = lax.Precision.HIGHEST

RMS_EPS = 1e-6
S5_GROUP = 16
S5_STATE = 64
S5_CHUNK = 16
S5_OCT = 8
HEAD_DIM = 128
GDN_CHUNK = 64
GDN_TILE = 256
GDN_HEADS_PER_TRIP = 8
CONV_K = 4
N_EXPERTS = 16
N_GROUPS = 4
EXPERTS_PER_GROUP = N_EXPERTS // N_GROUPS
TOP_K = 2
MIB = 1024 * 1024


def _cparams(semantics, vmem_mib):
    return pltpu.CompilerParams(dimension_semantics=semantics, vmem_limit_bytes=vmem_mib * MIB)


def _silu(x):
    return x * jax.nn.sigmoid(x)


def _softplus(x):
    return jnp.maximum(x, 0.0) + jnp.log(1.0 + jnp.exp(-jnp.abs(x)))


def _rms_scale(x):
    return lax.rsqrt(jnp.mean(x * x, axis=-1, keepdims=True) + RMS_EPS)


def _pack_halves(x):
    n = x.shape[1] // 2
    lo = lax.bitcast_convert_type(x[:, :n].astype(BF16).astype(F32), jnp.uint32)
    hi = lax.bitcast_convert_type(x[:, n:].astype(BF16).astype(F32), jnp.uint32)
    return (lo >> 16) | hi


def _unpack_halves(p):
    lo = lax.bitcast_convert_type(p << 16, F32)
    hi = lax.bitcast_convert_type(p & jnp.uint32(0xFFFF0000), F32)
    return lo, hi


def _nt_dot(a, b):
    return lax.dot_general(a, b, (((1,), (1,)), ((), ())), preferred_element_type=F32)


def _tn_dot(a, b):
    return lax.dot_general(a, b, (((0,), (0,)), ((), ())), preferred_element_type=F32)


def _ada_kernel(c_ref, w_ref, b_ref, o_ref):
    ca = _silu(c_ref[...]).astype(BF16)
    o_ref[...] = jnp.dot(ca, w_ref[...].astype(BF16), preferred_element_type=F32) + b_ref[...]


def _ada(c_pad, w_ada, b_ada):
    depth, d, n = w_ada.shape
    rows = c_pad.shape[0]
    tn = min(n, 1024)
    return pl.pallas_call(
        _ada_kernel,
        grid=(depth, n // tn),
        in_specs=[pl.BlockSpec((rows, d), lambda l, j: (0, 0)),
                  pl.BlockSpec((None, d, tn), lambda l, j: (l, 0, j)),
                  pl.BlockSpec((None, 1, tn), lambda l, j: (l, 0, j))],
        out_specs=pl.BlockSpec((None, rows, tn), lambda l, j: (l, 0, j)),
        out_shape=jax.ShapeDtypeStruct((depth, rows, n), F32),
        compiler_params=_cparams(("parallel", "parallel"), 40),
    )(c_pad, w_ada, b_ada.reshape(depth, 1, n))


def _inproj_kernel(x_ref, mod_ref, nw_ref, w_ref, wsm_ref, o_ref, osm_ref, h_scr):
    @pl.when(pl.program_id(1) == 0)
    def _():
        x = x_ref[...]
        h = x * _rms_scale(x) * nw_ref[...] * (1.0 + mod_ref[1:2, :]) + mod_ref[0:1, :]
        hb = h.astype(BF16)
        h_scr[...] = hb
        osm_ref[...] = jnp.dot(hb, wsm_ref[...], preferred_element_type=F32)

    o_ref[...] = jnp.dot(h_scr[...], w_ref[...], preferred_element_type=F32)


def _inproj(x2, modl, nw, w_big, w_small, layer, seq):
    t, d = x2.shape
    n = w_big.shape[-1]
    tm = min(seq, 1024)
    tn = 1536
    per_batch = seq // tm
    return pl.pallas_call(
        _inproj_kernel,
        grid=(t // tm, n // tn),
        in_specs=[pl.BlockSpec((tm, d), lambda i, j: (i, 0)),
                  pl.BlockSpec((None, 6, d), lambda i, j: (i // per_batch, 0, 0)),
                  pl.BlockSpec((1, d), lambda i, j: (0, 0)),
                  pl.BlockSpec((None, d, tn), lambda i, j: (layer, 0, j)),
                  pl.BlockSpec((None, d, 128), lambda i, j: (layer, 0, 0))],
        out_specs=[pl.BlockSpec((tm, tn), lambda i, j: (i, j)),
                   pl.BlockSpec((tm, 128), lambda i, j: (i, 0))],
        out_shape=[jax.ShapeDtypeStruct((t, n), F32), jax.ShapeDtypeStruct((t, 128), F32)],
        scratch_shapes=[pltpu.VMEM((tm, d), BF16)],
        compiler_params=_cparams(("parallel", "arbitrary"), 52),
    )(x2, modl, nw, w_big, w_small)


def _s5_weights(a_re, a_im, log_dt, b_re, b_im, c_re, c_im, n_levels):
    tc = S5_CHUNK
    groups, p = a_re.shape
    n = b_re.shape[-1]
    oc = S5_OCT
    octs = groups // oc
    lam = lax.complex(a_re, a_im)
    dt = jnp.exp(log_dt)[:, None]
    log_a = lam * dt
    a_bar = jnp.exp(log_a)
    b_bar = ((a_bar - 1.0) / lam)[..., None] * lax.complex(b_re, b_im)
    c_mat = lax.complex(c_re, c_im)
    apow = jnp.exp(log_a[:, :, None] * jnp.arange(tc + 1, dtype=F32)[None, None, :])
    lane = jnp.arange(tc * n)
    e_t = (jnp.arange(tc)[:, None] == (lane // n)[None, :]).astype(F32)
    e_n = (jnp.arange(n)[:, None] == (lane % n)[None, :]).astype(F32)

    def expand(z, e):
        f = lambda r: jnp.einsum('gpk,kj->gpj', r, e, precision=HIGHEST)
        return lax.complex(f(jnp.real(z)), f(jnp.imag(z)))

    def re_negim_rows(z):
        return jnp.concatenate([jnp.real(z), -jnp.imag(z)], axis=1)

    c_rep = expand(jnp.transpose(c_mat, (0, 2, 1)), e_n)
    lag_c = re_negim_rows(expand(apow[:, :, :tc], e_t) * c_rep)
    hoc = re_negim_rows(expand(apow[:, :, 1:], e_t) * c_rep)
    b_t = jnp.transpose(b_bar, (0, 2, 1))
    b_cat = jnp.concatenate([jnp.real(b_t), jnp.imag(b_t)], axis=-1)
    kern = jnp.einsum('gnk,gkj->gnj', b_cat, lag_c, precision=HIGHEST)
    tcomp = jnp.stack([jnp.pad(kern[:, :, :(tc - s) * n], ((0, 0), (0, 0), (s * n, 0))) for s in range(tc)],
                      axis=1)
    tcomp = jnp.transpose(tcomp.reshape(octs, oc, tc, n, tc * n), (0, 2, 1, 3, 4)).reshape(octs, tc * oc * n, tc * n)
    ap_rev = jnp.transpose(apow[:, :, tc - 1 - jnp.arange(tc)], (0, 2, 1))
    hin_c = ap_rev[:, :, None, :] * b_t[:, None, :, :]
    hc = jnp.concatenate([jnp.real(hin_c), jnp.imag(hin_c)], axis=-1)
    hc = jnp.transpose(hc.reshape(octs, oc, tc, n, 2 * p), (0, 2, 1, 3, 4)).reshape(octs, tc * oc * n, 2 * p)
    hoc = hoc.reshape(octs, oc * 2 * p, tc * n)
    steps = (tc * (2.0 ** jnp.arange(n_levels, dtype=F32)))
    amul = jnp.exp(log_a[:, None, :] * steps[None, :, None])
    a1 = jnp.concatenate([jnp.real(amul), jnp.real(amul)], axis=-1)
    a2 = jnp.concatenate([-jnp.imag(amul), jnp.imag(amul)], axis=-1)

    def lanes_by_group(a):
        return jnp.transpose(a.reshape(octs, oc, n_levels, 2 * p), (0, 2, 1, 3)).reshape(octs, n_levels, oc * 2 * p)

    return (tcomp.astype(BF16), hc.astype(BF16), hoc.astype(BF16),
            lanes_by_group(a1).astype(F32), lanes_by_group(a2).astype(F32))


def _s5_replicate():
    r = lax.broadcasted_iota(jnp.int32, (S5_CHUNK * S5_GROUP, S5_CHUNK * 128), 0)
    c = lax.broadcasted_iota(jnp.int32, (S5_CHUNK * S5_GROUP, S5_CHUNK * 128), 1)
    return ((r // S5_GROUP == c // 128) & (r % S5_GROUP == c % S5_GROUP)).astype(BF16)


def _s5_core_kernel(u_ref, tc_ref, hc_ref, hoc_ref, a1_ref, a2_ref, rep_ref, y_ref, toep_scr, hin_scr, hout_scr,
                    *, n_levels):
    tc = S5_CHUNK
    lanes = u_ref.shape[1]
    state_lanes = hc_ref.shape[1]

    @pl.when(pl.program_id(1) == 0)
    def _():
        kd = toep_scr.shape[1]
        rb = 256
        col_g = (lax.broadcasted_iota(jnp.int32, (rb, kd), 1) // S5_GROUP) % S5_OCT
        row_i = lax.broadcasted_iota(jnp.int32, (rb, kd), 0)
        for i in range(toep_scr.shape[0] // rb):
            full = jnp.dot(tc_ref[i * rb:(i + 1) * rb, :], rep_ref[...], preferred_element_type=F32)
            row_g = ((row_i + i * rb) // S5_GROUP) % S5_OCT
            toep_scr[i * rb:(i + 1) * rb, :] = jnp.where(row_g == col_g, full, 0.0).astype(BF16)
        for i in range(hout_scr.shape[0] // rb):
            full = jnp.dot(hoc_ref[i * rb:(i + 1) * rb, :], rep_ref[...], preferred_element_type=F32)
            row_g = (row_i + i * rb) // state_lanes
            hout_scr[i * rb:(i + 1) * rb, :] = jnp.where(row_g == col_g, full, 0.0).astype(BF16)
        hrow_g = (lax.broadcasted_iota(jnp.int32, hc_ref.shape, 0) // S5_GROUP) % S5_OCT
        for g in range(S5_OCT):
            hin_scr[:, g * state_lanes:(g + 1) * state_lanes] = jnp.where(hrow_g == g, hc_ref[...], 0.0).astype(BF16)

    rows = u_ref.shape[0] // tc
    x = jnp.concatenate([u_ref[pl.ds(t, rows, stride=tc), :] for t in range(tc)], axis=1).astype(BF16)
    s_loc = jnp.dot(x, hin_scr[...], preferred_element_type=F32)
    pos = lax.broadcasted_iota(jnp.int32, (rows, state_lanes), 0)
    s_in = []
    for g in range(S5_OCT):
        ln = slice(g * state_lanes, (g + 1) * state_lanes)
        s = s_loc[:, ln]
        for k in range(n_levels):
            d = 1 << k
            prev = jnp.where(pos >= d, pltpu.roll(s, d, axis=0), 0.0)
            s = s + a1_ref[k:k + 1, ln] * prev + a2_ref[k:k + 1, ln] * pltpu.roll(prev, state_lanes // 2, axis=1)
        s_in.append(jnp.where(pos >= 1, pltpu.roll(s, 1, axis=0), 0.0).astype(BF16))
    y = (jnp.dot(x, toep_scr[...], preferred_element_type=F32)
         + jnp.dot(jnp.concatenate(s_in, axis=1), hout_scr[...], preferred_element_type=F32))
    for t in range(tc):
        y_ref[pl.ds(t, rows, stride=tc), :] = y[:, t * lanes:(t + 1) * lanes]


def _s5_core(proj, u_lane_block, tcomp, hc, hoc, a1, a2, rep, layer, batch, seq):
    t = proj.shape[0]
    _, octs, kdim, tn = tcomp.shape
    sl = hc.shape[-1]
    sw = hoc.shape[2]
    n_levels = a1.shape[2]
    lanes = kdim // S5_CHUNK
    kern = functools.partial(_s5_core_kernel, n_levels=n_levels)
    return pl.pallas_call(
        kern,
        grid=(octs, batch),
        in_specs=[pl.BlockSpec((seq, lanes), lambda o, b: (b, u_lane_block + o)),
                  pl.BlockSpec((None, None, kdim, tn), lambda o, b: (layer, o, 0, 0)),
                  pl.BlockSpec((None, None, kdim, sl), lambda o, b: (layer, o, 0, 0)),
                  pl.BlockSpec((None, None, sw, tn), lambda o, b: (layer, o, 0, 0)),
                  pl.BlockSpec((None, None, n_levels, sw), lambda o, b: (layer, o, 0, 0)),
                  pl.BlockSpec((None, None, n_levels, sw), lambda o, b: (layer, o, 0, 0)),
                  pl.BlockSpec((tn, kdim), lambda o, b: (0, 0))],
        out_specs=pl.BlockSpec((seq, lanes), lambda o, b: (b, o)),
        out_shape=jax.ShapeDtypeStruct((t, octs * lanes), F32),
        scratch_shapes=[pltpu.VMEM((kdim, kdim), BF16), pltpu.VMEM((kdim, sw), BF16), pltpu.VMEM((sw, kdim), BF16)],
        compiler_params=_cparams(("parallel", "arbitrary"), 48),
    )(proj, tcomp, hc, hoc, a1, a2, rep)


def _s5_post_kernel(y_ref, u_ref, d_ref, w_ref, b_ref, o_ref):
    y = jax.nn.gelu(y_ref[...] + d_ref[...] * u_ref[...])
    gate = jnp.dot(y.astype(BF16), w_ref[...], preferred_element_type=F32) + b_ref[...]
    o_ref[...] = (y * jax.nn.sigmoid(gate)).astype(o_ref.dtype)


def _s5_post(y_lin, proj, u_col, d_skip, w_glu, b_glu):
    t, w = y_lin.shape
    tm = min(t, 512)
    return pl.pallas_call(
        _s5_post_kernel,
        grid=(t // tm,),
        in_specs=[pl.BlockSpec((tm, w), lambda i: (i, 0)),
                  pl.BlockSpec((tm, w), lambda i: (i, u_col)),
                  pl.BlockSpec((1, w), lambda i: (0, 0)),
                  pl.BlockSpec((w, w), lambda i: (0, 0)),
                  pl.BlockSpec((1, w), lambda i: (0, 0))],
        out_specs=pl.BlockSpec((tm, w), lambda i: (i, 0)),
        out_shape=jax.ShapeDtypeStruct((t, w), BF16),
        compiler_params=_cparams(("parallel",), 32),
    )(y_lin, proj, d_skip, w_glu, b_glu)


def _gdn_kernel(q_ref, k_ref, v_ref, qh_ref, kh_ref, vh_ref, z_ref, sm_ref, cq_ref, ck_ref, cv_ref,
                hp_ref, nw_ref, tril_ref, cmask_ref, lmask_ref, o_ref, betab, gcb, grow, st_scr,
                cat_q, cat_k, cat_v, *, heads):
    tt = GDN_TILE
    hd = HEAD_DIM
    nck = tt // GDN_CHUNK
    first = pl.program_id(1) == 0

    @pl.when(first)
    def _():
        st_scr[...] = jnp.zeros_like(st_scr)

    sm = sm_ref[...]
    hp = hp_ref[...]
    beta_all = jax.nn.sigmoid(sm)
    g_all = hp[0:1, :] * _softplus(sm + hp[1:2, :])
    gc_all = jnp.dot(tril_ref[...], g_all, precision=HIGHEST, preferred_element_type=F32)
    gc_t = gc_all.T
    for h in range(heads):
        betab[h] = jnp.broadcast_to(beta_all[:, h:h + 1], (tt, hd))
        gcb[h] = jnp.broadcast_to(gc_all[:, heads + h:heads + h + 1], (tt, hd))
        grow[h:h + 1, :] = gc_t[heads + h:heads + h + 1, :]

    eye = (lax.broadcasted_iota(jnp.int32, (tt, tt), 0) == lax.broadcasted_iota(jnp.int32, (tt, tt), 1)).astype(F32)
    n_lv = lmask_ref.shape[0]

    for x_ref, halo_ref, cat in ((q_ref, qh_ref, cat_q), (k_ref, kh_ref, cat_k), (v_ref, vh_ref, cat_v)):
        cat[0:8, :] = jnp.where(first, 0.0, halo_ref[...])
        cat[8:8 + tt, :] = x_ref[...]

    def conv_silu(cat, cw_ref, hs):
        w = cw_ref[:, pl.ds(hs, hd)]
        acc = cat[pl.ds(8, tt), pl.ds(hs, hd)] * w[CONV_K - 1:CONV_K, :]
        for s in range(1, CONV_K):
            acc = acc + cat[pl.ds(8 - s, tt), pl.ds(hs, hd)] * w[CONV_K - 1 - s:CONV_K - s, :]
        return _silu(acc)

    def heads_lockstep(hs_idx):
        each = lambda f, *ls: [f(*a) for a in zip(*ls)]
        hs = [pl.multiple_of(h * hd, hd) for h in hs_idx]
        q = [conv_silu(cat_q, cq_ref, o) for o in hs]
        k = [conv_silu(cat_k, ck_ref, o) for o in hs]
        v = [conv_silu(cat_v, cv_ref, o) for o in hs]
        qn = each(lambda a: a * (lax.rsqrt(jnp.sum(a * a, axis=-1, keepdims=True) + RMS_EPS) * (hd ** -0.5)), q)
        kn = each(lambda a: a * lax.rsqrt(jnp.sum(a * a, axis=-1, keepdims=True) + RMS_EPS), k)
        bb = [betab[h] for h in hs_idx]
        gc = [gcb[h] for h in hs_idx]
        gr = [grow[pl.ds(h, 1), :] for h in hs_idx]
        decay = each(lambda c, r: jnp.exp(jnp.where(cmask_ref[0] > 0.0, jnp.concatenate([c, c], axis=1) - r, -1e30)),
                     gc, gr)
        kb = each(lambda a, b: a * b, kn, bb)
        knb = each(lambda a: a.astype(BF16), kn)
        lfull = each(lambda a, b, dc: _nt_dot(a.astype(BF16), b) * dc, kb, knb, decay)
        lmat = each(lambda a: a.astype(BF16), lfull)
        tinv = each(lambda a: eye - a * cmask_ref[1], lfull)
        for lv in range(1, n_lv):
            d = 1 << lv
            nb = tt // d
            tb = each(lambda a: a.astype(BF16), tinv)
            bd = each(lambda m: m * lmask_ref[lv], lmat)
            if d % 8 == 0:
                odd = each(lambda a: jnp.concatenate([a[i * d:(i + 1) * d] for i in range(1, nb, 2)], axis=0), tinv)
                tl = each(lambda a, b: jnp.dot(a.astype(BF16), b, preferred_element_type=F32).astype(BF16), odd, bd)
                upd = each(lambda a, b, c: a - jnp.dot(b, c, preferred_element_type=F32), odd, tl, tb)
                tinv = each(lambda a, u: jnp.concatenate(
                    [a[i * d:(i + 1) * d] if i % 2 == 0 else u[(i // 2) * d:(i // 2 + 1) * d] for i in range(nb)],
                    axis=0), tinv, upd)
            else:
                tl = each(lambda a, b: jnp.dot(a, b, preferred_element_type=F32).astype(BF16), tb, bd)
                tinv = each(lambda a, b, c: a - jnp.dot(b, c, preferred_element_type=F32), tinv, tl, tb)
        eg = each(jnp.exp, gc)
        rhs = each(lambda a, e, b, c: jnp.concatenate([a * e, b * c], axis=1).astype(BF16), kb, eg, v, bb)
        wu = each(lambda a, b: jnp.dot(a.astype(BF16), b, preferred_element_type=F32).astype(BF16), tinv, rhs)
        attn = each(lambda a, b, dc: (_nt_dot(a.astype(BF16), b) * dc).astype(BF16), qn, knb, decay)
        awu = each(lambda a, b: jnp.dot(a, b, preferred_element_type=F32), attn, wu)
        qp = each(lambda a, e, b: (a * e - b[:, :hd]).astype(BF16), qn, eg, awu)
        s = [st_scr[h] for h in hs_idx]
        outs = [[] for _ in hs_idx]
        for c in range(nck):
            r0 = c * GDN_CHUNK
            r1 = r0 + GDN_CHUNK
            gl = each(lambda a: a[r1 - 1:r1, :], gc)
            kd = each(lambda a, l, g: (a[r0:r1] * jnp.exp(l - g[r0:r1])).astype(BF16), kn, gl, gc)
            m = each(lambda a, b: _tn_dot(a, b[r0:r1]), kd, wu)
            sb = each(lambda a: a.astype(BF16), s)
            for i, (a, b, y) in enumerate(zip(qp, sb, awu)):
                outs[i].append(jnp.dot(a[r0:r1], b, preferred_element_type=F32) + y[r0:r1, hd:])
            s = each(lambda a, l, mm, b: a * jnp.exp(l) - jnp.dot(mm[:, :hd].astype(BF16), b,
                                                                  preferred_element_type=F32) + mm[:, hd:],
                     s, gl, m, sb)
        for i, h in enumerate(hs_idx):
            st_scr[h] = s[i]
            o = jnp.concatenate(outs[i], axis=0)
            z = z_ref[:, pl.ds(hs[i], hd)]
            o_ref[:, pl.ds(hs[i], hd)] = (o * _rms_scale(o) * nw_ref[...] * _silu(z)).astype(o_ref.dtype)

    def trip(j, carry):
        heads_lockstep([GDN_HEADS_PER_TRIP * j + i for i in range(GDN_HEADS_PER_TRIP)])
        return carry

    lax.fori_loop(0, heads // GDN_HEADS_PER_TRIP, trip, 0)


def _gdn_masks():
    tt = GDN_TILE
    ri = lax.broadcasted_iota(jnp.int32, (tt, tt), 0)
    ci = lax.broadcasted_iota(jnp.int32, (tt, tt), 1)
    same = (ri // GDN_CHUNK) == (ci // GDN_CHUNK)
    causal = same & (ci <= ri)
    tril = causal.astype(F32)
    cmask = jnp.stack([tril, (((ri % 2) == 1) & (ci == ri - 1)).astype(F32)])
    levels = []
    d = 1
    while d < GDN_CHUNK:
        levels.append((((ri // d) % 2) == 1) & ((ci // d) == (ri // d) - 1))
        d *= 2
    return tril, cmask, jnp.stack(levels).astype(BF16)


def _gdn(proj, small, conv_w, head_params, norm_w, masks, cols, batch, seq, heads):
    t = proj.shape[0]
    tt = GDN_TILE
    width = heads * HEAD_DIM
    tiles = seq // tt
    cq, ck, cv, cz = cols
    tril, cmask, lmask = masks

    def cur(col):
        return pl.BlockSpec((tt, width), lambda b, i: (b * tiles + i, col))

    def halo(col):
        return pl.BlockSpec((8, width), lambda b, i: (jnp.maximum((b * tiles + i) * (tt // 8) - 1, 0), col))

    def cw(col):
        return pl.BlockSpec((CONV_K, width), lambda b, i: (0, col))

    kern = functools.partial(_gdn_kernel, heads=heads)
    return pl.pallas_call(
        kern,
        grid=(batch, tiles),
        in_specs=[cur(cq), cur(ck), cur(cv), halo(cq), halo(ck), halo(cv), cur(cz),
                  pl.BlockSpec((tt, 128), lambda b, i: (b * tiles + i, 0)),
                  cw(0), cw(1), cw(2),
                  pl.BlockSpec((8, 128), lambda b, i: (0, 0)),
                  pl.BlockSpec((1, HEAD_DIM), lambda b, i: (0, 0)),
                  pl.BlockSpec((tt, tt), lambda b, i: (0, 0)),
                  pl.BlockSpec(cmask.shape, lambda b, i: (0, 0, 0)),
                  pl.BlockSpec(lmask.shape, lambda b, i: (0, 0, 0))],
        out_specs=pl.BlockSpec((tt, width), lambda b, i: (b * tiles + i, 0)),
        out_shape=jax.ShapeDtypeStruct((t, width), BF16),
        scratch_shapes=[pltpu.VMEM((heads, tt, HEAD_DIM), F32),
                        pltpu.VMEM((heads, tt, HEAD_DIM), F32),
                        pltpu.VMEM((8, tt), F32),
                        pltpu.VMEM((heads, HEAD_DIM, HEAD_DIM), F32),
                        pltpu.VMEM((tt + 8, width), F32),
                        pltpu.VMEM((tt + 8, width), F32),
                        pltpu.VMEM((tt + 8, width), F32)],
        compiler_params=_cparams(("parallel", "arbitrary"), 40),
    )(proj, proj, proj, proj, proj, proj, proj, small, conv_w, conv_w, conv_w, head_params, norm_w,
      tril, cmask, lmask)


def _route(h, wr_hl, rb):
    h_hi = h.astype(BF16)
    h_lo = (h - h_hi.astype(F32)).astype(BF16)
    first = jnp.dot(h_hi, wr_hl[...], preferred_element_type=F32)
    logits = first[:, :128] + first[:, 128:] + jnp.dot(h_lo, wr_hl[:, :128], preferred_element_type=F32)
    logits = logits.T[:N_EXPERTS]
    scores = jax.nn.sigmoid(logits)
    sel = scores + rb
    sel_rows = [sel[e:e + 1, :] for e in range(N_EXPERTS)]
    score_rows = [scores[e:e + 1, :] for e in range(N_EXPERTS)]
    epg = EXPERTS_PER_GROUP
    best_score = None
    best_group = None
    for g in range(N_GROUPS):
        rows = sel_rows[g * epg:(g + 1) * epg]
        gs = None
        for a in range(epg):
            for b in range(a + 1, epg):
                pair = rows[a] + rows[b]
                gs = pair if gs is None else jnp.maximum(gs, pair)
        if g == 0:
            best_score, best_group = gs, jnp.zeros_like(gs)
        else:
            better = gs > best_score
            best_group = jnp.where(better, float(g), best_group)
            best_score = jnp.where(better, gs, best_score)
    in_sel = []
    in_score = []
    for j in range(epg):
        a = sel_rows[j]
        b = score_rows[j]
        for g in range(1, N_GROUPS):
            pick = best_group == float(g)
            a = jnp.where(pick, sel_rows[g * epg + j], a)
            b = jnp.where(pick, score_rows[g * epg + j], b)
        in_sel.append(a)
        in_score.append(b)
    i1, m1, w1 = jnp.zeros_like(in_sel[0]), in_sel[0], in_score[0]
    for j in range(1, epg):
        better = in_sel[j] > m1
        i1 = jnp.where(better, float(j), i1)
        m1 = jnp.where(better, in_sel[j], m1)
        w1 = jnp.where(better, in_score[j], w1)
    i2 = m2 = w2 = None
    for j in range(epg):
        cand = jnp.where(i1 == float(j), -jnp.inf, in_sel[j])
        if j == 0:
            i2, m2, w2 = jnp.zeros_like(cand), cand, in_score[0]
        else:
            better = cand > m2
            i2 = jnp.where(better, float(j), i2)
            m2 = jnp.where(better, cand, m2)
            w2 = jnp.where(better, in_score[j], w2)
    total = w1 + w2
    zero = jnp.zeros_like(w1)
    return jnp.concatenate([best_group * epg + i1, best_group * epg + i2, w1 / total, w2 / total,
                            zero, zero, zero, zero], axis=0)


def _merge_kernel(ya_ref, yb_ref, ga_ref, gb_ref, x_ref, mod_ref, wa_ref, wb_ref, wo_ref, nw_ref, wrt_ref, rb_ref,
                  o_ref, h_ref, r_ref):
    pa = jnp.dot(ya_ref[...], wa_ref[...], preferred_element_type=F32)
    pb = jnp.dot(yb_ref[...], wb_ref[...], preferred_element_type=F32)
    merged = jax.nn.sigmoid(ga_ref[...]) * pa + jax.nn.sigmoid(gb_ref[...]) * pb
    out = jnp.dot(merged.astype(BF16), wo_ref[...], preferred_element_type=F32)
    xn = x_ref[...] + mod_ref[2:3, :] * out
    o_ref[...] = xn
    h = xn * _rms_scale(xn) * nw_ref[...] * (1.0 + mod_ref[4:5, :]) + mod_ref[3:4, :]
    h_ref[...] = _pack_halves(h)
    r_ref[...] = _route(h, wrt_ref, rb_ref[...])


def _merge(ya, yb, proj, x2, modl, wa, wb, wo, nw2, wr_t, rb, seq, ga_col, gb_col):
    t, d = x2.shape
    w = ya.shape[1]
    tm = min(seq, 256)
    per_batch = seq // tm
    const = dict(pipeline_mode=pl.Buffered(1))
    return pl.pallas_call(
        _merge_kernel,
        grid=(t // tm,),
        in_specs=[pl.BlockSpec((tm, w), lambda i: (i, 0)),
                  pl.BlockSpec((tm, w), lambda i: (i, 0)),
                  pl.BlockSpec((tm, d), lambda i: (i, ga_col)),
                  pl.BlockSpec((tm, d), lambda i: (i, gb_col)),
                  pl.BlockSpec((tm, d), lambda i: (i, 0)),
                  pl.BlockSpec((None, 6, d), lambda i: (i // per_batch, 0, 0)),
                  pl.BlockSpec((w, d), lambda i: (0, 0), **const),
                  pl.BlockSpec((w, d), lambda i: (0, 0), **const),
                  pl.BlockSpec((d, d), lambda i: (0, 0), **const),
                  pl.BlockSpec((1, d), lambda i: (0, 0)),
                  pl.BlockSpec((d, 256), lambda i: (0, 0)),
                  pl.BlockSpec((N_EXPERTS, 1), lambda i: (0, 0))],
        out_specs=[pl.BlockSpec((tm, d), lambda i: (i, 0)),
                   pl.BlockSpec((tm, d // 2), lambda i: (i, 0)),
                   pl.BlockSpec((8, tm), lambda i: (0, i))],
        out_shape=[jax.ShapeDtypeStruct((t, d), F32), jax.ShapeDtypeStruct((t, d // 2), jnp.uint32),
                   jax.ShapeDtypeStruct((8, t), F32)],
        compiler_params=_cparams(("parallel",), 52),
    )(ya, yb, proj, proj, x2, modl, wa, wb, wo, nw2, wr_t, rb)


def _moe_kernel(te_ref, nu_ref, x_ref, wg_ref, wu_ref, wd_ref, o_ref):
    used = pl.program_id(0) < nu_ref[0]

    @pl.when(used)
    def _():
        x_lo, x_hi = _unpack_halves(x_ref[...])
        x_lo = x_lo.astype(BF16)
        x_hi = x_hi.astype(BF16)
        half = x_lo.shape[1]

        def up(w_ref):
            return (jnp.dot(x_lo, w_ref[:half, :], preferred_element_type=F32)
                    + jnp.dot(x_hi, w_ref[half:, :], preferred_element_type=F32))

        hid = (_silu(up(wg_ref)) * up(wu_ref)).astype(BF16)
        o_ref[...] = _pack_halves(jnp.dot(hid, wd_ref[...], preferred_element_type=F32))

    @pl.when(jnp.logical_not(used))
    def _():
        o_ref[...] = jnp.zeros_like(o_ref)


def _cast_kernel(w_ref, o_ref):
    o_ref[...] = w_ref[...].astype(o_ref.dtype)


def _expert_weights_bf16(w, layer):
    _, e, a, b = w.shape
    ta = min(a, (1 << 20) // b)
    return pl.pallas_call(
        _cast_kernel,
        grid=(e, a // ta),
        in_specs=[pl.BlockSpec((None, None, ta, b), lambda i, j: (layer, i, j, 0))],
        out_specs=pl.BlockSpec((None, ta, b), lambda i, j: (i, j, 0)),
        out_shape=jax.ShapeDtypeStruct((e, a, b), BF16),
        compiler_params=_cparams(("parallel", "parallel"), 32),
    )(w)


def _moe_experts(tile_expert, n_used, xs, wg, wu, wd, tm):
    p, dh = xs.shape
    d = 2 * dh
    f = wg.shape[-1]
    grid_spec = pltpu.PrefetchScalarGridSpec(
        num_scalar_prefetch=2,
        grid=(p // tm,),
        in_specs=[pl.BlockSpec((tm, dh), lambda i, te, nu: (i, 0)),
                  pl.BlockSpec((None, d, f), lambda i, te, nu: (te[i], 0, 0)),
                  pl.BlockSpec((None, d, f), lambda i, te, nu: (te[i], 0, 0)),
                  pl.BlockSpec((None, f, d), lambda i, te, nu: (te[i], 0, 0))],
        out_specs=pl.BlockSpec((tm, dh), lambda i, te, nu: (i, 0)),
    )
    return pl.pallas_call(
        _moe_kernel,
        grid_spec=grid_spec,
        out_shape=jax.ShapeDtypeStruct((p, dh), jnp.uint32),
        compiler_params=_cparams(("arbitrary",), 52),
    )(tile_expert, n_used, xs, wg, wu, wd)


def _dispatch_plan(expert_idx, tm, n_tiles):
    flat_e = expert_idx.reshape(-1)
    onehot = (flat_e[:, None] == jnp.arange(N_EXPERTS, dtype=jnp.int32)[None, :]).astype(jnp.int32)
    csum = jnp.cumsum(onehot, axis=0)
    rank = jnp.sum((csum - onehot) * onehot, axis=1)
    sizes = csum[-1]
    padded = ((sizes + tm - 1) // tm) * tm
    pad_end = jnp.cumsum(padded)
    pad_start = pad_end - padded
    dest = pad_start[flat_e] + rank
    tile_start = jnp.arange(n_tiles, dtype=jnp.int32) * tm
    tile_expert = jnp.minimum(jnp.sum((tile_start[:, None] >= pad_end[None, :]).astype(jnp.int32), axis=1),
                              N_EXPERTS - 1).astype(jnp.int32)
    n_used = (pad_end[-1] // tm).astype(jnp.int32).reshape(1)
    return dest.astype(jnp.int32), tile_expert, n_used


def _combine_kernel(x_ref, y0_ref, y1_ref, gw_ref, mod_ref, fw_ref, o_ref, *, final_norm):
    gw = gw_ref[...]
    y0_lo, y0_hi = _unpack_halves(y0_ref[...])
    y1_lo, y1_hi = _unpack_halves(y1_ref[...])
    moe = jnp.concatenate([gw[:, 0:1] * y0_lo + gw[:, 1:2] * y1_lo, gw[:, 0:1] * y0_hi + gw[:, 1:2] * y1_hi], axis=1)
    xn = x_ref[...] + mod_ref[5:6, :] * moe
    if final_norm:
        xn = xn * _rms_scale(xn) * fw_ref[...]
    o_ref[...] = xn


def _combine(x2, y_pairs, gate_w, modl, final_w, seq, final_norm):
    t, d = x2.shape
    tm = min(seq, 512)
    per_batch = seq // tm
    nblk = t // tm
    return pl.pallas_call(
        functools.partial(_combine_kernel, final_norm=final_norm),
        grid=(nblk,),
        in_specs=[pl.BlockSpec((tm, d), lambda i: (i, 0)),
                  pl.BlockSpec((tm, d // 2), lambda i: (i, 0)),
                  pl.BlockSpec((tm, d // 2), lambda i: (i + nblk, 0)),
                  pl.BlockSpec((tm, TOP_K), lambda i: (i, 0)),
                  pl.BlockSpec((None, 6, d), lambda i: (i // per_batch, 0, 0)),
                  pl.BlockSpec((1, d), lambda i: (0, 0))],
        out_specs=pl.BlockSpec((tm, d), lambda i: (i, 0)),
        out_shape=jax.ShapeDtypeStruct((t, d), F32),
        compiler_params=_cparams(("parallel",), 40),
    )(x2, y_pairs, y_pairs, gate_w, modl, final_w)


def kernel(x, c, w_ada, b_ada, norm1_w, norm2_w, w_in, s5_a_re, s5_a_im, s5_log_dt, s5_b_re, s5_b_im, s5_c_re, s5_c_im, s5_d, s5_w_glu, s5_b_glu, gdn_conv_w, gdn_a_log, gdn_dt_bias, gdn_norm_w, w_proj_a, w_proj_b, w_out, w_router, router_bias, w_gate, w_up, w_down, final_norm_w):
    batch, seq, d = x.shape
    depth = w_ada.shape[0]
    t = batch * seq
    s5w = s5_d.shape[1]
    gdw = w_proj_b.shape[1]
    heads = gdn_a_log.shape[1]
    assert gdw == heads * HEAD_DIM and s5w == gdw and d == 2 * gdw
    assert seq % GDN_TILE == 0 and seq % S5_CHUNK == 0
    n_levels = int(math.log2(seq // S5_CHUNK))
    assert S5_CHUNK << n_levels == seq and S5_OCT * S5_GROUP == 128

    c_pad = jnp.zeros((8, d), F32).at[:batch].set(c)
    mod = _ada(c_pad, w_ada, b_ada)

    o_u, o_q, o_k, o_v, o_z = 0, s5w, 2 * s5w, 3 * s5w, 4 * s5w
    o_ba = 5 * s5w
    o_ga = o_ba + 2 * heads
    o_gb = o_ga + d
    col_u, col_q, col_k, col_v, col_z = 4, 5, 6, 7, 8

    gdn_masks = _gdn_masks()
    s5_rep = _s5_replicate()
    s5_ops = jax.vmap(functools.partial(_s5_weights, n_levels=n_levels))(
        s5_a_re, s5_a_im, s5_log_dt, s5_b_re, s5_b_im, s5_c_re, s5_c_im)

    wr_hi = w_router.astype(BF16)
    wr_lo = (w_router - wr_hi.astype(F32)).astype(BF16)
    pad_e = ((0, 0), (0, 128 - N_EXPERTS))
    wr_t = jnp.concatenate([jnp.pad(wr_hi, pad_e), jnp.pad(wr_lo, pad_e)], axis=1)
    rb = router_bias.reshape(N_EXPERTS, 1).astype(F32)

    tm_e = min(512, t)
    n_tiles = (TOP_K * t) // tm_e + N_EXPERTS
    p_rows = n_tiles * tm_e
    xs_zero = jnp.zeros((p_rows, d // 2), jnp.uint32)

    w_big = jnp.concatenate([w_in[:, :, o_ga:o_ga + d], w_in[:, :, o_gb:o_gb + d], w_in[:, :, o_u:o_ba]],
                            axis=2).astype(BF16)
    w_small = jnp.pad(w_in[:, :, o_ba:o_ga], ((0, 0), (0, 0), (0, 128 - 2 * heads))).astype(BF16)

    x2 = x.reshape(t, d)
    for l in range(depth):
        modl = mod[l, :batch].reshape(batch, 6, d)
        proj, small = _inproj(x2, modl, norm1_w[l].reshape(1, d), w_big, w_small, l, seq)

        y_lin = _s5_core(proj, col_u * s5w // 128, *s5_ops, s5_rep, l, batch, seq)
        y_a = _s5_post(y_lin, proj, col_u, s5_d[l].reshape(1, s5w), s5_w_glu[l].astype(BF16),
                       s5_b_glu[l].reshape(1, s5w))

        head_params = jnp.zeros((8, 128), F32)
        head_params = head_params.at[0, heads:2 * heads].set(-jnp.exp(gdn_a_log[l]))
        head_params = head_params.at[1, heads:2 * heads].set(gdn_dt_bias[l])
        y_b = _gdn(proj, small, gdn_conv_w[l], head_params, gdn_norm_w[l].reshape(1, HEAD_DIM), gdn_masks,
                   (col_q, col_k, col_v, col_z), batch, seq, heads)

        x2, h2, route = _merge(y_a, y_b, proj, x2, modl, w_proj_a[l].astype(BF16), w_proj_b[l].astype(BF16),
                               w_out[l].astype(BF16), norm2_w[l].reshape(1, d), wr_t, rb, seq, 0, 1)

        expert_idx = route[0:TOP_K].T.astype(jnp.int32)
        gate_w = route[TOP_K:2 * TOP_K].T
        dest, tile_expert, n_used = _dispatch_plan(expert_idx, tm_e, n_tiles)
        dest2 = dest.reshape(t, TOP_K)
        xs = xs_zero
        for k in range(TOP_K):
            xs = xs.at[dest2[:, k]].set(h2, unique_indices=True, mode='promise_in_bounds')
        wg_l, wu_l, wd_l = (_expert_weights_bf16(w, l) for w in (w_gate, w_up, w_down))
        ys = _moe_experts(tile_expert, n_used, xs, wg_l, wu_l, wd_l, tm_e)
        y_pairs = jnp.take(ys, dest.reshape(t, TOP_K).T.reshape(-1), axis=0, mode='clip')
        x2 = _combine(x2, y_pairs, gate_w, modl, final_norm_w.reshape(1, d), seq, l == depth - 1)

    return x2.reshape(batch, seq, d)
```

```python
import functools
import math

import jax
import jax.numpy as jnp
from jax import lax
from jax.experimental import pallas as pl
from jax.experimental.pallas import tpu as pltpu

F32 = jnp.float32
BF16 = jnp.bfloat16
HIGHEST = lax.Precision.HIGHEST

RMS_EPS = 1e-6
S5_GROUP = 16
S5_STATE = 64
S5_CHUNK = 16
S5_OCT = 8
HEAD_DIM = 128
GDN_CHUNK = 64
GDN_TILE = 256
GDN_HEADS_PER_TRIP = 8
CONV_K = 4
N_EXPERTS = 16
N_GROUPS = 4
EXPERTS_PER_GROUP = N_EXPERTS // N_GROUPS
TOP_K = 2
MIB = 1024 * 1024


def _cparams(semantics, vmem_mib):
    return pltpu.CompilerParams(dimension_semantics=semantics, vmem_limit_bytes=vmem_mib * MIB)


def _silu(x):
    return x * jax.nn.sigmoid(x)


def _softplus(x):
    return jnp.maximum(x, 0.0) + jnp.log(1.0 + jnp.exp(-jnp.abs(x)))


def _rms_scale(x):
    return lax.rsqrt(jnp.mean(x * x, axis=-1, keepdims=True) + RMS_EPS)


def _pack_halves(x):
    n = x.shape[1] // 2
    lo = lax.bitcast_convert_type(x[:, :n].astype(BF16).astype(F32), jnp.uint32)
    hi = lax.bitcast_convert_type(x[:, n:].astype(BF16).astype(F32), jnp.uint32)
    return (lo >> 16) | hi


def _unpack_halves(p):
    lo = lax.bitcast_convert_type(p << 16, F32)
    hi = lax.bitcast_convert_type(p & jnp.uint32(0xFFFF0000), F32)
    return lo, hi


def _nt_dot(a, b):
    return lax.dot_general(a, b, (((1,), (1,)), ((), ())), preferred_element_type=F32)


def _tn_dot(a, b):
    return lax.dot_general(a, b, (((0,), (0,)), ((), ())), preferred_element_type=F32)


def _ada_kernel(c_ref, w_ref, b_ref, o_ref):
    ca = _silu(c_ref[...]).astype(BF16)
    o_ref[...] = jnp.dot(ca, w_ref[...].astype(BF16), preferred_element_type=F32) + b_ref[...]


def _ada(c_pad, w_ada, b_ada):
    depth, d, n = w_ada.shape
    rows = c_pad.shape[0]
    tn = min(n, 1024)
    return pl.pallas_call(
        _ada_kernel,
        grid=(depth, n // tn),
        in_specs=[pl.BlockSpec((rows, d), lambda l, j: (0, 0)),
                  pl.BlockSpec((None, d, tn), lambda l, j: (l, 0, j)),
                  pl.BlockSpec((None, 1, tn), lambda l, j: (l, 0, j))],
        out_specs=pl.BlockSpec((None, rows, tn), lambda l, j: (l, 0, j)),
        out_shape=jax.ShapeDtypeStruct((depth, rows, n), F32),
        compiler_params=_cparams(("parallel", "parallel"), 40),
    )(c_pad, w_ada, b_ada.reshape(depth, 1, n))


def _inproj_kernel(x_ref, mod_ref, nw_ref, w_ref, wsm_ref, o_ref, osm_ref, h_scr):
    @pl.when(pl.program_id(1) == 0)
    def _():
        x = x_ref[...]
        h = x * _rms_scale(x) * nw_ref[...] * (1.0 + mod_ref[1:2, :]) + mod_ref[0:1, :]
        hb = h.astype(BF16)
        h_scr[...] = hb
        osm_ref[...] = jnp.dot(hb, wsm_ref[...], preferred_element_type=F32)

    o_ref[...] = jnp.dot(h_scr[...], w_ref[...], preferred_element_type=F32)


def _inproj(x2, modl, nw, w_big, w_small, layer, seq):
    t, d = x2.shape
    n = w_big.shape[-1]
    tm = min(seq, 1024)
    tn = 1536
    per_batch = seq // tm
    return pl.pallas_call(
        _inproj_kernel,
        grid=(t // tm, n // tn),
        in_specs=[pl.BlockSpec((tm, d), lambda i, j: (i, 0)),
                  pl.BlockSpec((None, 6, d), lambda i, j: (i // per_batch, 0, 0)),
                  pl.BlockSpec((1, d), lambda i, j: (0, 0)),
                  pl.BlockSpec((None, d, tn), lambda i, j: (layer, 0, j)),
                  pl.BlockSpec((None, d, 128), lambda i, j: (layer, 0, 0))],
        out_specs=[pl.BlockSpec((tm, tn), lambda i, j: (i, j)),
                   pl.BlockSpec((tm, 128), lambda i, j: (i, 0))],
        out_shape=[jax.ShapeDtypeStruct((t, n), F32), jax.ShapeDtypeStruct((t, 128), F32)],
        scratch_shapes=[pltpu.VMEM((tm, d), BF16)],
        compiler_params=_cparams(("parallel", "arbitrary"), 52),
    )(x2, modl, nw, w_big, w_small)


def _s5_weights(a_re, a_im, log_dt, b_re, b_im, c_re, c_im, n_levels):
    tc = S5_CHUNK
    groups, p = a_re.shape
    n = b_re.shape[-1]
    oc = S5_OCT
    octs = groups // oc
    lam = lax.complex(a_re, a_im)
    dt = jnp.exp(log_dt)[:, None]
    log_a = lam * dt
    a_bar = jnp.exp(log_a)
    b_bar = ((a_bar - 1.0) / lam)[..., None] * lax.complex(b_re, b_im)
    c_mat = lax.complex(c_re, c_im)
    apow = jnp.exp(log_a[:, :, None] * jnp.arange(tc + 1, dtype=F32)[None, None, :])
    lane = jnp.arange(tc * n)
    e_t = (jnp.arange(tc)[:, None] == (lane // n)[None, :]).astype(F32)
    e_n = (jnp.arange(n)[:, None] == (lane % n)[None, :]).astype(F32)

    def expand(z, e):
        f = lambda r: jnp.einsum('gpk,kj->gpj', r, e, precision=HIGHEST)
        return lax.complex(f(jnp.real(z)), f(jnp.imag(z)))

    def re_negim_rows(z):
        return jnp.concatenate([jnp.real(z), -jnp.imag(z)], axis=1)

    c_rep = expand(jnp.transpose(c_mat, (0, 2, 1)), e_n)
    lag_c = re_negim_rows(expand(apow[:, :, :tc], e_t) * c_rep)
    hoc = re_negim_rows(expand(apow[:, :, 1:], e_t) * c_rep)
    b_t = jnp.transpose(b_bar, (0, 2, 1))
    b_cat = jnp.concatenate([jnp.real(b_t), jnp.imag(b_t)], axis=-1)
    kern = jnp.einsum('gnk,gkj->gnj', b_cat, lag_c, precision=HIGHEST)
    tcomp = jnp.stack([jnp.pad(kern[:, :, :(tc - s) * n], ((0, 0), (0, 0), (s * n, 0))) for s in range(tc)],
                      axis=1)
    tcomp = jnp.transpose(tcomp.reshape(octs, oc, tc, n, tc * n), (0, 2, 1, 3, 4)).reshape(octs, tc * oc * n, tc * n)
    ap_rev = jnp.transpose(apow[:, :, tc - 1 - jnp.arange(tc)], (0, 2, 1))
    hin_c = ap_rev[:, :, None, :] * b_t[:, None, :, :]
    hc = jnp.concatenate([jnp.real(hin_c), jnp.imag(hin_c)], axis=-1)
    hc = jnp.transpose(hc.reshape(octs, oc, tc, n, 2 * p), (0, 2, 1, 3, 4)).reshape(octs, tc * oc * n, 2 * p)
    hoc = hoc.reshape(octs, oc * 2 * p, tc * n)
    steps = (tc * (2.0 ** jnp.arange(n_levels, dtype=F32)))
    amul = jnp.exp(log_a[:, None, :] * steps[None, :, None])
    a1 = jnp.concatenate([jnp.real(amul), jnp.real(amul)], axis=-1)
    a2 = jnp.concatenate([-jnp.imag(amul), jnp.imag(amul)], axis=-1)

    def lanes_by_group(a):
        return jnp.transpose(a.reshape(octs, oc, n_levels, 2 * p), (0, 2, 1, 3)).reshape(octs, n_levels, oc * 2 * p)

    return (tcomp.astype(BF16), hc.astype(BF16), hoc.astype(BF16),
            lanes_by_group(a1).astype(F32), lanes_by_group(a2).astype(F32))


def _s5_replicate():
    r = lax.broadcasted_iota(jnp.int32, (S5_CHUNK * S5_GROUP, S5_CHUNK * 128), 0)
    c = lax.broadcasted_iota(jnp.int32, (S5_CHUNK * S5_GROUP, S5_CHUNK * 128), 1)
    return ((r // S5_GROUP == c // 128) & (r % S5_GROUP == c % S5_GROUP)).astype(BF16)


def _s5_core_kernel(u_ref, tc_ref, hc_ref, hoc_ref, a1_ref, a2_ref, rep_ref, y_ref, toep_scr, hin_scr, hout_scr,
                    *, n_levels):
    tc = S5_CHUNK
    lanes = u_ref.shape[1]
    state_lanes = hc_ref.shape[1]

    @pl.when(pl.program_id(1) == 0)
    def _():
        kd = toep_scr.shape[1]
        rb = 256
        col_g = (lax.broadcasted_iota(jnp.int32, (rb, kd), 1) // S5_GROUP) % S5_OCT
        row_i = lax.broadcasted_iota(jnp.int32, (rb, kd), 0)
        for i in range(toep_scr.shape[0] // rb):
            full = jnp.dot(tc_ref[i * rb:(i + 1) * rb, :], rep_ref[...], preferred_element_type=F32)
            row_g = ((row_i + i * rb) // S5_GROUP) % S5_OCT
            toep_scr[i * rb:(i + 1) * rb, :] = jnp.where(row_g == col_g, full, 0.0).astype(BF16)
        for i in range(hout_scr.shape[0] // rb):
            full = jnp.dot(hoc_ref[i * rb:(i + 1) * rb, :], rep_ref[...], preferred_element_type=F32)
            row_g = (row_i + i * rb) // state_lanes
            hout_scr[i * rb:(i + 1) * rb, :] = jnp.where(row_g == col_g, full, 0.0).astype(BF16)
        hrow_g = (lax.broadcasted_iota(jnp.int32, hc_ref.shape, 0) // S5_GROUP) % S5_OCT
        for g in range(S5_OCT):
            hin_scr[:, g * state_lanes:(g + 1) * state_lanes] = jnp.where(hrow_g == g, hc_ref[...], 0.0).astype(BF16)

    rows = u_ref.shape[0] // tc
    x = jnp.concatenate([u_ref[pl.ds(t, rows, stride=tc), :] for t in range(tc)], axis=1).astype(BF16)
    s_loc = jnp.dot(x, hin_scr[...], preferred_element_type=F32)
    pos = lax.broadcasted_iota(jnp.int32, (rows, state_lanes), 0)
    s_in = []
    for g in range(S5_OCT):
        ln = slice(g * state_lanes, (g + 1) * state_lanes)
        s = s_loc[:, ln]
        for k in range(n_levels):
            d = 1 << k
            prev = jnp.where(pos >= d, pltpu.roll(s, d, axis=0), 0.0)
            s = s + a1_ref[k:k + 1, ln] * prev + a2_ref[k:k + 1, ln] * pltpu.roll(prev, state_lanes // 2, axis=1)
        s_in.append(jnp.where(pos >= 1, pltpu.roll(s, 1, axis=0), 0.0).astype(BF16))
    y = (jnp.dot(x, toep_scr[...], preferred_element_type=F32)
         + jnp.dot(jnp.concatenate(s_in, axis=1), hout_scr[...], preferred_element_type=F32))
    for t in range(tc):
        y_ref[pl.ds(t, rows, stride=tc), :] = y[:, t * lanes:(t + 1) * lanes]


def _s5_core(proj, u_lane_block, tcomp, hc, hoc, a1, a2, rep, layer, batch, seq):
    t = proj.shape[0]
    _, octs, kdim, tn = tcomp.shape
    sl = hc.shape[-1]
    sw = hoc.shape[2]
    n_levels = a1.shape[2]
    lanes = kdim // S5_CHUNK
    kern = functools.partial(_s5_core_kernel, n_levels=n_levels)
    return pl.pallas_call(
        kern,
        grid=(octs, batch),
        in_specs=[pl.BlockSpec((seq, lanes), lambda o, b: (b, u_lane_block + o)),
                  pl.BlockSpec((None, None, kdim, tn), lambda o, b: (layer, o, 0, 0)),
                  pl.BlockSpec((None, None, kdim, sl), lambda o, b: (layer, o, 0, 0)),
                  pl.BlockSpec((None, None, sw, tn), lambda o, b: (layer, o, 0, 0)),
                  pl.BlockSpec((None, None, n_levels, sw), lambda o, b: (layer, o, 0, 0)),
                  pl.BlockSpec((None, None, n_levels, sw), lambda o, b: (layer, o, 0, 0)),
                  pl.BlockSpec((tn, kdim), lambda o, b: (0, 0))],
        out_specs=pl.BlockSpec((seq, lanes), lambda o, b: (b, o)),
        out_shape=jax.ShapeDtypeStruct((t, octs * lanes), F32),
        scratch_shapes=[pltpu.VMEM((kdim, kdim), BF16), pltpu.VMEM((kdim, sw), BF16), pltpu.VMEM((sw, kdim), BF16)],
        compiler_params=_cparams(("parallel", "arbitrary"), 48),
    )(proj, tcomp, hc, hoc, a1, a2, rep)


def _s5_post_kernel(y_ref, u_ref, d_ref, w_ref, b_ref, o_ref):
    y = jax.nn.gelu(y_ref[...] + d_ref[...] * u_ref[...])
    gate = jnp.dot(y.astype(BF16), w_ref[...], preferred_element_type=F32) + b_ref[...]
    o_ref[...] = (y * jax.nn.sigmoid(gate)).astype(o_ref.dtype)


def _s5_post(y_lin, proj, u_col, d_skip, w_glu, b_glu):
    t, w = y_lin.shape
    tm = min(t, 512)
    return pl.pallas_call(
        _s5_post_kernel,
        grid=(t // tm,),
        in_specs=[pl.BlockSpec((tm, w), lambda i: (i, 0)),
                  pl.BlockSpec((tm, w), lambda i: (i, u_col)),
                  pl.BlockSpec((1, w), lambda i: (0, 0)),
                  pl.BlockSpec((w, w), lambda i: (0, 0)),
                  pl.BlockSpec((1, w), lambda i: (0, 0))],
        out_specs=pl.BlockSpec((tm, w), lambda i: (i, 0)),
        out_shape=jax.ShapeDtypeStruct((t, w), BF16),
        compiler_params=_cparams(("parallel",), 32),
    )(y_lin, proj, d_skip, w_glu, b_glu)


def _gdn_kernel(q_ref, k_ref, v_ref, qh_ref, kh_ref, vh_ref, z_ref, sm_ref, cq_ref, ck_ref, cv_ref,
                hp_ref, nw_ref, tril_ref, cmask_ref, lmask_ref, o_ref, betab, gcb, grow, st_scr,
                cat_q, cat_k, cat_v, *, heads):
    tt = GDN_TILE
    hd = HEAD_DIM
    nck = tt // GDN_CHUNK
    first = pl.program_id(1) == 0

    @pl.when(first)
    def _():
        st_scr[...] = jnp.zeros_like(st_scr)

    sm = sm_ref[...]
    hp = hp_ref[...]
    beta_all = jax.nn.sigmoid(sm)
    g_all = hp[0:1, :] * _softplus(sm + hp[1:2, :])
    gc_all = jnp.dot(tril_ref[...], g_all, precision=HIGHEST, preferred_element_type=F32)
    gc_t = gc_all.T
    for h in range(heads):
        betab[h] = jnp.broadcast_to(beta_all[:, h:h + 1], (tt, hd))
        gcb[h] = jnp.broadcast_to(gc_all[:, heads + h:heads + h + 1], (tt, hd))
        grow[h:h + 1, :] = gc_t[heads + h:heads + h + 1, :]

    eye = (lax.broadcasted_iota(jnp.int32, (tt, tt), 0) == lax.broadcasted_iota(jnp.int32, (tt, tt), 1)).astype(F32)
    n_lv = lmask_ref.shape[0]

    for x_ref, halo_ref, cat in ((q_ref, qh_ref, cat_q), (k_ref, kh_ref, cat_k), (v_ref, vh_ref, cat_v)):
        cat[0:8, :] = jnp.where(first, 0.0, halo_ref[...])
        cat[8:8 + tt, :] = x_ref[...]

    def conv_silu(cat, cw_ref, hs):
        w = cw_ref[:, pl.ds(hs, hd)]
        acc = cat[pl.ds(8, tt), pl.ds(hs, hd)] * w[CONV_K - 1:CONV_K, :]
        for s in range(1, CONV_K):
            acc = acc + cat[pl.ds(8 - s, tt), pl.ds(hs, hd)] * w[CONV_K - 1 - s:CONV_K - s, :]
        return _silu(acc)

    def heads_lockstep(hs_idx):
        each = lambda f, *ls: [f(*a) for a in zip(*ls)]
        hs = [pl.multiple_of(h * hd, hd) for h in hs_idx]
        q = [conv_silu(cat_q, cq_ref, o) for o in hs]
        k = [conv_silu(cat_k, ck_ref, o) for o in hs]
        v = [conv_silu(cat_v, cv_ref, o) for o in hs]
        qn = each(lambda a: a * (lax.rsqrt(jnp.sum(a * a, axis=-1, keepdims=True) + RMS_EPS) * (hd ** -0.5)), q)
        kn = each(lambda a: a * lax.rsqrt(jnp.sum(a * a, axis=-1, keepdims=True) + RMS_EPS), k)
        bb = [betab[h] for h in hs_idx]
        gc = [gcb[h] for h in hs_idx]
        gr = [grow[pl.ds(h, 1), :] for h in hs_idx]
        decay = each(lambda c, r: jnp.exp(jnp.where(cmask_ref[0] > 0.0, jnp.concatenate([c, c], axis=1) - r, -1e30)),
                     gc, gr)
        kb = each(lambda a, b: a * b, kn, bb)
        knb = each(lambda a: a.astype(BF16), kn)
        lfull = each(lambda a, b, dc: _nt_dot(a.astype(BF16), b) * dc, kb, knb, decay)
        lmat = each(lambda a: a.astype(BF16), lfull)
        tinv = each(lambda a: eye - a * cmask_ref[1], lfull)
        for lv in range(1, n_lv):
            d = 1 << lv
            nb = tt // d
            tb = each(lambda a: a.astype(BF16), tinv)
            bd = each(lambda m: m * lmask_ref[lv], lmat)
            if d % 8 == 0:
                odd = each(lambda a: jnp.concatenate([a[i * d:(i + 1) * d] for i in range(1, nb, 2)], axis=0), tinv)
                tl = each(lambda a, b: jnp.dot(a.astype(BF16), b, preferred_element_type=F32).astype(BF16), odd, bd)
                upd = each(lambda a, b, c: a - jnp.dot(b, c, preferred_element_type=F32), odd, tl, tb)
                tinv = each(lambda a, u: jnp.concatenate(
                    [a[i * d:(i + 1) * d] if i % 2 == 0 else u[(i // 2) * d:(i // 2 + 1) * d] for i in range(nb)],
                    axis=0), tinv, upd)
            else:
                tl = each(lambda a, b: jnp.dot(a, b, preferred_element_type=F32).astype(BF16), tb, bd)
                tinv = each(lambda a, b, c: a - jnp.dot(b, c, preferred_element_type=F32), tinv, tl, tb)
        eg = each(jnp.exp, gc)
        rhs = each(lambda a, e, b, c: jnp.concatenate([a * e, b * c], axis=1).astype(BF16), kb, eg, v, bb)
        wu = each(lambda a, b: jnp.dot(a.astype(BF16), b, preferred_element_type=F32).astype(BF16), tinv, rhs)
        attn = each(lambda a, b, dc: (_nt_dot(a.astype(BF16), b) * dc).astype(BF16), qn, knb, decay)
        awu = each(lambda a, b: jnp.dot(a, b, preferred_element_type=F32), attn, wu)
        qp = each(lambda a, e, b: (a * e - b[:, :hd]).astype(BF16), qn, eg, awu)
        s = [st_scr[h] for h in hs_idx]
        outs = [[] for _ in hs_idx]
        for c in range(nck):
            r0 = c * GDN_CHUNK
            r1 = r0 + GDN_CHUNK
            gl = each(lambda a: a[r1 - 1:r1, :], gc)
            kd = each(lambda a, l, g: (a[r0:r1] * jnp.exp(l - g[r0:r1])).astype(BF16), kn, gl, gc)
            m = each(lambda a, b: _tn_dot(a, b[r0:r1]), kd, wu)
            sb = each(lambda a: a.astype(BF16), s)
            for i, (a, b, y) in enumerate(zip(qp, sb, awu)):
                outs[i].append(jnp.dot(a[r0:r1], b, preferred_element_type=F32) + y[r0:r1, hd:])
            s = each(lambda a, l, mm, b: a * jnp.exp(l) - jnp.dot(mm[:, :hd].astype(BF16), b,
                                                                  preferred_element_type=F32) + mm[:, hd:],
                     s, gl, m, sb)
        for i, h in enumerate(hs_idx):
            st_scr[h] = s[i]
            o = jnp.concatenate(outs[i], axis=0)
            z = z_ref[:, pl.ds(hs[i], hd)]
            o_ref[:, pl.ds(hs[i], hd)] = (o * _rms_scale(o) * nw_ref[...] * _silu(z)).astype(o_ref.dtype)

    def trip(j, carry):
        heads_lockstep([GDN_HEADS_PER_TRIP * j + i for i in range(GDN_HEADS_PER_TRIP)])
        return carry

    lax.fori_loop(0, heads // GDN_HEADS_PER_TRIP, trip, 0)


def _gdn_masks():
    tt = GDN_TILE
    ri = lax.broadcasted_iota(jnp.int32, (tt, tt), 0)
    ci = lax.broadcasted_iota(jnp.int32, (tt, tt), 1)
    same = (ri // GDN_CHUNK) == (ci // GDN_CHUNK)
    causal = same & (ci <= ri)
    tril = causal.astype(F32)
    cmask = jnp.stack([tril, (((ri % 2) == 1) & (ci == ri - 1)).astype(F32)])
    levels = []
    d = 1
    while d < GDN_CHUNK:
        levels.append((((ri // d) % 2) == 1) & ((ci // d) == (ri // d) - 1))
        d *= 2
    return tril, cmask, jnp.stack(levels).astype(BF16)


def _gdn(proj, small, conv_w, head_params, norm_w, masks, cols, batch, seq, heads):
    t = proj.shape[0]
    tt = GDN_TILE
    width = heads * HEAD_DIM
    tiles = seq // tt
    cq, ck, cv, cz = cols
    tril, cmask, lmask = masks

    def cur(col):
        return pl.BlockSpec((tt, width), lambda b, i: (b * tiles + i, col))

    def halo(col):
        return pl.BlockSpec((8, width), lambda b, i: (jnp.maximum((b * tiles + i) * (tt // 8) - 1, 0), col))

    def cw(col):
        return pl.BlockSpec((CONV_K, width), lambda b, i: (0, col))

    kern = functools.partial(_gdn_kernel, heads=heads)
    return pl.pallas_call(
        kern,
        grid=(batch, tiles),
        in_specs=[cur(cq), cur(ck), cur(cv), halo(cq), halo(ck), halo(cv), cur(cz),
                  pl.BlockSpec((tt, 128), lambda b, i: (b * tiles + i, 0)),
                  cw(0), cw(1), cw(2),
                  pl.BlockSpec((8, 128), lambda b, i: (0, 0)),
                  pl.BlockSpec((1, HEAD_DIM), lambda b, i: (0, 0)),
                  pl.BlockSpec((tt, tt), lambda b, i: (0, 0)),
                  pl.BlockSpec(cmask.shape, lambda b, i: (0, 0, 0)),
                  pl.BlockSpec(lmask.shape, lambda b, i: (0, 0, 0))],
        out_specs=pl.BlockSpec((tt, width), lambda b, i: (b * tiles + i, 0)),
        out_shape=jax.ShapeDtypeStruct((t, width), BF16),
        scratch_shapes=[pltpu.VMEM((heads, tt, HEAD_DIM), F32),
                        pltpu.VMEM((heads, tt, HEAD_DIM), F32),
                        pltpu.VMEM((8, tt), F32),
                        pltpu.VMEM((heads, HEAD_DIM, HEAD_DIM), F32),
                        pltpu.VMEM((tt + 8, width), F32),
                        pltpu.VMEM((tt + 8, width), F32),
                        pltpu.VMEM((tt + 8, width), F32)],
        compiler_params=_cparams(("parallel", "arbitrary"), 40),
    )(proj, proj, proj, proj, proj, proj, proj, small, conv_w, conv_w, conv_w, head_params, norm_w,
      tril, cmask, lmask)


def _route(h, wr_hl, rb):
    h_hi = h.astype(BF16)
    h_lo = (h - h_hi.astype(F32)).astype(BF16)
    first = jnp.dot(h_hi, wr_hl[...], preferred_element_type=F32)
    logits = first[:, :128] + first[:, 128:] + jnp.dot(h_lo, wr_hl[:, :128], preferred_element_type=F32)
    logits = logits.T[:N_EXPERTS]
    scores = jax.nn.sigmoid(logits)
    sel = scores + rb
    sel_rows = [sel[e:e + 1, :] for e in range(N_EXPERTS)]
    score_rows = [scores[e:e + 1, :] for e in range(N_EXPERTS)]
    epg = EXPERTS_PER_GROUP
    best_score = None
    best_group = None
    for g in range(N_GROUPS):
        rows = sel_rows[g * epg:(g + 1) * epg]
        gs = None
        for a in range(epg):
            for b in range(a + 1, epg):
                pair = rows[a] + rows[b]
                gs = pair if gs is None else jnp.maximum(gs, pair)
        if g == 0:
            best_score, best_group = gs, jnp.zeros_like(gs)
        else:
            better = gs > best_score
            best_group = jnp.where(better, float(g), best_group)
            best_score = jnp.where(better, gs, best_score)
    in_sel = []
    in_score = []
    for j in range(epg):
        a = sel_rows[j]
        b = score_rows[j]
        for g in range(1, N_GROUPS):
            pick = best_group == float(g)
            a = jnp.where(pick, sel_rows[g * epg + j], a)
            b = jnp.where(pick, score_rows[g * epg + j], b)
        in_sel.append(a)
        in_score.append(b)
    i1, m1, w1 = jnp.zeros_like(in_sel[0]), in_sel[0], in_score[0]
    for j in range(1, epg):
        better = in_sel[j] > m1
        i1 = jnp.where(better, float(j), i1)
        m1 = jnp.where(better, in_sel[j], m1)
        w1 = jnp.where(better, in_score[j], w1)
    i2 = m2 = w2 = None
    for j in range(epg):
        cand = jnp.where(i1 == float(j), -jnp.inf, in_sel[j])
        if j == 0:
            i2, m2, w2 = jnp.zeros_like(cand), cand, in_score[0]
        else:
            better = cand > m2
            i2 = jnp.where(better, float(j), i2)
            m2 = jnp.where(better, cand, m2)
            w2 = jnp.where(better, in_score[j], w2)
    total = w1 + w2
    zero = jnp.zeros_like(w1)
    return jnp.concatenate([best_group * epg + i1, best_group * epg + i2, w1 / total, w2 / total,
                            zero, zero, zero, zero], axis=0)


def _merge_kernel(ya_ref, yb_ref, ga_ref, gb_ref, x_ref, mod_ref, wa_ref, wb_ref, wo_ref, nw_ref, wrt_ref, rb_ref,
                  o_ref, h_ref, r_ref):
    pa = jnp.dot(ya_ref[...], wa_ref[...], preferred_element_type=F32)
    pb = jnp.dot(yb_ref[...], wb_ref[...], preferred_element_type=F32)
    merged = jax.nn.sigmoid(ga_ref[...]) * pa + jax.nn.sigmoid(gb_ref[...]) * pb
    out = jnp.dot(merged.astype(BF16), wo_ref[...], preferred_element_type=F32)
    xn = x_ref[...] + mod_ref[2:3, :] * out
    o_ref[...] = xn
    h = xn * _rms_scale(xn) * nw_ref[...] * (1.0 + mod_ref[4:5, :]) + mod_ref[3:4, :]
    h_ref[...] = _pack_halves(h)
    r_ref[...] = _route(h, wrt_ref, rb_ref[...])


def _merge(ya, yb, proj, x2, modl, wa, wb, wo, nw2, wr_t, rb, seq, ga_col, gb_col):
    t, d = x2.shape
    w = ya.shape[1]
    tm = min(seq, 256)
    per_batch = seq // tm
    const = dict(pipeline_mode=pl.Buffered(1))
    return pl.pallas_call(
        _merge_kernel,
        grid=(t // tm,),
        in_specs=[pl.BlockSpec((tm, w), lambda i: (i, 0)),
                  pl.BlockSpec((tm, w), lambda i: (i, 0)),
                  pl.BlockSpec((tm, d), lambda i: (i, ga_col)),
                  pl.BlockSpec((tm, d), lambda i: (i, gb_col)),
                  pl.BlockSpec((tm, d), lambda i: (i, 0)),
                  pl.BlockSpec((None, 6, d), lambda i: (i // per_batch, 0, 0)),
                  pl.BlockSpec((w, d), lambda i: (0, 0), **const),
                  pl.BlockSpec((w, d), lambda i: (0, 0), **const),
                  pl.BlockSpec((d, d), lambda i: (0, 0), **const),
                  pl.BlockSpec((1, d), lambda i: (0, 0)),
                  pl.BlockSpec((d, 256), lambda i: (0, 0)),
                  pl.BlockSpec((N_EXPERTS, 1), lambda i: (0, 0))],
        out_specs=[pl.BlockSpec((tm, d), lambda i: (i, 0)),
                   pl.BlockSpec((tm, d // 2), lambda i: (i, 0)),
                   pl.BlockSpec((8, tm), lambda i: (0, i))],
        out_shape=[jax.ShapeDtypeStruct((t, d), F32), jax.ShapeDtypeStruct((t, d // 2), jnp.uint32),
                   jax.ShapeDtypeStruct((8, t), F32)],
        compiler_params=_cparams(("parallel",), 52),
    )(ya, yb, proj, proj, x2, modl, wa, wb, wo, nw2, wr_t, rb)


def _moe_kernel(te_ref, nu_ref, x_ref, wg_ref, wu_ref, wd_ref, o_ref):
    used = pl.program_id(0) < nu_ref[0]

    @pl.when(used)
    def _():
        x_lo, x_hi = _unpack_halves(x_ref[...])
        x_lo = x_lo.astype(BF16)
        x_hi = x_hi.astype(BF16)
        half = x_lo.shape[1]

        def up(w_ref):
            return (jnp.dot(x_lo, w_ref[:half, :], preferred_element_type=F32)
                    + jnp.dot(x_hi, w_ref[half:, :], preferred_element_type=F32))

        hid = (_silu(up(wg_ref)) * up(wu_ref)).astype(BF16)
        o_ref[...] = _pack_halves(jnp.dot(hid, wd_ref[...], preferred_element_type=F32))

    @pl.when(jnp.logical_not(used))
    def _():
        o_ref[...] = jnp.zeros_like(o_ref)


def _cast_kernel(w_ref, o_ref):
    o_ref[...] = w_ref[...].astype(o_ref.dtype)


def _expert_weights_bf16(w, layer):
    _, e, a, b = w.shape
    ta = min(a, (1 << 20) // b)
    return pl.pallas_call(
        _cast_kernel,
        grid=(e, a // ta),
        in_specs=[pl.BlockSpec((None, None, ta, b), lambda i, j: (layer, i, j, 0))],
        out_specs=pl.BlockSpec((None, ta, b), lambda i, j: (i, j, 0)),
        out_shape=jax.ShapeDtypeStruct((e, a, b), BF16),
        compiler_params=_cparams(("parallel", "parallel"), 32),
    )(w)


def _moe_experts(tile_expert, n_used, xs, wg, wu, wd, tm):
    p, dh = xs.shape
    d = 2 * dh
    f = wg.shape[-1]
    grid_spec = pltpu.PrefetchScalarGridSpec(
        num_scalar_prefetch=2,
        grid=(p // tm,),
        in_specs=[pl.BlockSpec((tm, dh), lambda i, te, nu: (i, 0)),
                  pl.BlockSpec((None, d, f), lambda i, te, nu: (te[i], 0, 0)),
                  pl.BlockSpec((None, d, f), lambda i, te, nu: (te[i], 0, 0)),
                  pl.BlockSpec((None, f, d), lambda i, te, nu: (te[i], 0, 0))],
        out_specs=pl.BlockSpec((tm, dh), lambda i, te, nu: (i, 0)),
    )
    return pl.pallas_call(
        _moe_kernel,
        grid_spec=grid_spec,
        out_shape=jax.ShapeDtypeStruct((p, dh), jnp.uint32),
        compiler_params=_cparams(("arbitrary",), 52),
    )(tile_expert, n_used, xs, wg, wu, wd)


def _dispatch_plan(expert_idx, tm, n_tiles):
    flat_e = expert_idx.reshape(-1)
    onehot = (flat_e[:, None] == jnp.arange(N_EXPERTS, dtype=jnp.int32)[None, :]).astype(jnp.int32)
    csum = jnp.cumsum(onehot, axis=0)
    rank = jnp.sum((csum - onehot) * onehot, axis=1)
    sizes = csum[-1]
    padded = ((sizes + tm - 1) // tm) * tm
    pad_end = jnp.cumsum(padded)
    pad_start = pad_end - padded
    dest = pad_start[flat_e] + rank
    tile_start = jnp.arange(n_tiles, dtype=jnp.int32) * tm
    tile_expert = jnp.minimum(jnp.sum((tile_start[:, None] >= pad_end[None, :]).astype(jnp.int32), axis=1),
                              N_EXPERTS - 1).astype(jnp.int32)
    n_used = (pad_end[-1] // tm).astype(jnp.int32).reshape(1)
    dest = dest.astype(jnp.int32)
    n_slots = flat_e.shape[0]
    _, by_slot = lax.sort_key_val(dest, jnp.arange(n_slots, dtype=jnp.int32) // TOP_K)
    start = jnp.cumsum(sizes) - sizes
    tile_off = tile_start - pad_start[tile_expert]
    tile_valid = jnp.clip(sizes[tile_expert] - tile_off, 0, tm)
    lane = jnp.arange(tm, dtype=jnp.int32)[None, :]
    pick = jnp.clip((start[tile_expert] + tile_off)[:, None] + lane, 0, n_slots - 1)
    filler = (tile_start[:, None] + lane) % (n_slots // TOP_K)
    src_tok = jnp.where(lane < tile_valid[:, None], jnp.take(by_slot, pick.reshape(-1)).reshape(n_tiles, tm), filler)
    return dest, src_tok.reshape(-1).astype(jnp.int32), tile_expert, n_used


def _combine_kernel(x_ref, y0_ref, y1_ref, gw_ref, mod_ref, fw_ref, o_ref, *, final_norm):
    gw = gw_ref[...]
    y0_lo, y0_hi = _unpack_halves(y0_ref[...])
    y1_lo, y1_hi = _unpack_halves(y1_ref[...])
    moe = jnp.concatenate([gw[:, 0:1] * y0_lo + gw[:, 1:2] * y1_lo, gw[:, 0:1] * y0_hi + gw[:, 1:2] * y1_hi], axis=1)
    xn = x_ref[...] + mod_ref[5:6, :] * moe
    if final_norm:
        xn = xn * _rms_scale(xn) * fw_ref[...]
    o_ref[...] = xn


def _combine(x2, y_pairs, gate_w, modl, final_w, seq, final_norm):
    t, d = x2.shape
    tm = min(seq, 512)
    per_batch = seq // tm
    nblk = t // tm
    return pl.pallas_call(
        functools.partial(_combine_kernel, final_norm=final_norm),
        grid=(nblk,),
        in_specs=[pl.BlockSpec((tm, d), lambda i: (i, 0)),
                  pl.BlockSpec((tm, d // 2), lambda i: (i, 0)),
                  pl.BlockSpec((tm, d // 2), lambda i: (i + nblk, 0)),
                  pl.BlockSpec((tm, TOP_K), lambda i: (i, 0)),
                  pl.BlockSpec((None, 6, d), lambda i: (i // per_batch, 0, 0)),
                  pl.BlockSpec((1, d), lambda i: (0, 0))],
        out_specs=pl.BlockSpec((tm, d), lambda i: (i, 0)),
        out_shape=jax.ShapeDtypeStruct((t, d), F32),
        compiler_params=_cparams(("parallel",), 40),
    )(x2, y_pairs, y_pairs, gate_w, modl, final_w)


def kernel(x, c, w_ada, b_ada, norm1_w, norm2_w, w_in, s5_a_re, s5_a_im, s5_log_dt, s5_b_re, s5_b_im, s5_c_re, s5_c_im, s5_d, s5_w_glu, s5_b_glu, gdn_conv_w, gdn_a_log, gdn_dt_bias, gdn_norm_w, w_proj_a, w_proj_b, w_out, w_router, router_bias, w_gate, w_up, w_down, final_norm_w):
    batch, seq, d = x.shape
    depth = w_ada.shape[0]
    t = batch * seq
    s5w = s5_d.shape[1]
    gdw = w_proj_b.shape[1]
    heads = gdn_a_log.shape[1]
    assert gdw == heads * HEAD_DIM and s5w == gdw and d == 2 * gdw
    assert seq % GDN_TILE == 0 and seq % S5_CHUNK == 0
    n_levels = int(math.log2(seq // S5_CHUNK))
    assert S5_CHUNK << n_levels == seq and S5_OCT * S5_GROUP == 128

    c_pad = jnp.zeros((8, d), F32).at[:batch].set(c)
    mod = _ada(c_pad, w_ada, b_ada)

    o_u, o_q, o_k, o_v, o_z = 0, s5w, 2 * s5w, 3 * s5w, 4 * s5w
    o_ba = 5 * s5w
    o_ga = o_ba + 2 * heads
    o_gb = o_ga + d
    col_u, col_q, col_k, col_v, col_z = 4, 5, 6, 7, 8

    gdn_masks = _gdn_masks()
    s5_rep = _s5_replicate()
    s5_ops = jax.vmap(functools.partial(_s5_weights, n_levels=n_levels))(
        s5_a_re, s5_a_im, s5_log_dt, s5_b_re, s5_b_im, s5_c_re, s5_c_im)

    wr_hi = w_router.astype(BF16)
    wr_lo = (w_router - wr_hi.astype(F32)).astype(BF16)
    pad_e = ((0, 0), (0, 128 - N_EXPERTS))
    wr_t = jnp.concatenate([jnp.pad(wr_hi, pad_e), jnp.pad(wr_lo, pad_e)], axis=1)
    rb = router_bias.reshape(N_EXPERTS, 1).astype(F32)

    tm_e = min(512, t)
    n_tiles = (TOP_K * t) // tm_e + N_EXPERTS

    w_big = jnp.concatenate([w_in[:, :, o_ga:o_ga + d], w_in[:, :, o_gb:o_gb + d], w_in[:, :, o_u:o_ba]],
                            axis=2).astype(BF16)
    w_small = jnp.pad(w_in[:, :, o_ba:o_ga], ((0, 0), (0, 0), (0, 128 - 2 * heads))).astype(BF16)

    x2 = x.reshape(t, d)
    for l in range(depth):
        modl = mod[l, :batch].reshape(batch, 6, d)
        proj, small = _inproj(x2, modl, norm1_w[l].reshape(1, d), w_big, w_small, l, seq)

        y_lin = _s5_core(proj, col_u * s5w // 128, *s5_ops, s5_rep, l, batch, seq)
        y_a = _s5_post(y_lin, proj, col_u, s5_d[l].reshape(1, s5w), s5_w_glu[l].astype(BF16),
                       s5_b_glu[l].reshape(1, s5w))

        head_params = jnp.zeros((8, 128), F32)
        head_params = head_params.at[0, heads:2 * heads].set(-jnp.exp(gdn_a_log[l]))
        head_params = head_params.at[1, heads:2 * heads].set(gdn_dt_bias[l])
        y_b = _gdn(proj, small, gdn_conv_w[l], head_params, gdn_norm_w[l].reshape(1, HEAD_DIM), gdn_masks,
                   (col_q, col_k, col_v, col_z), batch, seq, heads)

        x2, h2, route = _merge(y_a, y_b, proj, x2, modl, w_proj_a[l].astype(BF16), w_proj_b[l].astype(BF16),
                               w_out[l].astype(BF16), norm2_w[l].reshape(1, d), wr_t, rb, seq, 0, 1)

        expert_idx = route[0:TOP_K].T.astype(jnp.int32)
        gate_w = route[TOP_K:2 * TOP_K].T
        dest, src_tok, tile_expert, n_used = _dispatch_plan(expert_idx, tm_e, n_tiles)
        xs = jnp.take(h2, src_tok, axis=0, mode='clip')
        wg_l, wu_l, wd_l = (_expert_weights_bf16(w, l) for w in (w_gate, w_up, w_down))
        ys = _moe_experts(tile_expert, n_used, xs, wg_l, wu_l, wd_l, tm_e)
        y_pairs = jnp.take(ys, dest.reshape(t, TOP_K).T.reshape(-1), axis=0, mode='clip')
        x2 = _combine(x2, y_pairs, gate_w, modl, final_norm_w.reshape(1, d), seq, l == depth - 1)

    return x2.reshape(batch, seq, d)
```

```python
import functools
import math

import jax
import jax.numpy as jnp
from jax import lax
from jax.experimental import pallas as pl
from jax.experimental.pallas import tpu as pltpu

F32 = jnp.float32
BF16 = jnp.bfloat16
HIGHEST = lax.Precision.HIGHEST

RMS_EPS = 1e-6
S5_GROUP = 16
S5_STATE = 64
S5_CHUNK = 16
S5_OCT = 8
HEAD_DIM = 128
GDN_CHUNK = 64
GDN_TILE = 256
GDN_HEADS_PER_TRIP = 8
CONV_K = 4
N_EXPERTS = 16
N_GROUPS = 4
EXPERTS_PER_GROUP = N_EXPERTS // N_GROUPS
TOP_K = 2
MIB = 1024 * 1024


def _cparams(semantics, vmem_mib):
    return pltpu.CompilerParams(dimension_semantics=semantics, vmem_limit_bytes=vmem_mib * MIB)


def _silu(x):
    return x * jax.nn.sigmoid(x)


def _softplus(x):
    return jnp.maximum(x, 0.0) + jnp.log(1.0 + jnp.exp(-jnp.abs(x)))


def _rms_scale(x):
    return lax.rsqrt(jnp.mean(x * x, axis=-1, keepdims=True) + RMS_EPS)


def _pack_halves(x):
    n = x.shape[1] // 2
    lo = lax.bitcast_convert_type(x[:, :n].astype(BF16).astype(F32), jnp.uint32)
    hi = lax.bitcast_convert_type(x[:, n:].astype(BF16).astype(F32), jnp.uint32)
    return (lo >> 16) | hi


def _unpack_halves(p):
    lo = lax.bitcast_convert_type(p << 16, F32)
    hi = lax.bitcast_convert_type(p & jnp.uint32(0xFFFF0000), F32)
    return lo, hi


def _nt_dot(a, b):
    return lax.dot_general(a, b, (((1,), (1,)), ((), ())), preferred_element_type=F32)


def _tn_dot(a, b):
    return lax.dot_general(a, b, (((0,), (0,)), ((), ())), preferred_element_type=F32)


def _ada_kernel(c_ref, w_ref, b_ref, o_ref):
    ca = _silu(c_ref[...]).astype(BF16)
    o_ref[...] = jnp.dot(ca, w_ref[...].astype(BF16), preferred_element_type=F32) + b_ref[...]


def _ada(c_pad, w_ada, b_ada):
    depth, d, n = w_ada.shape
    rows = c_pad.shape[0]
    tn = min(n, 1024)
    return pl.pallas_call(
        _ada_kernel,
        grid=(depth, n // tn),
        in_specs=[pl.BlockSpec((rows, d), lambda l, j: (0, 0)),
                  pl.BlockSpec((None, d, tn), lambda l, j: (l, 0, j)),
                  pl.BlockSpec((None, 1, tn), lambda l, j: (l, 0, j))],
        out_specs=pl.BlockSpec((None, rows, tn), lambda l, j: (l, 0, j)),
        out_shape=jax.ShapeDtypeStruct((depth, rows, n), F32),
        compiler_params=_cparams(("parallel", "parallel"), 40),
    )(c_pad, w_ada, b_ada.reshape(depth, 1, n))


def _inproj_kernel(x_ref, mod_ref, nw_ref, w_ref, wsm_ref, o_ref, osm_ref, h_scr):
    @pl.when(pl.program_id(1) == 0)
    def _():
        x = x_ref[...]
        h = x * _rms_scale(x) * nw_ref[...] * (1.0 + mod_ref[1:2, :]) + mod_ref[0:1, :]
        hb = h.astype(BF16)
        h_scr[...] = hb
        osm_ref[...] = jnp.dot(hb, wsm_ref[...], preferred_element_type=F32)

    o_ref[...] = jnp.dot(h_scr[...], w_ref[...], preferred_element_type=F32)


def _inproj(x2, modl, nw, w_big, w_small, layer, seq):
    t, d = x2.shape
    n = w_big.shape[-1]
    tm = min(seq, 1024)
    tn = 1536
    per_batch = seq // tm
    return pl.pallas_call(
        _inproj_kernel,
        grid=(t // tm, n // tn),
        in_specs=[pl.BlockSpec((tm, d), lambda i, j: (i, 0)),
                  pl.BlockSpec((None, 6, d), lambda i, j: (i // per_batch, 0, 0)),
                  pl.BlockSpec((1, d), lambda i, j: (0, 0)),
                  pl.BlockSpec((None, d, tn), lambda i, j: (layer, 0, j)),
                  pl.BlockSpec((None, d, 128), lambda i, j: (layer, 0, 0))],
        out_specs=[pl.BlockSpec((tm, tn), lambda i, j: (i, j)),
                   pl.BlockSpec((tm, 128), lambda i, j: (i, 0))],
        out_shape=[jax.ShapeDtypeStruct((t, n), F32), jax.ShapeDtypeStruct((t, 128), F32)],
        scratch_shapes=[pltpu.VMEM((tm, d), BF16)],
        compiler_params=_cparams(("parallel", "arbitrary"), 52),
    )(x2, modl, nw, w_big, w_small)


def _repack_kernel(a_ref, b_ref, o_ref, *, gate_blocks, shift):
    c = pl.program_id(2)
    w = o_ref.shape[1]

    @pl.when(c < gate_blocks)
    def _():
        x = jnp.concatenate([a_ref[...], b_ref[...]], axis=1)
        o_ref[...] = pltpu.roll(x, x.shape[1] - shift, axis=1)[:, :w].astype(o_ref.dtype)

    @pl.when(c >= gate_blocks)
    def _():
        o_ref[...] = a_ref[...].astype(o_ref.dtype)


def _repack_in_proj(w_in, gate_start, gate_cols, lead_cols):
    depth, d, _ = w_in.shape
    wb = 1024
    shift = gate_start % 128
    base = gate_start - shift
    assert base % wb == 0 and gate_cols % wb == 0 and lead_cols % wb == 0 and lead_cols == base
    gate_blocks = gate_cols // wb
    rows = 512
    kern = functools.partial(_repack_kernel, gate_blocks=gate_blocks, shift=shift)
    return pl.pallas_call(
        kern,
        grid=(depth, d // rows, (gate_cols + lead_cols) // wb),
        in_specs=[pl.BlockSpec((None, rows, wb),
                               lambda l, r, c: (l, r, jnp.where(c < gate_blocks, base // wb + c, c - gate_blocks))),
                  pl.BlockSpec((None, rows, 128),
                               lambda l, r, c: (l, r, jnp.where(c < gate_blocks, (base + (c + 1) * wb) // 128, 0)))],
        out_specs=pl.BlockSpec((None, rows, wb), lambda l, r, c: (l, r, c)),
        out_shape=jax.ShapeDtypeStruct((depth, d, gate_cols + lead_cols), BF16),
        compiler_params=_cparams(("parallel", "parallel", "arbitrary"), 32),
    )(w_in, w_in)


def _s5_weights(a_re, a_im, log_dt, b_re, b_im, c_re, c_im, n_levels):
    tc = S5_CHUNK
    groups, p = a_re.shape
    n = b_re.shape[-1]
    oc = S5_OCT
    octs = groups // oc
    lam = lax.complex(a_re, a_im)
    dt = jnp.exp(log_dt)[:, None]
    log_a = lam * dt
    a_bar = jnp.exp(log_a)
    b_bar = ((a_bar - 1.0) / lam)[..., None] * lax.complex(b_re, b_im)
    c_mat = lax.complex(c_re, c_im)
    apow = jnp.exp(log_a[:, :, None] * jnp.arange(tc + 1, dtype=F32)[None, None, :])
    lane = jnp.arange(tc * n)
    e_t = (jnp.arange(tc)[:, None] == (lane // n)[None, :]).astype(F32)
    e_n = (jnp.arange(n)[:, None] == (lane % n)[None, :]).astype(F32)

    def expand(z, e):
        f = lambda r: jnp.einsum('gpk,kj->gpj', r, e, precision=HIGHEST)
        return lax.complex(f(jnp.real(z)), f(jnp.imag(z)))

    def re_negim_rows(z):
        return jnp.concatenate([jnp.real(z), -jnp.imag(z)], axis=1)

    c_rep = expand(jnp.transpose(c_mat, (0, 2, 1)), e_n)
    lag_c = re_negim_rows(expand(apow[:, :, :tc], e_t) * c_rep)
    hoc = re_negim_rows(expand(apow[:, :, 1:], e_t) * c_rep)
    b_t = jnp.transpose(b_bar, (0, 2, 1))
    b_cat = jnp.concatenate([jnp.real(b_t), jnp.imag(b_t)], axis=-1)
    kern = jnp.einsum('gnk,gkj->gnj', b_cat, lag_c, precision=HIGHEST)
    tcomp = jnp.stack([jnp.pad(kern[:, :, :(tc - s) * n], ((0, 0), (0, 0), (s * n, 0))) for s in range(tc)],
                      axis=1)
    tcomp = jnp.transpose(tcomp.reshape(octs, oc, tc, n, tc * n), (0, 2, 1, 3, 4)).reshape(octs, tc * oc * n, tc * n)
    ap_rev = jnp.transpose(apow[:, :, tc - 1 - jnp.arange(tc)], (0, 2, 1))
    hin_c = ap_rev[:, :, None, :] * b_t[:, None, :, :]
    hc = jnp.concatenate([jnp.real(hin_c), jnp.imag(hin_c)], axis=-1)
    hc = jnp.transpose(hc.reshape(octs, oc, tc, n, 2 * p), (0, 2, 1, 3, 4)).reshape(octs, tc * oc * n, 2 * p)
    hoc = hoc.reshape(octs, oc * 2 * p, tc * n)
    steps = (tc * (2.0 ** jnp.arange(n_levels, dtype=F32)))
    amul = jnp.exp(log_a[:, None, :] * steps[None, :, None])
    a1 = jnp.concatenate([jnp.real(amul), jnp.real(amul)], axis=-1)
    a2 = jnp.concatenate([-jnp.imag(amul), jnp.imag(amul)], axis=-1)

    def lanes_by_group(a):
        return jnp.transpose(a.reshape(octs, oc, n_levels, 2 * p), (0, 2, 1, 3)).reshape(octs, n_levels, oc * 2 * p)

    return (tcomp.astype(BF16), hc.astype(BF16), hoc.astype(BF16),
            lanes_by_group(a1).astype(F32), lanes_by_group(a2).astype(F32))


def _s5_replicate():
    r = lax.broadcasted_iota(jnp.int32, (S5_CHUNK * S5_GROUP, S5_CHUNK * 128), 0)
    c = lax.broadcasted_iota(jnp.int32, (S5_CHUNK * S5_GROUP, S5_CHUNK * 128), 1)
    return ((r // S5_GROUP == c // 128) & (r % S5_GROUP == c % S5_GROUP)).astype(BF16)


def _s5_core_kernel(u_ref, tc_ref, hc_ref, hoc_ref, a1_ref, a2_ref, rep_ref, y_ref, toep_scr, hin_scr, hout_scr,
                    *, n_levels):
    tc = S5_CHUNK
    lanes = u_ref.shape[1]
    state_lanes = hc_ref.shape[1]

    @pl.when(pl.program_id(1) == 0)
    def _():
        kd = toep_scr.shape[1]
        rb = 256
        col_g = (lax.broadcasted_iota(jnp.int32, (rb, kd), 1) // S5_GROUP) % S5_OCT
        row_i = lax.broadcasted_iota(jnp.int32, (rb, kd), 0)
        for i in range(toep_scr.shape[0] // rb):
            full = jnp.dot(tc_ref[i * rb:(i + 1) * rb, :], rep_ref[...], preferred_element_type=F32)
            row_g = ((row_i + i * rb) // S5_GROUP) % S5_OCT
            toep_scr[i * rb:(i + 1) * rb, :] = jnp.where(row_g == col_g, full, 0.0).astype(BF16)
        for i in range(hout_scr.shape[0] // rb):
            full = jnp.dot(hoc_ref[i * rb:(i + 1) * rb, :], rep_ref[...], preferred_element_type=F32)
            row_g = (row_i + i * rb) // state_lanes
            hout_scr[i * rb:(i + 1) * rb, :] = jnp.where(row_g == col_g, full, 0.0).astype(BF16)
        hrow_g = (lax.broadcasted_iota(jnp.int32, hc_ref.shape, 0) // S5_GROUP) % S5_OCT
        for g in range(S5_OCT):
            hin_scr[:, g * state_lanes:(g + 1) * state_lanes] = jnp.where(hrow_g == g, hc_ref[...], 0.0).astype(BF16)

    rows = u_ref.shape[0] // tc
    x = jnp.concatenate([u_ref[pl.ds(t, rows, stride=tc), :] for t in range(tc)], axis=1).astype(BF16)
    s_loc = jnp.dot(x, hin_scr[...], preferred_element_type=F32)
    pos = lax.broadcasted_iota(jnp.int32, (rows, state_lanes), 0)
    s_in = []
    for g in range(S5_OCT):
        ln = slice(g * state_lanes, (g + 1) * state_lanes)
        s = s_loc[:, ln]
        for k in range(n_levels):
            d = 1 << k
            prev = jnp.where(pos >= d, pltpu.roll(s, d, axis=0), 0.0)
            s = s + a1_ref[k:k + 1, ln] * prev + a2_ref[k:k + 1, ln] * pltpu.roll(prev, state_lanes // 2, axis=1)
        s_in.append(jnp.where(pos >= 1, pltpu.roll(s, 1, axis=0), 0.0).astype(BF16))
    y = (jnp.dot(x, toep_scr[...], preferred_element_type=F32)
         + jnp.dot(jnp.concatenate(s_in, axis=1), hout_scr[...], preferred_element_type=F32))
    for t in range(tc):
        y_ref[pl.ds(t, rows, stride=tc), :] = y[:, t * lanes:(t + 1) * lanes]


def _s5_core(proj, u_lane_block, tcomp, hc, hoc, a1, a2, rep, layer, batch, seq):
    t = proj.shape[0]
    _, octs, kdim, tn = tcomp.shape
    sl = hc.shape[-1]
    sw = hoc.shape[2]
    n_levels = a1.shape[2]
    lanes = kdim // S5_CHUNK
    kern = functools.partial(_s5_core_kernel, n_levels=n_levels)
    return pl.pallas_call(
        kern,
        grid=(octs, batch),
        in_specs=[pl.BlockSpec((seq, lanes), lambda o, b: (b, u_lane_block + o)),
                  pl.BlockSpec((None, None, kdim, tn), lambda o, b: (layer, o, 0, 0)),
                  pl.BlockSpec((None, None, kdim, sl), lambda o, b: (layer, o, 0, 0)),
                  pl.BlockSpec((None, None, sw, tn), lambda o, b: (layer, o, 0, 0)),
                  pl.BlockSpec((None, None, n_levels, sw), lambda o, b: (layer, o, 0, 0)),
                  pl.BlockSpec((None, None, n_levels, sw), lambda o, b: (layer, o, 0, 0)),
                  pl.BlockSpec((tn, kdim), lambda o, b: (0, 0))],
        out_specs=pl.BlockSpec((seq, lanes), lambda o, b: (b, o)),
        out_shape=jax.ShapeDtypeStruct((t, octs * lanes), F32),
        scratch_shapes=[pltpu.VMEM((kdim, kdim), BF16), pltpu.VMEM((kdim, sw), BF16), pltpu.VMEM((sw, kdim), BF16)],
        compiler_params=_cparams(("parallel", "arbitrary"), 48),
    )(proj, tcomp, hc, hoc, a1, a2, rep)


def _s5_post_kernel(y_ref, u_ref, d_ref, w_ref, b_ref, o_ref):
    y = jax.nn.gelu(y_ref[...] + d_ref[...] * u_ref[...])
    gate = jnp.dot(y.astype(BF16), w_ref[...], preferred_element_type=F32) + b_ref[...]
    o_ref[...] = (y * jax.nn.sigmoid(gate)).astype(o_ref.dtype)


def _s5_post(y_lin, proj, u_col, d_skip, w_glu, b_glu):
    t, w = y_lin.shape
    tm = min(t, 512)
    return pl.pallas_call(
        _s5_post_kernel,
        grid=(t // tm,),
        in_specs=[pl.BlockSpec((tm, w), lambda i: (i, 0)),
                  pl.BlockSpec((tm, w), lambda i: (i, u_col)),
                  pl.BlockSpec((1, w), lambda i: (0, 0)),
                  pl.BlockSpec((w, w), lambda i: (0, 0)),
                  pl.BlockSpec((1, w), lambda i: (0, 0))],
        out_specs=pl.BlockSpec((tm, w), lambda i: (i, 0)),
        out_shape=jax.ShapeDtypeStruct((t, w), BF16),
        compiler_params=_cparams(("parallel",), 32),
    )(y_lin, proj, d_skip, w_glu, b_glu)


def _gdn_kernel(q_ref, k_ref, v_ref, qh_ref, kh_ref, vh_ref, z_ref, sm_ref, cq_ref, ck_ref, cv_ref,
                hp_ref, nw_ref, tril_ref, cmask_ref, lmask_ref, o_ref, betab, gcb, grow, st_scr,
                cat_q, cat_k, cat_v, *, heads):
    tt = GDN_TILE
    hd = HEAD_DIM
    nck = tt // GDN_CHUNK
    first = pl.program_id(1) == 0

    @pl.when(first)
    def _():
        st_scr[...] = jnp.zeros_like(st_scr)

    sm = sm_ref[...]
    hp = hp_ref[...]
    beta_all = jax.nn.sigmoid(sm)
    g_all = hp[0:1, :] * _softplus(sm + hp[1:2, :])
    gc_all = jnp.dot(tril_ref[...], g_all, precision=HIGHEST, preferred_element_type=F32)
    gc_t = gc_all.T
    for h in range(heads):
        betab[h] = jnp.broadcast_to(beta_all[:, h:h + 1], (tt, hd))
        gcb[h] = jnp.broadcast_to(gc_all[:, heads + h:heads + h + 1], (tt, hd))
        grow[h:h + 1, :] = gc_t[heads + h:heads + h + 1, :]

    eye = (lax.broadcasted_iota(jnp.int32, (tt, tt), 0) == lax.broadcasted_iota(jnp.int32, (tt, tt), 1)).astype(F32)
    n_lv = lmask_ref.shape[0]

    for x_ref, halo_ref, cat in ((q_ref, qh_ref, cat_q), (k_ref, kh_ref, cat_k), (v_ref, vh_ref, cat_v)):
        cat[0:8, :] = jnp.where(first, 0.0, halo_ref[...])
        cat[8:8 + tt, :] = x_ref[...]

    def conv_silu(cat, cw_ref, hs):
        w = cw_ref[:, pl.ds(hs, hd)]
        acc = cat[pl.ds(8, tt), pl.ds(hs, hd)] * w[CONV_K - 1:CONV_K, :]
        for s in range(1, CONV_K):
            acc = acc + cat[pl.ds(8 - s, tt), pl.ds(hs, hd)] * w[CONV_K - 1 - s:CONV_K - s, :]
        return _silu(acc)

    def heads_lockstep(hs_idx):
        each = lambda f, *ls: [f(*a) for a in zip(*ls)]
        hs = [pl.multiple_of(h * hd, hd) for h in hs_idx]
        q = [conv_silu(cat_q, cq_ref, o) for o in hs]
        k = [conv_silu(cat_k, ck_ref, o) for o in hs]
        v = [conv_silu(cat_v, cv_ref, o) for o in hs]
        qn = each(lambda a: a * (lax.rsqrt(jnp.sum(a * a, axis=-1, keepdims=True) + RMS_EPS) * (hd ** -0.5)), q)
        kn = each(lambda a: a * lax.rsqrt(jnp.sum(a * a, axis=-1, keepdims=True) + RMS_EPS), k)
        bb = [betab[h] for h in hs_idx]
        gc = [gcb[h] for h in hs_idx]
        gr = [grow[pl.ds(h, 1), :] for h in hs_idx]
        decay = each(lambda c, r: jnp.exp(jnp.where(cmask_ref[0] > 0.0, jnp.concatenate([c, c], axis=1) - r, -1e30)),
                     gc, gr)
        kb = each(lambda a, b: a * b, kn, bb)
        knb = each(lambda a: a.astype(BF16), kn)
        lfull = each(lambda a, b, dc: _nt_dot(a.astype(BF16), b) * dc, kb, knb, decay)
        lmat = each(lambda a: a.astype(BF16), lfull)
        tinv = each(lambda a: eye - a * cmask_ref[1], lfull)
        for lv in range(1, n_lv):
            d = 1 << lv
            nb = tt // d
            tb = each(lambda a: a.astype(BF16), tinv)
            bd = each(lambda m: m * lmask_ref[lv], lmat)
            if d % 8 == 0:
                odd = each(lambda a: jnp.concatenate([a[i * d:(i + 1) * d] for i in range(1, nb, 2)], axis=0), tinv)
                tl = each(lambda a, b: jnp.dot(a.astype(BF16), b, preferred_element_type=F32).astype(BF16), odd, bd)
                upd = each(lambda a, b, c: a - jnp.dot(b, c, preferred_element_type=F32), odd, tl, tb)
                tinv = each(lambda a, u: jnp.concatenate(
                    [a[i * d:(i + 1) * d] if i % 2 == 0 else u[(i // 2) * d:(i // 2 + 1) * d] for i in range(nb)],
                    axis=0), tinv, upd)
            else:
                tl = each(lambda a, b: jnp.dot(a, b, preferred_element_type=F32).astype(BF16), tb, bd)
                tinv = each(lambda a, b, c: a - jnp.dot(b, c, preferred_element_type=F32), tinv, tl, tb)
        eg = each(jnp.exp, gc)
        rhs = each(lambda a, e, b, c: jnp.concatenate([a * e, b * c], axis=1).astype(BF16), kb, eg, v, bb)
        wu = each(lambda a, b: jnp.dot(a.astype(BF16), b, preferred_element_type=F32).astype(BF16), tinv, rhs)
        attn = each(lambda a, b, dc: (_nt_dot(a.astype(BF16), b) * dc).astype(BF16), qn, knb, decay)
        awu = each(lambda a, b: jnp.dot(a, b, preferred_element_type=F32), attn, wu)
        qp = each(lambda a, e, b: (a * e - b[:, :hd]).astype(BF16), qn, eg, awu)
        s = [st_scr[h] for h in hs_idx]
        outs = [[] for _ in hs_idx]
        for c in range(nck):
            r0 = c * GDN_CHUNK
            r1 = r0 + GDN_CHUNK
            gl = each(lambda a: a[r1 - 1:r1, :], gc)
            kd = each(lambda a, l, g: (a[r0:r1] * jnp.exp(l - g[r0:r1])).astype(BF16), kn, gl, gc)
            m = each(lambda a, b: _tn_dot(a, b[r0:r1]), kd, wu)
            sb = each(lambda a: a.astype(BF16), s)
            for i, (a, b, y) in enumerate(zip(qp, sb, awu)):
                outs[i].append(jnp.dot(a[r0:r1], b, preferred_element_type=F32) + y[r0:r1, hd:])
            s = each(lambda a, l, mm, b: a * jnp.exp(l) - jnp.dot(mm[:, :hd].astype(BF16), b,
                                                                  preferred_element_type=F32) + mm[:, hd:],
                     s, gl, m, sb)
        for i, h in enumerate(hs_idx):
            st_scr[h] = s[i]
            o = jnp.concatenate(outs[i], axis=0)
            z = z_ref[:, pl.ds(hs[i], hd)]
            o_ref[:, pl.ds(hs[i], hd)] = (o * _rms_scale(o) * nw_ref[...] * _silu(z)).astype(o_ref.dtype)

    def trip(j, carry):
        heads_lockstep([GDN_HEADS_PER_TRIP * j + i for i in range(GDN_HEADS_PER_TRIP)])
        return carry

    lax.fori_loop(0, heads // GDN_HEADS_PER_TRIP, trip, 0)


def _gdn_masks():
    tt = GDN_TILE
    ri = lax.broadcasted_iota(jnp.int32, (tt, tt), 0)
    ci = lax.broadcasted_iota(jnp.int32, (tt, tt), 1)
    same = (ri // GDN_CHUNK) == (ci // GDN_CHUNK)
    causal = same & (ci <= ri)
    tril = causal.astype(F32)
    cmask = jnp.stack([tril, (((ri % 2) == 1) & (ci == ri - 1)).astype(F32)])
    levels = []
    d = 1
    while d < GDN_CHUNK:
        levels.append((((ri // d) % 2) == 1) & ((ci // d) == (ri // d) - 1))
        d *= 2
    return tril, cmask, jnp.stack(levels).astype(BF16)


def _gdn(proj, small, conv_w, head_params, norm_w, masks, cols, batch, seq, heads):
    t = proj.shape[0]
    tt = GDN_TILE
    width = heads * HEAD_DIM
    tiles = seq // tt
    cq, ck, cv, cz = cols
    tril, cmask, lmask = masks

    def cur(col):
        return pl.BlockSpec((tt, width), lambda b, i: (b * tiles + i, col))

    def halo(col):
        return pl.BlockSpec((8, width), lambda b, i: (jnp.maximum((b * tiles + i) * (tt // 8) - 1, 0), col))

    def cw(col):
        return pl.BlockSpec((CONV_K, width), lambda b, i: (0, col))

    kern = functools.partial(_gdn_kernel, heads=heads)
    return pl.pallas_call(
        kern,
        grid=(batch, tiles),
        in_specs=[cur(cq), cur(ck), cur(cv), halo(cq), halo(ck), halo(cv), cur(cz),
                  pl.BlockSpec((tt, 128), lambda b, i: (b * tiles + i, 0)),
                  cw(0), cw(1), cw(2),
                  pl.BlockSpec((8, 128), lambda b, i: (0, 0)),
                  pl.BlockSpec((1, HEAD_DIM), lambda b, i: (0, 0)),
                  pl.BlockSpec((tt, tt), lambda b, i: (0, 0)),
                  pl.BlockSpec(cmask.shape, lambda b, i: (0, 0, 0)),
                  pl.BlockSpec(lmask.shape, lambda b, i: (0, 0, 0))],
        out_specs=pl.BlockSpec((tt, width), lambda b, i: (b * tiles + i, 0)),
        out_shape=jax.ShapeDtypeStruct((t, width), BF16),
        scratch_shapes=[pltpu.VMEM((heads, tt, HEAD_DIM), F32),
                        pltpu.VMEM((heads, tt, HEAD_DIM), F32),
                        pltpu.VMEM((8, tt), F32),
                        pltpu.VMEM((heads, HEAD_DIM, HEAD_DIM), F32),
                        pltpu.VMEM((tt + 8, width), F32),
                        pltpu.VMEM((tt + 8, width), F32),
                        pltpu.VMEM((tt + 8, width), F32)],
        compiler_params=_cparams(("parallel", "arbitrary"), 40),
    )(proj, proj, proj, proj, proj, proj, proj, small, conv_w, conv_w, conv_w, head_params, norm_w,
      tril, cmask, lmask)


def _route(h, wr_hl, rb):
    h_hi = h.astype(BF16)
    h_lo = (h - h_hi.astype(F32)).astype(BF16)
    first = jnp.dot(h_hi, wr_hl[...], preferred_element_type=F32)
    logits = first[:, :128] + first[:, 128:] + jnp.dot(h_lo, wr_hl[:, :128], preferred_element_type=F32)
    logits = logits.T[:N_EXPERTS]
    scores = jax.nn.sigmoid(logits)
    sel = scores + rb
    sel_rows = [sel[e:e + 1, :] for e in range(N_EXPERTS)]
    score_rows = [scores[e:e + 1, :] for e in range(N_EXPERTS)]
    epg = EXPERTS_PER_GROUP
    best_score = None
    best_group = None
    for g in range(N_GROUPS):
        rows = sel_rows[g * epg:(g + 1) * epg]
        gs = None
        for a in range(epg):
            for b in range(a + 1, epg):
                pair = rows[a] + rows[b]
                gs = pair if gs is None else jnp.maximum(gs, pair)
        if g == 0:
            best_score, best_group = gs, jnp.zeros_like(gs)
        else:
            better = gs > best_score
            best_group = jnp.where(better, float(g), best_group)
            best_score = jnp.where(better, gs, best_score)
    in_sel = []
    in_score = []
    for j in range(epg):
        a = sel_rows[j]
        b = score_rows[j]
        for g in range(1, N_GROUPS):
            pick = best_group == float(g)
            a = jnp.where(pick, sel_rows[g * epg + j], a)
            b = jnp.where(pick, score_rows[g * epg + j], b)
        in_sel.append(a)
        in_score.append(b)
    i1, m1, w1 = jnp.zeros_like(in_sel[0]), in_sel[0], in_score[0]
    for j in range(1, epg):
        better = in_sel[j] > m1
        i1 = jnp.where(better, float(j), i1)
        m1 = jnp.where(better, in_sel[j], m1)
        w1 = jnp.where(better, in_score[j], w1)
    i2 = m2 = w2 = None
    for j in range(epg):
        cand = jnp.where(i1 == float(j), -jnp.inf, in_sel[j])
        if j == 0:
            i2, m2, w2 = jnp.zeros_like(cand), cand, in_score[0]
        else:
            better = cand > m2
            i2 = jnp.where(better, float(j), i2)
            m2 = jnp.where(better, cand, m2)
            w2 = jnp.where(better, in_score[j], w2)
    total = w1 + w2
    zero = jnp.zeros_like(w1)
    return jnp.concatenate([best_group * epg + i1, best_group * epg + i2, w1 / total, w2 / total,
                            zero, zero, zero, zero], axis=0)


def _merge_kernel(ya_ref, yb_ref, ga_ref, gb_ref, x_ref, mod_ref, wa_ref, wb_ref, wo_ref, nw_ref, wrt_ref, rb_ref,
                  o_ref, h_ref, r_ref):
    pa = jnp.dot(ya_ref[...], wa_ref[...], preferred_element_type=F32)
    pb = jnp.dot(yb_ref[...], wb_ref[...], preferred_element_type=F32)
    merged = jax.nn.sigmoid(ga_ref[...]) * pa + jax.nn.sigmoid(gb_ref[...]) * pb
    out = jnp.dot(merged.astype(BF16), wo_ref[...], preferred_element_type=F32)
    xn = x_ref[...] + mod_ref[2:3, :] * out
    o_ref[...] = xn
    h = xn * _rms_scale(xn) * nw_ref[...] * (1.0 + mod_ref[4:5, :]) + mod_ref[3:4, :]
    h_ref[...] = _pack_halves(h)
    r_ref[...] = _route(h, wrt_ref, rb_ref[...])


def _merge(ya, yb, proj, x2, modl, wa, wb, wo, nw2, wr_t, rb, seq, ga_col, gb_col):
    t, d = x2.shape
    w = ya.shape[1]
    tm = min(seq, 256)
    per_batch = seq // tm
    const = dict(pipeline_mode=pl.Buffered(1))
    return pl.pallas_call(
        _merge_kernel,
        grid=(t // tm,),
        in_specs=[pl.BlockSpec((tm, w), lambda i: (i, 0)),
                  pl.BlockSpec((tm, w), lambda i: (i, 0)),
                  pl.BlockSpec((tm, d), lambda i: (i, ga_col)),
                  pl.BlockSpec((tm, d), lambda i: (i, gb_col)),
                  pl.BlockSpec((tm, d), lambda i: (i, 0)),
                  pl.BlockSpec((None, 6, d), lambda i: (i // per_batch, 0, 0)),
                  pl.BlockSpec((w, d), lambda i: (0, 0), **const),
                  pl.BlockSpec((w, d), lambda i: (0, 0), **const),
                  pl.BlockSpec((d, d), lambda i: (0, 0), **const),
                  pl.BlockSpec((1, d), lambda i: (0, 0)),
                  pl.BlockSpec((d, 256), lambda i: (0, 0)),
                  pl.BlockSpec((N_EXPERTS, 1), lambda i: (0, 0))],
        out_specs=[pl.BlockSpec((tm, d), lambda i: (i, 0)),
                   pl.BlockSpec((tm, d // 2), lambda i: (i, 0)),
                   pl.BlockSpec((8, tm), lambda i: (0, i))],
        out_shape=[jax.ShapeDtypeStruct((t, d), F32), jax.ShapeDtypeStruct((t, d // 2), jnp.uint32),
                   jax.ShapeDtypeStruct((8, t), F32)],
        compiler_params=_cparams(("parallel",), 52),
    )(ya, yb, proj, proj, x2, modl, wa, wb, wo, nw2, wr_t, rb)


def _moe_kernel(te_ref, nu_ref, x_ref, wg_ref, wu_ref, wd_ref, o_ref):
    used = pl.program_id(0) < nu_ref[0]

    @pl.when(used)
    def _():
        x_lo, x_hi = _unpack_halves(x_ref[...])
        x_lo = x_lo.astype(BF16)
        x_hi = x_hi.astype(BF16)
        half = x_lo.shape[1]

        def up(w_ref):
            return (jnp.dot(x_lo, w_ref[:half, :], preferred_element_type=F32)
                    + jnp.dot(x_hi, w_ref[half:, :], preferred_element_type=F32))

        hid = (_silu(up(wg_ref)) * up(wu_ref)).astype(BF16)
        o_ref[...] = _pack_halves(jnp.dot(hid, wd_ref[...], preferred_element_type=F32))

    @pl.when(jnp.logical_not(used))
    def _():
        o_ref[...] = jnp.zeros_like(o_ref)


def _cast_kernel(w_ref, o_ref):
    o_ref[...] = w_ref[...].astype(o_ref.dtype)


def _expert_weights_bf16(w, layer):
    _, e, a, b = w.shape
    ta = min(a, (1 << 20) // b)
    return pl.pallas_call(
        _cast_kernel,
        grid=(e, a // ta),
        in_specs=[pl.BlockSpec((None, None, ta, b), lambda i, j: (layer, i, j, 0))],
        out_specs=pl.BlockSpec((None, ta, b), lambda i, j: (i, j, 0)),
        out_shape=jax.ShapeDtypeStruct((e, a, b), BF16),
        compiler_params=_cparams(("parallel", "parallel"), 32),
    )(w)


def _moe_experts(tile_expert, n_used, xs, wg, wu, wd, tm):
    p, dh = xs.shape
    d = 2 * dh
    f = wg.shape[-1]
    grid_spec = pltpu.PrefetchScalarGridSpec(
        num_scalar_prefetch=2,
        grid=(p // tm,),
        in_specs=[pl.BlockSpec((tm, dh), lambda i, te, nu: (i, 0)),
                  pl.BlockSpec((None, d, f), lambda i, te, nu: (te[i], 0, 0)),
                  pl.BlockSpec((None, d, f), lambda i, te, nu: (te[i], 0, 0)),
                  pl.BlockSpec((None, f, d), lambda i, te, nu: (te[i], 0, 0))],
        out_specs=pl.BlockSpec((tm, dh), lambda i, te, nu: (i, 0)),
    )
    return pl.pallas_call(
        _moe_kernel,
        grid_spec=grid_spec,
        out_shape=jax.ShapeDtypeStruct((p, dh), jnp.uint32),
        compiler_params=_cparams(("arbitrary",), 52),
    )(tile_expert, n_used, xs, wg, wu, wd)


def _dispatch_plan(expert_idx, tm, n_tiles):
    flat_e = expert_idx.reshape(-1)
    onehot = (flat_e[:, None] == jnp.arange(N_EXPERTS, dtype=jnp.int32)[None, :]).astype(jnp.int32)
    csum = jnp.cumsum(onehot, axis=0)
    rank = jnp.sum((csum - onehot) * onehot, axis=1)
    sizes = csum[-1]
    padded = ((sizes + tm - 1) // tm) * tm
    pad_end = jnp.cumsum(padded)
    pad_start = pad_end - padded
    dest = pad_start[flat_e] + rank
    tile_start = jnp.arange(n_tiles, dtype=jnp.int32) * tm
    tile_expert = jnp.minimum(jnp.sum((tile_start[:, None] >= pad_end[None, :]).astype(jnp.int32), axis=1),
                              N_EXPERTS - 1).astype(jnp.int32)
    n_used = (pad_end[-1] // tm).astype(jnp.int32).reshape(1)
    dest = dest.astype(jnp.int32)
    n_slots = flat_e.shape[0]
    _, by_slot = lax.sort_key_val(dest, jnp.arange(n_slots, dtype=jnp.int32) // TOP_K)
    start = jnp.cumsum(sizes) - sizes
    tile_off = tile_start - pad_start[tile_expert]
    tile_valid = jnp.clip(sizes[tile_expert] - tile_off, 0, tm)
    lane = jnp.arange(tm, dtype=jnp.int32)[None, :]
    pick = jnp.clip((start[tile_expert] + tile_off)[:, None] + lane, 0, n_slots - 1)
    filler = (tile_start[:, None] + lane) % (n_slots // TOP_K)
    src_tok = jnp.where(lane < tile_valid[:, None], jnp.take(by_slot, pick.reshape(-1)).reshape(n_tiles, tm), filler)
    return dest, src_tok.reshape(-1).astype(jnp.int32), tile_expert, n_used


def _combine_kernel(x_ref, y0_ref, y1_ref, gw_ref, mod_ref, fw_ref, o_ref, *, final_norm):
    gw = gw_ref[...]
    y0_lo, y0_hi = _unpack_halves(y0_ref[...])
    y1_lo, y1_hi = _unpack_halves(y1_ref[...])
    moe = jnp.concatenate([gw[:, 0:1] * y0_lo + gw[:, 1:2] * y1_lo, gw[:, 0:1] * y0_hi + gw[:, 1:2] * y1_hi], axis=1)
    xn = x_ref[...] + mod_ref[5:6, :] * moe
    if final_norm:
        xn = xn * _rms_scale(xn) * fw_ref[...]
    o_ref[...] = xn


def _combine(x2, y_pairs, gate_w, modl, final_w, seq, final_norm):
    t, d = x2.shape
    tm = min(seq, 512)
    per_batch = seq // tm
    nblk = t // tm
    return pl.pallas_call(
        functools.partial(_combine_kernel, final_norm=final_norm),
        grid=(nblk,),
        in_specs=[pl.BlockSpec((tm, d), lambda i: (i, 0)),
                  pl.BlockSpec((tm, d // 2), lambda i: (i, 0)),
                  pl.BlockSpec((tm, d // 2), lambda i: (i + nblk, 0)),
                  pl.BlockSpec((tm, TOP_K), lambda i: (i, 0)),
                  pl.BlockSpec((None, 6, d), lambda i: (i // per_batch, 0, 0)),
                  pl.BlockSpec((1, d), lambda i: (0, 0))],
        out_specs=pl.BlockSpec((tm, d), lambda i: (i, 0)),
        out_shape=jax.ShapeDtypeStruct((t, d), F32),
        compiler_params=_cparams(("parallel",), 40),
    )(x2, y_pairs, y_pairs, gate_w, modl, final_w)


def kernel(x, c, w_ada, b_ada, norm1_w, norm2_w, w_in, s5_a_re, s5_a_im, s5_log_dt, s5_b_re, s5_b_im, s5_c_re, s5_c_im, s5_d, s5_w_glu, s5_b_glu, gdn_conv_w, gdn_a_log, gdn_dt_bias, gdn_norm_w, w_proj_a, w_proj_b, w_out, w_router, router_bias, w_gate, w_up, w_down, final_norm_w):
    batch, seq, d = x.shape
    depth = w_ada.shape[0]
    t = batch * seq
    s5w = s5_d.shape[1]
    gdw = w_proj_b.shape[1]
    heads = gdn_a_log.shape[1]
    assert gdw == heads * HEAD_DIM and s5w == gdw and d == 2 * gdw
    assert seq % GDN_TILE == 0 and seq % S5_CHUNK == 0
    n_levels = int(math.log2(seq // S5_CHUNK))
    assert S5_CHUNK << n_levels == seq and S5_OCT * S5_GROUP == 128

    c_pad = jnp.zeros((8, d), F32).at[:batch].set(c)
    mod = _ada(c_pad, w_ada, b_ada)

    o_u, o_q, o_k, o_v, o_z = 0, s5w, 2 * s5w, 3 * s5w, 4 * s5w
    o_ba = 5 * s5w
    o_ga = o_ba + 2 * heads
    o_gb = o_ga + d
    col_u, col_q, col_k, col_v, col_z = 4, 5, 6, 7, 8

    gdn_masks = _gdn_masks()
    s5_rep = _s5_replicate()
    s5_ops = jax.vmap(functools.partial(_s5_weights, n_levels=n_levels))(
        s5_a_re, s5_a_im, s5_log_dt, s5_b_re, s5_b_im, s5_c_re, s5_c_im)

    wr_hi = w_router.astype(BF16)
    wr_lo = (w_router - wr_hi.astype(F32)).astype(BF16)
    pad_e = ((0, 0), (0, 128 - N_EXPERTS))
    wr_t = jnp.concatenate([jnp.pad(wr_hi, pad_e), jnp.pad(wr_lo, pad_e)], axis=1)
    rb = router_bias.reshape(N_EXPERTS, 1).astype(F32)

    tm_e = min(512, t)
    n_tiles = (TOP_K * t) // tm_e + N_EXPERTS

    w_big = _repack_in_proj(w_in, o_ga, 2 * d, o_ba)
    w_small = jnp.pad(w_in[:, :, o_ba:o_ga], ((0, 0), (0, 0), (0, 128 - 2 * heads))).astype(BF16)

    x2 = x.reshape(t, d)
    for l in range(depth):
        modl = mod[l, :batch].reshape(batch, 6, d)
        proj, small = _inproj(x2, modl, norm1_w[l].reshape(1, d), w_big, w_small, l, seq)

        y_lin = _s5_core(proj, col_u * s5w // 128, *s5_ops, s5_rep, l, batch, seq)
        y_a = _s5_post(y_lin, proj, col_u, s5_d[l].reshape(1, s5w), s5_w_glu[l].astype(BF16),
                       s5_b_glu[l].reshape(1, s5w))

        head_params = jnp.zeros((8, 128), F32)
        head_params = head_params.at[0, heads:2 * heads].set(-jnp.exp(gdn_a_log[l]))
        head_params = head_params.at[1, heads:2 * heads].set(gdn_dt_bias[l])
        y_b = _gdn(proj, small, gdn_conv_w[l], head_params, gdn_norm_w[l].reshape(1, HEAD_DIM), gdn_masks,
                   (col_q, col_k, col_v, col_z), batch, seq, heads)

        x2, h2, route = _merge(y_a, y_b, proj, x2, modl, w_proj_a[l].astype(BF16), w_proj_b[l].astype(BF16),
                               w_out[l].astype(BF16), norm2_w[l].reshape(1, d), wr_t, rb, seq, 0, 1)

        expert_idx = route[0:TOP_K].T.astype(jnp.int32)
        gate_w = route[TOP_K:2 * TOP_K].T
        dest, src_tok, tile_expert, n_used = _dispatch_plan(expert_idx, tm_e, n_tiles)
        xs = jnp.take(h2, src_tok, axis=0, mode='clip')
        if l == 0:
            wg_l, wu_l = _expert_weights_bf16(w_gate, 0), _expert_weights_bf16(w_up, 0)
        wd_l = _expert_weights_bf16(w_down, l)
        ys = _moe_experts(tile_expert, n_used, xs, wg_l, wu_l, wd_l, tm_e)
        y_pairs = jnp.take(ys, dest.reshape(t, TOP_K).T.reshape(-1), axis=0, mode='clip')
        if l + 1 < depth:
            wg_l, wu_l = _expert_weights_bf16(w_gate, l + 1), _expert_weights_bf16(w_up, l + 1)
        x2 = _combine(x2, y_pairs, gate_w, modl, final_norm_w.reshape(1, d), seq, l == depth - 1)

    return x2.reshape(batch, seq, d)
```

```python
import functools
import math

import jax
import jax.numpy as jnp
from jax import lax
from jax.experimental import pallas as pl
from jax.experimental.pallas import tpu as pltpu

F32 = jnp.float32
BF16 = jnp.bfloat16
HIGHEST = lax.Precision.HIGHEST

RMS_EPS = 1e-6
S5_GROUP = 16
S5_STATE = 64
S5_CHUNK = 16
S5_OCT = 8
HEAD_DIM = 128
GDN_CHUNK = 64
GDN_TILE = 256
GDN_HEADS_PER_TRIP = 8
CONV_K = 4
N_EXPERTS = 16
N_GROUPS = 4
EXPERTS_PER_GROUP = N_EXPERTS // N_GROUPS
TOP_K = 2
MIB = 1024 * 1024


def _cparams(semantics, vmem_mib):
    return pltpu.CompilerParams(dimension_semantics=semantics, vmem_limit_bytes=vmem_mib * MIB)


def _silu(x):
    return x * jax.nn.sigmoid(x)


def _softplus(x):
    return jnp.maximum(x, 0.0) + jnp.log(1.0 + jnp.exp(-jnp.abs(x)))


def _rms_scale(x):
    return lax.rsqrt(jnp.mean(x * x, axis=-1, keepdims=True) + RMS_EPS)


def _pack_halves(x):
    n = x.shape[1] // 2
    lo = lax.bitcast_convert_type(x[:, :n].astype(BF16).astype(F32), jnp.uint32)
    hi = lax.bitcast_convert_type(x[:, n:].astype(BF16).astype(F32), jnp.uint32)
    return (lo >> 16) | hi


def _unpack_halves(p):
    lo = lax.bitcast_convert_type(p << 16, F32)
    hi = lax.bitcast_convert_type(p & jnp.uint32(0xFFFF0000), F32)
    return lo, hi


def _nt_dot(a, b):
    return lax.dot_general(a, b, (((1,), (1,)), ((), ())), preferred_element_type=F32)


def _tn_dot(a, b):
    return lax.dot_general(a, b, (((0,), (0,)), ((), ())), preferred_element_type=F32)


def _ada_kernel(c_ref, w_ref, b_ref, o_ref):
    ca = _silu(c_ref[...]).astype(BF16)
    o_ref[...] = jnp.dot(ca, w_ref[...].astype(BF16), preferred_element_type=F32) + b_ref[...]


def _ada(c_pad, w_ada, b_ada):
    depth, d, n = w_ada.shape
    rows = c_pad.shape[0]
    tn = min(n, 1024)
    return pl.pallas_call(
        _ada_kernel,
        grid=(depth, n // tn),
        in_specs=[pl.BlockSpec((rows, d), lambda l, j: (0, 0)),
                  pl.BlockSpec((None, d, tn), lambda l, j: (l, 0, j)),
                  pl.BlockSpec((None, 1, tn), lambda l, j: (l, 0, j))],
        out_specs=pl.BlockSpec((None, rows, tn), lambda l, j: (l, 0, j)),
        out_shape=jax.ShapeDtypeStruct((depth, rows, n), F32),
        compiler_params=_cparams(("parallel", "parallel"), 40),
    )(c_pad, w_ada, b_ada.reshape(depth, 1, n))


def _inproj_kernel(x_ref, mod_ref, nw_ref, w_ref, wsm_ref, o_ref, osm_ref, h_scr):
    @pl.when(pl.program_id(1) == 0)
    def _():
        x = x_ref[...]
        h = x * _rms_scale(x) * nw_ref[...] * (1.0 + mod_ref[1:2, :]) + mod_ref[0:1, :]
        hb = h.astype(BF16)
        h_scr[...] = hb
        osm_ref[...] = _nt_dot(hb, wsm_ref[...])

    o_ref[...] = _nt_dot(h_scr[...], w_ref[...])


def _inproj(x2, modl, nw, w_big, w_small, layer, seq):
    t, d = x2.shape
    n = w_big.shape[1]
    tm = min(seq, 1024)
    tn = 1536
    per_batch = seq // tm
    return pl.pallas_call(
        _inproj_kernel,
        grid=(t // tm, n // tn),
        in_specs=[pl.BlockSpec((tm, d), lambda i, j: (i, 0)),
                  pl.BlockSpec((None, 6, d), lambda i, j: (i // per_batch, 0, 0)),
                  pl.BlockSpec((1, d), lambda i, j: (0, 0)),
                  pl.BlockSpec((None, tn, d), lambda i, j: (layer, j, 0)),
                  pl.BlockSpec((None, 128, d), lambda i, j: (layer, 0, 0))],
        out_specs=[pl.BlockSpec((tm, tn), lambda i, j: (i, j)),
                   pl.BlockSpec((tm, 128), lambda i, j: (i, 0))],
        out_shape=[jax.ShapeDtypeStruct((t, n), F32), jax.ShapeDtypeStruct((t, 128), F32)],
        scratch_shapes=[pltpu.VMEM((tm, d), BF16)],
        compiler_params=_cparams(("parallel", "arbitrary"), 52),
    )(x2, modl, nw, w_big, w_small)


def _repack_kernel(a_ref, b_ref, o_ref, *, gate_blocks, shift):
    c = pl.program_id(1)

    @pl.when(c < gate_blocks)
    def _():
        x = jnp.concatenate([a_ref[...], b_ref[...]], axis=0)
        o_ref[...] = x[shift:shift + o_ref.shape[0]].astype(o_ref.dtype)

    @pl.when(c >= gate_blocks)
    def _():
        o_ref[...] = a_ref[...].astype(o_ref.dtype)


def _repack_in_proj(w_in_t, gate_start, gate_cols, lead_cols):
    depth, _, d = w_in_t.shape
    rb = 512
    shift = gate_start % rb
    base = gate_start - shift
    assert shift % 16 == 0 and gate_cols % rb == 0 and lead_cols % rb == 0 and lead_cols == base
    gate_blocks = gate_cols // rb
    kern = functools.partial(_repack_kernel, gate_blocks=gate_blocks, shift=shift)
    return pl.pallas_call(
        kern,
        grid=(depth, (gate_cols + lead_cols) // rb),
        in_specs=[pl.BlockSpec((None, rb, d),
                               lambda l, c: (l, jnp.where(c < gate_blocks, base // rb + c, c - gate_blocks), 0)),
                  pl.BlockSpec((None, shift, d),
                               lambda l, c: (l, jnp.where(c < gate_blocks, (base + (c + 1) * rb) // shift, 0), 0))],
        out_specs=pl.BlockSpec((None, rb, d), lambda l, c: (l, c, 0)),
        out_shape=jax.ShapeDtypeStruct((depth, gate_cols + lead_cols, d), BF16),
        compiler_params=_cparams(("parallel", "arbitrary"), 32),
    )(w_in_t, w_in_t)


def _s5_weights(a_re, a_im, log_dt, b_re, b_im, c_re, c_im, n_levels):
    tc = S5_CHUNK
    groups, p = a_re.shape
    n = b_re.shape[-1]
    oc = S5_OCT
    octs = groups // oc
    lam = lax.complex(a_re, a_im)
    dt = jnp.exp(log_dt)[:, None]
    log_a = lam * dt
    a_bar = jnp.exp(log_a)
    b_bar = ((a_bar - 1.0) / lam)[..., None] * lax.complex(b_re, b_im)
    c_mat = lax.complex(c_re, c_im)
    apow = jnp.exp(log_a[:, :, None] * jnp.arange(tc + 1, dtype=F32)[None, None, :])
    lane = jnp.arange(tc * n)
    e_t = (jnp.arange(tc)[:, None] == (lane // n)[None, :]).astype(F32)
    e_n = (jnp.arange(n)[:, None] == (lane % n)[None, :]).astype(F32)

    def expand(z, e):
        f = lambda r: jnp.einsum('gpk,kj->gpj', r, e, precision=HIGHEST)
        return lax.complex(f(jnp.real(z)), f(jnp.imag(z)))

    def re_negim_rows(z):
        return jnp.concatenate([jnp.real(z), -jnp.imag(z)], axis=1)

    c_rep = expand(jnp.transpose(c_mat, (0, 2, 1)), e_n)
    lag_c = re_negim_rows(expand(apow[:, :, :tc], e_t) * c_rep)
    hoc = re_negim_rows(expand(apow[:, :, 1:], e_t) * c_rep)
    b_t = jnp.transpose(b_bar, (0, 2, 1))
    b_cat = jnp.concatenate([jnp.real(b_t), jnp.imag(b_t)], axis=-1)
    kern = jnp.einsum('gnk,gkj->gnj', b_cat, lag_c, precision=HIGHEST)
    tcomp = jnp.stack([jnp.pad(kern[:, :, :(tc - s) * n], ((0, 0), (0, 0), (s * n, 0))) for s in range(tc)],
                      axis=1)
    tcomp = jnp.transpose(tcomp.reshape(octs, oc, tc, n, tc * n), (0, 2, 1, 3, 4)).reshape(octs, tc * oc * n, tc * n)
    ap_rev = jnp.transpose(apow[:, :, tc - 1 - jnp.arange(tc)], (0, 2, 1))
    hin_c = ap_rev[:, :, None, :] * b_t[:, None, :, :]
    hc = jnp.concatenate([jnp.real(hin_c), jnp.imag(hin_c)], axis=-1)
    hc = jnp.transpose(hc.reshape(octs, oc, tc, n, 2 * p), (0, 2, 1, 3, 4)).reshape(octs, tc * oc * n, 2 * p)
    hoc = hoc.reshape(octs, oc * 2 * p, tc * n)
    steps = (tc * (2.0 ** jnp.arange(n_levels, dtype=F32)))
    amul = jnp.exp(log_a[:, None, :] * steps[None, :, None])
    a1 = jnp.concatenate([jnp.real(amul), jnp.real(amul)], axis=-1)
    a2 = jnp.concatenate([-jnp.imag(amul), jnp.imag(amul)], axis=-1)

    def lanes_by_group(a):
        return jnp.transpose(a.reshape(octs, oc, n_levels, 2 * p), (0, 2, 1, 3)).reshape(octs, n_levels, oc * 2 * p)

    return (tcomp.astype(BF16), hc.astype(BF16), hoc.astype(BF16),
            lanes_by_group(a1).astype(F32), lanes_by_group(a2).astype(F32))


def _s5_replicate():
    r = lax.broadcasted_iota(jnp.int32, (S5_CHUNK * S5_GROUP, S5_CHUNK * 128), 0)
    c = lax.broadcasted_iota(jnp.int32, (S5_CHUNK * S5_GROUP, S5_CHUNK * 128), 1)
    return ((r // S5_GROUP == c // 128) & (r % S5_GROUP == c % S5_GROUP)).astype(BF16)


def _s5_core_kernel(u_ref, tc_ref, hc_ref, hoc_ref, a1_ref, a2_ref, rep_ref, y_ref, toep_scr, hin_scr, hout_scr,
                    *, n_levels):
    tc = S5_CHUNK
    lanes = u_ref.shape[1]
    state_lanes = hc_ref.shape[1]

    @pl.when(pl.program_id(1) == 0)
    def _():
        kd = toep_scr.shape[1]
        rb = 256
        col_g = (lax.broadcasted_iota(jnp.int32, (rb, kd), 1) // S5_GROUP) % S5_OCT
        row_i = lax.broadcasted_iota(jnp.int32, (rb, kd), 0)
        for i in range(toep_scr.shape[0] // rb):
            full = jnp.dot(tc_ref[i * rb:(i + 1) * rb, :], rep_ref[...], preferred_element_type=F32)
            row_g = ((row_i + i * rb) // S5_GROUP) % S5_OCT
            toep_scr[i * rb:(i + 1) * rb, :] = jnp.where(row_g == col_g, full, 0.0).astype(BF16)
        for i in range(hout_scr.shape[0] // rb):
            full = jnp.dot(hoc_ref[i * rb:(i + 1) * rb, :], rep_ref[...], preferred_element_type=F32)
            row_g = (row_i + i * rb) // state_lanes
            hout_scr[i * rb:(i + 1) * rb, :] = jnp.where(row_g == col_g, full, 0.0).astype(BF16)
        hrow_g = (lax.broadcasted_iota(jnp.int32, hc_ref.shape, 0) // S5_GROUP) % S5_OCT
        for g in range(S5_OCT):
            hin_scr[:, g * state_lanes:(g + 1) * state_lanes] = jnp.where(hrow_g == g, hc_ref[...], 0.0).astype(BF16)

    rows = u_ref.shape[0] // tc
    x = jnp.concatenate([u_ref[pl.ds(t, rows, stride=tc), :] for t in range(tc)], axis=1).astype(BF16)
    s_loc = jnp.dot(x, hin_scr[...], preferred_element_type=F32)
    pos = lax.broadcasted_iota(jnp.int32, (rows, state_lanes), 0)
    s_in = []
    for g in range(S5_OCT):
        ln = slice(g * state_lanes, (g + 1) * state_lanes)
        s = s_loc[:, ln]
        for k in range(n_levels):
            d = 1 << k
            prev = jnp.where(pos >= d, pltpu.roll(s, d, axis=0), 0.0)
            s = s + a1_ref[k:k + 1, ln] * prev + a2_ref[k:k + 1, ln] * pltpu.roll(prev, state_lanes // 2, axis=1)
        s_in.append(jnp.where(pos >= 1, pltpu.roll(s, 1, axis=0), 0.0).astype(BF16))
    y = (jnp.dot(x, toep_scr[...], preferred_element_type=F32)
         + jnp.dot(jnp.concatenate(s_in, axis=1), hout_scr[...], preferred_element_type=F32))
    for t in range(tc):
        y_ref[pl.ds(t, rows, stride=tc), :] = y[:, t * lanes:(t + 1) * lanes]


def _s5_core(proj, u_lane_block, tcomp, hc, hoc, a1, a2, rep, layer, batch, seq):
    t = proj.shape[0]
    _, octs, kdim, tn = tcomp.shape
    sl = hc.shape[-1]
    sw = hoc.shape[2]
    n_levels = a1.shape[2]
    lanes = kdim // S5_CHUNK
    kern = functools.partial(_s5_core_kernel, n_levels=n_levels)
    return pl.pallas_call(
        kern,
        grid=(octs, batch),
        in_specs=[pl.BlockSpec((seq, lanes), lambda o, b: (b, u_lane_block + o)),
                  pl.BlockSpec((None, None, kdim, tn), lambda o, b: (layer, o, 0, 0)),
                  pl.BlockSpec((None, None, kdim, sl), lambda o, b: (layer, o, 0, 0)),
                  pl.BlockSpec((None, None, sw, tn), lambda o, b: (layer, o, 0, 0)),
                  pl.BlockSpec((None, None, n_levels, sw), lambda o, b: (layer, o, 0, 0)),
                  pl.BlockSpec((None, None, n_levels, sw), lambda o, b: (layer, o, 0, 0)),
                  pl.BlockSpec((tn, kdim), lambda o, b: (0, 0))],
        out_specs=pl.BlockSpec((seq, lanes), lambda o, b: (b, o)),
        out_shape=jax.ShapeDtypeStruct((t, octs * lanes), F32),
        scratch_shapes=[pltpu.VMEM((kdim, kdim), BF16), pltpu.VMEM((kdim, sw), BF16), pltpu.VMEM((sw, kdim), BF16)],
        compiler_params=_cparams(("parallel", "arbitrary"), 48),
    )(proj, tcomp, hc, hoc, a1, a2, rep)


def _s5_post_kernel(y_ref, u_ref, d_ref, w_ref, b_ref, o_ref):
    y = jax.nn.gelu(y_ref[...] + d_ref[...] * u_ref[...])
    gate = jnp.dot(y.astype(BF16), w_ref[...], preferred_element_type=F32) + b_ref[...]
    o_ref[...] = (y * jax.nn.sigmoid(gate)).astype(o_ref.dtype)


def _s5_post(y_lin, proj, u_col, d_skip, w_glu, b_glu):
    t, w = y_lin.shape
    tm = min(t, 512)
    return pl.pallas_call(
        _s5_post_kernel,
        grid=(t // tm,),
        in_specs=[pl.BlockSpec((tm, w), lambda i: (i, 0)),
                  pl.BlockSpec((tm, w), lambda i: (i, u_col)),
                  pl.BlockSpec((1, w), lambda i: (0, 0)),
                  pl.BlockSpec((w, w), lambda i: (0, 0)),
                  pl.BlockSpec((1, w), lambda i: (0, 0))],
        out_specs=pl.BlockSpec((tm, w), lambda i: (i, 0)),
        out_shape=jax.ShapeDtypeStruct((t, w), BF16),
        compiler_params=_cparams(("parallel",), 32),
    )(y_lin, proj, d_skip, w_glu, b_glu)


def _gdn_kernel(q_ref, k_ref, v_ref, qh_ref, kh_ref, vh_ref, z_ref, sm_ref, cq_ref, ck_ref, cv_ref,
                hp_ref, nw_ref, tril_ref, cmask_ref, lmask_ref, o_ref, betab, gcb, grow, st_scr,
                cat_q, cat_k, cat_v, *, heads):
    tt = GDN_TILE
    hd = HEAD_DIM
    nck = tt // GDN_CHUNK
    first = pl.program_id(1) == 0

    @pl.when(first)
    def _():
        st_scr[...] = jnp.zeros_like(st_scr)

    sm = sm_ref[...]
    hp = hp_ref[...]
    beta_all = jax.nn.sigmoid(sm)
    g_all = hp[0:1, :] * _softplus(sm + hp[1:2, :])
    gc_all = jnp.dot(tril_ref[...], g_all, precision=HIGHEST, preferred_element_type=F32)
    gc_t = gc_all.T
    for h in range(heads):
        betab[h] = jnp.broadcast_to(beta_all[:, h:h + 1], (tt, hd))
        gcb[h] = jnp.broadcast_to(gc_all[:, heads + h:heads + h + 1], (tt, hd))
        grow[h:h + 1, :] = gc_t[heads + h:heads + h + 1, :]

    eye = (lax.broadcasted_iota(jnp.int32, (tt, tt), 0) == lax.broadcasted_iota(jnp.int32, (tt, tt), 1)).astype(F32)
    n_lv = lmask_ref.shape[0]

    for x_ref, halo_ref, cat in ((q_ref, qh_ref, cat_q), (k_ref, kh_ref, cat_k), (v_ref, vh_ref, cat_v)):
        cat[0:8, :] = jnp.where(first, 0.0, halo_ref[...])
        cat[8:8 + tt, :] = x_ref[...]

    def conv_silu(cat, cw_ref, hs):
        w = cw_ref[:, pl.ds(hs, hd)]
        acc = cat[pl.ds(8, tt), pl.ds(hs, hd)] * w[CONV_K - 1:CONV_K, :]
        for s in range(1, CONV_K):
            acc = acc + cat[pl.ds(8 - s, tt), pl.ds(hs, hd)] * w[CONV_K - 1 - s:CONV_K - s, :]
        return _silu(acc)

    def heads_lockstep(hs_idx):
        each = lambda f, *ls: [f(*a) for a in zip(*ls)]
        hs = [pl.multiple_of(h * hd, hd) for h in hs_idx]
        q = [conv_silu(cat_q, cq_ref, o) for o in hs]
        k = [conv_silu(cat_k, ck_ref, o) for o in hs]
        v = [conv_silu(cat_v, cv_ref, o) for o in hs]
        qn = each(lambda a: a * (lax.rsqrt(jnp.sum(a * a, axis=-1, keepdims=True) + RMS_EPS) * (hd ** -0.5)), q)
        kn = each(lambda a: a * lax.rsqrt(jnp.sum(a * a, axis=-1, keepdims=True) + RMS_EPS), k)
        bb = [betab[h] for h in hs_idx]
        gc = [gcb[h] for h in hs_idx]
        gr = [grow[pl.ds(h, 1), :] for h in hs_idx]
        decay = each(lambda c, r: jnp.exp(jnp.where(cmask_ref[0] > 0.0, jnp.concatenate([c, c], axis=1) - r, -1e30)),
                     gc, gr)
        kb = each(lambda a, b: a * b, kn, bb)
        knb = each(lambda a: a.astype(BF16), kn)
        lfull = each(lambda a, b, dc: _nt_dot(a.astype(BF16), b) * dc, kb, knb, decay)
        lmat = each(lambda a: a.astype(BF16), lfull)
        tinv = each(lambda a: eye - a * cmask_ref[1], lfull)
        for lv in range(1, n_lv):
            d = 1 << lv
            nb = tt // d
            tb = each(lambda a: a.astype(BF16), tinv)
            bd = each(lambda m: m * lmask_ref[lv], lmat)
            if d % 8 == 0:
                odd = each(lambda a: jnp.concatenate([a[i * d:(i + 1) * d] for i in range(1, nb, 2)], axis=0), tinv)
                tl = each(lambda a, b: jnp.dot(a.astype(BF16), b, preferred_element_type=F32).astype(BF16), odd, bd)
                upd = each(lambda a, b, c: a - jnp.dot(b, c, preferred_element_type=F32), odd, tl, tb)
                tinv = each(lambda a, u: jnp.concatenate(
                    [a[i * d:(i + 1) * d] if i % 2 == 0 else u[(i // 2) * d:(i // 2 + 1) * d] for i in range(nb)],
                    axis=0), tinv, upd)
            else:
                tl = each(lambda a, b: jnp.dot(a, b, preferred_element_type=F32).astype(BF16), tb, bd)
                tinv = each(lambda a, b, c: a - jnp.dot(b, c, preferred_element_type=F32), tinv, tl, tb)
        eg = each(jnp.exp, gc)
        rhs = each(lambda a, e, b, c: jnp.concatenate([a * e, b * c], axis=1).astype(BF16), kb, eg, v, bb)
        wu = each(lambda a, b: jnp.dot(a.astype(BF16), b, preferred_element_type=F32).astype(BF16), tinv, rhs)
        attn = each(lambda a, b, dc: (_nt_dot(a.astype(BF16), b) * dc).astype(BF16), qn, knb, decay)
        awu = each(lambda a, b: jnp.dot(a, b, preferred_element_type=F32), attn, wu)
        qp = each(lambda a, e, b: (a * e - b[:, :hd]).astype(BF16), qn, eg, awu)
        s = [st_scr[h] for h in hs_idx]
        outs = [[] for _ in hs_idx]
        for c in range(nck):
            r0 = c * GDN_CHUNK
            r1 = r0 + GDN_CHUNK
            gl = each(lambda a: a[r1 - 1:r1, :], gc)
            kd = each(lambda a, l, g: (a[r0:r1] * jnp.exp(l - g[r0:r1])).astype(BF16), kn, gl, gc)
            m = each(lambda a, b: _tn_dot(a, b[r0:r1]), kd, wu)
            sb = each(lambda a: a.astype(BF16), s)
            for i, (a, b, y) in enumerate(zip(qp, sb, awu)):
                outs[i].append(jnp.dot(a[r0:r1], b, preferred_element_type=F32) + y[r0:r1, hd:])
            s = each(lambda a, l, mm, b: a * jnp.exp(l) - jnp.dot(mm[:, :hd].astype(BF16), b,
                                                                  preferred_element_type=F32) + mm[:, hd:],
                     s, gl, m, sb)
        for i, h in enumerate(hs_idx):
            st_scr[h] = s[i]
            o = jnp.concatenate(outs[i], axis=0)
            z = z_ref[:, pl.ds(hs[i], hd)]
            o_ref[:, pl.ds(hs[i], hd)] = (o * _rms_scale(o) * nw_ref[...] * _silu(z)).astype(o_ref.dtype)

    def trip(j, carry):
        heads_lockstep([GDN_HEADS_PER_TRIP * j + i for i in range(GDN_HEADS_PER_TRIP)])
        return carry

    lax.fori_loop(0, heads // GDN_HEADS_PER_TRIP, trip, 0)


def _gdn_masks():
    tt = GDN_TILE
    ri = lax.broadcasted_iota(jnp.int32, (tt, tt), 0)
    ci = lax.broadcasted_iota(jnp.int32, (tt, tt), 1)
    same = (ri // GDN_CHUNK) == (ci // GDN_CHUNK)
    causal = same & (ci <= ri)
    tril = causal.astype(F32)
    cmask = jnp.stack([tril, (((ri % 2) == 1) & (ci == ri - 1)).astype(F32)])
    levels = []
    d = 1
    while d < GDN_CHUNK:
        levels.append((((ri // d) % 2) == 1) & ((ci // d) == (ri // d) - 1))
        d *= 2
    return tril, cmask, jnp.stack(levels).astype(BF16)


def _gdn(proj, small, conv_w, head_params, norm_w, masks, cols, batch, seq, heads):
    t = proj.shape[0]
    tt = GDN_TILE
    width = heads * HEAD_DIM
    tiles = seq // tt
    cq, ck, cv, cz = cols
    tril, cmask, lmask = masks

    def cur(col):
        return pl.BlockSpec((tt, width), lambda b, i: (b * tiles + i, col))

    def halo(col):
        return pl.BlockSpec((8, width), lambda b, i: (jnp.maximum((b * tiles + i) * (tt // 8) - 1, 0), col))

    def cw(col):
        return pl.BlockSpec((CONV_K, width), lambda b, i: (0, col))

    kern = functools.partial(_gdn_kernel, heads=heads)
    return pl.pallas_call(
        kern,
        grid=(batch, tiles),
        in_specs=[cur(cq), cur(ck), cur(cv), halo(cq), halo(ck), halo(cv), cur(cz),
                  pl.BlockSpec((tt, 128), lambda b, i: (b * tiles + i, 0)),
                  cw(0), cw(1), cw(2),
                  pl.BlockSpec((8, 128), lambda b, i: (0, 0)),
                  pl.BlockSpec((1, HEAD_DIM), lambda b, i: (0, 0)),
                  pl.BlockSpec((tt, tt), lambda b, i: (0, 0)),
                  pl.BlockSpec(cmask.shape, lambda b, i: (0, 0, 0)),
                  pl.BlockSpec(lmask.shape, lambda b, i: (0, 0, 0))],
        out_specs=pl.BlockSpec((tt, width), lambda b, i: (b * tiles + i, 0)),
        out_shape=jax.ShapeDtypeStruct((t, width), BF16),
        scratch_shapes=[pltpu.VMEM((heads, tt, HEAD_DIM), F32),
                        pltpu.VMEM((heads, tt, HEAD_DIM), F32),
                        pltpu.VMEM((8, tt), F32),
                        pltpu.VMEM((heads, HEAD_DIM, HEAD_DIM), F32),
                        pltpu.VMEM((tt + 8, width), F32),
                        pltpu.VMEM((tt + 8, width), F32),
                        pltpu.VMEM((tt + 8, width), F32)],
        compiler_params=_cparams(("parallel", "arbitrary"), 40),
    )(proj, proj, proj, proj, proj, proj, proj, small, conv_w, conv_w, conv_w, head_params, norm_w,
      tril, cmask, lmask)


def _route(h, wr_hl, rb):
    h_hi = h.astype(BF16)
    h_lo = (h - h_hi.astype(F32)).astype(BF16)
    first = jnp.dot(h_hi, wr_hl[...], preferred_element_type=F32)
    logits = first[:, :128] + first[:, 128:] + jnp.dot(h_lo, wr_hl[:, :128], preferred_element_type=F32)
    logits = logits.T[:N_EXPERTS]
    scores = jax.nn.sigmoid(logits)
    sel = scores + rb
    sel_rows = [sel[e:e + 1, :] for e in range(N_EXPERTS)]
    score_rows = [scores[e:e + 1, :] for e in range(N_EXPERTS)]
    epg = EXPERTS_PER_GROUP
    best_score = None
    best_group = None
    for g in range(N_GROUPS):
        rows = sel_rows[g * epg:(g + 1) * epg]
        gs = None
        for a in range(epg):
            for b in range(a + 1, epg):
                pair = rows[a] + rows[b]
                gs = pair if gs is None else jnp.maximum(gs, pair)
        if g == 0:
            best_score, best_group = gs, jnp.zeros_like(gs)
        else:
            better = gs > best_score
            best_group = jnp.where(better, float(g), best_group)
            best_score = jnp.where(better, gs, best_score)
    in_sel = []
    in_score = []
    for j in range(epg):
        a = sel_rows[j]
        b = score_rows[j]
        for g in range(1, N_GROUPS):
            pick = best_group == float(g)
            a = jnp.where(pick, sel_rows[g * epg + j], a)
            b = jnp.where(pick, score_rows[g * epg + j], b)
        in_sel.append(a)
        in_score.append(b)
    i1, m1, w1 = jnp.zeros_like(in_sel[0]), in_sel[0], in_score[0]
    for j in range(1, epg):
        better = in_sel[j] > m1
        i1 = jnp.where(better, float(j), i1)
        m1 = jnp.where(better, in_sel[j], m1)
        w1 = jnp.where(better, in_score[j], w1)
    i2 = m2 = w2 = None
    for j in range(epg):
        cand = jnp.where(i1 == float(j), -jnp.inf, in_sel[j])
        if j == 0:
            i2, m2, w2 = jnp.zeros_like(cand), cand, in_score[0]
        else:
            better = cand > m2
            i2 = jnp.where(better, float(j), i2)
            m2 = jnp.where(better, cand, m2)
            w2 = jnp.where(better, in_score[j], w2)
    total = w1 + w2
    zero = jnp.zeros_like(w1)
    return jnp.concatenate([best_group * epg + i1, best_group * epg + i2, w1 / total, w2 / total,
                            zero, zero, zero, zero], axis=0)


def _merge_kernel(ya_ref, yb_ref, ga_ref, gb_ref, x_ref, mod_ref, wa_ref, wb_ref, wo_ref, nw_ref, wrt_ref, rb_ref,
                  o_ref, h_ref, r_ref):
    pa = jnp.dot(ya_ref[...], wa_ref[...], preferred_element_type=F32)
    pb = jnp.dot(yb_ref[...], wb_ref[...], preferred_element_type=F32)
    merged = jax.nn.sigmoid(ga_ref[...]) * pa + jax.nn.sigmoid(gb_ref[...]) * pb
    out = jnp.dot(merged.astype(BF16), wo_ref[...], preferred_element_type=F32)
    xn = x_ref[...] + mod_ref[2:3, :] * out
    o_ref[...] = xn
    h = xn * _rms_scale(xn) * nw_ref[...] * (1.0 + mod_ref[4:5, :]) + mod_ref[3:4, :]
    h_ref[...] = _pack_halves(h)
    r_ref[...] = _route(h, wrt_ref, rb_ref[...])


def _merge(ya, yb, proj, x2, modl, wa, wb, wo, nw2, wr_t, rb, seq, ga_col, gb_col):
    t, d = x2.shape
    w = ya.shape[1]
    tm = min(seq, 256)
    per_batch = seq // tm
    const = dict(pipeline_mode=pl.Buffered(1))
    return pl.pallas_call(
        _merge_kernel,
        grid=(t // tm,),
        in_specs=[pl.BlockSpec((tm, w), lambda i: (i, 0)),
                  pl.BlockSpec((tm, w), lambda i: (i, 0)),
                  pl.BlockSpec((tm, d), lambda i: (i, ga_col)),
                  pl.BlockSpec((tm, d), lambda i: (i, gb_col)),
                  pl.BlockSpec((tm, d), lambda i: (i, 0)),
                  pl.BlockSpec((None, 6, d), lambda i: (i // per_batch, 0, 0)),
                  pl.BlockSpec((w, d), lambda i: (0, 0), **const),
                  pl.BlockSpec((w, d), lambda i: (0, 0), **const),
                  pl.BlockSpec((d, d), lambda i: (0, 0), **const),
                  pl.BlockSpec((1, d), lambda i: (0, 0)),
                  pl.BlockSpec((d, 256), lambda i: (0, 0)),
                  pl.BlockSpec((N_EXPERTS, 1), lambda i: (0, 0))],
        out_specs=[pl.BlockSpec((tm, d), lambda i: (i, 0)),
                   pl.BlockSpec((tm, d // 2), lambda i: (i, 0)),
                   pl.BlockSpec((8, tm), lambda i: (0, i))],
        out_shape=[jax.ShapeDtypeStruct((t, d), F32), jax.ShapeDtypeStruct((t, d // 2), jnp.uint32),
                   jax.ShapeDtypeStruct((8, t), F32)],
        compiler_params=_cparams(("parallel",), 52),
    )(ya, yb, proj, proj, x2, modl, wa, wb, wo, nw2, wr_t, rb)


def _moe_kernel(te_ref, nu_ref, x_ref, wg_ref, wu_ref, wd_ref, o_ref):
    used = pl.program_id(0) < nu_ref[0]

    @pl.when(used)
    def _():
        x_lo, x_hi = _unpack_halves(x_ref[...])
        x_lo = x_lo.astype(BF16)
        x_hi = x_hi.astype(BF16)
        half = x_lo.shape[1]

        def up(w_ref):
            return (jnp.dot(x_lo, w_ref[:half, :], preferred_element_type=F32)
                    + jnp.dot(x_hi, w_ref[half:, :], preferred_element_type=F32))

        hid = (_silu(up(wg_ref)) * up(wu_ref)).astype(BF16)
        o_ref[...] = _pack_halves(jnp.dot(hid, wd_ref[...], preferred_element_type=F32))

    @pl.when(jnp.logical_not(used))
    def _():
        o_ref[...] = jnp.zeros_like(o_ref)


def _cast_kernel(w_ref, o_ref):
    o_ref[...] = w_ref[...].astype(o_ref.dtype)


def _expert_weights_bf16(w, layer):
    _, e, a, b = w.shape
    ta = min(a, (1 << 20) // b)
    return pl.pallas_call(
        _cast_kernel,
        grid=(e, a // ta),
        in_specs=[pl.BlockSpec((None, None, ta, b), lambda i, j: (layer, i, j, 0))],
        out_specs=pl.BlockSpec((None, ta, b), lambda i, j: (i, j, 0)),
        out_shape=jax.ShapeDtypeStruct((e, a, b), BF16),
        compiler_params=_cparams(("parallel", "parallel"), 32),
    )(w)


def _moe_experts(tile_expert, n_used, xs, wg, wu, wd, tm):
    p, dh = xs.shape
    d = 2 * dh
    f = wg.shape[-1]
    grid_spec = pltpu.PrefetchScalarGridSpec(
        num_scalar_prefetch=2,
        grid=(p // tm,),
        in_specs=[pl.BlockSpec((tm, dh), lambda i, te, nu: (i, 0)),
                  pl.BlockSpec((None, d, f), lambda i, te, nu: (te[i], 0, 0)),
                  pl.BlockSpec((None, d, f), lambda i, te, nu: (te[i], 0, 0)),
                  pl.BlockSpec((None, f, d), lambda i, te, nu: (te[i], 0, 0))],
        out_specs=pl.BlockSpec((tm, dh), lambda i, te, nu: (i, 0)),
    )
    return pl.pallas_call(
        _moe_kernel,
        grid_spec=grid_spec,
        out_shape=jax.ShapeDtypeStruct((p, dh), jnp.uint32),
        compiler_params=_cparams(("arbitrary",), 52),
    )(tile_expert, n_used, xs, wg, wu, wd)


def _dispatch_plan(expert_idx, tm, n_tiles):
    flat_e = expert_idx.reshape(-1)
    onehot = (flat_e[:, None] == jnp.arange(N_EXPERTS, dtype=jnp.int32)[None, :]).astype(jnp.int32)
    csum = jnp.cumsum(onehot, axis=0)
    rank = jnp.sum((csum - onehot) * onehot, axis=1)
    sizes = csum[-1]
    padded = ((sizes + tm - 1) // tm) * tm
    pad_end = jnp.cumsum(padded)
    pad_start = pad_end - padded
    dest = pad_start[flat_e] + rank
    tile_start = jnp.arange(n_tiles, dtype=jnp.int32) * tm
    tile_expert = jnp.minimum(jnp.sum((tile_start[:, None] >= pad_end[None, :]).astype(jnp.int32), axis=1),
                              N_EXPERTS - 1).astype(jnp.int32)
    n_used = (pad_end[-1] // tm).astype(jnp.int32).reshape(1)
    dest = dest.astype(jnp.int32)
    n_slots = flat_e.shape[0]
    _, by_slot = lax.sort_key_val(dest, jnp.arange(n_slots, dtype=jnp.int32) // TOP_K)
    start = jnp.cumsum(sizes) - sizes
    tile_off = tile_start - pad_start[tile_expert]
    tile_valid = jnp.clip(sizes[tile_expert] - tile_off, 0, tm)
    lane = jnp.arange(tm, dtype=jnp.int32)[None, :]
    pick = jnp.clip((start[tile_expert] + tile_off)[:, None] + lane, 0, n_slots - 1)
    filler = (tile_start[:, None] + lane) % (n_slots // TOP_K)
    src_tok = jnp.where(lane < tile_valid[:, None], jnp.take(by_slot, pick.reshape(-1)).reshape(n_tiles, tm), filler)
    return dest, src_tok.reshape(-1).astype(jnp.int32), tile_expert, n_used


def _combine_kernel(x_ref, y0_ref, y1_ref, gw_ref, mod_ref, fw_ref, o_ref, *, final_norm):
    gw = gw_ref[...]
    y0_lo, y0_hi = _unpack_halves(y0_ref[...])
    y1_lo, y1_hi = _unpack_halves(y1_ref[...])
    moe = jnp.concatenate([gw[:, 0:1] * y0_lo + gw[:, 1:2] * y1_lo, gw[:, 0:1] * y0_hi + gw[:, 1:2] * y1_hi], axis=1)
    xn = x_ref[...] + mod_ref[5:6, :] * moe
    if final_norm:
        xn = xn * _rms_scale(xn) * fw_ref[...]
    o_ref[...] = xn


def _combine(x2, y_pairs, gate_w, modl, final_w, seq, final_norm):
    t, d = x2.shape
    tm = min(seq, 512)
    per_batch = seq // tm
    nblk = t // tm
    return pl.pallas_call(
        functools.partial(_combine_kernel, final_norm=final_norm),
        grid=(nblk,),
        in_specs=[pl.BlockSpec((tm, d), lambda i: (i, 0)),
                  pl.BlockSpec((tm, d // 2), lambda i: (i, 0)),
                  pl.BlockSpec((tm, d // 2), lambda i: (i + nblk, 0)),
                  pl.BlockSpec((tm, TOP_K), lambda i: (i, 0)),
                  pl.BlockSpec((None, 6, d), lambda i: (i // per_batch, 0, 0)),
                  pl.BlockSpec((1, d), lambda i: (0, 0))],
        out_specs=pl.BlockSpec((tm, d), lambda i: (i, 0)),
        out_shape=jax.ShapeDtypeStruct((t, d), F32),
        compiler_params=_cparams(("parallel",), 40),
    )(x2, y_pairs, y_pairs, gate_w, modl, final_w)


def kernel(x, c, w_ada, b_ada, norm1_w, norm2_w, w_in, s5_a_re, s5_a_im, s5_log_dt, s5_b_re, s5_b_im, s5_c_re, s5_c_im, s5_d, s5_w_glu, s5_b_glu, gdn_conv_w, gdn_a_log, gdn_dt_bias, gdn_norm_w, w_proj_a, w_proj_b, w_out, w_router, router_bias, w_gate, w_up, w_down, final_norm_w):
    batch, seq, d = x.shape
    depth = w_ada.shape[0]
    t = batch * seq
    s5w = s5_d.shape[1]
    gdw = w_proj_b.shape[1]
    heads = gdn_a_log.shape[1]
    assert gdw == heads * HEAD_DIM and s5w == gdw and d == 2 * gdw
    assert seq % GDN_TILE == 0 and seq % S5_CHUNK == 0
    n_levels = int(math.log2(seq // S5_CHUNK))
    assert S5_CHUNK << n_levels == seq and S5_OCT * S5_GROUP == 128

    c_pad = jnp.zeros((8, d), F32).at[:batch].set(c)
    mod = _ada(c_pad, w_ada, b_ada)

    o_u, o_q, o_k, o_v, o_z = 0, s5w, 2 * s5w, 3 * s5w, 4 * s5w
    o_ba = 5 * s5w
    o_ga = o_ba + 2 * heads
    o_gb = o_ga + d
    col_u, col_q, col_k, col_v, col_z = 4, 5, 6, 7, 8

    gdn_masks = _gdn_masks()
    s5_rep = _s5_replicate()
    s5_ops = jax.vmap(functools.partial(_s5_weights, n_levels=n_levels))(
        s5_a_re, s5_a_im, s5_log_dt, s5_b_re, s5_b_im, s5_c_re, s5_c_im)

    wr_hi = w_router.astype(BF16)
    wr_lo = (w_router - wr_hi.astype(F32)).astype(BF16)
    pad_e = ((0, 0), (0, 128 - N_EXPERTS))
    wr_t = jnp.concatenate([jnp.pad(wr_hi, pad_e), jnp.pad(wr_lo, pad_e)], axis=1)
    rb = router_bias.reshape(N_EXPERTS, 1).astype(F32)

    tm_e = min(512, t)
    n_tiles = (TOP_K * t) // tm_e + N_EXPERTS

    w_in_t = jnp.transpose(w_in, (0, 2, 1))
    w_big = _repack_in_proj(w_in_t, o_ga, 2 * d, o_ba)
    w_small = jnp.pad(w_in_t[:, o_ba:o_ga], ((0, 0), (0, 128 - 2 * heads), (0, 0))).astype(BF16)

    x2 = x.reshape(t, d)
    for l in range(depth):
        modl = mod[l, :batch].reshape(batch, 6, d)
        proj, small = _inproj(x2, modl, norm1_w[l].reshape(1, d), w_big, w_small, l, seq)

        y_lin = _s5_core(proj, col_u * s5w // 128, *s5_ops, s5_rep, l, batch, seq)
        y_a = _s5_post(y_lin, proj, col_u, s5_d[l].reshape(1, s5w), s5_w_glu[l].astype(BF16),
                       s5_b_glu[l].reshape(1, s5w))

        head_params = jnp.zeros((8, 128), F32)
        head_params = head_params.at[0, heads:2 * heads].set(-jnp.exp(gdn_a_log[l]))
        head_params = head_params.at[1, heads:2 * heads].set(gdn_dt_bias[l])
        y_b = _gdn(proj, small, gdn_conv_w[l], head_params, gdn_norm_w[l].reshape(1, HEAD_DIM), gdn_masks,
                   (col_q, col_k, col_v, col_z), batch, seq, heads)

        x2, h2, route = _merge(y_a, y_b, proj, x2, modl, w_proj_a[l].astype(BF16), w_proj_b[l].astype(BF16),
                               w_out[l].astype(BF16), norm2_w[l].reshape(1, d), wr_t, rb, seq, 0, 1)

        expert_idx = route[0:TOP_K].T.astype(jnp.int32)
        gate_w = route[TOP_K:2 * TOP_K].T
        dest, src_tok, tile_expert, n_used = _dispatch_plan(expert_idx, tm_e, n_tiles)
        xs = jnp.take(h2, src_tok, axis=0, mode='clip')
        if l == 0:
            wg_l, wu_l = _expert_weights_bf16(w_gate, 0), _expert_weights_bf16(w_up, 0)
        wd_l = _expert_weights_bf16(w_down, l)
        ys = _moe_experts(tile_expert, n_used, xs, wg_l, wu_l, wd_l, tm_e)
        y_pairs = jnp.take(ys, dest.reshape(t, TOP_K).T.reshape(-1), axis=0, mode='clip')
        if l + 1 < depth:
            wg_l, wu_l = _expert_weights_bf16(w_gate, l + 1), _expert_weights_bf16(w_up, l + 1)
        x2 = _combine(x2, y_pairs, gate_w, modl, final_norm_w.reshape(1, d), seq, l == depth - 1)

    return x2.reshape(batch, seq, d)
```

```python
import functools
import math

import jax
import jax.numpy as jnp
from jax import lax
from jax.experimental import pallas as pl
from jax.experimental.pallas import tpu as pltpu

F32 = jnp.float32
BF16 = jnp.bfloat16
HIGHEST = lax.Precision.HIGHEST

RMS_EPS = 1e-6
S5_GROUP = 16
S5_STATE = 64
S5_CHUNK = 16
S5_OCT = 8
HEAD_DIM = 128
GDN_CHUNK = 64
GDN_TILE = 256
GDN_HEADS_PER_TRIP = 8
CONV_K = 4
N_EXPERTS = 16
N_GROUPS = 4
EXPERTS_PER_GROUP = N_EXPERTS // N_GROUPS
TOP_K = 2
MIB = 1024 * 1024


def _cparams(semantics, vmem_mib):
    return pltpu.CompilerParams(dimension_semantics=semantics, vmem_limit_bytes=vmem_mib * MIB)


def _silu(x):
    return x * jax.nn.sigmoid(x)


def _softplus(x):
    return jnp.maximum(x, 0.0) + jnp.log(1.0 + jnp.exp(-jnp.abs(x)))


def _rms_scale(x):
    return lax.rsqrt(jnp.mean(x * x, axis=-1, keepdims=True) + RMS_EPS)


def _pack_halves(x):
    n = x.shape[1] // 2
    lo = lax.bitcast_convert_type(x[:, :n].astype(BF16).astype(F32), jnp.uint32)
    hi = lax.bitcast_convert_type(x[:, n:].astype(BF16).astype(F32), jnp.uint32)
    return (lo >> 16) | hi


def _unpack_halves(p):
    lo = lax.bitcast_convert_type(p << 16, F32)
    hi = lax.bitcast_convert_type(p & jnp.uint32(0xFFFF0000), F32)
    return lo, hi


def _nt_dot(a, b):
    return lax.dot_general(a, b, (((1,), (1,)), ((), ())), preferred_element_type=F32)


def _tn_dot(a, b):
    return lax.dot_general(a, b, (((0,), (0,)), ((), ())), preferred_element_type=F32)


def _ada_kernel(c_ref, w_ref, b_ref, o_ref):
    ca = _silu(c_ref[...]).astype(BF16)
    o_ref[...] = jnp.dot(ca, w_ref[...].astype(BF16), preferred_element_type=F32) + b_ref[...]


def _ada(c_pad, w_ada, b_ada):
    depth, d, n = w_ada.shape
    rows = c_pad.shape[0]
    tn = min(n, 1024)
    return pl.pallas_call(
        _ada_kernel,
        grid=(depth, n // tn),
        in_specs=[pl.BlockSpec((rows, d), lambda l, j: (0, 0)),
                  pl.BlockSpec((None, d, tn), lambda l, j: (l, 0, j)),
                  pl.BlockSpec((None, 1, tn), lambda l, j: (l, 0, j))],
        out_specs=pl.BlockSpec((None, rows, tn), lambda l, j: (l, 0, j)),
        out_shape=jax.ShapeDtypeStruct((depth, rows, n), F32),
        compiler_params=_cparams(("parallel", "parallel"), 40),
    )(c_pad, w_ada, b_ada.reshape(depth, 1, n))


def _inproj_kernel(x_ref, mod_ref, nw_ref, w_ref, wsm_ref, o_ref, osm_ref, h_scr):
    @pl.when(pl.program_id(1) == 0)
    def _():
        x = x_ref[...]
        h = x * _rms_scale(x) * nw_ref[...] * (1.0 + mod_ref[1:2, :]) + mod_ref[0:1, :]
        hb = h.astype(BF16)
        h_scr[...] = hb
        osm_ref[...] = _nt_dot(hb, wsm_ref[...])

    o_ref[...] = _nt_dot(h_scr[...], w_ref[...])


def _inproj(x2, modl, nw, w_big, w_small, layer, seq):
    t, d = x2.shape
    n = w_big.shape[1]
    tm = min(seq, 1024)
    tn = 1536
    per_batch = seq // tm
    return pl.pallas_call(
        _inproj_kernel,
        grid=(t // tm, n // tn),
        in_specs=[pl.BlockSpec((tm, d), lambda i, j: (i, 0)),
                  pl.BlockSpec((None, 6, d), lambda i, j: (i // per_batch, 0, 0)),
                  pl.BlockSpec((1, d), lambda i, j: (0, 0)),
                  pl.BlockSpec((None, tn, d), lambda i, j: (layer, j, 0)),
                  pl.BlockSpec((None, 128, d), lambda i, j: (layer, 0, 0))],
        out_specs=[pl.BlockSpec((tm, tn), lambda i, j: (i, j)),
                   pl.BlockSpec((tm, 128), lambda i, j: (i, 0))],
        out_shape=[jax.ShapeDtypeStruct((t, n), F32), jax.ShapeDtypeStruct((t, 128), F32)],
        scratch_shapes=[pltpu.VMEM((tm, d), BF16)],
        compiler_params=_cparams(("parallel", "arbitrary"), 52),
    )(x2, modl, nw, w_big, w_small)


def _repack_kernel(a_ref, b_ref, o_ref, *, gate_blocks, shift):
    c = pl.program_id(1)

    @pl.when(c < gate_blocks)
    def _():
        x = jnp.concatenate([a_ref[...], b_ref[...]], axis=0)
        o_ref[...] = x[shift:shift + o_ref.shape[0]].astype(o_ref.dtype)

    @pl.when(c >= gate_blocks)
    def _():
        o_ref[...] = a_ref[...].astype(o_ref.dtype)


def _repack_in_proj(w_in_t, gate_start, gate_cols, lead_cols):
    depth, _, d = w_in_t.shape
    rb = 512
    shift = gate_start % rb
    base = gate_start - shift
    assert shift % 16 == 0 and gate_cols % rb == 0 and lead_cols % rb == 0 and lead_cols == base
    gate_blocks = gate_cols // rb
    kern = functools.partial(_repack_kernel, gate_blocks=gate_blocks, shift=shift)
    return pl.pallas_call(
        kern,
        grid=(depth, (gate_cols + lead_cols) // rb),
        in_specs=[pl.BlockSpec((None, rb, d),
                               lambda l, c: (l, jnp.where(c < gate_blocks, base // rb + c, c - gate_blocks), 0)),
                  pl.BlockSpec((None, shift, d),
                               lambda l, c: (l, jnp.where(c < gate_blocks, (base + (c + 1) * rb) // shift, 0), 0))],
        out_specs=pl.BlockSpec((None, rb, d), lambda l, c: (l, c, 0)),
        out_shape=jax.ShapeDtypeStruct((depth, gate_cols + lead_cols, d), BF16),
        compiler_params=_cparams(("parallel", "arbitrary"), 32),
    )(w_in_t, w_in_t)


def _s5_weights(a_re, a_im, log_dt, b_re, b_im, c_re, c_im, n_levels):
    tc = S5_CHUNK
    groups, p = a_re.shape
    n = b_re.shape[-1]
    oc = S5_OCT
    octs = groups // oc
    lam = lax.complex(a_re, a_im)
    dt = jnp.exp(log_dt)[:, None]
    log_a = lam * dt
    a_bar = jnp.exp(log_a)
    b_bar = ((a_bar - 1.0) / lam)[..., None] * lax.complex(b_re, b_im)
    c_mat = lax.complex(c_re, c_im)
    apow = jnp.exp(log_a[:, :, None] * jnp.arange(tc + 1, dtype=F32)[None, None, :])
    lane = jnp.arange(tc * n)
    e_t = (jnp.arange(tc)[:, None] == (lane // n)[None, :]).astype(F32)
    e_n = (jnp.arange(n)[:, None] == (lane % n)[None, :]).astype(F32)

    def expand(z, e):
        f = lambda r: jnp.einsum('gpk,kj->gpj', r, e, precision=HIGHEST)
        return lax.complex(f(jnp.real(z)), f(jnp.imag(z)))

    def re_negim_rows(z):
        return jnp.concatenate([jnp.real(z), -jnp.imag(z)], axis=1)

    c_rep = expand(jnp.transpose(c_mat, (0, 2, 1)), e_n)
    lag_c = re_negim_rows(expand(apow[:, :, :tc], e_t) * c_rep)
    hoc = re_negim_rows(expand(apow[:, :, 1:], e_t) * c_rep)
    b_t = jnp.transpose(b_bar, (0, 2, 1))
    b_cat = jnp.concatenate([jnp.real(b_t), jnp.imag(b_t)], axis=-1)
    kern = jnp.einsum('gnk,gkj->gnj', b_cat, lag_c, precision=HIGHEST)
    tcomp = jnp.stack([jnp.pad(kern[:, :, :(tc - s) * n], ((0, 0), (0, 0), (s * n, 0))) for s in range(tc)],
                      axis=1)
    tcomp = jnp.transpose(tcomp.reshape(octs, oc, tc, n, tc * n), (0, 2, 1, 3, 4)).reshape(octs, tc * oc * n, tc * n)
    ap_rev = jnp.transpose(apow[:, :, tc - 1 - jnp.arange(tc)], (0, 2, 1))
    hin_c = ap_rev[:, :, None, :] * b_t[:, None, :, :]
    hc = jnp.concatenate([jnp.real(hin_c), jnp.imag(hin_c)], axis=-1)
    hc = jnp.transpose(hc.reshape(octs, oc, tc, n, 2 * p), (0, 2, 1, 3, 4)).reshape(octs, tc * oc * n, 2 * p)
    hoc = hoc.reshape(octs, oc * 2 * p, tc * n)
    steps = (tc * (2.0 ** jnp.arange(n_levels, dtype=F32)))
    amul = jnp.exp(log_a[:, None, :] * steps[None, :, None])
    a1 = jnp.concatenate([jnp.real(amul), jnp.real(amul)], axis=-1)
    a2 = jnp.concatenate([-jnp.imag(amul), jnp.imag(amul)], axis=-1)

    def lanes_by_group(a):
        return jnp.transpose(a.reshape(octs, oc, n_levels, 2 * p), (0, 2, 1, 3)).reshape(octs, n_levels, oc * 2 * p)

    return (tcomp.astype(BF16), hc.astype(BF16), hoc.astype(BF16),
            lanes_by_group(a1).astype(F32), lanes_by_group(a2).astype(F32))


def _s5_replicate():
    r = lax.broadcasted_iota(jnp.int32, (S5_CHUNK * S5_GROUP, S5_CHUNK * 128), 0)
    c = lax.broadcasted_iota(jnp.int32, (S5_CHUNK * S5_GROUP, S5_CHUNK * 128), 1)
    return ((r // S5_GROUP == c // 128) & (r % S5_GROUP == c % S5_GROUP)).astype(BF16)


def _s5_core_kernel(u_ref, tc_ref, hc_ref, hoc_ref, a1_ref, a2_ref, rep_ref, y_ref, toep_scr, hin_scr, hout_scr,
                    *, n_levels):
    tc = S5_CHUNK
    lanes = u_ref.shape[1]
    state_lanes = hc_ref.shape[1]

    @pl.when(pl.program_id(1) == 0)
    def _():
        kd = toep_scr.shape[1]
        rb = 256
        col_g = (lax.broadcasted_iota(jnp.int32, (rb, kd), 1) // S5_GROUP) % S5_OCT
        row_i = lax.broadcasted_iota(jnp.int32, (rb, kd), 0)
        for i in range(toep_scr.shape[0] // rb):
            full = jnp.dot(tc_ref[i * rb:(i + 1) * rb, :], rep_ref[...], preferred_element_type=F32)
            row_g = ((row_i + i * rb) // S5_GROUP) % S5_OCT
            toep_scr[i * rb:(i + 1) * rb, :] = jnp.where(row_g == col_g, full, 0.0).astype(BF16)
        for i in range(hout_scr.shape[0] // rb):
            full = jnp.dot(hoc_ref[i * rb:(i + 1) * rb, :], rep_ref[...], preferred_element_type=F32)
            row_g = (row_i + i * rb) // state_lanes
            hout_scr[i * rb:(i + 1) * rb, :] = jnp.where(row_g == col_g, full, 0.0).astype(BF16)
        hrow_g = (lax.broadcasted_iota(jnp.int32, hc_ref.shape, 0) // S5_GROUP) % S5_OCT
        for g in range(S5_OCT):
            hin_scr[:, g * state_lanes:(g + 1) * state_lanes] = jnp.where(hrow_g == g, hc_ref[...], 0.0).astype(BF16)

    rows = u_ref.shape[0] // tc
    x = jnp.concatenate([u_ref[pl.ds(t, rows, stride=tc), :] for t in range(tc)], axis=1).astype(BF16)
    s_loc = jnp.dot(x, hin_scr[...], preferred_element_type=F32)
    pos = lax.broadcasted_iota(jnp.int32, (rows, state_lanes), 0)
    s_in = []
    for g in range(S5_OCT):
        ln = slice(g * state_lanes, (g + 1) * state_lanes)
        s = s_loc[:, ln]
        for k in range(n_levels):
            d = 1 << k
            prev = jnp.where(pos >= d, pltpu.roll(s, d, axis=0), 0.0)
            s = s + a1_ref[k:k + 1, ln] * prev + a2_ref[k:k + 1, ln] * pltpu.roll(prev, state_lanes // 2, axis=1)
        s_in.append(jnp.where(pos >= 1, pltpu.roll(s, 1, axis=0), 0.0).astype(BF16))
    y = (jnp.dot(x, toep_scr[...], preferred_element_type=F32)
         + jnp.dot(jnp.concatenate(s_in, axis=1), hout_scr[...], preferred_element_type=F32))
    for t in range(tc):
        y_ref[pl.ds(t, rows, stride=tc), :] = y[:, t * lanes:(t + 1) * lanes]


def _s5_core(proj, u_lane_block, tcomp, hc, hoc, a1, a2, rep, layer, batch, seq):
    t = proj.shape[0]
    _, octs, kdim, tn = tcomp.shape
    sl = hc.shape[-1]
    sw = hoc.shape[2]
    n_levels = a1.shape[2]
    lanes = kdim // S5_CHUNK
    kern = functools.partial(_s5_core_kernel, n_levels=n_levels)
    return pl.pallas_call(
        kern,
        grid=(octs, batch),
        in_specs=[pl.BlockSpec((seq, lanes), lambda o, b: (b, u_lane_block + o)),
                  pl.BlockSpec((None, None, kdim, tn), lambda o, b: (layer, o, 0, 0)),
                  pl.BlockSpec((None, None, kdim, sl), lambda o, b: (layer, o, 0, 0)),
                  pl.BlockSpec((None, None, sw, tn), lambda o, b: (layer, o, 0, 0)),
                  pl.BlockSpec((None, None, n_levels, sw), lambda o, b: (layer, o, 0, 0)),
                  pl.BlockSpec((None, None, n_levels, sw), lambda o, b: (layer, o, 0, 0)),
                  pl.BlockSpec((tn, kdim), lambda o, b: (0, 0))],
        out_specs=pl.BlockSpec((seq, lanes), lambda o, b: (b, o)),
        out_shape=jax.ShapeDtypeStruct((t, octs * lanes), F32),
        scratch_shapes=[pltpu.VMEM((kdim, kdim), BF16), pltpu.VMEM((kdim, sw), BF16), pltpu.VMEM((sw, kdim), BF16)],
        compiler_params=_cparams(("parallel", "arbitrary"), 48),
    )(proj, tcomp, hc, hoc, a1, a2, rep)


def _s5_post_kernel(y_ref, u_ref, d_ref, w_ref, b_ref, o_ref):
    y = jax.nn.gelu(y_ref[...] + d_ref[...] * u_ref[...])
    gate = jnp.dot(y.astype(BF16), w_ref[...], preferred_element_type=F32) + b_ref[...]
    o_ref[...] = (y * jax.nn.sigmoid(gate)).astype(o_ref.dtype)


def _s5_post(y_lin, proj, u_col, d_skip, w_glu, b_glu):
    t, w = y_lin.shape
    tm = min(t, 512)
    return pl.pallas_call(
        _s5_post_kernel,
        grid=(t // tm,),
        in_specs=[pl.BlockSpec((tm, w), lambda i: (i, 0)),
                  pl.BlockSpec((tm, w), lambda i: (i, u_col)),
                  pl.BlockSpec((1, w), lambda i: (0, 0)),
                  pl.BlockSpec((w, w), lambda i: (0, 0)),
                  pl.BlockSpec((1, w), lambda i: (0, 0))],
        out_specs=pl.BlockSpec((tm, w), lambda i: (i, 0)),
        out_shape=jax.ShapeDtypeStruct((t, w), BF16),
        compiler_params=_cparams(("parallel",), 32),
    )(y_lin, proj, d_skip, w_glu, b_glu)


def _gdn_kernel(q_ref, k_ref, v_ref, qh_ref, kh_ref, vh_ref, z_ref, sm_ref, cq_ref, ck_ref, cv_ref,
                hp_ref, nw_ref, tril_ref, cmask_ref, lmask_ref, o_ref, betab, gcb, grow, st_scr,
                cat_q, cat_k, cat_v, *, heads):
    tt = GDN_TILE
    hd = HEAD_DIM
    nck = tt // GDN_CHUNK
    first = pl.program_id(1) == 0

    @pl.when(first)
    def _():
        st_scr[...] = jnp.zeros_like(st_scr)

    sm = sm_ref[...]
    hp = hp_ref[...]
    beta_all = jax.nn.sigmoid(sm)
    g_all = hp[0:1, :] * _softplus(sm + hp[1:2, :])
    gc_all = jnp.dot(tril_ref[...], g_all, precision=HIGHEST, preferred_element_type=F32)
    gc_t = gc_all.T
    for h in range(heads):
        betab[h] = jnp.broadcast_to(beta_all[:, h:h + 1], (tt, hd))
        gcb[h] = jnp.broadcast_to(gc_all[:, heads + h:heads + h + 1], (tt, hd))
        grow[h:h + 1, :] = gc_t[heads + h:heads + h + 1, :]

    eye = (lax.broadcasted_iota(jnp.int32, (tt, tt), 0) == lax.broadcasted_iota(jnp.int32, (tt, tt), 1)).astype(F32)
    n_lv = lmask_ref.shape[0]

    for x_ref, halo_ref, cat in ((q_ref, qh_ref, cat_q), (k_ref, kh_ref, cat_k), (v_ref, vh_ref, cat_v)):
        cat[0:8, :] = jnp.where(first, 0.0, halo_ref[...])
        cat[8:8 + tt, :] = x_ref[...]

    def conv_silu(cat, cw_ref, hs):
        w = cw_ref[:, pl.ds(hs, hd)]
        acc = cat[pl.ds(8, tt), pl.ds(hs, hd)] * w[CONV_K - 1:CONV_K, :]
        for s in range(1, CONV_K):
            acc = acc + cat[pl.ds(8 - s, tt), pl.ds(hs, hd)] * w[CONV_K - 1 - s:CONV_K - s, :]
        return _silu(acc)

    def heads_lockstep(hs_idx):
        each = lambda f, *ls: [f(*a) for a in zip(*ls)]
        hs = [pl.multiple_of(h * hd, hd) for h in hs_idx]
        q = [conv_silu(cat_q, cq_ref, o) for o in hs]
        k = [conv_silu(cat_k, ck_ref, o) for o in hs]
        v = [conv_silu(cat_v, cv_ref, o) for o in hs]
        qn = each(lambda a: a * (lax.rsqrt(jnp.sum(a * a, axis=-1, keepdims=True) + RMS_EPS) * (hd ** -0.5)), q)
        kn = each(lambda a: a * lax.rsqrt(jnp.sum(a * a, axis=-1, keepdims=True) + RMS_EPS), k)
        bb = [betab[h] for h in hs_idx]
        gc = [gcb[h] for h in hs_idx]
        gr = [grow[pl.ds(h, 1), :] for h in hs_idx]
        decay = each(lambda c, r: jnp.exp(jnp.where(cmask_ref[0] > 0.0, jnp.concatenate([c, c], axis=1) - r, -1e30)),
                     gc, gr)
        kb = each(lambda a, b: a * b, kn, bb)
        knb = each(lambda a: a.astype(BF16), kn)
        lfull = each(lambda a, b, dc: _nt_dot(a.astype(BF16), b) * dc, kb, knb, decay)
        lmat = each(lambda a: a.astype(BF16), lfull)
        tinv = each(lambda a: eye - a * cmask_ref[1], lfull)
        for lv in range(1, n_lv):
            d = 1 << lv
            nb = tt // d
            tb = each(lambda a: a.astype(BF16), tinv)
            bd = each(lambda m: m * lmask_ref[lv], lmat)
            if d % 8 == 0:
                odd = each(lambda a: jnp.concatenate([a[i * d:(i + 1) * d] for i in range(1, nb, 2)], axis=0), tinv)
                tl = each(lambda a, b: jnp.dot(a.astype(BF16), b, preferred_element_type=F32).astype(BF16), odd, bd)
                upd = each(lambda a, b, c: a - jnp.dot(b, c, preferred_element_type=F32), odd, tl, tb)
                tinv = each(lambda a, u: jnp.concatenate(
                    [a[i * d:(i + 1) * d] if i % 2 == 0 else u[(i // 2) * d:(i // 2 + 1) * d] for i in range(nb)],
                    axis=0), tinv, upd)
            else:
                tl = each(lambda a, b: jnp.dot(a, b, preferred_element_type=F32).astype(BF16), tb, bd)
                tinv = each(lambda a, b, c: a - jnp.dot(b, c, preferred_element_type=F32), tinv, tl, tb)
        eg = each(jnp.exp, gc)
        rhs = each(lambda a, e, b, c: jnp.concatenate([a * e, b * c], axis=1).astype(BF16), kb, eg, v, bb)
        wu = each(lambda a, b: jnp.dot(a.astype(BF16), b, preferred_element_type=F32).astype(BF16), tinv, rhs)
        attn = each(lambda a, b, dc: (_nt_dot(a.astype(BF16), b) * dc).astype(BF16), qn, knb, decay)
        awu = each(lambda a, b: jnp.dot(a, b, preferred_element_type=F32), attn, wu)
        qp = each(lambda a, e, b: (a * e - b[:, :hd]).astype(BF16), qn, eg, awu)
        s = [st_scr[h] for h in hs_idx]
        outs = [[] for _ in hs_idx]
        for c in range(nck):
            r0 = c * GDN_CHUNK
            r1 = r0 + GDN_CHUNK
            gl = each(lambda a: a[r1 - 1:r1, :], gc)
            kd = each(lambda a, l, g: (a[r0:r1] * jnp.exp(l - g[r0:r1])).astype(BF16), kn, gl, gc)
            m = each(lambda a, b: _tn_dot(a, b[r0:r1]), kd, wu)
            sb = each(lambda a: a.astype(BF16), s)
            for i, (a, b, y) in enumerate(zip(qp, sb, awu)):
                outs[i].append(jnp.dot(a[r0:r1], b, preferred_element_type=F32) + y[r0:r1, hd:])
            s = each(lambda a, l, mm, b: a * jnp.exp(l) - jnp.dot(mm[:, :hd].astype(BF16), b,
                                                                  preferred_element_type=F32) + mm[:, hd:],
                     s, gl, m, sb)
        for i, h in enumerate(hs_idx):
            st_scr[h] = s[i]
            o = jnp.concatenate(outs[i], axis=0)
            z = z_ref[:, pl.ds(hs[i], hd)]
            o_ref[:, pl.ds(hs[i], hd)] = (o * _rms_scale(o) * nw_ref[...] * _silu(z)).astype(o_ref.dtype)

    def trip(j, carry):
        heads_lockstep([GDN_HEADS_PER_TRIP * j + i for i in range(GDN_HEADS_PER_TRIP)])
        return carry

    lax.fori_loop(0, heads // GDN_HEADS_PER_TRIP, trip, 0)


def _gdn_masks():
    tt = GDN_TILE
    ri = lax.broadcasted_iota(jnp.int32, (tt, tt), 0)
    ci = lax.broadcasted_iota(jnp.int32, (tt, tt), 1)
    same = (ri // GDN_CHUNK) == (ci // GDN_CHUNK)
    causal = same & (ci <= ri)
    tril = causal.astype(F32)
    cmask = jnp.stack([tril, (((ri % 2) == 1) & (ci == ri - 1)).astype(F32)])
    levels = []
    d = 1
    while d < GDN_CHUNK:
        levels.append((((ri // d) % 2) == 1) & ((ci // d) == (ri // d) - 1))
        d *= 2
    return tril, cmask, jnp.stack(levels).astype(BF16)


def _gdn(proj, small, conv_w, head_params, norm_w, masks, cols, batch, seq, heads):
    t = proj.shape[0]
    tt = GDN_TILE
    width = heads * HEAD_DIM
    tiles = seq // tt
    cq, ck, cv, cz = cols
    tril, cmask, lmask = masks

    def cur(col):
        return pl.BlockSpec((tt, width), lambda b, i: (b * tiles + i, col))

    def halo(col):
        return pl.BlockSpec((8, width), lambda b, i: (jnp.maximum((b * tiles + i) * (tt // 8) - 1, 0), col))

    def cw(col):
        return pl.BlockSpec((CONV_K, width), lambda b, i: (0, col))

    kern = functools.partial(_gdn_kernel, heads=heads)
    return pl.pallas_call(
        kern,
        grid=(batch, tiles),
        in_specs=[cur(cq), cur(ck), cur(cv), halo(cq), halo(ck), halo(cv), cur(cz),
                  pl.BlockSpec((tt, 128), lambda b, i: (b * tiles + i, 0)),
                  cw(0), cw(1), cw(2),
                  pl.BlockSpec((8, 128), lambda b, i: (0, 0)),
                  pl.BlockSpec((1, HEAD_DIM), lambda b, i: (0, 0)),
                  pl.BlockSpec((tt, tt), lambda b, i: (0, 0)),
                  pl.BlockSpec(cmask.shape, lambda b, i: (0, 0, 0)),
                  pl.BlockSpec(lmask.shape, lambda b, i: (0, 0, 0))],
        out_specs=pl.BlockSpec((tt, width), lambda b, i: (b * tiles + i, 0)),
        out_shape=jax.ShapeDtypeStruct((t, width), BF16),
        scratch_shapes=[pltpu.VMEM((heads, tt, HEAD_DIM), F32),
                        pltpu.VMEM((heads, tt, HEAD_DIM), F32),
                        pltpu.VMEM((8, tt), F32),
                        pltpu.VMEM((heads, HEAD_DIM, HEAD_DIM), F32),
                        pltpu.VMEM((tt + 8, width), F32),
                        pltpu.VMEM((tt + 8, width), F32),
                        pltpu.VMEM((tt + 8, width), F32)],
        compiler_params=_cparams(("parallel", "arbitrary"), 40),
    )(proj, proj, proj, proj, proj, proj, proj, small, conv_w, conv_w, conv_w, head_params, norm_w,
      tril, cmask, lmask)


def _route(h, wr_hl, rb):
    h_hi = h.astype(BF16)
    h_lo = (h - h_hi.astype(F32)).astype(BF16)
    first = jnp.dot(h_hi, wr_hl[...], preferred_element_type=F32)
    logits = first[:, :128] + first[:, 128:] + jnp.dot(h_lo, wr_hl[:, :128], preferred_element_type=F32)
    logits = logits.T[:N_EXPERTS]
    scores = jax.nn.sigmoid(logits)
    sel = scores + rb
    sel_rows = [sel[e:e + 1, :] for e in range(N_EXPERTS)]
    score_rows = [scores[e:e + 1, :] for e in range(N_EXPERTS)]
    epg = EXPERTS_PER_GROUP
    best_score = None
    best_group = None
    for g in range(N_GROUPS):
        rows = sel_rows[g * epg:(g + 1) * epg]
        gs = None
        for a in range(epg):
            for b in range(a + 1, epg):
                pair = rows[a] + rows[b]
                gs = pair if gs is None else jnp.maximum(gs, pair)
        if g == 0:
            best_score, best_group = gs, jnp.zeros_like(gs)
        else:
            better = gs > best_score
            best_group = jnp.where(better, float(g), best_group)
            best_score = jnp.where(better, gs, best_score)
    in_sel = []
    in_score = []
    for j in range(epg):
        a = sel_rows[j]
        b = score_rows[j]
        for g in range(1, N_GROUPS):
            pick = best_group == float(g)
            a = jnp.where(pick, sel_rows[g * epg + j], a)
            b = jnp.where(pick, score_rows[g * epg + j], b)
        in_sel.append(a)
        in_score.append(b)
    i1, m1, w1 = jnp.zeros_like(in_sel[0]), in_sel[0], in_score[0]
    for j in range(1, epg):
        better = in_sel[j] > m1
        i1 = jnp.where(better, float(j), i1)
        m1 = jnp.where(better, in_sel[j], m1)
        w1 = jnp.where(better, in_score[j], w1)
    i2 = m2 = w2 = None
    for j in range(epg):
        cand = jnp.where(i1 == float(j), -jnp.inf, in_sel[j])
        if j == 0:
            i2, m2, w2 = jnp.zeros_like(cand), cand, in_score[0]
        else:
            better = cand > m2
            i2 = jnp.where(better, float(j), i2)
            m2 = jnp.where(better, cand, m2)
            w2 = jnp.where(better, in_score[j], w2)
    total = w1 + w2
    zero = jnp.zeros_like(w1)
    return jnp.concatenate([best_group * epg + i1, best_group * epg + i2, w1 / total, w2 / total,
                            zero, zero, zero, zero], axis=0)


def _merge_kernel(ya_ref, yb_ref, ga_ref, gb_ref, x_ref, mod_ref, wa_ref, wb_ref, wo_ref, nw_ref, wrt_ref, rb_ref,
                  o_ref, h_ref, r_ref):
    pa = jnp.dot(ya_ref[...], wa_ref[...], preferred_element_type=F32)
    pb = jnp.dot(yb_ref[...], wb_ref[...], preferred_element_type=F32)
    merged = jax.nn.sigmoid(ga_ref[...]) * pa + jax.nn.sigmoid(gb_ref[...]) * pb
    out = jnp.dot(merged.astype(BF16), wo_ref[...], preferred_element_type=F32)
    xn = x_ref[...] + mod_ref[2:3, :] * out
    o_ref[...] = xn
    h = xn * _rms_scale(xn) * nw_ref[...] * (1.0 + mod_ref[4:5, :]) + mod_ref[3:4, :]
    h_ref[...] = _pack_halves(h)
    r_ref[...] = _route(h, wrt_ref, rb_ref[...])


def _merge(ya, yb, proj, x2, modl, wa, wb, wo, nw2, wr_t, rb, seq, ga_col, gb_col):
    t, d = x2.shape
    w = ya.shape[1]
    tm = min(seq, 256)
    per_batch = seq // tm
    const = dict(pipeline_mode=pl.Buffered(1))
    return pl.pallas_call(
        _merge_kernel,
        grid=(t // tm,),
        in_specs=[pl.BlockSpec((tm, w), lambda i: (i, 0)),
                  pl.BlockSpec((tm, w), lambda i: (i, 0)),
                  pl.BlockSpec((tm, d), lambda i: (i, ga_col)),
                  pl.BlockSpec((tm, d), lambda i: (i, gb_col)),
                  pl.BlockSpec((tm, d), lambda i: (i, 0)),
                  pl.BlockSpec((None, 6, d), lambda i: (i // per_batch, 0, 0)),
                  pl.BlockSpec((w, d), lambda i: (0, 0), **const),
                  pl.BlockSpec((w, d), lambda i: (0, 0), **const),
                  pl.BlockSpec((d, d), lambda i: (0, 0), **const),
                  pl.BlockSpec((1, d), lambda i: (0, 0)),
                  pl.BlockSpec((d, 256), lambda i: (0, 0)),
                  pl.BlockSpec((N_EXPERTS, 1), lambda i: (0, 0))],
        out_specs=[pl.BlockSpec((tm, d), lambda i: (i, 0)),
                   pl.BlockSpec((tm, d // 2), lambda i: (i, 0)),
                   pl.BlockSpec((8, tm), lambda i: (0, i))],
        out_shape=[jax.ShapeDtypeStruct((t, d), F32), jax.ShapeDtypeStruct((t, d // 2), jnp.uint32),
                   jax.ShapeDtypeStruct((8, t), F32)],
        compiler_params=_cparams(("parallel",), 52),
    )(ya, yb, proj, proj, x2, modl, wa, wb, wo, nw2, wr_t, rb)


def _moe_kernel(te_ref, nu_ref, x_ref, wg_ref, wu_ref, wd_ref, o_ref):
    used = pl.program_id(0) < nu_ref[0]

    @pl.when(used)
    def _():
        x_lo, x_hi = _unpack_halves(x_ref[...])
        x_lo = x_lo.astype(BF16)
        x_hi = x_hi.astype(BF16)
        half = x_lo.shape[1]

        def up(w_ref):
            return (jnp.dot(x_lo, w_ref[:half, :], preferred_element_type=F32)
                    + jnp.dot(x_hi, w_ref[half:, :], preferred_element_type=F32))

        hid = (_silu(up(wg_ref)) * up(wu_ref)).astype(BF16)
        o_ref[...] = _pack_halves(jnp.dot(hid, wd_ref[...], preferred_element_type=F32))

    @pl.when(jnp.logical_not(used))
    def _():
        o_ref[...] = jnp.zeros_like(o_ref)


def _cast_kernel(w_ref, after_ref, o_ref, done_ref):
    del after_ref
    o_ref[...] = w_ref[...].astype(o_ref.dtype)
    done_ref[...] = jnp.zeros_like(done_ref)


def _expert_weights_bf16(w, layer, after):
    _, e, a, b = w.shape
    ta = min(a, (1 << 20) // b)
    return pl.pallas_call(
        _cast_kernel,
        grid=(e, a // ta),
        in_specs=[pl.BlockSpec((None, None, ta, b), lambda i, j: (layer, i, j, 0)),
                  pl.BlockSpec((8, 128), lambda i, j: (0, 0))],
        out_specs=[pl.BlockSpec((None, ta, b), lambda i, j: (i, j, 0)),
                   pl.BlockSpec((8, 128), lambda i, j: (0, 0))],
        out_shape=[jax.ShapeDtypeStruct((e, a, b), BF16), jax.ShapeDtypeStruct((8, 128), F32)],
        compiler_params=_cparams(("arbitrary", "arbitrary"), 32),
    )(w, after)


def _moe_experts(tile_expert, n_used, xs, wg, wu, wd, tm):
    p, dh = xs.shape
    d = 2 * dh
    f = wg.shape[-1]
    grid_spec = pltpu.PrefetchScalarGridSpec(
        num_scalar_prefetch=2,
        grid=(p // tm,),
        in_specs=[pl.BlockSpec((tm, dh), lambda i, te, nu: (i, 0)),
                  pl.BlockSpec((None, d, f), lambda i, te, nu: (te[i], 0, 0)),
                  pl.BlockSpec((None, d, f), lambda i, te, nu: (te[i], 0, 0)),
                  pl.BlockSpec((None, f, d), lambda i, te, nu: (te[i], 0, 0))],
        out_specs=pl.BlockSpec((tm, dh), lambda i, te, nu: (i, 0)),
    )
    return pl.pallas_call(
        _moe_kernel,
        grid_spec=grid_spec,
        out_shape=jax.ShapeDtypeStruct((p, dh), jnp.uint32),
        compiler_params=_cparams(("arbitrary",), 52),
    )(tile_expert, n_used, xs, wg, wu, wd)


def _dispatch_plan(expert_idx, tm, n_tiles):
    flat_e = expert_idx.reshape(-1)
    onehot = (flat_e[:, None] == jnp.arange(N_EXPERTS, dtype=jnp.int32)[None, :]).astype(jnp.int32)
    csum = jnp.cumsum(onehot, axis=0)
    rank = jnp.sum((csum - onehot) * onehot, axis=1)
    sizes = csum[-1]
    padded = ((sizes + tm - 1) // tm) * tm
    pad_end = jnp.cumsum(padded)
    pad_start = pad_end - padded
    dest = pad_start[flat_e] + rank
    tile_start = jnp.arange(n_tiles, dtype=jnp.int32) * tm
    tile_expert = jnp.minimum(jnp.sum((tile_start[:, None] >= pad_end[None, :]).astype(jnp.int32), axis=1),
                              N_EXPERTS - 1).astype(jnp.int32)
    n_used = (pad_end[-1] // tm).astype(jnp.int32).reshape(1)
    dest = dest.astype(jnp.int32)
    n_slots = flat_e.shape[0]
    _, by_slot = lax.sort_key_val(dest, jnp.arange(n_slots, dtype=jnp.int32) // TOP_K)
    start = jnp.cumsum(sizes) - sizes
    tile_off = tile_start - pad_start[tile_expert]
    tile_valid = jnp.clip(sizes[tile_expert] - tile_off, 0, tm)
    lane = jnp.arange(tm, dtype=jnp.int32)[None, :]
    pick = jnp.clip((start[tile_expert] + tile_off)[:, None] + lane, 0, n_slots - 1)
    filler = (tile_start[:, None] + lane) % (n_slots // TOP_K)
    src_tok = jnp.where(lane < tile_valid[:, None], jnp.take(by_slot, pick.reshape(-1)).reshape(n_tiles, tm), filler)
    return dest, src_tok.reshape(-1).astype(jnp.int32), tile_expert, n_used


def _combine_kernel(x_ref, y0_ref, y1_ref, gw_ref, mod_ref, fw_ref, *rest, final_norm):
    o_ref = rest[-1]
    gw = gw_ref[...]
    y0_lo, y0_hi = _unpack_halves(y0_ref[...])
    y1_lo, y1_hi = _unpack_halves(y1_ref[...])
    moe = jnp.concatenate([gw[:, 0:1] * y0_lo + gw[:, 1:2] * y1_lo, gw[:, 0:1] * y0_hi + gw[:, 1:2] * y1_hi], axis=1)
    xn = x_ref[...] + mod_ref[5:6, :] * moe
    if final_norm:
        xn = xn * _rms_scale(xn) * fw_ref[...]
    o_ref[...] = xn


def _combine(x2, y_pairs, gate_w, modl, final_w, seq, final_norm, ties=()):
    t, d = x2.shape
    tm = min(seq, 512)
    per_batch = seq // tm
    nblk = t // tm
    return pl.pallas_call(
        functools.partial(_combine_kernel, final_norm=final_norm),
        grid=(nblk,),
        in_specs=[pl.BlockSpec((tm, d), lambda i: (i, 0)),
                  pl.BlockSpec((tm, d // 2), lambda i: (i, 0)),
                  pl.BlockSpec((tm, d // 2), lambda i: (i + nblk, 0)),
                  pl.BlockSpec((tm, TOP_K), lambda i: (i, 0)),
                  pl.BlockSpec((None, 6, d), lambda i: (i // per_batch, 0, 0)),
                  pl.BlockSpec((1, d), lambda i: (0, 0))] + [pl.BlockSpec((8, 128), lambda i: (0, 0)) for _ in ties],
        out_specs=pl.BlockSpec((tm, d), lambda i: (i, 0)),
        out_shape=jax.ShapeDtypeStruct((t, d), F32),
        compiler_params=_cparams(("parallel",), 40),
    )(x2, y_pairs, y_pairs, gate_w, modl, final_w, *ties)


def kernel(x, c, w_ada, b_ada, norm1_w, norm2_w, w_in, s5_a_re, s5_a_im, s5_log_dt, s5_b_re, s5_b_im, s5_c_re, s5_c_im, s5_d, s5_w_glu, s5_b_glu, gdn_conv_w, gdn_a_log, gdn_dt_bias, gdn_norm_w, w_proj_a, w_proj_b, w_out, w_router, router_bias, w_gate, w_up, w_down, final_norm_w):
    batch, seq, d = x.shape
    depth = w_ada.shape[0]
    t = batch * seq
    s5w = s5_d.shape[1]
    gdw = w_proj_b.shape[1]
    heads = gdn_a_log.shape[1]
    assert gdw == heads * HEAD_DIM and s5w == gdw and d == 2 * gdw
    assert seq % GDN_TILE == 0 and seq % S5_CHUNK == 0
    n_levels = int(math.log2(seq // S5_CHUNK))
    assert S5_CHUNK << n_levels == seq and S5_OCT * S5_GROUP == 128

    c_pad = jnp.zeros((8, d), F32).at[:batch].set(c)
    mod = _ada(c_pad, w_ada, b_ada)

    o_u, o_q, o_k, o_v, o_z = 0, s5w, 2 * s5w, 3 * s5w, 4 * s5w
    o_ba = 5 * s5w
    o_ga = o_ba + 2 * heads
    o_gb = o_ga + d
    col_u, col_q, col_k, col_v, col_z = 4, 5, 6, 7, 8

    gdn_masks = _gdn_masks()
    s5_rep = _s5_replicate()
    s5_ops = jax.vmap(functools.partial(_s5_weights, n_levels=n_levels))(
        s5_a_re, s5_a_im, s5_log_dt, s5_b_re, s5_b_im, s5_c_re, s5_c_im)

    wr_hi = w_router.astype(BF16)
    wr_lo = (w_router - wr_hi.astype(F32)).astype(BF16)
    pad_e = ((0, 0), (0, 128 - N_EXPERTS))
    wr_t = jnp.concatenate([jnp.pad(wr_hi, pad_e), jnp.pad(wr_lo, pad_e)], axis=1)
    rb = router_bias.reshape(N_EXPERTS, 1).astype(F32)

    tm_e = min(512, t)
    n_tiles = (TOP_K * t) // tm_e + N_EXPERTS

    w_in_t = jnp.transpose(w_in, (0, 2, 1))
    w_big = _repack_in_proj(w_in_t, o_ga, 2 * d, o_ba)
    w_small = jnp.pad(w_in_t[:, o_ba:o_ga], ((0, 0), (0, 128 - 2 * heads), (0, 0))).astype(BF16)

    x2 = x.reshape(t, d)
    for l in range(depth):
        modl = mod[l, :batch].reshape(batch, 6, d)
        proj, small = _inproj(x2, modl, norm1_w[l].reshape(1, d), w_big, w_small, l, seq)

        y_lin = _s5_core(proj, col_u * s5w // 128, *s5_ops, s5_rep, l, batch, seq)
        y_a = _s5_post(y_lin, proj, col_u, s5_d[l].reshape(1, s5w), s5_w_glu[l].astype(BF16),
                       s5_b_glu[l].reshape(1, s5w))

        head_params = jnp.zeros((8, 128), F32)
        head_params = head_params.at[0, heads:2 * heads].set(-jnp.exp(gdn_a_log[l]))
        head_params = head_params.at[1, heads:2 * heads].set(gdn_dt_bias[l])
        y_b = _gdn(proj, small, gdn_conv_w[l], head_params, gdn_norm_w[l].reshape(1, HEAD_DIM), gdn_masks,
                   (col_q, col_k, col_v, col_z), batch, seq, heads)

        x2, h2, route = _merge(y_a, y_b, proj, x2, modl, w_proj_a[l].astype(BF16), w_proj_b[l].astype(BF16),
                               w_out[l].astype(BF16), norm2_w[l].reshape(1, d), wr_t, rb, seq, 0, 1)

        expert_idx = route[0:TOP_K].T.astype(jnp.int32)
        gate_w = route[TOP_K:2 * TOP_K].T
        dest, src_tok, tile_expert, n_used = _dispatch_plan(expert_idx, tm_e, n_tiles)
        xs = jnp.take(h2, src_tok, axis=0, mode='clip')
        if l == 0:
            (wg_l, _), (wu_l, _) = _expert_weights_bf16(w_gate, 0, h2), _expert_weights_bf16(w_up, 0, h2)
        wd_l, _ = _expert_weights_bf16(w_down, l, h2)
        ys = _moe_experts(tile_expert, n_used, xs, wg_l, wu_l, wd_l, tm_e)
        y_pairs = jnp.take(ys, dest.reshape(t, TOP_K).T.reshape(-1), axis=0, mode='clip')
        ties = ()
        if l + 1 < depth:
            (wg_l, tie_g), (wu_l, tie_u) = _expert_weights_bf16(w_gate, l + 1, ys), _expert_weights_bf16(w_up, l + 1, ys)
            ties = (tie_g, tie_u)
        x2 = _combine(x2, y_pairs, gate_w, modl, final_norm_w.reshape(1, d), seq, l == depth - 1, ties)

    return x2.reshape(batch, seq, d)
```

```python
import functools
import math

import jax
import jax.numpy as jnp
from jax import lax
from jax.experimental import pallas as pl
from jax.experimental.pallas import tpu as pltpu

F32 = jnp.float32
BF16 = jnp.bfloat16
HIGHEST = lax.Precision.HIGHEST

RMS_EPS = 1e-6
S5_GROUP = 16
S5_STATE = 64
S5_CHUNK = 16
S5_OCT = 8
HEAD_DIM = 128
GDN_CHUNK = 64
GDN_TILE = 256
GDN_HEADS_PER_TRIP = 8
CONV_K = 4
N_EXPERTS = 16
N_GROUPS = 4
EXPERTS_PER_GROUP = N_EXPERTS // N_GROUPS
TOP_K = 2
MIB = 1024 * 1024


def _cparams(semantics, vmem_mib):
    return pltpu.CompilerParams(dimension_semantics=semantics, vmem_limit_bytes=vmem_mib * MIB)


def _silu(x):
    return x * jax.nn.sigmoid(x)


def _softplus(x):
    return jnp.maximum(x, 0.0) + jnp.log(1.0 + jnp.exp(-jnp.abs(x)))


def _rms_scale(x):
    return lax.rsqrt(jnp.mean(x * x, axis=-1, keepdims=True) + RMS_EPS)


def _pack_halves(x):
    n = x.shape[1] // 2
    lo = lax.bitcast_convert_type(x[:, :n].astype(BF16).astype(F32), jnp.uint32)
    hi = lax.bitcast_convert_type(x[:, n:].astype(BF16).astype(F32), jnp.uint32)
    return (lo >> 16) | hi


def _unpack_halves(p):
    lo = lax.bitcast_convert_type(p << 16, F32)
    hi = lax.bitcast_convert_type(p & jnp.uint32(0xFFFF0000), F32)
    return lo, hi


def _nt_dot(a, b):
    return lax.dot_general(a, b, (((1,), (1,)), ((), ())), preferred_element_type=F32)


def _tn_dot(a, b):
    return lax.dot_general(a, b, (((0,), (0,)), ((), ())), preferred_element_type=F32)


def _ada_kernel(c_ref, w_ref, b_ref, o_ref):
    ca = _silu(c_ref[...]).astype(BF16)
    o_ref[...] = jnp.dot(ca, w_ref[...].astype(BF16), preferred_element_type=F32) + b_ref[...]


def _ada(c_pad, w_ada, b_ada):
    depth, d, n = w_ada.shape
    rows = c_pad.shape[0]
    tn = min(n, 1024)
    return pl.pallas_call(
        _ada_kernel,
        grid=(depth, n // tn),
        in_specs=[pl.BlockSpec((rows, d), lambda l, j: (0, 0)),
                  pl.BlockSpec((None, d, tn), lambda l, j: (l, 0, j)),
                  pl.BlockSpec((None, 1, tn), lambda l, j: (l, 0, j))],
        out_specs=pl.BlockSpec((None, rows, tn), lambda l, j: (l, 0, j)),
        out_shape=jax.ShapeDtypeStruct((depth, rows, n), F32),
        compiler_params=_cparams(("parallel", "parallel"), 40),
    )(c_pad, w_ada, b_ada.reshape(depth, 1, n))


def _inproj_kernel(x_ref, mod_ref, nw_ref, w_ref, wsm_ref, o_ref, osm_ref, h_scr):
    @pl.when(pl.program_id(1) == 0)
    def _():
        x = x_ref[...]
        h = x * _rms_scale(x) * nw_ref[...] * (1.0 + mod_ref[1:2, :]) + mod_ref[0:1, :]
        hb = h.astype(BF16)
        h_scr[...] = hb
        osm_ref[...] = _nt_dot(hb, wsm_ref[...])

    o_ref[...] = _nt_dot(h_scr[...], w_ref[...])


def _inproj(x2, modl, nw, w_big, w_small, layer, seq):
    t, d = x2.shape
    n = w_big.shape[1]
    tm = min(seq, 1024)
    tn = 1536
    per_batch = seq // tm
    return pl.pallas_call(
        _inproj_kernel,
        grid=(t // tm, n // tn),
        in_specs=[pl.BlockSpec((tm, d), lambda i, j: (i, 0)),
                  pl.BlockSpec((None, 6, d), lambda i, j: (i // per_batch, 0, 0)),
                  pl.BlockSpec((1, d), lambda i, j: (0, 0)),
                  pl.BlockSpec((None, tn, d), lambda i, j: (layer, j, 0)),
                  pl.BlockSpec((None, 128, d), lambda i, j: (layer, 0, 0))],
        out_specs=[pl.BlockSpec((tm, tn), lambda i, j: (i, j)),
                   pl.BlockSpec((tm, 128), lambda i, j: (i, 0))],
        out_shape=[jax.ShapeDtypeStruct((t, n), F32), jax.ShapeDtypeStruct((t, 128), F32)],
        scratch_shapes=[pltpu.VMEM((tm, d), BF16)],
        compiler_params=_cparams(("parallel", "arbitrary"), 52),
    )(x2, modl, nw, w_big, w_small)


def _repack_kernel(a_ref, b_ref, o_ref, *, gate_blocks, shift):
    c = pl.program_id(1)

    @pl.when(c < gate_blocks)
    def _():
        x = jnp.concatenate([a_ref[...], b_ref[...]], axis=0)
        o_ref[...] = x[shift:shift + o_ref.shape[0]].astype(o_ref.dtype)

    @pl.when(c >= gate_blocks)
    def _():
        o_ref[...] = a_ref[...].astype(o_ref.dtype)


def _repack_in_proj(w_in_t, gate_start, gate_cols, lead_cols):
    depth, _, d = w_in_t.shape
    rb = 512
    shift = gate_start % rb
    base = gate_start - shift
    assert shift % 16 == 0 and gate_cols % rb == 0 and lead_cols % rb == 0 and lead_cols == base
    gate_blocks = gate_cols // rb
    kern = functools.partial(_repack_kernel, gate_blocks=gate_blocks, shift=shift)
    return pl.pallas_call(
        kern,
        grid=(depth, (gate_cols + lead_cols) // rb),
        in_specs=[pl.BlockSpec((None, rb, d),
                               lambda l, c: (l, jnp.where(c < gate_blocks, base // rb + c, c - gate_blocks), 0)),
                  pl.BlockSpec((None, shift, d),
                               lambda l, c: (l, jnp.where(c < gate_blocks, (base + (c + 1) * rb) // shift, 0), 0))],
        out_specs=pl.BlockSpec((None, rb, d), lambda l, c: (l, c, 0)),
        out_shape=jax.ShapeDtypeStruct((depth, gate_cols + lead_cols, d), BF16),
        compiler_params=_cparams(("parallel", "arbitrary"), 32),
    )(w_in_t, w_in_t)


def _s5_weights(a_re, a_im, log_dt, b_re, b_im, c_re, c_im, n_levels):
    tc = S5_CHUNK
    groups, p = a_re.shape
    n = b_re.shape[-1]
    oc = S5_OCT
    octs = groups // oc
    lam = lax.complex(a_re, a_im)
    dt = jnp.exp(log_dt)[:, None]
    log_a = lam * dt
    a_bar = jnp.exp(log_a)
    b_bar = ((a_bar - 1.0) / lam)[..., None] * lax.complex(b_re, b_im)
    c_mat = lax.complex(c_re, c_im)
    apow = jnp.exp(log_a[:, :, None] * jnp.arange(tc + 1, dtype=F32)[None, None, :])
    lane = jnp.arange(tc * n)
    e_t = (jnp.arange(tc)[:, None] == (lane // n)[None, :]).astype(F32)
    e_n = (jnp.arange(n)[:, None] == (lane % n)[None, :]).astype(F32)

    def expand(z, e):
        f = lambda r: jnp.einsum('gpk,kj->gpj', r, e, precision=HIGHEST)
        return lax.complex(f(jnp.real(z)), f(jnp.imag(z)))

    def re_negim_rows(z):
        return jnp.concatenate([jnp.real(z), -jnp.imag(z)], axis=1)

    c_rep = expand(jnp.transpose(c_mat, (0, 2, 1)), e_n)
    lag_c = re_negim_rows(expand(apow[:, :, :tc], e_t) * c_rep)
    hoc = re_negim_rows(expand(apow[:, :, 1:], e_t) * c_rep)
    b_t = jnp.transpose(b_bar, (0, 2, 1))
    b_cat = jnp.concatenate([jnp.real(b_t), jnp.imag(b_t)], axis=-1)
    kern = jnp.einsum('gnk,gkj->gnj', b_cat, lag_c, precision=HIGHEST)
    tcomp = jnp.stack([jnp.pad(kern[:, :, :(tc - s) * n], ((0, 0), (0, 0), (s * n, 0))) for s in range(tc)],
                      axis=1)
    tcomp = jnp.transpose(tcomp.reshape(octs, oc, tc, n, tc * n), (0, 2, 1, 3, 4)).reshape(octs, tc * oc * n, tc * n)
    ap_rev = jnp.transpose(apow[:, :, tc - 1 - jnp.arange(tc)], (0, 2, 1))
    hin_c = ap_rev[:, :, None, :] * b_t[:, None, :, :]
    hc = jnp.concatenate([jnp.real(hin_c), jnp.imag(hin_c)], axis=-1)
    hc = jnp.transpose(hc.reshape(octs, oc, tc, n, 2 * p), (0, 2, 1, 3, 4)).reshape(octs, tc * oc * n, 2 * p)
    hoc = hoc.reshape(octs, oc * 2 * p, tc * n)
    steps = (tc * (2.0 ** jnp.arange(n_levels, dtype=F32)))
    amul = jnp.exp(log_a[:, None, :] * steps[None, :, None])
    a1 = jnp.concatenate([jnp.real(amul), jnp.real(amul)], axis=-1)
    a2 = jnp.concatenate([-jnp.imag(amul), jnp.imag(amul)], axis=-1)

    def lanes_by_group(a):
        return jnp.transpose(a.reshape(octs, oc, n_levels, 2 * p), (0, 2, 1, 3)).reshape(octs, n_levels, oc * 2 * p)

    return (tcomp.astype(BF16), hc.astype(BF16), hoc.astype(BF16),
            lanes_by_group(a1).astype(F32), lanes_by_group(a2).astype(F32))


def _s5_replicate():
    r = lax.broadcasted_iota(jnp.int32, (S5_CHUNK * S5_GROUP, S5_CHUNK * 128), 0)
    c = lax.broadcasted_iota(jnp.int32, (S5_CHUNK * S5_GROUP, S5_CHUNK * 128), 1)
    return ((r // S5_GROUP == c // 128) & (r % S5_GROUP == c % S5_GROUP)).astype(BF16)


def _s5_core_kernel(u_ref, tc_ref, hc_ref, hoc_ref, a1_ref, a2_ref, rep_ref, y_ref, toep_scr, hin_scr, hout_scr,
                    *, n_levels):
    tc = S5_CHUNK
    lanes = u_ref.shape[1]
    state_lanes = hc_ref.shape[1]

    @pl.when(pl.program_id(1) == 0)
    def _():
        kd = toep_scr.shape[1]
        rb = 256
        col_g = (lax.broadcasted_iota(jnp.int32, (rb, kd), 1) // S5_GROUP) % S5_OCT
        row_i = lax.broadcasted_iota(jnp.int32, (rb, kd), 0)
        for i in range(toep_scr.shape[0] // rb):
            full = jnp.dot(tc_ref[i * rb:(i + 1) * rb, :], rep_ref[...], preferred_element_type=F32)
            row_g = ((row_i + i * rb) // S5_GROUP) % S5_OCT
            toep_scr[i * rb:(i + 1) * rb, :] = jnp.where(row_g == col_g, full, 0.0).astype(BF16)
        for i in range(hout_scr.shape[0] // rb):
            full = jnp.dot(hoc_ref[i * rb:(i + 1) * rb, :], rep_ref[...], preferred_element_type=F32)
            row_g = (row_i + i * rb) // state_lanes
            hout_scr[i * rb:(i + 1) * rb, :] = jnp.where(row_g == col_g, full, 0.0).astype(BF16)
        hrow_g = (lax.broadcasted_iota(jnp.int32, hc_ref.shape, 0) // S5_GROUP) % S5_OCT
        for g in range(S5_OCT):
            hin_scr[:, g * state_lanes:(g + 1) * state_lanes] = jnp.where(hrow_g == g, hc_ref[...], 0.0).astype(BF16)

    rows = u_ref.shape[0] // tc
    x = jnp.concatenate([u_ref[pl.ds(t, rows, stride=tc), :] for t in range(tc)], axis=1).astype(BF16)
    s_loc = jnp.dot(x, hin_scr[...], preferred_element_type=F32)
    pos = lax.broadcasted_iota(jnp.int32, (rows, state_lanes), 0)
    s_in = []
    for g in range(S5_OCT):
        ln = slice(g * state_lanes, (g + 1) * state_lanes)
        s = s_loc[:, ln]
        for k in range(n_levels):
            d = 1 << k
            prev = jnp.where(pos >= d, pltpu.roll(s, d, axis=0), 0.0)
            s = s + a1_ref[k:k + 1, ln] * prev + a2_ref[k:k + 1, ln] * pltpu.roll(prev, state_lanes // 2, axis=1)
        s_in.append(jnp.where(pos >= 1, pltpu.roll(s, 1, axis=0), 0.0).astype(BF16))
    y = (jnp.dot(x, toep_scr[...], preferred_element_type=F32)
         + jnp.dot(jnp.concatenate(s_in, axis=1), hout_scr[...], preferred_element_type=F32))
    for t in range(tc):
        y_ref[pl.ds(t, rows, stride=tc), :] = y[:, t * lanes:(t + 1) * lanes]


def _s5_core(proj, u_lane_block, tcomp, hc, hoc, a1, a2, rep, layer, batch, seq):
    t = proj.shape[0]
    _, octs, kdim, tn = tcomp.shape
    sl = hc.shape[-1]
    sw = hoc.shape[2]
    n_levels = a1.shape[2]
    lanes = kdim // S5_CHUNK
    kern = functools.partial(_s5_core_kernel, n_levels=n_levels)
    return pl.pallas_call(
        kern,
        grid=(octs, batch),
        in_specs=[pl.BlockSpec((seq, lanes), lambda o, b: (b, u_lane_block + o)),
                  pl.BlockSpec((None, None, kdim, tn), lambda o, b: (layer, o, 0, 0)),
                  pl.BlockSpec((None, None, kdim, sl), lambda o, b: (layer, o, 0, 0)),
                  pl.BlockSpec((None, None, sw, tn), lambda o, b: (layer, o, 0, 0)),
                  pl.BlockSpec((None, None, n_levels, sw), lambda o, b: (layer, o, 0, 0)),
                  pl.BlockSpec((None, None, n_levels, sw), lambda o, b: (layer, o, 0, 0)),
                  pl.BlockSpec((tn, kdim), lambda o, b: (0, 0))],
        out_specs=pl.BlockSpec((seq, lanes), lambda o, b: (b, o)),
        out_shape=jax.ShapeDtypeStruct((t, octs * lanes), F32),
        scratch_shapes=[pltpu.VMEM((kdim, kdim), BF16), pltpu.VMEM((kdim, sw), BF16), pltpu.VMEM((sw, kdim), BF16)],
        compiler_params=_cparams(("parallel", "arbitrary"), 48),
    )(proj, tcomp, hc, hoc, a1, a2, rep)


def _gdn_kernel(q_ref, k_ref, v_ref, qh_ref, kh_ref, vh_ref, z_ref, sm_ref, cq_ref, ck_ref, cv_ref,
                hp_ref, nw_ref, tril_ref, cmask_ref, lmask_ref, o_ref, betab, gcb, grow, st_scr,
                cat_q, cat_k, cat_v, *, heads):
    tt = GDN_TILE
    hd = HEAD_DIM
    nck = tt // GDN_CHUNK
    first = pl.program_id(1) == 0

    @pl.when(first)
    def _():
        st_scr[...] = jnp.zeros_like(st_scr)

    sm = sm_ref[...]
    hp = hp_ref[...]
    beta_all = jax.nn.sigmoid(sm)
    g_all = hp[0:1, :] * _softplus(sm + hp[1:2, :])
    gc_all = jnp.dot(tril_ref[...], g_all, precision=HIGHEST, preferred_element_type=F32)
    gc_t = gc_all.T
    for h in range(heads):
        betab[h] = jnp.broadcast_to(beta_all[:, h:h + 1], (tt, hd))
        gcb[h] = jnp.broadcast_to(gc_all[:, heads + h:heads + h + 1], (tt, hd))
        grow[h:h + 1, :] = gc_t[heads + h:heads + h + 1, :]

    eye = (lax.broadcasted_iota(jnp.int32, (tt, tt), 0) == lax.broadcasted_iota(jnp.int32, (tt, tt), 1)).astype(F32)
    n_lv = lmask_ref.shape[0]

    for x_ref, halo_ref, cat in ((q_ref, qh_ref, cat_q), (k_ref, kh_ref, cat_k), (v_ref, vh_ref, cat_v)):
        cat[0:8, :] = jnp.where(first, 0.0, halo_ref[...])
        cat[8:8 + tt, :] = x_ref[...]

    def conv_silu(cat, cw_ref, hs):
        w = cw_ref[:, pl.ds(hs, hd)]
        acc = cat[pl.ds(8, tt), pl.ds(hs, hd)] * w[CONV_K - 1:CONV_K, :]
        for s in range(1, CONV_K):
            acc = acc + cat[pl.ds(8 - s, tt), pl.ds(hs, hd)] * w[CONV_K - 1 - s:CONV_K - s, :]
        return _silu(acc)

    def heads_lockstep(hs_idx):
        each = lambda f, *ls: [f(*a) for a in zip(*ls)]
        hs = [pl.multiple_of(h * hd, hd) for h in hs_idx]
        q = [conv_silu(cat_q, cq_ref, o) for o in hs]
        k = [conv_silu(cat_k, ck_ref, o) for o in hs]
        v = [conv_silu(cat_v, cv_ref, o) for o in hs]
        qn = each(lambda a: a * (lax.rsqrt(jnp.sum(a * a, axis=-1, keepdims=True) + RMS_EPS) * (hd ** -0.5)), q)
        kn = each(lambda a: a * lax.rsqrt(jnp.sum(a * a, axis=-1, keepdims=True) + RMS_EPS), k)
        bb = [betab[h] for h in hs_idx]
        gc = [gcb[h] for h in hs_idx]
        gr = [grow[pl.ds(h, 1), :] for h in hs_idx]
        decay = each(lambda c, r: jnp.exp(jnp.where(cmask_ref[0] > 0.0, jnp.concatenate([c, c], axis=1) - r, -1e30)),
                     gc, gr)
        kb = each(lambda a, b: a * b, kn, bb)
        knb = each(lambda a: a.astype(BF16), kn)
        lfull = each(lambda a, b, dc: _nt_dot(a.astype(BF16), b) * dc, kb, knb, decay)
        lmat = each(lambda a: a.astype(BF16), lfull)
        tinv = each(lambda a: eye - a * cmask_ref[1], lfull)
        for lv in range(1, n_lv):
            d = 1 << lv
            nb = tt // d
            tb = each(lambda a: a.astype(BF16), tinv)
            bd = each(lambda m: m * lmask_ref[lv], lmat)
            if d % 8 == 0:
                odd = each(lambda a: jnp.concatenate([a[i * d:(i + 1) * d] for i in range(1, nb, 2)], axis=0), tinv)
                tl = each(lambda a, b: jnp.dot(a.astype(BF16), b, preferred_element_type=F32).astype(BF16), odd, bd)
                upd = each(lambda a, b, c: a - jnp.dot(b, c, preferred_element_type=F32), odd, tl, tb)
                tinv = each(lambda a, u: jnp.concatenate(
                    [a[i * d:(i + 1) * d] if i % 2 == 0 else u[(i // 2) * d:(i // 2 + 1) * d] for i in range(nb)],
                    axis=0), tinv, upd)
            else:
                tl = each(lambda a, b: jnp.dot(a, b, preferred_element_type=F32).astype(BF16), tb, bd)
                tinv = each(lambda a, b, c: a - jnp.dot(b, c, preferred_element_type=F32), tinv, tl, tb)
        eg = each(jnp.exp, gc)
        rhs = each(lambda a, e, b, c: jnp.concatenate([a * e, b * c], axis=1).astype(BF16), kb, eg, v, bb)
        wu = each(lambda a, b: jnp.dot(a.astype(BF16), b, preferred_element_type=F32).astype(BF16), tinv, rhs)
        attn = each(lambda a, b, dc: (_nt_dot(a.astype(BF16), b) * dc).astype(BF16), qn, knb, decay)
        awu = each(lambda a, b: jnp.dot(a, b, preferred_element_type=F32), attn, wu)
        qp = each(lambda a, e, b: (a * e - b[:, :hd]).astype(BF16), qn, eg, awu)
        s = [st_scr[h] for h in hs_idx]
        outs = [[] for _ in hs_idx]
        for c in range(nck):
            r0 = c * GDN_CHUNK
            r1 = r0 + GDN_CHUNK
            gl = each(lambda a: a[r1 - 1:r1, :], gc)
            kd = each(lambda a, l, g: (a[r0:r1] * jnp.exp(l - g[r0:r1])).astype(BF16), kn, gl, gc)
            m = each(lambda a, b: _tn_dot(a, b[r0:r1]), kd, wu)
            sb = each(lambda a: a.astype(BF16), s)
            for i, (a, b, y) in enumerate(zip(qp, sb, awu)):
                outs[i].append(jnp.dot(a[r0:r1], b, preferred_element_type=F32) + y[r0:r1, hd:])
            s = each(lambda a, l, mm, b: a * jnp.exp(l) - jnp.dot(mm[:, :hd].astype(BF16), b,
                                                                  preferred_element_type=F32) + mm[:, hd:],
                     s, gl, m, sb)
        for i, h in enumerate(hs_idx):
            st_scr[h] = s[i]
            o = jnp.concatenate(outs[i], axis=0)
            z = z_ref[:, pl.ds(hs[i], hd)]
            o_ref[:, pl.ds(hs[i], hd)] = (o * _rms_scale(o) * nw_ref[...] * _silu(z)).astype(o_ref.dtype)

    def trip(j, carry):
        heads_lockstep([GDN_HEADS_PER_TRIP * j + i for i in range(GDN_HEADS_PER_TRIP)])
        return carry

    lax.fori_loop(0, heads // GDN_HEADS_PER_TRIP, trip, 0)


def _gdn_masks():
    tt = GDN_TILE
    ri = lax.broadcasted_iota(jnp.int32, (tt, tt), 0)
    ci = lax.broadcasted_iota(jnp.int32, (tt, tt), 1)
    same = (ri // GDN_CHUNK) == (ci // GDN_CHUNK)
    causal = same & (ci <= ri)
    tril = causal.astype(F32)
    cmask = jnp.stack([tril, (((ri % 2) == 1) & (ci == ri - 1)).astype(F32)])
    levels = []
    d = 1
    while d < GDN_CHUNK:
        levels.append((((ri // d) % 2) == 1) & ((ci // d) == (ri // d) - 1))
        d *= 2
    return tril, cmask, jnp.stack(levels).astype(BF16)


def _gdn(proj, small, conv_w, head_params, norm_w, masks, cols, batch, seq, heads):
    t = proj.shape[0]
    tt = GDN_TILE
    width = heads * HEAD_DIM
    tiles = seq // tt
    cq, ck, cv, cz = cols
    tril, cmask, lmask = masks

    def cur(col):
        return pl.BlockSpec((tt, width), lambda b, i: (b * tiles + i, col))

    def halo(col):
        return pl.BlockSpec((8, width), lambda b, i: (jnp.maximum((b * tiles + i) * (tt // 8) - 1, 0), col))

    def cw(col):
        return pl.BlockSpec((CONV_K, width), lambda b, i: (0, col))

    kern = functools.partial(_gdn_kernel, heads=heads)
    return pl.pallas_call(
        kern,
        grid=(batch, tiles),
        in_specs=[cur(cq), cur(ck), cur(cv), halo(cq), halo(ck), halo(cv), cur(cz),
                  pl.BlockSpec((tt, 128), lambda b, i: (b * tiles + i, 0)),
                  cw(0), cw(1), cw(2),
                  pl.BlockSpec((8, 128), lambda b, i: (0, 0)),
                  pl.BlockSpec((1, HEAD_DIM), lambda b, i: (0, 0)),
                  pl.BlockSpec((tt, tt), lambda b, i: (0, 0)),
                  pl.BlockSpec(cmask.shape, lambda b, i: (0, 0, 0)),
                  pl.BlockSpec(lmask.shape, lambda b, i: (0, 0, 0))],
        out_specs=pl.BlockSpec((tt, width), lambda b, i: (b * tiles + i, 0)),
        out_shape=jax.ShapeDtypeStruct((t, width), BF16),
        scratch_shapes=[pltpu.VMEM((heads, tt, HEAD_DIM), F32),
                        pltpu.VMEM((heads, tt, HEAD_DIM), F32),
                        pltpu.VMEM((8, tt), F32),
                        pltpu.VMEM((heads, HEAD_DIM, HEAD_DIM), F32),
                        pltpu.VMEM((tt + 8, width), F32),
                        pltpu.VMEM((tt + 8, width), F32),
                        pltpu.VMEM((tt + 8, width), F32)],
        compiler_params=_cparams(("parallel", "arbitrary"), 40),
    )(proj, proj, proj, proj, proj, proj, proj, small, conv_w, conv_w, conv_w, head_params, norm_w,
      tril, cmask, lmask)


def _route(h, wr_hl, rb):
    h_hi = h.astype(BF16)
    h_lo = (h - h_hi.astype(F32)).astype(BF16)
    first = jnp.dot(h_hi, wr_hl[...], preferred_element_type=F32)
    logits = first[:, :128] + first[:, 128:] + jnp.dot(h_lo, wr_hl[:, :128], preferred_element_type=F32)
    logits = logits.T[:N_EXPERTS]
    scores = jax.nn.sigmoid(logits)
    sel = scores + rb
    sel_rows = [sel[e:e + 1, :] for e in range(N_EXPERTS)]
    score_rows = [scores[e:e + 1, :] for e in range(N_EXPERTS)]
    epg = EXPERTS_PER_GROUP
    best_score = None
    best_group = None
    for g in range(N_GROUPS):
        rows = sel_rows[g * epg:(g + 1) * epg]
        gs = None
        for a in range(epg):
            for b in range(a + 1, epg):
                pair = rows[a] + rows[b]
                gs = pair if gs is None else jnp.maximum(gs, pair)
        if g == 0:
            best_score, best_group = gs, jnp.zeros_like(gs)
        else:
            better = gs > best_score
            best_group = jnp.where(better, float(g), best_group)
            best_score = jnp.where(better, gs, best_score)
    in_sel = []
    in_score = []
    for j in range(epg):
        a = sel_rows[j]
        b = score_rows[j]
        for g in range(1, N_GROUPS):
            pick = best_group == float(g)
            a = jnp.where(pick, sel_rows[g * epg + j], a)
            b = jnp.where(pick, score_rows[g * epg + j], b)
        in_sel.append(a)
        in_score.append(b)
    i1, m1, w1 = jnp.zeros_like(in_sel[0]), in_sel[0], in_score[0]
    for j in range(1, epg):
        better = in_sel[j] > m1
        i1 = jnp.where(better, float(j), i1)
        m1 = jnp.where(better, in_sel[j], m1)
        w1 = jnp.where(better, in_score[j], w1)
    i2 = m2 = w2 = None
    for j in range(epg):
        cand = jnp.where(i1 == float(j), -jnp.inf, in_sel[j])
        if j == 0:
            i2, m2, w2 = jnp.zeros_like(cand), cand, in_score[0]
        else:
            better = cand > m2
            i2 = jnp.where(better, float(j), i2)
            m2 = jnp.where(better, cand, m2)
            w2 = jnp.where(better, in_score[j], w2)
    total = w1 + w2
    zero = jnp.zeros_like(w1)
    return jnp.concatenate([best_group * epg + i1, best_group * epg + i2, w1 / total, w2 / total,
                            zero, zero, zero, zero], axis=0)


def _merge_kernel(yl_ref, u_ref, sd_ref, wglu_ref, bglu_ref, yb_ref, ga_ref, gb_ref, x_ref, mod_ref, wa_ref, wb_ref,
                  wo_ref, nw_ref, wrt_ref, rb_ref, o_ref, h_ref, r_ref):
    y = jax.nn.gelu(yl_ref[...] + sd_ref[...] * u_ref[...])
    glu = jnp.dot(y.astype(BF16), wglu_ref[...], preferred_element_type=F32) + bglu_ref[...]
    ya = (y * jax.nn.sigmoid(glu)).astype(BF16)
    pa = jnp.dot(ya, wa_ref[...], preferred_element_type=F32)
    pb = jnp.dot(yb_ref[...], wb_ref[...], preferred_element_type=F32)
    merged = jax.nn.sigmoid(ga_ref[...]) * pa + jax.nn.sigmoid(gb_ref[...]) * pb
    out = jnp.dot(merged.astype(BF16), wo_ref[...], preferred_element_type=F32)
    xn = x_ref[...] + mod_ref[2:3, :] * out
    o_ref[...] = xn
    h = xn * _rms_scale(xn) * nw_ref[...] * (1.0 + mod_ref[4:5, :]) + mod_ref[3:4, :]
    h_ref[...] = _pack_halves(h)
    r_ref[...] = _route(h, wrt_ref, rb_ref[...])


def _merge(y_lin, yb, proj, x2, modl, s5_tail, wa, wb, wo, nw2, wr_t, rb, seq, u_col, ga_col, gb_col):
    t, d = x2.shape
    w = y_lin.shape[1]
    d_skip, w_glu, b_glu = s5_tail
    tm = min(seq, 256)
    per_batch = seq // tm
    const = dict(pipeline_mode=pl.Buffered(1))
    return pl.pallas_call(
        _merge_kernel,
        grid=(t // tm,),
        in_specs=[pl.BlockSpec((tm, w), lambda i: (i, 0)),
                  pl.BlockSpec((tm, w), lambda i: (i, u_col)),
                  pl.BlockSpec((1, w), lambda i: (0, 0)),
                  pl.BlockSpec((w, w), lambda i: (0, 0), **const),
                  pl.BlockSpec((1, w), lambda i: (0, 0)),
                  pl.BlockSpec((tm, w), lambda i: (i, 0)),
                  pl.BlockSpec((tm, d), lambda i: (i, ga_col)),
                  pl.BlockSpec((tm, d), lambda i: (i, gb_col)),
                  pl.BlockSpec((tm, d), lambda i: (i, 0)),
                  pl.BlockSpec((None, 6, d), lambda i: (i // per_batch, 0, 0)),
                  pl.BlockSpec((w, d), lambda i: (0, 0), **const),
                  pl.BlockSpec((w, d), lambda i: (0, 0), **const),
                  pl.BlockSpec((d, d), lambda i: (0, 0), **const),
                  pl.BlockSpec((1, d), lambda i: (0, 0)),
                  pl.BlockSpec((d, 256), lambda i: (0, 0)),
                  pl.BlockSpec((N_EXPERTS, 1), lambda i: (0, 0))],
        out_specs=[pl.BlockSpec((tm, d), lambda i: (i, 0)),
                   pl.BlockSpec((tm, d // 2), lambda i: (i, 0)),
                   pl.BlockSpec((8, tm), lambda i: (0, i))],
        out_shape=[jax.ShapeDtypeStruct((t, d), F32), jax.ShapeDtypeStruct((t, d // 2), jnp.uint32),
                   jax.ShapeDtypeStruct((8, t), F32)],
        compiler_params=_cparams(("parallel",), 52),
    )(y_lin, proj, d_skip, w_glu, b_glu, yb, proj, proj, x2, modl, wa, wb, wo, nw2, wr_t, rb)


def _moe_kernel(te_ref, nu_ref, x_ref, wg_ref, wu_ref, wd_ref, o_ref):
    used = pl.program_id(0) < nu_ref[0]

    @pl.when(used)
    def _():
        x_lo, x_hi = _unpack_halves(x_ref[...])
        x_lo = x_lo.astype(BF16)
        x_hi = x_hi.astype(BF16)
        half = x_lo.shape[1]

        def up(w_ref):
            return (jnp.dot(x_lo, w_ref[:half, :], preferred_element_type=F32)
                    + jnp.dot(x_hi, w_ref[half:, :], preferred_element_type=F32))

        hid = (_silu(up(wg_ref)) * up(wu_ref)).astype(BF16)
        o_ref[...] = _pack_halves(jnp.dot(hid, wd_ref[...], preferred_element_type=F32))

    @pl.when(jnp.logical_not(used))
    def _():
        o_ref[...] = jnp.zeros_like(o_ref)


def _cast_kernel(w_ref, after_ref, o_ref, done_ref):
    del after_ref
    o_ref[...] = w_ref[...].astype(o_ref.dtype)
    done_ref[...] = jnp.zeros_like(done_ref)


def _expert_weights_bf16(w, layer, after):
    _, e, a, b = w.shape
    ta = min(a, (1 << 20) // b)
    return pl.pallas_call(
        _cast_kernel,
        grid=(e, a // ta),
        in_specs=[pl.BlockSpec((None, None, ta, b), lambda i, j: (layer, i, j, 0)),
                  pl.BlockSpec((8, 128), lambda i, j: (0, 0))],
        out_specs=[pl.BlockSpec((None, ta, b), lambda i, j: (i, j, 0)),
                   pl.BlockSpec((8, 128), lambda i, j: (0, 0))],
        out_shape=[jax.ShapeDtypeStruct((e, a, b), BF16), jax.ShapeDtypeStruct((8, 128), F32)],
        compiler_params=_cparams(("arbitrary", "arbitrary"), 32),
    )(w, after)


def _moe_experts(tile_expert, n_used, xs, wg, wu, wd, tm):
    p, dh = xs.shape
    d = 2 * dh
    f = wg.shape[-1]
    grid_spec = pltpu.PrefetchScalarGridSpec(
        num_scalar_prefetch=2,
        grid=(p // tm,),
        in_specs=[pl.BlockSpec((tm, dh), lambda i, te, nu: (i, 0)),
                  pl.BlockSpec((None, d, f), lambda i, te, nu: (te[i], 0, 0)),
                  pl.BlockSpec((None, d, f), lambda i, te, nu: (te[i], 0, 0)),
                  pl.BlockSpec((None, f, d), lambda i, te, nu: (te[i], 0, 0))],
        out_specs=pl.BlockSpec((tm, dh), lambda i, te, nu: (i, 0)),
    )
    return pl.pallas_call(
        _moe_kernel,
        grid_spec=grid_spec,
        out_shape=jax.ShapeDtypeStruct((p, dh), jnp.uint32),
        compiler_params=_cparams(("arbitrary",), 52),
    )(tile_expert, n_used, xs, wg, wu, wd)


def _dispatch_plan(expert_idx, tm, n_tiles):
    flat_e = expert_idx.reshape(-1)
    onehot = (flat_e[:, None] == jnp.arange(N_EXPERTS, dtype=jnp.int32)[None, :]).astype(jnp.int32)
    csum = jnp.cumsum(onehot, axis=0)
    rank = jnp.sum((csum - onehot) * onehot, axis=1)
    sizes = csum[-1]
    padded = ((sizes + tm - 1) // tm) * tm
    pad_end = jnp.cumsum(padded)
    pad_start = pad_end - padded
    dest = pad_start[flat_e] + rank
    tile_start = jnp.arange(n_tiles, dtype=jnp.int32) * tm
    tile_expert = jnp.minimum(jnp.sum((tile_start[:, None] >= pad_end[None, :]).astype(jnp.int32), axis=1),
                              N_EXPERTS - 1).astype(jnp.int32)
    n_used = (pad_end[-1] // tm).astype(jnp.int32).reshape(1)
    dest = dest.astype(jnp.int32)
    n_slots = flat_e.shape[0]
    _, by_slot = lax.sort_key_val(dest, jnp.arange(n_slots, dtype=jnp.int32) // TOP_K)
    start = jnp.cumsum(sizes) - sizes
    tile_off = tile_start - pad_start[tile_expert]
    tile_valid = jnp.clip(sizes[tile_expert] - tile_off, 0, tm)
    lane = jnp.arange(tm, dtype=jnp.int32)[None, :]
    pick = jnp.clip((start[tile_expert] + tile_off)[:, None] + lane, 0, n_slots - 1)
    filler = (tile_start[:, None] + lane) % (n_slots // TOP_K)
    src_tok = jnp.where(lane < tile_valid[:, None], jnp.take(by_slot, pick.reshape(-1)).reshape(n_tiles, tm), filler)
    return dest, src_tok.reshape(-1).astype(jnp.int32), tile_expert, n_used


def _combine_kernel(x_ref, y0_ref, y1_ref, gw_ref, mod_ref, fw_ref, *rest, final_norm):
    o_ref = rest[-1]
    gw = gw_ref[...]
    y0_lo, y0_hi = _unpack_halves(y0_ref[...])
    y1_lo, y1_hi = _unpack_halves(y1_ref[...])
    moe = jnp.concatenate([gw[:, 0:1] * y0_lo + gw[:, 1:2] * y1_lo, gw[:, 0:1] * y0_hi + gw[:, 1:2] * y1_hi], axis=1)
    xn = x_ref[...] + mod_ref[5:6, :] * moe
    if final_norm:
        xn = xn * _rms_scale(xn) * fw_ref[...]
    o_ref[...] = xn


def _combine(x2, y_pairs, gate_w, modl, final_w, seq, final_norm, ties=()):
    t, d = x2.shape
    tm = min(seq, 512)
    per_batch = seq // tm
    nblk = t // tm
    return pl.pallas_call(
        functools.partial(_combine_kernel, final_norm=final_norm),
        grid=(nblk,),
        in_specs=[pl.BlockSpec((tm, d), lambda i: (i, 0)),
                  pl.BlockSpec((tm, d // 2), lambda i: (i, 0)),
                  pl.BlockSpec((tm, d // 2), lambda i: (i + nblk, 0)),
                  pl.BlockSpec((tm, TOP_K), lambda i: (i, 0)),
                  pl.BlockSpec((None, 6, d), lambda i: (i // per_batch, 0, 0)),
                  pl.BlockSpec((1, d), lambda i: (0, 0))] + [pl.BlockSpec((8, 128), lambda i: (0, 0)) for _ in ties],
        out_specs=pl.BlockSpec((tm, d), lambda i: (i, 0)),
        out_shape=jax.ShapeDtypeStruct((t, d), F32),
        compiler_params=_cparams(("parallel",), 40),
    )(x2, y_pairs, y_pairs, gate_w, modl, final_w, *ties)


def kernel(x, c, w_ada, b_ada, norm1_w, norm2_w, w_in, s5_a_re, s5_a_im, s5_log_dt, s5_b_re, s5_b_im, s5_c_re, s5_c_im, s5_d, s5_w_glu, s5_b_glu, gdn_conv_w, gdn_a_log, gdn_dt_bias, gdn_norm_w, w_proj_a, w_proj_b, w_out, w_router, router_bias, w_gate, w_up, w_down, final_norm_w):
    batch, seq, d = x.shape
    depth = w_ada.shape[0]
    t = batch * seq
    s5w = s5_d.shape[1]
    gdw = w_proj_b.shape[1]
    heads = gdn_a_log.shape[1]
    assert gdw == heads * HEAD_DIM and s5w == gdw and d == 2 * gdw
    assert seq % GDN_TILE == 0 and seq % S5_CHUNK == 0
    n_levels = int(math.log2(seq // S5_CHUNK))
    assert S5_CHUNK << n_levels == seq and S5_OCT * S5_GROUP == 128

    c_pad = jnp.zeros((8, d), F32).at[:batch].set(c)
    mod = _ada(c_pad, w_ada, b_ada)

    o_u, o_q, o_k, o_v, o_z = 0, s5w, 2 * s5w, 3 * s5w, 4 * s5w
    o_ba = 5 * s5w
    o_ga = o_ba + 2 * heads
    o_gb = o_ga + d
    col_u, col_q, col_k, col_v, col_z = 4, 5, 6, 7, 8

    gdn_masks = _gdn_masks()
    s5_rep = _s5_replicate()
    s5_ops = jax.vmap(functools.partial(_s5_weights, n_levels=n_levels))(
        s5_a_re, s5_a_im, s5_log_dt, s5_b_re, s5_b_im, s5_c_re, s5_c_im)

    wr_hi = w_router.astype(BF16)
    wr_lo = (w_router - wr_hi.astype(F32)).astype(BF16)
    pad_e = ((0, 0), (0, 128 - N_EXPERTS))
    wr_t = jnp.concatenate([jnp.pad(wr_hi, pad_e), jnp.pad(wr_lo, pad_e)], axis=1)
    rb = router_bias.reshape(N_EXPERTS, 1).astype(F32)

    tm_e = min(512, t)
    n_tiles = (TOP_K * t) // tm_e + N_EXPERTS

    w_in_t = jnp.transpose(w_in, (0, 2, 1))
    w_big = _repack_in_proj(w_in_t, o_ga, 2 * d, o_ba)
    w_small = jnp.pad(w_in_t[:, o_ba:o_ga], ((0, 0), (0, 128 - 2 * heads), (0, 0))).astype(BF16)

    x2 = x.reshape(t, d)
    for l in range(depth):
        modl = mod[l, :batch].reshape(batch, 6, d)
        proj, small = _inproj(x2, modl, norm1_w[l].reshape(1, d), w_big, w_small, l, seq)

        y_lin = _s5_core(proj, col_u * s5w // 128, *s5_ops, s5_rep, l, batch, seq)
        s5_tail = (s5_d[l].reshape(1, s5w), s5_w_glu[l].astype(BF16), s5_b_glu[l].reshape(1, s5w))

        head_params = jnp.zeros((8, 128), F32)
        head_params = head_params.at[0, heads:2 * heads].set(-jnp.exp(gdn_a_log[l]))
        head_params = head_params.at[1, heads:2 * heads].set(gdn_dt_bias[l])
        y_b = _gdn(proj, small, gdn_conv_w[l], head_params, gdn_norm_w[l].reshape(1, HEAD_DIM), gdn_masks,
                   (col_q, col_k, col_v, col_z), batch, seq, heads)

        x2, h2, route = _merge(y_lin, y_b, proj, x2, modl, s5_tail, w_proj_a[l].astype(BF16),
                               w_proj_b[l].astype(BF16), w_out[l].astype(BF16), norm2_w[l].reshape(1, d), wr_t, rb,
                               seq, col_u, 0, 1)

        expert_idx = route[0:TOP_K].T.astype(jnp.int32)
        gate_w = route[TOP_K:2 * TOP_K].T
        dest, src_tok, tile_expert, n_used = _dispatch_plan(expert_idx, tm_e, n_tiles)
        xs = jnp.take(h2, src_tok, axis=0, mode='clip')
        if l == 0:
            (wg_l, _), (wu_l, _) = _expert_weights_bf16(w_gate, 0, h2), _expert_weights_bf16(w_up, 0, h2)
        wd_l, _ = _expert_weights_bf16(w_down, l, h2)
        ys = _moe_experts(tile_expert, n_used, xs, wg_l, wu_l, wd_l, tm_e)
        y_pairs = jnp.take(ys, dest.reshape(t, TOP_K).T.reshape(-1), axis=0, mode='clip')
        ties = ()
        if l + 1 < depth:
            (wg_l, tie_g), (wu_l, tie_u) = _expert_weights_bf16(w_gate, l + 1, ys), _expert_weights_bf16(w_up, l + 1, ys)
            ties = (tie_g, tie_u)
        x2 = _combine(x2, y_pairs, gate_w, modl, final_norm_w.reshape(1, d), seq, l == depth - 1, ties)

    return x2.reshape(batch, seq, d)
```

```python
import functools
import math

import jax
import jax.numpy as jnp
from jax import lax
from jax.experimental import pallas as pl
from jax.experimental.pallas import tpu as pltpu

F32 = jnp.float32
BF16 = jnp.bfloat16
HIGHEST = lax.Precision.HIGHEST

RMS_EPS = 1e-6
S5_GROUP = 16
S5_STATE = 64
S5_CHUNK = 16
S5_OCT = 8
HEAD_DIM = 128
GDN_CHUNK = 64
GDN_TILE = 256
GDN_HEADS_PER_TRIP = 8
CONV_K = 4
N_EXPERTS = 16
N_GROUPS = 4
EXPERTS_PER_GROUP = N_EXPERTS // N_GROUPS
TOP_K = 2
MIB = 1024 * 1024


def _cparams(semantics, vmem_mib):
    return pltpu.CompilerParams(dimension_semantics=semantics, vmem_limit_bytes=vmem_mib * MIB)


def _silu(x):
    return x * jax.nn.sigmoid(x)


def _softplus(x):
    return jnp.maximum(x, 0.0) + jnp.log(1.0 + jnp.exp(-jnp.abs(x)))


def _rms_scale(x):
    return lax.rsqrt(jnp.mean(x * x, axis=-1, keepdims=True) + RMS_EPS)


def _pack_halves(x):
    n = x.shape[1] // 2
    lo = lax.bitcast_convert_type(x[:, :n].astype(BF16).astype(F32), jnp.uint32)
    hi = lax.bitcast_convert_type(x[:, n:].astype(BF16).astype(F32), jnp.uint32)
    return (lo >> 16) | hi


def _unpack_halves(p):
    lo = lax.bitcast_convert_type(p << 16, F32)
    hi = lax.bitcast_convert_type(p & jnp.uint32(0xFFFF0000), F32)
    return lo, hi


def _nt_dot(a, b):
    return lax.dot_general(a, b, (((1,), (1,)), ((), ())), preferred_element_type=F32)


def _tn_dot(a, b):
    return lax.dot_general(a, b, (((0,), (0,)), ((), ())), preferred_element_type=F32)


def _ada_kernel(c_ref, w_ref, b_ref, o_ref):
    ca = _silu(c_ref[...]).astype(BF16)
    o_ref[...] = jnp.dot(ca, w_ref[...].astype(BF16), preferred_element_type=F32) + b_ref[...]


def _ada(c_pad, w_ada, b_ada):
    depth, d, n = w_ada.shape
    rows = c_pad.shape[0]
    tn = min(n, 1024)
    return pl.pallas_call(
        _ada_kernel,
        grid=(depth, n // tn),
        in_specs=[pl.BlockSpec((rows, d), lambda l, j: (0, 0)),
                  pl.BlockSpec((None, d, tn), lambda l, j: (l, 0, j)),
                  pl.BlockSpec((None, 1, tn), lambda l, j: (l, 0, j))],
        out_specs=pl.BlockSpec((None, rows, tn), lambda l, j: (l, 0, j)),
        out_shape=jax.ShapeDtypeStruct((depth, rows, n), F32),
        compiler_params=_cparams(("parallel", "parallel"), 40),
    )(c_pad, w_ada, b_ada.reshape(depth, 1, n))


def _inproj_kernel(x_ref, mod_ref, nw_ref, w_ref, wsm_ref, o_ref, osm_ref, h_scr):
    @pl.when(pl.program_id(1) == 0)
    def _():
        x = x_ref[...]
        h = x * _rms_scale(x) * nw_ref[...] * (1.0 + mod_ref[1:2, :]) + mod_ref[0:1, :]
        hb = h.astype(BF16)
        h_scr[...] = hb
        osm_ref[...] = _nt_dot(hb, wsm_ref[...])

    o_ref[...] = _nt_dot(h_scr[...], w_ref[...])


def _inproj(x2, modl, nw, w_big, w_small, layer, seq):
    t, d = x2.shape
    n = w_big.shape[1]
    tm = min(seq, 1024)
    tn = 1536
    per_batch = seq // tm
    return pl.pallas_call(
        _inproj_kernel,
        grid=(t // tm, n // tn),
        in_specs=[pl.BlockSpec((tm, d), lambda i, j: (i, 0)),
                  pl.BlockSpec((None, 6, d), lambda i, j: (i // per_batch, 0, 0)),
                  pl.BlockSpec((1, d), lambda i, j: (0, 0)),
                  pl.BlockSpec((None, tn, d), lambda i, j: (layer, j, 0)),
                  pl.BlockSpec((None, 128, d), lambda i, j: (layer, 0, 0))],
        out_specs=[pl.BlockSpec((tm, tn), lambda i, j: (i, j)),
                   pl.BlockSpec((tm, 128), lambda i, j: (i, 0))],
        out_shape=[jax.ShapeDtypeStruct((t, n), F32), jax.ShapeDtypeStruct((t, 128), F32)],
        scratch_shapes=[pltpu.VMEM((tm, d), BF16)],
        compiler_params=_cparams(("parallel", "arbitrary"), 52),
    )(x2, modl, nw, w_big, w_small)


def _repack_kernel(a_ref, b_ref, o_ref, *, gate_blocks, shift):
    c = pl.program_id(1)

    @pl.when(c < gate_blocks)
    def _():
        x = jnp.concatenate([a_ref[...], b_ref[...]], axis=0)
        o_ref[...] = x[shift:shift + o_ref.shape[0]].astype(o_ref.dtype)

    @pl.when(c >= gate_blocks)
    def _():
        o_ref[...] = a_ref[...].astype(o_ref.dtype)


def _repack_in_proj(w_in_t, gate_start, gate_cols, lead_cols):
    depth, _, d = w_in_t.shape
    rb = 512
    shift = gate_start % rb
    base = gate_start - shift
    assert shift % 16 == 0 and gate_cols % rb == 0 and lead_cols % rb == 0 and lead_cols == base
    gate_blocks = gate_cols // rb
    kern = functools.partial(_repack_kernel, gate_blocks=gate_blocks, shift=shift)
    return pl.pallas_call(
        kern,
        grid=(depth, (gate_cols + lead_cols) // rb),
        in_specs=[pl.BlockSpec((None, rb, d),
                               lambda l, c: (l, jnp.where(c < gate_blocks, base // rb + c, c - gate_blocks), 0)),
                  pl.BlockSpec((None, shift, d),
                               lambda l, c: (l, jnp.where(c < gate_blocks, (base + (c + 1) * rb) // shift, 0), 0))],
        out_specs=pl.BlockSpec((None, rb, d), lambda l, c: (l, c, 0)),
        out_shape=jax.ShapeDtypeStruct((depth, gate_cols + lead_cols, d), BF16),
        compiler_params=_cparams(("parallel", "arbitrary"), 32),
    )(w_in_t, w_in_t)


def _s5_weights(a_re, a_im, log_dt, b_re, b_im, c_re, c_im, n_levels):
    tc = S5_CHUNK
    groups, p = a_re.shape
    n = b_re.shape[-1]
    oc = S5_OCT
    octs = groups // oc
    lam = lax.complex(a_re, a_im)
    dt = jnp.exp(log_dt)[:, None]
    log_a = lam * dt
    a_bar = jnp.exp(log_a)
    b_bar = ((a_bar - 1.0) / lam)[..., None] * lax.complex(b_re, b_im)
    c_mat = lax.complex(c_re, c_im)
    apow = jnp.exp(log_a[:, :, None] * jnp.arange(tc + 1, dtype=F32)[None, None, :])
    lane = jnp.arange(tc * n)
    e_t = (jnp.arange(tc)[:, None] == (lane // n)[None, :]).astype(F32)
    e_n = (jnp.arange(n)[:, None] == (lane % n)[None, :]).astype(F32)

    def expand(z, e):
        f = lambda r: jnp.einsum('gpk,kj->gpj', r, e, precision=HIGHEST)
        return lax.complex(f(jnp.real(z)), f(jnp.imag(z)))

    def re_negim_rows(z):
        return jnp.concatenate([jnp.real(z), -jnp.imag(z)], axis=1)

    c_rep = expand(jnp.transpose(c_mat, (0, 2, 1)), e_n)
    lag_c = re_negim_rows(expand(apow[:, :, :tc], e_t) * c_rep)
    hoc = re_negim_rows(expand(apow[:, :, 1:], e_t) * c_rep)
    b_t = jnp.transpose(b_bar, (0, 2, 1))
    b_cat = jnp.concatenate([jnp.real(b_t), jnp.imag(b_t)], axis=-1)
    kern = jnp.einsum('gnk,gkj->gnj', b_cat, lag_c, precision=HIGHEST)
    tcomp = jnp.stack([jnp.pad(kern[:, :, :(tc - s) * n], ((0, 0), (0, 0), (s * n, 0))) for s in range(tc)],
                      axis=1)
    tcomp = jnp.transpose(tcomp.reshape(octs, oc, tc, n, tc * n), (0, 2, 1, 3, 4)).reshape(octs, tc * oc * n, tc * n)
    ap_rev = jnp.transpose(apow[:, :, tc - 1 - jnp.arange(tc)], (0, 2, 1))
    hin_c = ap_rev[:, :, None, :] * b_t[:, None, :, :]
    hc = jnp.concatenate([jnp.real(hin_c), jnp.imag(hin_c)], axis=-1)
    hc = jnp.transpose(hc.reshape(octs, oc, tc, n, 2 * p), (0, 2, 1, 3, 4)).reshape(octs, tc * oc * n, 2 * p)
    hoc = hoc.reshape(octs, oc * 2 * p, tc * n)
    steps = (tc * (2.0 ** jnp.arange(n_levels, dtype=F32)))
    amul = jnp.exp(log_a[:, None, :] * steps[None, :, None])
    a1 = jnp.concatenate([jnp.real(amul), jnp.real(amul)], axis=-1)
    a2 = jnp.concatenate([-jnp.imag(amul), jnp.imag(amul)], axis=-1)

    def lanes_by_group(a):
        return jnp.transpose(a.reshape(octs, oc, n_levels, 2 * p), (0, 2, 1, 3)).reshape(octs, n_levels, oc * 2 * p)

    return (tcomp.astype(BF16), hc.astype(BF16), hoc.astype(BF16),
            lanes_by_group(a1).astype(F32), lanes_by_group(a2).astype(F32))


def _s5_replicate():
    r = lax.broadcasted_iota(jnp.int32, (S5_CHUNK * S5_GROUP, S5_CHUNK * 128), 0)
    c = lax.broadcasted_iota(jnp.int32, (S5_CHUNK * S5_GROUP, S5_CHUNK * 128), 1)
    return ((r // S5_GROUP == c // 128) & (r % S5_GROUP == c % S5_GROUP)).astype(BF16)


def _s5_core_kernel(u_ref, tc_ref, hc_ref, hoc_ref, a1_ref, a2_ref, rep_ref, y_ref, toep_scr, hin_scr, hout_scr,
                    *, n_levels):
    tc = S5_CHUNK
    lanes = u_ref.shape[1]
    state_lanes = hc_ref.shape[1]

    @pl.when(pl.program_id(1) == 0)
    def _():
        kd = toep_scr.shape[1]
        rb = 256
        col_g = (lax.broadcasted_iota(jnp.int32, (rb, kd), 1) // S5_GROUP) % S5_OCT
        row_i = lax.broadcasted_iota(jnp.int32, (rb, kd), 0)
        for i in range(toep_scr.shape[0] // rb):
            full = jnp.dot(tc_ref[i * rb:(i + 1) * rb, :], rep_ref[...], preferred_element_type=F32)
            row_g = ((row_i + i * rb) // S5_GROUP) % S5_OCT
            toep_scr[i * rb:(i + 1) * rb, :] = jnp.where(row_g == col_g, full, 0.0).astype(BF16)
        for i in range(hout_scr.shape[0] // rb):
            full = jnp.dot(hoc_ref[i * rb:(i + 1) * rb, :], rep_ref[...], preferred_element_type=F32)
            row_g = (row_i + i * rb) // state_lanes
            hout_scr[i * rb:(i + 1) * rb, :] = jnp.where(row_g == col_g, full, 0.0).astype(BF16)
        hrow_g = (lax.broadcasted_iota(jnp.int32, hc_ref.shape, 0) // S5_GROUP) % S5_OCT
        for g in range(S5_OCT):
            hin_scr[:, g * state_lanes:(g + 1) * state_lanes] = jnp.where(hrow_g == g, hc_ref[...], 0.0).astype(BF16)

    rows = u_ref.shape[0] // tc
    x = jnp.concatenate([u_ref[pl.ds(t, rows, stride=tc), :] for t in range(tc)], axis=1).astype(BF16)
    s_loc = jnp.dot(x, hin_scr[...], preferred_element_type=F32)
    pos = lax.broadcasted_iota(jnp.int32, (rows, state_lanes), 0)
    s_in = []
    for g in range(S5_OCT):
        ln = slice(g * state_lanes, (g + 1) * state_lanes)
        s = s_loc[:, ln]
        for k in range(n_levels):
            d = 1 << k
            prev = jnp.where(pos >= d, pltpu.roll(s, d, axis=0), 0.0)
            s = s + a1_ref[k:k + 1, ln] * prev + a2_ref[k:k + 1, ln] * pltpu.roll(prev, state_lanes // 2, axis=1)
        s_in.append(jnp.where(pos >= 1, pltpu.roll(s, 1, axis=0), 0.0).astype(BF16))
    y = (jnp.dot(x, toep_scr[...], preferred_element_type=F32)
         + jnp.dot(jnp.concatenate(s_in, axis=1), hout_scr[...], preferred_element_type=F32))
    for t in range(tc):
        y_ref[pl.ds(t, rows, stride=tc), :] = y[:, t * lanes:(t + 1) * lanes]


def _s5_core(proj, u_lane_block, tcomp, hc, hoc, a1, a2, rep, layer, batch, seq):
    t = proj.shape[0]
    _, octs, kdim, tn = tcomp.shape
    sl = hc.shape[-1]
    sw = hoc.shape[2]
    n_levels = a1.shape[2]
    lanes = kdim // S5_CHUNK
    kern = functools.partial(_s5_core_kernel, n_levels=n_levels)
    return pl.pallas_call(
        kern,
        grid=(octs, batch),
        in_specs=[pl.BlockSpec((seq, lanes), lambda o, b: (b, u_lane_block + o)),
                  pl.BlockSpec((None, None, kdim, tn), lambda o, b: (layer, o, 0, 0)),
                  pl.BlockSpec((None, None, kdim, sl), lambda o, b: (layer, o, 0, 0)),
                  pl.BlockSpec((None, None, sw, tn), lambda o, b: (layer, o, 0, 0)),
                  pl.BlockSpec((None, None, n_levels, sw), lambda o, b: (layer, o, 0, 0)),
                  pl.BlockSpec((None, None, n_levels, sw), lambda o, b: (layer, o, 0, 0)),
                  pl.BlockSpec((tn, kdim), lambda o, b: (0, 0))],
        out_specs=pl.BlockSpec((seq, lanes), lambda o, b: (b, o)),
        out_shape=jax.ShapeDtypeStruct((t, octs * lanes), F32),
        scratch_shapes=[pltpu.VMEM((kdim, kdim), BF16), pltpu.VMEM((kdim, sw), BF16), pltpu.VMEM((sw, kdim), BF16)],
        compiler_params=_cparams(("parallel", "arbitrary"), 48),
    )(proj, tcomp, hc, hoc, a1, a2, rep)


def _gdn_kernel(q_ref, k_ref, v_ref, qh_ref, kh_ref, vh_ref, z_ref, sm_ref, cq_ref, ck_ref, cv_ref,
                hp_ref, nw_ref, tril_ref, cmask_ref, lmask_ref, o_ref, betab, gcb, grow, st_scr,
                cat_q, cat_k, cat_v, *, heads):
    tt = GDN_TILE
    hd = HEAD_DIM
    nck = tt // GDN_CHUNK
    first = pl.program_id(1) == 0

    @pl.when(first)
    def _():
        st_scr[...] = jnp.zeros_like(st_scr)

    sm = sm_ref[...]
    hp = hp_ref[...]
    beta_all = jax.nn.sigmoid(sm)
    g_all = hp[0:1, :] * _softplus(sm + hp[1:2, :])
    gc_all = jnp.dot(tril_ref[...], g_all, precision=HIGHEST, preferred_element_type=F32)
    gc_t = gc_all.T
    for h in range(heads):
        betab[h] = jnp.broadcast_to(beta_all[:, h:h + 1], (tt, hd))
        gcb[h] = jnp.broadcast_to(gc_all[:, heads + h:heads + h + 1], (tt, hd))
        grow[h:h + 1, :] = gc_t[heads + h:heads + h + 1, :]

    eye = (lax.broadcasted_iota(jnp.int32, (tt, tt), 0) == lax.broadcasted_iota(jnp.int32, (tt, tt), 1)).astype(F32)
    n_lv = lmask_ref.shape[0]

    for x_ref, halo_ref, cat in ((q_ref, qh_ref, cat_q), (k_ref, kh_ref, cat_k), (v_ref, vh_ref, cat_v)):
        cat[0:8, :] = jnp.where(first, 0.0, halo_ref[...])
        cat[8:8 + tt, :] = x_ref[...]

    def conv_silu(cat, cw_ref, hs):
        w = cw_ref[:, pl.ds(hs, hd)]
        acc = cat[pl.ds(8, tt), pl.ds(hs, hd)] * w[CONV_K - 1:CONV_K, :]
        for s in range(1, CONV_K):
            acc = acc + cat[pl.ds(8 - s, tt), pl.ds(hs, hd)] * w[CONV_K - 1 - s:CONV_K - s, :]
        return _silu(acc)

    def heads_lockstep(hs_idx):
        each = lambda f, *ls: [f(*a) for a in zip(*ls)]
        hs = [pl.multiple_of(h * hd, hd) for h in hs_idx]
        q = [conv_silu(cat_q, cq_ref, o) for o in hs]
        k = [conv_silu(cat_k, ck_ref, o) for o in hs]
        v = [conv_silu(cat_v, cv_ref, o) for o in hs]
        qn = each(lambda a: a * (lax.rsqrt(jnp.sum(a * a, axis=-1, keepdims=True) + RMS_EPS) * (hd ** -0.5)), q)
        kn = each(lambda a: a * lax.rsqrt(jnp.sum(a * a, axis=-1, keepdims=True) + RMS_EPS), k)
        bb = [betab[h] for h in hs_idx]
        gc = [gcb[h] for h in hs_idx]
        gr = [grow[pl.ds(h, 1), :] for h in hs_idx]
        decay = each(lambda c, r: jnp.exp(jnp.where(cmask_ref[0] > 0.0, jnp.concatenate([c, c], axis=1) - r, -1e30)),
                     gc, gr)
        kb = each(lambda a, b: a * b, kn, bb)
        knb = each(lambda a: a.astype(BF16), kn)
        lfull = each(lambda a, b, dc: _nt_dot(a.astype(BF16), b) * dc, kb, knb, decay)
        lmat = each(lambda a: a.astype(BF16), lfull)
        tinv = each(lambda a: eye - a * cmask_ref[1], lfull)
        for lv in range(1, n_lv):
            d = 1 << lv
            nb = tt // d
            tb = each(lambda a: a.astype(BF16), tinv)
            bd = each(lambda m: m * lmask_ref[lv], lmat)
            if d % 8 == 0:
                odd = each(lambda a: jnp.concatenate([a[i * d:(i + 1) * d] for i in range(1, nb, 2)], axis=0), tinv)
                tl = each(lambda a, b: jnp.dot(a.astype(BF16), b, preferred_element_type=F32).astype(BF16), odd, bd)
                upd = each(lambda a, b, c: a - jnp.dot(b, c, preferred_element_type=F32), odd, tl, tb)
                tinv = each(lambda a, u: jnp.concatenate(
                    [a[i * d:(i + 1) * d] if i % 2 == 0 else u[(i // 2) * d:(i // 2 + 1) * d] for i in range(nb)],
                    axis=0), tinv, upd)
            else:
                tl = each(lambda a, b: jnp.dot(a, b, preferred_element_type=F32).astype(BF16), tb, bd)
                tinv = each(lambda a, b, c: a - jnp.dot(b, c, preferred_element_type=F32), tinv, tl, tb)
        eg = each(jnp.exp, gc)
        rhs = each(lambda a, e, b, c: jnp.concatenate([a * e, b * c], axis=1).astype(BF16), kb, eg, v, bb)
        wu = each(lambda a, b: jnp.dot(a.astype(BF16), b, preferred_element_type=F32).astype(BF16), tinv, rhs)
        attn = each(lambda a, b, dc: (_nt_dot(a.astype(BF16), b) * dc).astype(BF16), qn, knb, decay)
        awu = each(lambda a, b: jnp.dot(a, b, preferred_element_type=F32), attn, wu)
        qp = each(lambda a, e, b: (a * e - b[:, :hd]).astype(BF16), qn, eg, awu)
        s = [st_scr[h] for h in hs_idx]
        outs = [[] for _ in hs_idx]
        for c in range(nck):
            r0 = c * GDN_CHUNK
            r1 = r0 + GDN_CHUNK
            gl = each(lambda a: a[r1 - 1:r1, :], gc)
            kd = each(lambda a, l, g: (a[r0:r1] * jnp.exp(l - g[r0:r1])).astype(BF16), kn, gl, gc)
            m = each(lambda a, b: _tn_dot(a, b[r0:r1]), kd, wu)
            sb = each(lambda a: a.astype(BF16), s)
            for i, (a, b, y) in enumerate(zip(qp, sb, awu)):
                outs[i].append(jnp.dot(a[r0:r1], b, preferred_element_type=F32) + y[r0:r1, hd:])
            s = each(lambda a, l, mm, b: a * jnp.exp(l) - jnp.dot(mm[:, :hd].astype(BF16), b,
                                                                  preferred_element_type=F32) + mm[:, hd:],
                     s, gl, m, sb)
        for i, h in enumerate(hs_idx):
            st_scr[h] = s[i]
            o = jnp.concatenate(outs[i], axis=0)
            z = z_ref[:, pl.ds(hs[i], hd)]
            o_ref[:, pl.ds(hs[i], hd)] = (o * _rms_scale(o) * nw_ref[...] * _silu(z)).astype(o_ref.dtype)

    def trip(j, carry):
        heads_lockstep([GDN_HEADS_PER_TRIP * j + i for i in range(GDN_HEADS_PER_TRIP)])
        return carry

    lax.fori_loop(0, heads // GDN_HEADS_PER_TRIP, trip, 0)


def _gdn_masks():
    tt = GDN_TILE
    ri = lax.broadcasted_iota(jnp.int32, (tt, tt), 0)
    ci = lax.broadcasted_iota(jnp.int32, (tt, tt), 1)
    same = (ri // GDN_CHUNK) == (ci // GDN_CHUNK)
    causal = same & (ci <= ri)
    tril = causal.astype(F32)
    cmask = jnp.stack([tril, (((ri % 2) == 1) & (ci == ri - 1)).astype(F32)])
    levels = []
    d = 1
    while d < GDN_CHUNK:
        levels.append((((ri // d) % 2) == 1) & ((ci // d) == (ri // d) - 1))
        d *= 2
    return tril, cmask, jnp.stack(levels).astype(BF16)


def _gdn(proj, small, conv_w, head_params, norm_w, masks, cols, batch, seq, heads):
    t = proj.shape[0]
    tt = GDN_TILE
    width = heads * HEAD_DIM
    tiles = seq // tt
    cq, ck, cv, cz = cols
    tril, cmask, lmask = masks

    def cur(col):
        return pl.BlockSpec((tt, width), lambda b, i: (b * tiles + i, col))

    def halo(col):
        return pl.BlockSpec((8, width), lambda b, i: (jnp.maximum((b * tiles + i) * (tt // 8) - 1, 0), col))

    def cw(col):
        return pl.BlockSpec((CONV_K, width), lambda b, i: (0, col))

    kern = functools.partial(_gdn_kernel, heads=heads)
    return pl.pallas_call(
        kern,
        grid=(batch, tiles),
        in_specs=[cur(cq), cur(ck), cur(cv), halo(cq), halo(ck), halo(cv), cur(cz),
                  pl.BlockSpec((tt, 128), lambda b, i: (b * tiles + i, 0)),
                  cw(0), cw(1), cw(2),
                  pl.BlockSpec((8, 128), lambda b, i: (0, 0)),
                  pl.BlockSpec((1, HEAD_DIM), lambda b, i: (0, 0)),
                  pl.BlockSpec((tt, tt), lambda b, i: (0, 0)),
                  pl.BlockSpec(cmask.shape, lambda b, i: (0, 0, 0)),
                  pl.BlockSpec(lmask.shape, lambda b, i: (0, 0, 0))],
        out_specs=pl.BlockSpec((tt, width), lambda b, i: (b * tiles + i, 0)),
        out_shape=jax.ShapeDtypeStruct((t, width), BF16),
        scratch_shapes=[pltpu.VMEM((heads, tt, HEAD_DIM), F32),
                        pltpu.VMEM((heads, tt, HEAD_DIM), F32),
                        pltpu.VMEM((8, tt), F32),
                        pltpu.VMEM((heads, HEAD_DIM, HEAD_DIM), F32),
                        pltpu.VMEM((tt + 8, width), F32),
                        pltpu.VMEM((tt + 8, width), F32),
                        pltpu.VMEM((tt + 8, width), F32)],
        compiler_params=_cparams(("parallel", "arbitrary"), 40),
    )(proj, proj, proj, proj, proj, proj, proj, small, conv_w, conv_w, conv_w, head_params, norm_w,
      tril, cmask, lmask)


def _route(h, wr_hl, rb):
    h_hi = h.astype(BF16)
    h_lo = (h - h_hi.astype(F32)).astype(BF16)
    first = jnp.dot(h_hi, wr_hl[...], preferred_element_type=F32)
    logits = first[:, :128] + first[:, 128:] + jnp.dot(h_lo, wr_hl[:, :128], preferred_element_type=F32)
    logits = logits.T[:N_EXPERTS]
    scores = jax.nn.sigmoid(logits)
    sel = scores + rb
    sel_rows = [sel[e:e + 1, :] for e in range(N_EXPERTS)]
    score_rows = [scores[e:e + 1, :] for e in range(N_EXPERTS)]
    epg = EXPERTS_PER_GROUP
    best_score = None
    best_group = None
    for g in range(N_GROUPS):
        rows = sel_rows[g * epg:(g + 1) * epg]
        gs = None
        for a in range(epg):
            for b in range(a + 1, epg):
                pair = rows[a] + rows[b]
                gs = pair if gs is None else jnp.maximum(gs, pair)
        if g == 0:
            best_score, best_group = gs, jnp.zeros_like(gs)
        else:
            better = gs > best_score
            best_group = jnp.where(better, float(g), best_group)
            best_score = jnp.where(better, gs, best_score)
    in_sel = []
    in_score = []
    for j in range(epg):
        a = sel_rows[j]
        b = score_rows[j]
        for g in range(1, N_GROUPS):
            pick = best_group == float(g)
            a = jnp.where(pick, sel_rows[g * epg + j], a)
            b = jnp.where(pick, score_rows[g * epg + j], b)
        in_sel.append(a)
        in_score.append(b)
    i1, m1, w1 = jnp.zeros_like(in_sel[0]), in_sel[0], in_score[0]
    for j in range(1, epg):
        better = in_sel[j] > m1
        i1 = jnp.where(better, float(j), i1)
        m1 = jnp.where(better, in_sel[j], m1)
        w1 = jnp.where(better, in_score[j], w1)
    i2 = m2 = w2 = None
    for j in range(epg):
        cand = jnp.where(i1 == float(j), -jnp.inf, in_sel[j])
        if j == 0:
            i2, m2, w2 = jnp.zeros_like(cand), cand, in_score[0]
        else:
            better = cand > m2
            i2 = jnp.where(better, float(j), i2)
            m2 = jnp.where(better, cand, m2)
            w2 = jnp.where(better, in_score[j], w2)
    total = w1 + w2
    zero = jnp.zeros_like(w1)
    return jnp.concatenate([best_group * epg + i1, best_group * epg + i2, w1 / total, w2 / total,
                            zero, zero, zero, zero], axis=0)


def _merge_kernel(yl_ref, u_ref, sd_ref, wglu_ref, bglu_ref, yb_ref, ga_ref, gb_ref, x_ref, mod_ref, wa_ref, wb_ref,
                  wo_ref, nw_ref, wrt_ref, rb_ref, o_ref, h_ref, r_ref):
    y = jax.nn.gelu(yl_ref[...] + sd_ref[...] * u_ref[...])
    glu = jnp.dot(y.astype(BF16), wglu_ref[...], preferred_element_type=F32) + bglu_ref[...]
    ya = (y * jax.nn.sigmoid(glu)).astype(BF16)
    pa = jnp.dot(ya, wa_ref[...], preferred_element_type=F32)
    pb = jnp.dot(yb_ref[...], wb_ref[...], preferred_element_type=F32)
    merged = jax.nn.sigmoid(ga_ref[...]) * pa + jax.nn.sigmoid(gb_ref[...]) * pb
    out = jnp.dot(merged.astype(BF16), wo_ref[...], preferred_element_type=F32)
    xn = x_ref[...] + mod_ref[2:3, :] * out
    o_ref[...] = xn
    h = xn * _rms_scale(xn) * nw_ref[...] * (1.0 + mod_ref[4:5, :]) + mod_ref[3:4, :]
    h_ref[...] = _pack_halves(h)
    r_ref[...] = _route(h, wrt_ref, rb_ref[...])


def _merge(y_lin, yb, proj, x2, modl, s5_tail, wa, wb, wo, nw2, wr_t, rb, seq, u_col, ga_col, gb_col):
    t, d = x2.shape
    w = y_lin.shape[1]
    d_skip, w_glu, b_glu = s5_tail
    tm = min(seq, 256)
    per_batch = seq // tm
    const = dict(pipeline_mode=pl.Buffered(1))
    return pl.pallas_call(
        _merge_kernel,
        grid=(t // tm,),
        in_specs=[pl.BlockSpec((tm, w), lambda i: (i, 0)),
                  pl.BlockSpec((tm, w), lambda i: (i, u_col)),
                  pl.BlockSpec((1, w), lambda i: (0, 0)),
                  pl.BlockSpec((w, w), lambda i: (0, 0), **const),
                  pl.BlockSpec((1, w), lambda i: (0, 0)),
                  pl.BlockSpec((tm, w), lambda i: (i, 0)),
                  pl.BlockSpec((tm, d), lambda i: (i, ga_col)),
                  pl.BlockSpec((tm, d), lambda i: (i, gb_col)),
                  pl.BlockSpec((tm, d), lambda i: (i, 0)),
                  pl.BlockSpec((None, 6, d), lambda i: (i // per_batch, 0, 0)),
                  pl.BlockSpec((w, d), lambda i: (0, 0), **const),
                  pl.BlockSpec((w, d), lambda i: (0, 0), **const),
                  pl.BlockSpec((d, d), lambda i: (0, 0), **const),
                  pl.BlockSpec((1, d), lambda i: (0, 0)),
                  pl.BlockSpec((d, 256), lambda i: (0, 0)),
                  pl.BlockSpec((N_EXPERTS, 1), lambda i: (0, 0))],
        out_specs=[pl.BlockSpec((tm, d), lambda i: (i, 0)),
                   pl.BlockSpec((tm, d // 2), lambda i: (i, 0)),
                   pl.BlockSpec((8, tm), lambda i: (0, i))],
        out_shape=[jax.ShapeDtypeStruct((t, d), F32), jax.ShapeDtypeStruct((t, d // 2), jnp.uint32),
                   jax.ShapeDtypeStruct((8, t), F32)],
        compiler_params=_cparams(("parallel",), 52),
    )(y_lin, proj, d_skip, w_glu, b_glu, yb, proj, proj, x2, modl, wa, wb, wo, nw2, wr_t, rb)


def _moe_kernel(te_ref, nu_ref, x_ref, wg_ref, wu_ref, wd_ref, o_ref):
    used = pl.program_id(0) < nu_ref[0]

    @pl.when(used)
    def _():
        x_lo, x_hi = _unpack_halves(x_ref[...])
        x_lo = x_lo.astype(BF16)
        x_hi = x_hi.astype(BF16)
        half = x_lo.shape[1]

        def up(w_ref):
            return (jnp.dot(x_lo, w_ref[:half, :], preferred_element_type=F32)
                    + jnp.dot(x_hi, w_ref[half:, :], preferred_element_type=F32))

        hid = (_silu(up(wg_ref)) * up(wu_ref)).astype(BF16)
        o_ref[...] = _pack_halves(jnp.dot(hid, wd_ref[...], preferred_element_type=F32))

    @pl.when(jnp.logical_not(used))
    def _():
        o_ref[...] = jnp.zeros_like(o_ref)


def _cast_kernel(w_ref, after_ref, o_ref, done_ref):
    del after_ref
    o_ref[...] = w_ref[...].astype(o_ref.dtype)
    done_ref[...] = jnp.zeros_like(done_ref)


def _expert_weights_bf16(w, layer, after):
    _, e, a, b = w.shape
    ta = min(a, (1 << 20) // b)
    return pl.pallas_call(
        _cast_kernel,
        grid=(e, a // ta),
        in_specs=[pl.BlockSpec((None, None, ta, b), lambda i, j: (layer, i, j, 0)),
                  pl.BlockSpec((8, 128), lambda i, j: (0, 0))],
        out_specs=[pl.BlockSpec((None, ta, b), lambda i, j: (i, j, 0)),
                   pl.BlockSpec((8, 128), lambda i, j: (0, 0))],
        out_shape=[jax.ShapeDtypeStruct((e, a, b), BF16), jax.ShapeDtypeStruct((8, 128), F32)],
        compiler_params=_cparams(("arbitrary", "arbitrary"), 32),
    )(w, after)


def _moe_experts(tile_expert, n_used, xs, wg, wu, wd, tm):
    p, dh = xs.shape
    d = 2 * dh
    f = wg.shape[-1]
    grid_spec = pltpu.PrefetchScalarGridSpec(
        num_scalar_prefetch=2,
        grid=(p // tm,),
        in_specs=[pl.BlockSpec((tm, dh), lambda i, te, nu: (i, 0)),
                  pl.BlockSpec((None, d, f), lambda i, te, nu: (te[i], 0, 0)),
                  pl.BlockSpec((None, d, f), lambda i, te, nu: (te[i], 0, 0)),
                  pl.BlockSpec((None, f, d), lambda i, te, nu: (te[i], 0, 0))],
        out_specs=pl.BlockSpec((tm, dh), lambda i, te, nu: (i, 0)),
    )
    return pl.pallas_call(
        _moe_kernel,
        grid_spec=grid_spec,
        out_shape=jax.ShapeDtypeStruct((p, dh), jnp.uint32),
        compiler_params=_cparams(("arbitrary",), 52),
    )(tile_expert, n_used, xs, wg, wu, wd)


def _dispatch_plan(expert_rows, tm, n_tiles):
    flat_e = expert_rows.astype(jnp.int32).reshape(-1)
    onehot = (flat_e[:, None] == jnp.arange(N_EXPERTS, dtype=jnp.int32)[None, :]).astype(jnp.int32)
    csum = jnp.cumsum(onehot, axis=0)
    rank = jnp.sum((csum - onehot) * onehot, axis=1)
    sizes = csum[-1]
    padded = ((sizes + tm - 1) // tm) * tm
    pad_end = jnp.cumsum(padded)
    pad_start = pad_end - padded
    dest = pad_start[flat_e] + rank
    tile_start = jnp.arange(n_tiles, dtype=jnp.int32) * tm
    tile_expert = jnp.minimum(jnp.sum((tile_start[:, None] >= pad_end[None, :]).astype(jnp.int32), axis=1),
                              N_EXPERTS - 1).astype(jnp.int32)
    n_used = (pad_end[-1] // tm).astype(jnp.int32).reshape(1)
    dest = dest.astype(jnp.int32)
    n_slots = flat_e.shape[0]
    _, by_slot = lax.sort_key_val(dest, jnp.arange(n_slots, dtype=jnp.int32) % (n_slots // TOP_K))
    start = jnp.cumsum(sizes) - sizes
    tile_off = tile_start - pad_start[tile_expert]
    tile_valid = jnp.clip(sizes[tile_expert] - tile_off, 0, tm)
    lane = jnp.arange(tm, dtype=jnp.int32)[None, :]
    pick = jnp.clip((start[tile_expert] + tile_off)[:, None] + lane, 0, n_slots - 1)
    filler = (tile_start[:, None] + lane) % (n_slots // TOP_K)
    src_tok = jnp.where(lane < tile_valid[:, None], jnp.take(by_slot, pick.reshape(-1)).reshape(n_tiles, tm), filler)
    return dest, src_tok.reshape(-1).astype(jnp.int32), tile_expert, n_used


def _combine_kernel(x_ref, y0_ref, y1_ref, gw_ref, mod_ref, fw_ref, *rest, final_norm):
    o_ref = rest[-1]
    gw = gw_ref[...].T[:, TOP_K:2 * TOP_K]
    y0_lo, y0_hi = _unpack_halves(y0_ref[...])
    y1_lo, y1_hi = _unpack_halves(y1_ref[...])
    moe = jnp.concatenate([gw[:, 0:1] * y0_lo + gw[:, 1:2] * y1_lo, gw[:, 0:1] * y0_hi + gw[:, 1:2] * y1_hi], axis=1)
    xn = x_ref[...] + mod_ref[5:6, :] * moe
    if final_norm:
        xn = xn * _rms_scale(xn) * fw_ref[...]
    o_ref[...] = xn


def _combine(x2, y_pairs, gate_w, modl, final_w, seq, final_norm, ties=()):
    t, d = x2.shape
    tm = min(seq, 512)
    per_batch = seq // tm
    nblk = t // tm
    return pl.pallas_call(
        functools.partial(_combine_kernel, final_norm=final_norm),
        grid=(nblk,),
        in_specs=[pl.BlockSpec((tm, d), lambda i: (i, 0)),
                  pl.BlockSpec((tm, d // 2), lambda i: (i, 0)),
                  pl.BlockSpec((tm, d // 2), lambda i: (i + nblk, 0)),
                  pl.BlockSpec((8, tm), lambda i: (0, i)),
                  pl.BlockSpec((None, 6, d), lambda i: (i // per_batch, 0, 0)),
                  pl.BlockSpec((1, d), lambda i: (0, 0))] + [pl.BlockSpec((8, 128), lambda i: (0, 0)) for _ in ties],
        out_specs=pl.BlockSpec((tm, d), lambda i: (i, 0)),
        out_shape=jax.ShapeDtypeStruct((t, d), F32),
        compiler_params=_cparams(("parallel",), 40),
    )(x2, y_pairs, y_pairs, gate_w, modl, final_w, *ties)


def kernel(x, c, w_ada, b_ada, norm1_w, norm2_w, w_in, s5_a_re, s5_a_im, s5_log_dt, s5_b_re, s5_b_im, s5_c_re, s5_c_im, s5_d, s5_w_glu, s5_b_glu, gdn_conv_w, gdn_a_log, gdn_dt_bias, gdn_norm_w, w_proj_a, w_proj_b, w_out, w_router, router_bias, w_gate, w_up, w_down, final_norm_w):
    batch, seq, d = x.shape
    depth = w_ada.shape[0]
    t = batch * seq
    s5w = s5_d.shape[1]
    gdw = w_proj_b.shape[1]
    heads = gdn_a_log.shape[1]
    assert gdw == heads * HEAD_DIM and s5w == gdw and d == 2 * gdw
    assert seq % GDN_TILE == 0 and seq % S5_CHUNK == 0
    n_levels = int(math.log2(seq // S5_CHUNK))
    assert S5_CHUNK << n_levels == seq and S5_OCT * S5_GROUP == 128

    c_pad = jnp.zeros((8, d), F32).at[:batch].set(c)
    mod = _ada(c_pad, w_ada, b_ada)

    o_u, o_q, o_k, o_v, o_z = 0, s5w, 2 * s5w, 3 * s5w, 4 * s5w
    o_ba = 5 * s5w
    o_ga = o_ba + 2 * heads
    o_gb = o_ga + d
    col_u, col_q, col_k, col_v, col_z = 4, 5, 6, 7, 8

    gdn_masks = _gdn_masks()
    s5_rep = _s5_replicate()
    s5_ops = jax.vmap(functools.partial(_s5_weights, n_levels=n_levels))(
        s5_a_re, s5_a_im, s5_log_dt, s5_b_re, s5_b_im, s5_c_re, s5_c_im)

    wr_hi = w_router.astype(BF16)
    wr_lo = (w_router - wr_hi.astype(F32)).astype(BF16)
    pad_e = ((0, 0), (0, 128 - N_EXPERTS))
    wr_t = jnp.concatenate([jnp.pad(wr_hi, pad_e), jnp.pad(wr_lo, pad_e)], axis=1)
    rb = router_bias.reshape(N_EXPERTS, 1).astype(F32)

    tm_e = min(512, t)
    n_tiles = (TOP_K * t) // tm_e + N_EXPERTS

    w_in_t = jnp.transpose(w_in, (0, 2, 1))
    w_big = _repack_in_proj(w_in_t, o_ga, 2 * d, o_ba)
    w_small = jnp.pad(w_in_t[:, o_ba:o_ga], ((0, 0), (0, 128 - 2 * heads), (0, 0))).astype(BF16)

    x2 = x.reshape(t, d)
    for l in range(depth):
        modl = mod[l, :batch].reshape(batch, 6, d)
        proj, small = _inproj(x2, modl, norm1_w[l].reshape(1, d), w_big, w_small, l, seq)

        y_lin = _s5_core(proj, col_u * s5w // 128, *s5_ops, s5_rep, l, batch, seq)
        s5_tail = (s5_d[l].reshape(1, s5w), s5_w_glu[l].astype(BF16), s5_b_glu[l].reshape(1, s5w))

        head_params = jnp.zeros((8, 128), F32)
        head_params = head_params.at[0, heads:2 * heads].set(-jnp.exp(gdn_a_log[l]))
        head_params = head_params.at[1, heads:2 * heads].set(gdn_dt_bias[l])
        y_b = _gdn(proj, small, gdn_conv_w[l], head_params, gdn_norm_w[l].reshape(1, HEAD_DIM), gdn_masks,
                   (col_q, col_k, col_v, col_z), batch, seq, heads)

        x2, h2, route = _merge(y_lin, y_b, proj, x2, modl, s5_tail, w_proj_a[l].astype(BF16),
                               w_proj_b[l].astype(BF16), w_out[l].astype(BF16), norm2_w[l].reshape(1, d), wr_t, rb,
                               seq, col_u, 0, 1)

        dest, src_tok, tile_expert, n_used = _dispatch_plan(route[0:TOP_K], tm_e, n_tiles)
        xs = jnp.take(h2, src_tok, axis=0, mode='clip')
        if l == 0:
            (wg_l, _), (wu_l, _) = _expert_weights_bf16(w_gate, 0, h2), _expert_weights_bf16(w_up, 0, h2)
        wd_l, _ = _expert_weights_bf16(w_down, l, h2)
        ys = _moe_experts(tile_expert, n_used, xs, wg_l, wu_l, wd_l, tm_e)
        y_pairs = jnp.take(ys, dest, axis=0, mode='clip')
        ties = ()
        if l + 1 < depth:
            (wg_l, tie_g), (wu_l, tie_u) = _expert_weights_bf16(w_gate, l + 1, ys), _expert_weights_bf16(w_up, l + 1, ys)
            ties = (tie_g, tie_u)
        x2 = _combine(x2, y_pairs, route, modl, final_norm_w.reshape(1, d), seq, l == depth - 1, ties)

    return x2.reshape(batch, seq, d)
```

```python
import functools
import math

import jax
import jax.numpy as jnp
from jax import lax
from jax.experimental import pallas as pl
from jax.experimental.pallas import tpu as pltpu

F32 = jnp.float32
BF16 = jnp.bfloat16
HIGHEST = lax.Precision.HIGHEST

RMS_EPS = 1e-6
S5_GROUP = 16
S5_STATE = 64
S5_CHUNK = 16
S5_OCT = 8
HEAD_DIM = 128
GDN_CHUNK = 64
GDN_TILE = 256
GDN_HEADS_PER_TRIP = 8
CONV_K = 4
N_EXPERTS = 16
N_GROUPS = 4
EXPERTS_PER_GROUP = N_EXPERTS // N_GROUPS
TOP_K = 2
MIB = 1024 * 1024


def _cparams(semantics, vmem_mib):
    return pltpu.CompilerParams(dimension_semantics=semantics, vmem_limit_bytes=vmem_mib * MIB)


def _silu(x):
    return x * jax.nn.sigmoid(x)


def _softplus(x):
    return jnp.maximum(x, 0.0) + jnp.log(1.0 + jnp.exp(-jnp.abs(x)))


def _rms_scale(x):
    return lax.rsqrt(jnp.mean(x * x, axis=-1, keepdims=True) + RMS_EPS)


def _pack_halves(x):
    n = x.shape[1] // 2
    lo = lax.bitcast_convert_type(x[:, :n].astype(BF16).astype(F32), jnp.uint32)
    hi = lax.bitcast_convert_type(x[:, n:].astype(BF16).astype(F32), jnp.uint32)
    return (lo >> 16) | hi


def _unpack_halves(p):
    lo = lax.bitcast_convert_type(p << 16, F32)
    hi = lax.bitcast_convert_type(p & jnp.uint32(0xFFFF0000), F32)
    return lo, hi


def _nt_dot(a, b):
    return lax.dot_general(a, b, (((1,), (1,)), ((), ())), preferred_element_type=F32)


def _tn_dot(a, b):
    return lax.dot_general(a, b, (((0,), (0,)), ((), ())), preferred_element_type=F32)


def _ada_kernel(c_ref, w_ref, b_ref, o_ref):
    ca = _silu(c_ref[...]).astype(BF16)
    o_ref[...] = jnp.dot(ca, w_ref[...].astype(BF16), preferred_element_type=F32) + b_ref[...]


def _ada(c_pad, w_ada, b_ada):
    depth, d, n = w_ada.shape
    rows = c_pad.shape[0]
    tn = min(n, 1024)
    return pl.pallas_call(
        _ada_kernel,
        grid=(depth, n // tn),
        in_specs=[pl.BlockSpec((rows, d), lambda l, j: (0, 0)),
                  pl.BlockSpec((None, d, tn), lambda l, j: (l, 0, j)),
                  pl.BlockSpec((None, 1, tn), lambda l, j: (l, 0, j))],
        out_specs=pl.BlockSpec((None, rows, tn), lambda l, j: (l, 0, j)),
        out_shape=jax.ShapeDtypeStruct((depth, rows, n), F32),
        compiler_params=_cparams(("parallel", "parallel"), 40),
    )(c_pad, w_ada, b_ada.reshape(depth, 1, n))


def _inproj_kernel(x_ref, mod_ref, nw_ref, w_ref, wsm_ref, o_ref, osm_ref, h_scr):
    @pl.when(pl.program_id(1) == 0)
    def _():
        x = x_ref[...]
        h = x * _rms_scale(x) * nw_ref[...] * (1.0 + mod_ref[1:2, :]) + mod_ref[0:1, :]
        hb = h.astype(BF16)
        h_scr[...] = hb
        osm_ref[...] = _nt_dot(hb, wsm_ref[...])

    o_ref[...] = _nt_dot(h_scr[...], w_ref[...])


def _inproj(x2, modl, nw, w_big, w_small, layer, seq):
    t, d = x2.shape
    n = w_big.shape[1]
    tm = min(seq, 1024)
    tn = 1536
    per_batch = seq // tm
    return pl.pallas_call(
        _inproj_kernel,
        grid=(t // tm, n // tn),
        in_specs=[pl.BlockSpec((tm, d), lambda i, j: (i, 0)),
                  pl.BlockSpec((None, 6, d), lambda i, j: (i // per_batch, 0, 0)),
                  pl.BlockSpec((1, d), lambda i, j: (0, 0)),
                  pl.BlockSpec((None, tn, d), lambda i, j: (layer, j, 0)),
                  pl.BlockSpec((None, 128, d), lambda i, j: (layer, 0, 0))],
        out_specs=[pl.BlockSpec((tm, tn), lambda i, j: (i, j)),
                   pl.BlockSpec((tm, 128), lambda i, j: (i, 0))],
        out_shape=[jax.ShapeDtypeStruct((t, n), F32), jax.ShapeDtypeStruct((t, 128), F32)],
        scratch_shapes=[pltpu.VMEM((tm, d), BF16)],
        compiler_params=_cparams(("parallel", "arbitrary"), 52),
    )(x2, modl, nw, w_big, w_small)


def _repack_kernel(a_ref, b_ref, o_ref, *, gate_blocks, shift):
    c = pl.program_id(1)

    @pl.when(c < gate_blocks)
    def _():
        x = jnp.concatenate([a_ref[...], b_ref[...]], axis=0)
        o_ref[...] = x[shift:shift + o_ref.shape[0]].astype(o_ref.dtype)

    @pl.when(c >= gate_blocks)
    def _():
        o_ref[...] = a_ref[...].astype(o_ref.dtype)


def _repack_in_proj(w_in_t, gate_start, gate_cols, lead_cols):
    depth, _, d = w_in_t.shape
    rb = 512
    shift = gate_start % rb
    base = gate_start - shift
    assert shift % 16 == 0 and gate_cols % rb == 0 and lead_cols % rb == 0 and lead_cols == base
    gate_blocks = gate_cols // rb
    kern = functools.partial(_repack_kernel, gate_blocks=gate_blocks, shift=shift)
    return pl.pallas_call(
        kern,
        grid=(depth, (gate_cols + lead_cols) // rb),
        in_specs=[pl.BlockSpec((None, rb, d),
                               lambda l, c: (l, jnp.where(c < gate_blocks, base // rb + c, c - gate_blocks), 0)),
                  pl.BlockSpec((None, shift, d),
                               lambda l, c: (l, jnp.where(c < gate_blocks, (base + (c + 1) * rb) // shift, 0), 0))],
        out_specs=pl.BlockSpec((None, rb, d), lambda l, c: (l, c, 0)),
        out_shape=jax.ShapeDtypeStruct((depth, gate_cols + lead_cols, d), BF16),
        compiler_params=_cparams(("parallel", "arbitrary"), 32),
    )(w_in_t, w_in_t)


def _s5_weights(a_re, a_im, log_dt, b_re, b_im, c_re, c_im, n_levels):
    tc = S5_CHUNK
    groups, p = a_re.shape
    n = b_re.shape[-1]
    oc = S5_OCT
    octs = groups // oc
    lam = lax.complex(a_re, a_im)
    dt = jnp.exp(log_dt)[:, None]
    log_a = lam * dt
    a_bar = jnp.exp(log_a)
    b_bar = ((a_bar - 1.0) / lam)[..., None] * lax.complex(b_re, b_im)
    c_mat = lax.complex(c_re, c_im)
    apow = jnp.exp(log_a[:, :, None] * jnp.arange(tc + 1, dtype=F32)[None, None, :])
    lane = jnp.arange(tc * n)
    e_t = (jnp.arange(tc)[:, None] == (lane // n)[None, :]).astype(F32)
    e_n = (jnp.arange(n)[:, None] == (lane % n)[None, :]).astype(F32)

    def expand(z, e):
        f = lambda r: jnp.einsum('gpk,kj->gpj', r, e, precision=HIGHEST)
        return lax.complex(f(jnp.real(z)), f(jnp.imag(z)))

    def re_negim_rows(z):
        return jnp.concatenate([jnp.real(z), -jnp.imag(z)], axis=1)

    c_rep = expand(jnp.transpose(c_mat, (0, 2, 1)), e_n)
    lag_c = re_negim_rows(expand(apow[:, :, :tc], e_t) * c_rep)
    hoc = re_negim_rows(expand(apow[:, :, 1:], e_t) * c_rep)
    b_t = jnp.transpose(b_bar, (0, 2, 1))
    b_cat = jnp.concatenate([jnp.real(b_t), jnp.imag(b_t)], axis=-1)
    kern = jnp.einsum('gnk,gkj->gnj', b_cat, lag_c, precision=HIGHEST)
    tcomp = jnp.stack([jnp.pad(kern[:, :, :(tc - s) * n], ((0, 0), (0, 0), (s * n, 0))) for s in range(tc)],
                      axis=1)
    tcomp = jnp.transpose(tcomp.reshape(octs, oc, tc, n, tc * n), (0, 2, 1, 3, 4)).reshape(octs, tc * oc * n, tc * n)
    ap_rev = jnp.transpose(apow[:, :, tc - 1 - jnp.arange(tc)], (0, 2, 1))
    hin_c = ap_rev[:, :, None, :] * b_t[:, None, :, :]
    hc = jnp.concatenate([jnp.real(hin_c), jnp.imag(hin_c)], axis=-1)
    hc = jnp.transpose(hc.reshape(octs, oc, tc, n, 2 * p), (0, 2, 1, 3, 4)).reshape(octs, tc * oc * n, 2 * p)
    hoc = hoc.reshape(octs, oc * 2 * p, tc * n)
    steps = (tc * (2.0 ** jnp.arange(n_levels, dtype=F32)))
    amul = jnp.exp(log_a[:, None, :] * steps[None, :, None])
    a1 = jnp.concatenate([jnp.real(amul), jnp.real(amul)], axis=-1)
    a2 = jnp.concatenate([-jnp.imag(amul), jnp.imag(amul)], axis=-1)

    def lanes_by_group(a):
        return jnp.transpose(a.reshape(octs, oc, n_levels, 2 * p), (0, 2, 1, 3)).reshape(octs, n_levels, oc * 2 * p)

    return (tcomp.astype(BF16), hc.astype(BF16), hoc.astype(BF16),
            lanes_by_group(a1).astype(F32), lanes_by_group(a2).astype(F32))


def _s5_replicate():
    r = lax.broadcasted_iota(jnp.int32, (S5_CHUNK * S5_GROUP, S5_CHUNK * 128), 0)
    c = lax.broadcasted_iota(jnp.int32, (S5_CHUNK * S5_GROUP, S5_CHUNK * 128), 1)
    return ((r // S5_GROUP == c // 128) & (r % S5_GROUP == c % S5_GROUP)).astype(BF16)


def _s5_core_kernel(u_ref, tc_ref, hc_ref, hoc_ref, a1_ref, a2_ref, rep_ref, y_ref, toep_scr, hin_scr, hout_scr,
                    *, n_levels):
    tc = S5_CHUNK
    lanes = u_ref.shape[1]
    state_lanes = hc_ref.shape[1]

    @pl.when(pl.program_id(1) == 0)
    def _():
        kd = toep_scr.shape[1]
        rb = 256
        col_g = (lax.broadcasted_iota(jnp.int32, (rb, kd), 1) // S5_GROUP) % S5_OCT
        row_i = lax.broadcasted_iota(jnp.int32, (rb, kd), 0)
        for i in range(toep_scr.shape[0] // rb):
            full = jnp.dot(tc_ref[i * rb:(i + 1) * rb, :], rep_ref[...], preferred_element_type=F32)
            row_g = ((row_i + i * rb) // S5_GROUP) % S5_OCT
            toep_scr[i * rb:(i + 1) * rb, :] = jnp.where(row_g == col_g, full, 0.0).astype(BF16)
        for i in range(hout_scr.shape[0] // rb):
            full = jnp.dot(hoc_ref[i * rb:(i + 1) * rb, :], rep_ref[...], preferred_element_type=F32)
            row_g = (row_i + i * rb) // state_lanes
            hout_scr[i * rb:(i + 1) * rb, :] = jnp.where(row_g == col_g, full, 0.0).astype(BF16)
        hrow_g = (lax.broadcasted_iota(jnp.int32, hc_ref.shape, 0) // S5_GROUP) % S5_OCT
        for g in range(S5_OCT):
            hin_scr[:, g * state_lanes:(g + 1) * state_lanes] = jnp.where(hrow_g == g, hc_ref[...], 0.0).astype(BF16)

    rows = u_ref.shape[0] // tc
    x = jnp.concatenate([u_ref[pl.ds(t, rows, stride=tc), :] for t in range(tc)], axis=1).astype(BF16)
    s_loc = jnp.dot(x, hin_scr[...], preferred_element_type=F32)
    pos = lax.broadcasted_iota(jnp.int32, (rows, state_lanes), 0)
    s_in = []
    for g in range(S5_OCT):
        ln = slice(g * state_lanes, (g + 1) * state_lanes)
        s = s_loc[:, ln]
        for k in range(n_levels):
            d = 1 << k
            prev = jnp.where(pos >= d, pltpu.roll(s, d, axis=0), 0.0)
            s = s + a1_ref[k:k + 1, ln] * prev + a2_ref[k:k + 1, ln] * pltpu.roll(prev, state_lanes // 2, axis=1)
        s_in.append(jnp.where(pos >= 1, pltpu.roll(s, 1, axis=0), 0.0).astype(BF16))
    y = (jnp.dot(x, toep_scr[...], preferred_element_type=F32)
         + jnp.dot(jnp.concatenate(s_in, axis=1), hout_scr[...], preferred_element_type=F32))
    for t in range(tc):
        y_ref[pl.ds(t, rows, stride=tc), :] = y[:, t * lanes:(t + 1) * lanes]


def _s5_core(proj, u_lane_block, tcomp, hc, hoc, a1, a2, rep, layer, batch, seq):
    t = proj.shape[0]
    _, octs, kdim, tn = tcomp.shape
    sl = hc.shape[-1]
    sw = hoc.shape[2]
    n_levels = a1.shape[2]
    lanes = kdim // S5_CHUNK
    kern = functools.partial(_s5_core_kernel, n_levels=n_levels)
    return pl.pallas_call(
        kern,
        grid=(octs, batch),
        in_specs=[pl.BlockSpec((seq, lanes), lambda o, b: (b, u_lane_block + o)),
                  pl.BlockSpec((None, None, kdim, tn), lambda o, b: (layer, o, 0, 0)),
                  pl.BlockSpec((None, None, kdim, sl), lambda o, b: (layer, o, 0, 0)),
                  pl.BlockSpec((None, None, sw, tn), lambda o, b: (layer, o, 0, 0)),
                  pl.BlockSpec((None, None, n_levels, sw), lambda o, b: (layer, o, 0, 0)),
                  pl.BlockSpec((None, None, n_levels, sw), lambda o, b: (layer, o, 0, 0)),
                  pl.BlockSpec((tn, kdim), lambda o, b: (0, 0))],
        out_specs=pl.BlockSpec((seq, lanes), lambda o, b: (b, o)),
        out_shape=jax.ShapeDtypeStruct((t, octs * lanes), F32),
        scratch_shapes=[pltpu.VMEM((kdim, kdim), BF16), pltpu.VMEM((kdim, sw), BF16), pltpu.VMEM((sw, kdim), BF16)],
        compiler_params=_cparams(("parallel", "arbitrary"), 48),
    )(proj, tcomp, hc, hoc, a1, a2, rep)


def _gdn_kernel(q_ref, k_ref, v_ref, qh_ref, kh_ref, vh_ref, z_ref, sm_ref, cq_ref, ck_ref, cv_ref,
                hp_ref, nw_ref, tril_ref, cmask_ref, lmask_ref, o_ref, betab, gcb, grow, st_scr,
                cat_q, cat_k, cat_v, *, heads):
    tt = GDN_TILE
    hd = HEAD_DIM
    nck = tt // GDN_CHUNK
    first = pl.program_id(1) == 0

    @pl.when(first)
    def _():
        st_scr[...] = jnp.zeros_like(st_scr)

    sm = sm_ref[...]
    hp = hp_ref[...]
    beta_all = jax.nn.sigmoid(sm)
    g_all = hp[0:1, :] * _softplus(sm + hp[1:2, :])
    gc_all = jnp.dot(tril_ref[...], g_all, precision=HIGHEST, preferred_element_type=F32)
    gc_t = gc_all.T
    for h in range(heads):
        betab[h] = jnp.broadcast_to(beta_all[:, h:h + 1], (tt, hd))
        gcb[h] = jnp.broadcast_to(gc_all[:, heads + h:heads + h + 1], (tt, hd))
        grow[h:h + 1, :] = gc_t[heads + h:heads + h + 1, :]

    eye = (lax.broadcasted_iota(jnp.int32, (tt, tt), 0) == lax.broadcasted_iota(jnp.int32, (tt, tt), 1)).astype(F32)
    n_lv = lmask_ref.shape[0]

    for x_ref, halo_ref, cat in ((q_ref, qh_ref, cat_q), (k_ref, kh_ref, cat_k), (v_ref, vh_ref, cat_v)):
        cat[0:8, :] = jnp.where(first, 0.0, halo_ref[...])
        cat[8:8 + tt, :] = x_ref[...]

    def conv_silu(cat, cw_ref, hs):
        w = cw_ref[:, pl.ds(hs, hd)]
        acc = cat[pl.ds(8, tt), pl.ds(hs, hd)] * w[CONV_K - 1:CONV_K, :]
        for s in range(1, CONV_K):
            acc = acc + cat[pl.ds(8 - s, tt), pl.ds(hs, hd)] * w[CONV_K - 1 - s:CONV_K - s, :]
        return _silu(acc)

    def heads_lockstep(hs_idx):
        each = lambda f, *ls: [f(*a) for a in zip(*ls)]
        hs = [pl.multiple_of(h * hd, hd) for h in hs_idx]
        q = [conv_silu(cat_q, cq_ref, o) for o in hs]
        k = [conv_silu(cat_k, ck_ref, o) for o in hs]
        v = [conv_silu(cat_v, cv_ref, o) for o in hs]
        qn = each(lambda a: a * (lax.rsqrt(jnp.sum(a * a, axis=-1, keepdims=True) + RMS_EPS) * (hd ** -0.5)), q)
        kn = each(lambda a: a * lax.rsqrt(jnp.sum(a * a, axis=-1, keepdims=True) + RMS_EPS), k)
        bb = [betab[h] for h in hs_idx]
        gc = [gcb[h] for h in hs_idx]
        gr = [grow[pl.ds(h, 1), :] for h in hs_idx]
        decay = each(lambda c, r: jnp.exp(jnp.where(cmask_ref[0] > 0.0, jnp.concatenate([c, c], axis=1) - r, -1e30)),
                     gc, gr)
        kb = each(lambda a, b: a * b, kn, bb)
        knb = each(lambda a: a.astype(BF16), kn)
        lfull = each(lambda a, b, dc: _nt_dot(a.astype(BF16), b) * dc, kb, knb, decay)
        lmat = each(lambda a: a.astype(BF16), lfull)
        tinv = each(lambda a: eye - a * cmask_ref[1], lfull)
        for lv in range(1, n_lv):
            d = 1 << lv
            nb = tt // d
            tb = each(lambda a: a.astype(BF16), tinv)
            bd = each(lambda m: m * lmask_ref[lv], lmat)
            if d % 8 == 0:
                odd = each(lambda a: jnp.concatenate([a[i * d:(i + 1) * d] for i in range(1, nb, 2)], axis=0), tinv)
                tl = each(lambda a, b: jnp.dot(a.astype(BF16), b, preferred_element_type=F32).astype(BF16), odd, bd)
                upd = each(lambda a, b, c: a - jnp.dot(b, c, preferred_element_type=F32), odd, tl, tb)
                tinv = each(lambda a, u: jnp.concatenate(
                    [a[i * d:(i + 1) * d] if i % 2 == 0 else u[(i // 2) * d:(i // 2 + 1) * d] for i in range(nb)],
                    axis=0), tinv, upd)
            else:
                tl = each(lambda a, b: jnp.dot(a, b, preferred_element_type=F32).astype(BF16), tb, bd)
                tinv = each(lambda a, b, c: a - jnp.dot(b, c, preferred_element_type=F32), tinv, tl, tb)
        eg = each(jnp.exp, gc)
        rhs = each(lambda a, e, b, c: jnp.concatenate([a * e, b * c], axis=1).astype(BF16), kb, eg, v, bb)
        wu = each(lambda a, b: jnp.dot(a.astype(BF16), b, preferred_element_type=F32).astype(BF16), tinv, rhs)
        attn = each(lambda a, b, dc: (_nt_dot(a.astype(BF16), b) * dc).astype(BF16), qn, knb, decay)
        awu = each(lambda a, b: jnp.dot(a, b, preferred_element_type=F32), attn, wu)
        qp = each(lambda a, e, b: (a * e - b[:, :hd]).astype(BF16), qn, eg, awu)
        s = [st_scr[h] for h in hs_idx]
        outs = [[] for _ in hs_idx]
        for c in range(nck):
            r0 = c * GDN_CHUNK
            r1 = r0 + GDN_CHUNK
            gl = each(lambda a: a[r1 - 1:r1, :], gc)
            kd = each(lambda a, l, g: (a[r0:r1] * jnp.exp(l - g[r0:r1])).astype(BF16), kn, gl, gc)
            m = each(lambda a, b: _tn_dot(a, b[r0:r1]), kd, wu)
            sb = each(lambda a: a.astype(BF16), s)
            for i, (a, b, y) in enumerate(zip(qp, sb, awu)):
                outs[i].append(jnp.dot(a[r0:r1], b, preferred_element_type=F32) + y[r0:r1, hd:])
            s = each(lambda a, l, mm, b: a * jnp.exp(l) - jnp.dot(mm[:, :hd].astype(BF16), b,
                                                                  preferred_element_type=F32) + mm[:, hd:],
                     s, gl, m, sb)
        for i, h in enumerate(hs_idx):
            st_scr[h] = s[i]
            o = jnp.concatenate(outs[i], axis=0)
            z = z_ref[:, pl.ds(hs[i], hd)]
            o_ref[:, pl.ds(hs[i], hd)] = (o * _rms_scale(o) * nw_ref[...] * _silu(z)).astype(o_ref.dtype)

    def trip(j, carry):
        heads_lockstep([GDN_HEADS_PER_TRIP * j + i for i in range(GDN_HEADS_PER_TRIP)])
        return carry

    lax.fori_loop(0, heads // GDN_HEADS_PER_TRIP, trip, 0)


def _gdn_masks():
    tt = GDN_TILE
    ri = lax.broadcasted_iota(jnp.int32, (tt, tt), 0)
    ci = lax.broadcasted_iota(jnp.int32, (tt, tt), 1)
    same = (ri // GDN_CHUNK) == (ci // GDN_CHUNK)
    causal = same & (ci <= ri)
    tril = causal.astype(F32)
    cmask = jnp.stack([tril, (((ri % 2) == 1) & (ci == ri - 1)).astype(F32)])
    levels = []
    d = 1
    while d < GDN_CHUNK:
        levels.append((((ri // d) % 2) == 1) & ((ci // d) == (ri // d) - 1))
        d *= 2
    return tril, cmask, jnp.stack(levels).astype(BF16)


def _gdn(proj, small, conv_w, head_params, norm_w, masks, cols, batch, seq, heads):
    t = proj.shape[0]
    tt = GDN_TILE
    width = heads * HEAD_DIM
    tiles = seq // tt
    cq, ck, cv, cz = cols
    tril, cmask, lmask = masks

    def cur(col):
        return pl.BlockSpec((tt, width), lambda b, i: (b * tiles + i, col))

    def halo(col):
        return pl.BlockSpec((8, width), lambda b, i: (jnp.maximum((b * tiles + i) * (tt // 8) - 1, 0), col))

    def cw(col):
        return pl.BlockSpec((CONV_K, width), lambda b, i: (0, col))

    kern = functools.partial(_gdn_kernel, heads=heads)
    return pl.pallas_call(
        kern,
        grid=(batch, tiles),
        in_specs=[cur(cq), cur(ck), cur(cv), halo(cq), halo(ck), halo(cv), cur(cz),
                  pl.BlockSpec((tt, 128), lambda b, i: (b * tiles + i, 0)),
                  cw(0), cw(1), cw(2),
                  pl.BlockSpec((8, 128), lambda b, i: (0, 0)),
                  pl.BlockSpec((1, HEAD_DIM), lambda b, i: (0, 0)),
                  pl.BlockSpec((tt, tt), lambda b, i: (0, 0)),
                  pl.BlockSpec(cmask.shape, lambda b, i: (0, 0, 0)),
                  pl.BlockSpec(lmask.shape, lambda b, i: (0, 0, 0))],
        out_specs=pl.BlockSpec((tt, width), lambda b, i: (b * tiles + i, 0)),
        out_shape=jax.ShapeDtypeStruct((t, width), BF16),
        scratch_shapes=[pltpu.VMEM((heads, tt, HEAD_DIM), F32),
                        pltpu.VMEM((heads, tt, HEAD_DIM), F32),
                        pltpu.VMEM((8, tt), F32),
                        pltpu.VMEM((heads, HEAD_DIM, HEAD_DIM), F32),
                        pltpu.VMEM((tt + 8, width), F32),
                        pltpu.VMEM((tt + 8, width), F32),
                        pltpu.VMEM((tt + 8, width), F32)],
        compiler_params=_cparams(("parallel", "arbitrary"), 40),
    )(proj, proj, proj, proj, proj, proj, proj, small, conv_w, conv_w, conv_w, head_params, norm_w,
      tril, cmask, lmask)


def _route(h, wr_hl, rb):
    h_hi = h.astype(BF16)
    h_lo = (h - h_hi.astype(F32)).astype(BF16)
    first = jnp.dot(h_hi, wr_hl[...], preferred_element_type=F32)
    logits = first[:, :128] + first[:, 128:] + jnp.dot(h_lo, wr_hl[:, :128], preferred_element_type=F32)
    logits = logits.T[:N_EXPERTS]
    scores = jax.nn.sigmoid(logits)
    sel = scores + rb
    sel_rows = [sel[e:e + 1, :] for e in range(N_EXPERTS)]
    score_rows = [scores[e:e + 1, :] for e in range(N_EXPERTS)]
    epg = EXPERTS_PER_GROUP
    best_score = None
    best_group = None
    for g in range(N_GROUPS):
        rows = sel_rows[g * epg:(g + 1) * epg]
        gs = None
        for a in range(epg):
            for b in range(a + 1, epg):
                pair = rows[a] + rows[b]
                gs = pair if gs is None else jnp.maximum(gs, pair)
        if g == 0:
            best_score, best_group = gs, jnp.zeros_like(gs)
        else:
            better = gs > best_score
            best_group = jnp.where(better, float(g), best_group)
            best_score = jnp.where(better, gs, best_score)
    in_sel = []
    in_score = []
    for j in range(epg):
        a = sel_rows[j]
        b = score_rows[j]
        for g in range(1, N_GROUPS):
            pick = best_group == float(g)
            a = jnp.where(pick, sel_rows[g * epg + j], a)
            b = jnp.where(pick, score_rows[g * epg + j], b)
        in_sel.append(a)
        in_score.append(b)
    i1, m1, w1 = jnp.zeros_like(in_sel[0]), in_sel[0], in_score[0]
    for j in range(1, epg):
        better = in_sel[j] > m1
        i1 = jnp.where(better, float(j), i1)
        m1 = jnp.where(better, in_sel[j], m1)
        w1 = jnp.where(better, in_score[j], w1)
    i2 = m2 = w2 = None
    for j in range(epg):
        cand = jnp.where(i1 == float(j), -jnp.inf, in_sel[j])
        if j == 0:
            i2, m2, w2 = jnp.zeros_like(cand), cand, in_score[0]
        else:
            better = cand > m2
            i2 = jnp.where(better, float(j), i2)
            m2 = jnp.where(better, cand, m2)
            w2 = jnp.where(better, in_score[j], w2)
    total = w1 + w2
    zero = jnp.zeros_like(w1)
    return jnp.concatenate([best_group * epg + i1, best_group * epg + i2, w1 / total, w2 / total,
                            zero, zero, zero, zero], axis=0)


def _merge_kernel(yl_ref, u_ref, sd_ref, wglu_ref, bglu_ref, yb_ref, ga_ref, gb_ref, x_ref, mod_ref, wa_ref, wb_ref,
                  wo_ref, nw_ref, wrt_ref, rb_ref, o_ref, h_ref, r_ref):
    y = jax.nn.gelu(yl_ref[...] + sd_ref[...] * u_ref[...])
    glu = jnp.dot(y.astype(BF16), wglu_ref[...], preferred_element_type=F32) + bglu_ref[...]
    ya = (y * jax.nn.sigmoid(glu)).astype(BF16)
    pa = jnp.dot(ya, wa_ref[...], preferred_element_type=F32)
    pb = jnp.dot(yb_ref[...], wb_ref[...], preferred_element_type=F32)
    merged = jax.nn.sigmoid(ga_ref[...]) * pa + jax.nn.sigmoid(gb_ref[...]) * pb
    out = jnp.dot(merged.astype(BF16), wo_ref[...], preferred_element_type=F32)
    xn = x_ref[...] + mod_ref[2:3, :] * out
    o_ref[...] = xn
    h = xn * _rms_scale(xn) * nw_ref[...] * (1.0 + mod_ref[4:5, :]) + mod_ref[3:4, :]
    h_ref[...] = _pack_halves(h)
    r_ref[...] = _route(h, wrt_ref, rb_ref[...])


def _merge(y_lin, yb, proj, x2, modl, s5_tail, wa, wb, wo, nw2, wr_t, rb, seq, u_col, ga_col, gb_col):
    t, d = x2.shape
    w = y_lin.shape[1]
    d_skip, w_glu, b_glu = s5_tail
    tm = min(seq, 256)
    per_batch = seq // tm
    const = dict(pipeline_mode=pl.Buffered(1))
    return pl.pallas_call(
        _merge_kernel,
        grid=(t // tm,),
        in_specs=[pl.BlockSpec((tm, w), lambda i: (i, 0)),
                  pl.BlockSpec((tm, w), lambda i: (i, u_col)),
                  pl.BlockSpec((1, w), lambda i: (0, 0)),
                  pl.BlockSpec((w, w), lambda i: (0, 0), **const),
                  pl.BlockSpec((1, w), lambda i: (0, 0)),
                  pl.BlockSpec((tm, w), lambda i: (i, 0)),
                  pl.BlockSpec((tm, d), lambda i: (i, ga_col)),
                  pl.BlockSpec((tm, d), lambda i: (i, gb_col)),
                  pl.BlockSpec((tm, d), lambda i: (i, 0)),
                  pl.BlockSpec((None, 6, d), lambda i: (i // per_batch, 0, 0)),
                  pl.BlockSpec((w, d), lambda i: (0, 0), **const),
                  pl.BlockSpec((w, d), lambda i: (0, 0), **const),
                  pl.BlockSpec((d, d), lambda i: (0, 0), **const),
                  pl.BlockSpec((1, d), lambda i: (0, 0)),
                  pl.BlockSpec((d, 256), lambda i: (0, 0)),
                  pl.BlockSpec((N_EXPERTS, 1), lambda i: (0, 0))],
        out_specs=[pl.BlockSpec((tm, d), lambda i: (i, 0)),
                   pl.BlockSpec((tm, d // 2), lambda i: (i, 0)),
                   pl.BlockSpec((8, tm), lambda i: (0, i))],
        out_shape=[jax.ShapeDtypeStruct((t, d), F32), jax.ShapeDtypeStruct((t, d // 2), jnp.uint32),
                   jax.ShapeDtypeStruct((8, t), F32)],
        compiler_params=_cparams(("parallel",), 52),
    )(y_lin, proj, d_skip, w_glu, b_glu, yb, proj, proj, x2, modl, wa, wb, wo, nw2, wr_t, rb)


def _moe_kernel(te_ref, nu_ref, x_ref, wg_ref, wu_ref, wd_ref, o_ref):
    used = pl.program_id(0) < nu_ref[0]

    @pl.when(used)
    def _():
        x_lo, x_hi = _unpack_halves(x_ref[...])
        x_lo = x_lo.astype(BF16)
        x_hi = x_hi.astype(BF16)
        half = x_lo.shape[1]

        def up(w_ref):
            return (jnp.dot(x_lo, w_ref[:half, :], preferred_element_type=F32)
                    + jnp.dot(x_hi, w_ref[half:, :], preferred_element_type=F32))

        hid = (_silu(up(wg_ref)) * up(wu_ref)).astype(BF16)
        o_ref[...] = _pack_halves(jnp.dot(hid, wd_ref[...], preferred_element_type=F32))

    @pl.when(jnp.logical_not(used))
    def _():
        o_ref[...] = jnp.zeros_like(o_ref)


def _cast_kernel(w_ref, after_ref, o_ref, done_ref):
    del after_ref
    o_ref[...] = w_ref[...].astype(o_ref.dtype)
    done_ref[...] = jnp.zeros_like(done_ref)


def _expert_weights_bf16(w, layer, after):
    _, e, a, b = w.shape
    ta = min(a, (1 << 20) // b)
    return pl.pallas_call(
        _cast_kernel,
        grid=(e, a // ta),
        in_specs=[pl.BlockSpec((None, None, ta, b), lambda i, j: (layer, i, j, 0)),
                  pl.BlockSpec((8, 128), lambda i, j: (0, 0))],
        out_specs=[pl.BlockSpec((None, ta, b), lambda i, j: (i, j, 0)),
                   pl.BlockSpec((8, 128), lambda i, j: (0, 0))],
        out_shape=[jax.ShapeDtypeStruct((e, a, b), BF16), jax.ShapeDtypeStruct((8, 128), F32)],
        compiler_params=_cparams(("arbitrary", "arbitrary"), 32),
    )(w, after)


def _moe_experts(tile_expert, n_used, xs, wg, wu, wd, tm):
    p, dh = xs.shape
    d = 2 * dh
    f = wg.shape[-1]
    grid_spec = pltpu.PrefetchScalarGridSpec(
        num_scalar_prefetch=2,
        grid=(p // tm,),
        in_specs=[pl.BlockSpec((tm, dh), lambda i, te, nu: (i, 0)),
                  pl.BlockSpec((None, d, f), lambda i, te, nu: (te[i], 0, 0)),
                  pl.BlockSpec((None, d, f), lambda i, te, nu: (te[i], 0, 0)),
                  pl.BlockSpec((None, f, d), lambda i, te, nu: (te[i], 0, 0))],
        out_specs=pl.BlockSpec((tm, dh), lambda i, te, nu: (i, 0)),
    )
    return pl.pallas_call(
        _moe_kernel,
        grid_spec=grid_spec,
        out_shape=jax.ShapeDtypeStruct((p, dh), jnp.uint32),
        compiler_params=_cparams(("arbitrary",), 52),
    )(tile_expert, n_used, xs, wg, wu, wd)


def _dispatch_plan(expert_rows, tm, n_tiles):
    flat_e = expert_rows.astype(jnp.int32).reshape(-1)
    onehot = (flat_e[:, None] == jnp.arange(N_EXPERTS, dtype=jnp.int32)[None, :]).astype(jnp.int32)
    csum = jnp.cumsum(onehot, axis=0)
    rank = jnp.sum((csum - onehot) * onehot, axis=1)
    sizes = csum[-1]
    padded = ((sizes + tm - 1) // tm) * tm
    pad_end = jnp.cumsum(padded)
    pad_start = pad_end - padded
    dest = pad_start[flat_e] + rank
    tile_start = jnp.arange(n_tiles, dtype=jnp.int32) * tm
    tile_expert = jnp.minimum(jnp.sum((tile_start[:, None] >= pad_end[None, :]).astype(jnp.int32), axis=1),
                              N_EXPERTS - 1).astype(jnp.int32)
    n_used = (pad_end[-1] // tm).astype(jnp.int32).reshape(1)
    dest = dest.astype(jnp.int32)
    n_slots = flat_e.shape[0]
    tokens = n_slots // TOP_K
    n_fill = n_tiles * tm - n_slots
    fill_count = jnp.concatenate([padded - sizes, (n_tiles * tm - pad_end[-1])[None]])
    fill_end = jnp.cumsum(fill_count)
    first_free = jnp.concatenate([pad_start + sizes, pad_end[-1:]])
    j = jnp.arange(n_fill, dtype=jnp.int32)
    region = (j[:, None] >= fill_end[None, :]).astype(jnp.int32).sum(axis=1)
    shift = first_free - (fill_end - fill_count)
    fill_key = j + jnp.sum((region[:, None] == jnp.arange(N_EXPERTS + 1)[None, :]) * shift[None, :], axis=1)
    keys = jnp.concatenate([dest, fill_key.astype(jnp.int32)])
    vals = jnp.concatenate([jnp.arange(n_slots, dtype=jnp.int32) % tokens, j % tokens])
    _, src_tok = lax.sort_key_val(keys, vals)
    return dest, src_tok, tile_expert, n_used


def _combine_kernel(x_ref, y0_ref, y1_ref, gw_ref, mod_ref, fw_ref, *rest, final_norm):
    o_ref = rest[-1]
    gw = gw_ref[...].T[:, TOP_K:2 * TOP_K]
    y0_lo, y0_hi = _unpack_halves(y0_ref[...])
    y1_lo, y1_hi = _unpack_halves(y1_ref[...])
    moe = jnp.concatenate([gw[:, 0:1] * y0_lo + gw[:, 1:2] * y1_lo, gw[:, 0:1] * y0_hi + gw[:, 1:2] * y1_hi], axis=1)
    xn = x_ref[...] + mod_ref[5:6, :] * moe
    if final_norm:
        xn = xn * _rms_scale(xn) * fw_ref[...]
    o_ref[...] = xn


def _combine(x2, y_pairs, gate_w, modl, final_w, seq, final_norm, ties=()):
    t, d = x2.shape
    tm = min(seq, 512)
    per_batch = seq // tm
    nblk = t // tm
    return pl.pallas_call(
        functools.partial(_combine_kernel, final_norm=final_norm),
        grid=(nblk,),
        in_specs=[pl.BlockSpec((tm, d), lambda i: (i, 0)),
                  pl.BlockSpec((tm, d // 2), lambda i: (i, 0)),
                  pl.BlockSpec((tm, d // 2), lambda i: (i + nblk, 0)),
                  pl.BlockSpec((8, tm), lambda i: (0, i)),
                  pl.BlockSpec((None, 6, d), lambda i: (i // per_batch, 0, 0)),
                  pl.BlockSpec((1, d), lambda i: (0, 0))] + [pl.BlockSpec((8, 128), lambda i: (0, 0)) for _ in ties],
        out_specs=pl.BlockSpec((tm, d), lambda i: (i, 0)),
        out_shape=jax.ShapeDtypeStruct((t, d), F32),
        compiler_params=_cparams(("parallel",), 40),
    )(x2, y_pairs, y_pairs, gate_w, modl, final_w, *ties)


def kernel(x, c, w_ada, b_ada, norm1_w, norm2_w, w_in, s5_a_re, s5_a_im, s5_log_dt, s5_b_re, s5_b_im, s5_c_re, s5_c_im, s5_d, s5_w_glu, s5_b_glu, gdn_conv_w, gdn_a_log, gdn_dt_bias, gdn_norm_w, w_proj_a, w_proj_b, w_out, w_router, router_bias, w_gate, w_up, w_down, final_norm_w):
    batch, seq, d = x.shape
    depth = w_ada.shape[0]
    t = batch * seq
    s5w = s5_d.shape[1]
    gdw = w_proj_b.shape[1]
    heads = gdn_a_log.shape[1]
    assert gdw == heads * HEAD_DIM and s5w == gdw and d == 2 * gdw
    assert seq % GDN_TILE == 0 and seq % S5_CHUNK == 0
    n_levels = int(math.log2(seq // S5_CHUNK))
    assert S5_CHUNK << n_levels == seq and S5_OCT * S5_GROUP == 128

    c_pad = jnp.zeros((8, d), F32).at[:batch].set(c)
    mod = _ada(c_pad, w_ada, b_ada)

    o_u, o_q, o_k, o_v, o_z = 0, s5w, 2 * s5w, 3 * s5w, 4 * s5w
    o_ba = 5 * s5w
    o_ga = o_ba + 2 * heads
    o_gb = o_ga + d
    col_u, col_q, col_k, col_v, col_z = 4, 5, 6, 7, 8

    gdn_masks = _gdn_masks()
    s5_rep = _s5_replicate()
    s5_ops = jax.vmap(functools.partial(_s5_weights, n_levels=n_levels))(
        s5_a_re, s5_a_im, s5_log_dt, s5_b_re, s5_b_im, s5_c_re, s5_c_im)

    wr_hi = w_router.astype(BF16)
    wr_lo = (w_router - wr_hi.astype(F32)).astype(BF16)
    pad_e = ((0, 0), (0, 128 - N_EXPERTS))
    wr_t = jnp.concatenate([jnp.pad(wr_hi, pad_e), jnp.pad(wr_lo, pad_e)], axis=1)
    rb = router_bias.reshape(N_EXPERTS, 1).astype(F32)

    tm_e = min(512, t)
    n_tiles = (TOP_K * t) // tm_e + N_EXPERTS

    w_in_t = jnp.transpose(w_in, (0, 2, 1))
    w_big = _repack_in_proj(w_in_t, o_ga, 2 * d, o_ba)
    w_small = jnp.pad(w_in_t[:, o_ba:o_ga], ((0, 0), (0, 128 - 2 * heads), (0, 0))).astype(BF16)

    x2 = x.reshape(t, d)
    for l in range(depth):
        modl = mod[l, :batch].reshape(batch, 6, d)
        proj, small = _inproj(x2, modl, norm1_w[l].reshape(1, d), w_big, w_small, l, seq)

        y_lin = _s5_core(proj, col_u * s5w // 128, *s5_ops, s5_rep, l, batch, seq)
        s5_tail = (s5_d[l].reshape(1, s5w), s5_w_glu[l].astype(BF16), s5_b_glu[l].reshape(1, s5w))

        head_params = jnp.zeros((8, 128), F32)
        head_params = head_params.at[0, heads:2 * heads].set(-jnp.exp(gdn_a_log[l]))
        head_params = head_params.at[1, heads:2 * heads].set(gdn_dt_bias[l])
        y_b = _gdn(proj, small, gdn_conv_w[l], head_params, gdn_norm_w[l].reshape(1, HEAD_DIM), gdn_masks,
                   (col_q, col_k, col_v, col_z), batch, seq, heads)

        x2, h2, route = _merge(y_lin, y_b, proj, x2, modl, s5_tail, w_proj_a[l].astype(BF16),
                               w_proj_b[l].astype(BF16), w_out[l].astype(BF16), norm2_w[l].reshape(1, d), wr_t, rb,
                               seq, col_u, 0, 1)

        dest, src_tok, tile_expert, n_used = _dispatch_plan(route[0:TOP_K], tm_e, n_tiles)
        xs = jnp.take(h2, src_tok, axis=0, mode='clip')
        if l == 0:
            (wg_l, _), (wu_l, _) = _expert_weights_bf16(w_gate, 0, h2), _expert_weights_bf16(w_up, 0, h2)
        wd_l, _ = _expert_weights_bf16(w_down, l, h2)
        ys = _moe_experts(tile_expert, n_used, xs, wg_l, wu_l, wd_l, tm_e)
        y_pairs = jnp.take(ys, dest, axis=0, mode='clip')
        ties = ()
        if l + 1 < depth:
            (wg_l, tie_g), (wu_l, tie_u) = _expert_weights_bf16(w_gate, l + 1, ys), _expert_weights_bf16(w_up, l + 1, ys)
            ties = (tie_g, tie_u)
        x2 = _combine(x2, y_pairs, route, modl, final_norm_w.reshape(1, d), seq, l == depth - 1, ties)

    return x2.reshape(batch, seq, d)
```

```python
import functools
import math

import jax
import jax.numpy as jnp
from jax import lax
from jax.experimental import pallas as pl
from jax.experimental.pallas import tpu as pltpu

F32 = jnp.float32
BF16 = jnp.bfloat16
HIGHEST = lax.Precision.HIGHEST

RMS_EPS = 1e-6
S5_GROUP = 16
S5_STATE = 64
S5_CHUNK = 16
S5_OCT = 8
HEAD_DIM = 128
GDN_CHUNK = 64
GDN_TILE = 256
GDN_HEADS_PER_TRIP = 8
CONV_K = 4
N_EXPERTS = 16
N_GROUPS = 4
EXPERTS_PER_GROUP = N_EXPERTS // N_GROUPS
TOP_K = 2
MIB = 1024 * 1024


def _cparams(semantics, vmem_mib):
    return pltpu.CompilerParams(dimension_semantics=semantics, vmem_limit_bytes=vmem_mib * MIB)


def _silu(x):
    return x * jax.nn.sigmoid(x)


def _softplus(x):
    return jnp.maximum(x, 0.0) + jnp.log(1.0 + jnp.exp(-jnp.abs(x)))


def _rms_scale(x):
    return lax.rsqrt(jnp.mean(x * x, axis=-1, keepdims=True) + RMS_EPS)


def _pack_halves(x):
    n = x.shape[1] // 2
    lo = lax.bitcast_convert_type(x[:, :n].astype(BF16).astype(F32), jnp.uint32)
    hi = lax.bitcast_convert_type(x[:, n:].astype(BF16).astype(F32), jnp.uint32)
    return (lo >> 16) | hi


def _unpack_halves(p):
    lo = lax.bitcast_convert_type(p << 16, F32)
    hi = lax.bitcast_convert_type(p & jnp.uint32(0xFFFF0000), F32)
    return lo, hi


def _nt_dot(a, b):
    return lax.dot_general(a, b, (((1,), (1,)), ((), ())), preferred_element_type=F32)


def _tn_dot(a, b):
    return lax.dot_general(a, b, (((0,), (0,)), ((), ())), preferred_element_type=F32)


def _ada_kernel(c_ref, w_ref, b_ref, o_ref):
    ca = _silu(c_ref[...]).astype(BF16)
    o_ref[...] = jnp.dot(ca, w_ref[...].astype(BF16), preferred_element_type=F32) + b_ref[...]


def _ada(c_pad, w_ada, b_ada):
    depth, d, n = w_ada.shape
    rows = c_pad.shape[0]
    tn = min(n, 1024)
    return pl.pallas_call(
        _ada_kernel,
        grid=(depth, n // tn),
        in_specs=[pl.BlockSpec((rows, d), lambda l, j: (0, 0)),
                  pl.BlockSpec((None, d, tn), lambda l, j: (l, 0, j)),
                  pl.BlockSpec((None, 1, tn), lambda l, j: (l, 0, j))],
        out_specs=pl.BlockSpec((None, rows, tn), lambda l, j: (l, 0, j)),
        out_shape=jax.ShapeDtypeStruct((depth, rows, n), F32),
        compiler_params=_cparams(("parallel", "parallel"), 40),
    )(c_pad, w_ada, b_ada.reshape(depth, 1, n))


def _inproj_kernel(x_ref, mod_ref, nw_ref, w_ref, wsm_ref, o_ref, osm_ref, h_scr):
    @pl.when(pl.program_id(1) == 0)
    def _():
        x = x_ref[...]
        h = x * _rms_scale(x) * nw_ref[...] * (1.0 + mod_ref[1:2, :]) + mod_ref[0:1, :]
        hb = h.astype(BF16)
        h_scr[...] = hb
        osm_ref[...] = _nt_dot(hb, wsm_ref[...])

    o_ref[...] = _nt_dot(h_scr[...], w_ref[...])


def _inproj(x2, modl, nw, w_big, w_small, layer, seq):
    t, d = x2.shape
    n = w_big.shape[1]
    tm = min(seq, 1024)
    tn = 1536
    per_batch = seq // tm
    return pl.pallas_call(
        _inproj_kernel,
        grid=(t // tm, n // tn),
        in_specs=[pl.BlockSpec((tm, d), lambda i, j: (i, 0)),
                  pl.BlockSpec((None, 6, d), lambda i, j: (i // per_batch, 0, 0)),
                  pl.BlockSpec((1, d), lambda i, j: (0, 0)),
                  pl.BlockSpec((None, tn, d), lambda i, j: (layer, j, 0)),
                  pl.BlockSpec((None, 128, d), lambda i, j: (layer, 0, 0))],
        out_specs=[pl.BlockSpec((tm, tn), lambda i, j: (i, j)),
                   pl.BlockSpec((tm, 128), lambda i, j: (i, 0))],
        out_shape=[jax.ShapeDtypeStruct((t, n), F32), jax.ShapeDtypeStruct((t, 128), F32)],
        scratch_shapes=[pltpu.VMEM((tm, d), BF16)],
        compiler_params=_cparams(("parallel", "arbitrary"), 52),
    )(x2, modl, nw, w_big, w_small)


def _repack_kernel(a_ref, b_ref, o_ref, *, gate_blocks, shift):
    c = pl.program_id(1)

    @pl.when(c < gate_blocks)
    def _():
        x = jnp.concatenate([a_ref[...], b_ref[...]], axis=0)
        o_ref[...] = x[shift:shift + o_ref.shape[0]].astype(o_ref.dtype)

    @pl.when(c >= gate_blocks)
    def _():
        o_ref[...] = a_ref[...].astype(o_ref.dtype)


def _repack_in_proj(w_in_t, gate_start, gate_cols, lead_cols):
    depth, _, d = w_in_t.shape
    rb = 512
    shift = gate_start % rb
    base = gate_start - shift
    assert shift % 16 == 0 and gate_cols % rb == 0 and lead_cols % rb == 0 and lead_cols == base
    gate_blocks = gate_cols // rb
    kern = functools.partial(_repack_kernel, gate_blocks=gate_blocks, shift=shift)
    return pl.pallas_call(
        kern,
        grid=(depth, (gate_cols + lead_cols) // rb),
        in_specs=[pl.BlockSpec((None, rb, d),
                               lambda l, c: (l, jnp.where(c < gate_blocks, base // rb + c, c - gate_blocks), 0)),
                  pl.BlockSpec((None, shift, d),
                               lambda l, c: (l, jnp.where(c < gate_blocks, (base + (c + 1) * rb) // shift, 0), 0))],
        out_specs=pl.BlockSpec((None, rb, d), lambda l, c: (l, c, 0)),
        out_shape=jax.ShapeDtypeStruct((depth, gate_cols + lead_cols, d), BF16),
        compiler_params=_cparams(("parallel", "arbitrary"), 32),
    )(w_in_t, w_in_t)


def _s5_weights(a_re, a_im, log_dt, b_re, b_im, c_re, c_im, n_levels):
    tc = S5_CHUNK
    groups, p = a_re.shape
    n = b_re.shape[-1]
    oc = S5_OCT
    octs = groups // oc
    lam = lax.complex(a_re, a_im)
    dt = jnp.exp(log_dt)[:, None]
    log_a = lam * dt
    a_bar = jnp.exp(log_a)
    b_bar = ((a_bar - 1.0) / lam)[..., None] * lax.complex(b_re, b_im)
    c_mat = lax.complex(c_re, c_im)
    apow = jnp.exp(log_a[:, :, None] * jnp.arange(tc + 1, dtype=F32)[None, None, :])
    lane = jnp.arange(tc * n)
    e_t = (jnp.arange(tc)[:, None] == (lane // n)[None, :]).astype(F32)
    e_n = (jnp.arange(n)[:, None] == (lane % n)[None, :]).astype(F32)

    def expand(z, e):
        f = lambda r: jnp.einsum('gpk,kj->gpj', r, e, precision=HIGHEST)
        return lax.complex(f(jnp.real(z)), f(jnp.imag(z)))

    def re_negim_rows(z):
        return jnp.concatenate([jnp.real(z), -jnp.imag(z)], axis=1)

    c_rep = expand(jnp.transpose(c_mat, (0, 2, 1)), e_n)
    lag_c = re_negim_rows(expand(apow[:, :, :tc], e_t) * c_rep)
    hoc = re_negim_rows(expand(apow[:, :, 1:], e_t) * c_rep)
    b_t = jnp.transpose(b_bar, (0, 2, 1))
    b_cat = jnp.concatenate([jnp.real(b_t), jnp.imag(b_t)], axis=-1)
    kern = jnp.einsum('gnk,gkj->gnj', b_cat, lag_c, precision=HIGHEST)
    tcomp = jnp.stack([jnp.pad(kern[:, :, :(tc - s) * n], ((0, 0), (0, 0), (s * n, 0))) for s in range(tc)],
                      axis=1)
    tcomp = jnp.transpose(tcomp.reshape(octs, oc, tc, n, tc * n), (0, 2, 1, 3, 4)).reshape(octs, tc * oc * n, tc * n)
    ap_rev = jnp.transpose(apow[:, :, tc - 1 - jnp.arange(tc)], (0, 2, 1))
    hin_c = ap_rev[:, :, None, :] * b_t[:, None, :, :]
    hc = jnp.concatenate([jnp.real(hin_c), jnp.imag(hin_c)], axis=-1)
    hc = jnp.transpose(hc.reshape(octs, oc, tc, n, 2 * p), (0, 2, 1, 3, 4)).reshape(octs, tc * oc * n, 2 * p)
    hoc = hoc.reshape(octs, oc * 2 * p, tc * n)
    steps = (tc * (2.0 ** jnp.arange(n_levels, dtype=F32)))
    amul = jnp.exp(log_a[:, None, :] * steps[None, :, None])
    a1 = jnp.concatenate([jnp.real(amul), jnp.real(amul)], axis=-1)
    a2 = jnp.concatenate([-jnp.imag(amul), jnp.imag(amul)], axis=-1)

    def lanes_by_group(a):
        return jnp.transpose(a.reshape(octs, oc, n_levels, 2 * p), (0, 2, 1, 3)).reshape(octs, n_levels, oc * 2 * p)

    return (tcomp.astype(BF16), hc.astype(BF16), hoc.astype(BF16),
            lanes_by_group(a1).astype(F32), lanes_by_group(a2).astype(F32))


def _s5_replicate():
    r = lax.broadcasted_iota(jnp.int32, (S5_CHUNK * S5_GROUP, S5_CHUNK * 128), 0)
    c = lax.broadcasted_iota(jnp.int32, (S5_CHUNK * S5_GROUP, S5_CHUNK * 128), 1)
    return ((r // S5_GROUP == c // 128) & (r % S5_GROUP == c % S5_GROUP)).astype(BF16)


def _s5_core_kernel(u_ref, tc_ref, hc_ref, hoc_ref, a1_ref, a2_ref, rep_ref, y_ref, toep_scr, hin_scr, hout_scr,
                    *, n_levels):
    tc = S5_CHUNK
    lanes = u_ref.shape[1]
    state_lanes = hc_ref.shape[1]

    @pl.when(pl.program_id(1) == 0)
    def _():
        kd = toep_scr.shape[1]
        rb = 256
        col_g = (lax.broadcasted_iota(jnp.int32, (rb, kd), 1) // S5_GROUP) % S5_OCT
        row_i = lax.broadcasted_iota(jnp.int32, (rb, kd), 0)
        for i in range(toep_scr.shape[0] // rb):
            full = jnp.dot(tc_ref[i * rb:(i + 1) * rb, :], rep_ref[...], preferred_element_type=F32)
            row_g = ((row_i + i * rb) // S5_GROUP) % S5_OCT
            toep_scr[i * rb:(i + 1) * rb, :] = jnp.where(row_g == col_g, full, 0.0).astype(BF16)
        for i in range(hout_scr.shape[0] // rb):
            full = jnp.dot(hoc_ref[i * rb:(i + 1) * rb, :], rep_ref[...], preferred_element_type=F32)
            row_g = (row_i + i * rb) // state_lanes
            hout_scr[i * rb:(i + 1) * rb, :] = jnp.where(row_g == col_g, full, 0.0).astype(BF16)
        hrow_g = (lax.broadcasted_iota(jnp.int32, hc_ref.shape, 0) // S5_GROUP) % S5_OCT
        for g in range(S5_OCT):
            hin_scr[:, g * state_lanes:(g + 1) * state_lanes] = jnp.where(hrow_g == g, hc_ref[...], 0.0).astype(BF16)

    rows = u_ref.shape[0] // tc
    x = jnp.concatenate([u_ref[pl.ds(t, rows, stride=tc), :] for t in range(tc)], axis=1).astype(BF16)
    s_loc = jnp.dot(x, hin_scr[...], preferred_element_type=F32)
    pos = lax.broadcasted_iota(jnp.int32, (rows, state_lanes), 0)
    s_in = []
    for g in range(S5_OCT):
        ln = slice(g * state_lanes, (g + 1) * state_lanes)
        s = s_loc[:, ln]
        for k in range(n_levels):
            d = 1 << k
            prev = jnp.where(pos >= d, pltpu.roll(s, d, axis=0), 0.0)
            s = s + a1_ref[k:k + 1, ln] * prev + a2_ref[k:k + 1, ln] * pltpu.roll(prev, state_lanes // 2, axis=1)
        s_in.append(jnp.where(pos >= 1, pltpu.roll(s, 1, axis=0), 0.0).astype(BF16))
    y = (jnp.dot(x, toep_scr[...], preferred_element_type=F32)
         + jnp.dot(jnp.concatenate(s_in, axis=1), hout_scr[...], preferred_element_type=F32))
    for t in range(tc):
        y_ref[pl.ds(t, rows, stride=tc), :] = y[:, t * lanes:(t + 1) * lanes]


def _s5_core(proj, u_lane_block, tcomp, hc, hoc, a1, a2, rep, layer, batch, seq):
    t = proj.shape[0]
    _, octs, kdim, tn = tcomp.shape
    sl = hc.shape[-1]
    sw = hoc.shape[2]
    n_levels = a1.shape[2]
    lanes = kdim // S5_CHUNK
    kern = functools.partial(_s5_core_kernel, n_levels=n_levels)
    return pl.pallas_call(
        kern,
        grid=(octs, batch),
        in_specs=[pl.BlockSpec((seq, lanes), lambda o, b: (b, u_lane_block + o)),
                  pl.BlockSpec((None, None, kdim, tn), lambda o, b: (layer, o, 0, 0)),
                  pl.BlockSpec((None, None, kdim, sl), lambda o, b: (layer, o, 0, 0)),
                  pl.BlockSpec((None, None, sw, tn), lambda o, b: (layer, o, 0, 0)),
                  pl.BlockSpec((None, None, n_levels, sw), lambda o, b: (layer, o, 0, 0)),
                  pl.BlockSpec((None, None, n_levels, sw), lambda o, b: (layer, o, 0, 0)),
                  pl.BlockSpec((tn, kdim), lambda o, b: (0, 0))],
        out_specs=pl.BlockSpec((seq, lanes), lambda o, b: (b, o)),
        out_shape=jax.ShapeDtypeStruct((t, octs * lanes), F32),
        scratch_shapes=[pltpu.VMEM((kdim, kdim), BF16), pltpu.VMEM((kdim, sw), BF16), pltpu.VMEM((sw, kdim), BF16)],
        compiler_params=_cparams(("parallel", "arbitrary"), 48),
    )(proj, tcomp, hc, hoc, a1, a2, rep)


def _gdn_kernel(q_ref, k_ref, v_ref, qh_ref, kh_ref, vh_ref, z_ref, sm_ref, cq_ref, ck_ref, cv_ref,
                hp_ref, nw_ref, tril_ref, cmask_ref, lmask_ref, o_ref, betab, gcb, grow, st_scr,
                cat_q, cat_k, cat_v, *, heads):
    tt = GDN_TILE
    hd = HEAD_DIM
    nck = tt // GDN_CHUNK
    first = pl.program_id(1) == 0

    @pl.when(first)
    def _():
        st_scr[...] = jnp.zeros_like(st_scr)

    sm = sm_ref[...]
    hp = hp_ref[...]
    beta_all = jax.nn.sigmoid(sm)
    g_all = hp[0:1, :] * _softplus(sm + hp[1:2, :])
    gc_all = jnp.dot(tril_ref[...], g_all, precision=HIGHEST, preferred_element_type=F32)
    gc_t = gc_all.T
    for h in range(heads):
        betab[h] = jnp.broadcast_to(beta_all[:, h:h + 1], (tt, hd))
        gcb[h] = jnp.broadcast_to(gc_all[:, heads + h:heads + h + 1], (tt, hd))
        grow[h:h + 1, :] = gc_t[heads + h:heads + h + 1, :]

    eye = (lax.broadcasted_iota(jnp.int32, (tt, tt), 0) == lax.broadcasted_iota(jnp.int32, (tt, tt), 1)).astype(F32)
    n_lv = lmask_ref.shape[0]

    for x_ref, halo_ref, cat in ((q_ref, qh_ref, cat_q), (k_ref, kh_ref, cat_k), (v_ref, vh_ref, cat_v)):
        cat[0:8, :] = jnp.where(first, 0.0, halo_ref[...])
        cat[8:8 + tt, :] = x_ref[...]

    def conv_silu(cat, cw_ref, hs):
        w = cw_ref[:, pl.ds(hs, hd)]
        acc = cat[pl.ds(8, tt), pl.ds(hs, hd)] * w[CONV_K - 1:CONV_K, :]
        for s in range(1, CONV_K):
            acc = acc + cat[pl.ds(8 - s, tt), pl.ds(hs, hd)] * w[CONV_K - 1 - s:CONV_K - s, :]
        return _silu(acc)

    def heads_lockstep(hs_idx):
        each = lambda f, *ls: [f(*a) for a in zip(*ls)]
        hs = [pl.multiple_of(h * hd, hd) for h in hs_idx]
        q = [conv_silu(cat_q, cq_ref, o) for o in hs]
        k = [conv_silu(cat_k, ck_ref, o) for o in hs]
        v = [conv_silu(cat_v, cv_ref, o) for o in hs]
        qn = each(lambda a: a * (lax.rsqrt(jnp.sum(a * a, axis=-1, keepdims=True) + RMS_EPS) * (hd ** -0.5)), q)
        kn = each(lambda a: a * lax.rsqrt(jnp.sum(a * a, axis=-1, keepdims=True) + RMS_EPS), k)
        bb = [betab[h] for h in hs_idx]
        gc = [gcb[h] for h in hs_idx]
        gr = [grow[pl.ds(h, 1), :] for h in hs_idx]
        decay = each(lambda c, r: jnp.exp(jnp.where(cmask_ref[0] > 0.0, jnp.concatenate([c, c], axis=1) - r, -1e30)),
                     gc, gr)
        kb = each(lambda a, b: a * b, kn, bb)
        knb = each(lambda a: a.astype(BF16), kn)
        lfull = each(lambda a, b, dc: _nt_dot(a.astype(BF16), b) * dc, kb, knb, decay)
        lmat = each(lambda a: a.astype(BF16), lfull)
        tinv = each(lambda a: eye - a * cmask_ref[1], lfull)
        for lv in range(1, n_lv):
            d = 1 << lv
            nb = tt // d
            tb = each(lambda a: a.astype(BF16), tinv)
            bd = each(lambda m: m * lmask_ref[lv], lmat)
            if d % 8 == 0:
                odd = each(lambda a: jnp.concatenate([a[i * d:(i + 1) * d] for i in range(1, nb, 2)], axis=0), tinv)
                tl = each(lambda a, b: jnp.dot(a.astype(BF16), b, preferred_element_type=F32).astype(BF16), odd, bd)
                upd = each(lambda a, b, c: a - jnp.dot(b, c, preferred_element_type=F32), odd, tl, tb)
                tinv = each(lambda a, u: jnp.concatenate(
                    [a[i * d:(i + 1) * d] if i % 2 == 0 else u[(i // 2) * d:(i // 2 + 1) * d] for i in range(nb)],
                    axis=0), tinv, upd)
            else:
                tl = each(lambda a, b: jnp.dot(a, b, preferred_element_type=F32).astype(BF16), tb, bd)
                tinv = each(lambda a, b, c: a - jnp.dot(b, c, preferred_element_type=F32), tinv, tl, tb)
        eg = each(jnp.exp, gc)
        rhs = each(lambda a, e, b, c: jnp.concatenate([a * e, b * c], axis=1).astype(BF16), kb, eg, v, bb)
        wu = each(lambda a, b: jnp.dot(a.astype(BF16), b, preferred_element_type=F32).astype(BF16), tinv, rhs)
        attn = each(lambda a, b, dc: (_nt_dot(a.astype(BF16), b) * dc).astype(BF16), qn, knb, decay)
        awu = each(lambda a, b: jnp.dot(a, b, preferred_element_type=F32), attn, wu)
        qp = each(lambda a, e, b: (a * e - b[:, :hd]).astype(BF16), qn, eg, awu)
        s = [st_scr[h] for h in hs_idx]
        outs = [[] for _ in hs_idx]
        for c in range(nck):
            r0 = c * GDN_CHUNK
            r1 = r0 + GDN_CHUNK
            gl = each(lambda a: a[r1 - 1:r1, :], gc)
            kd = each(lambda a, l, g: (a[r0:r1] * jnp.exp(l - g[r0:r1])).astype(BF16), kn, gl, gc)
            m = each(lambda a, b: _tn_dot(a, b[r0:r1]), kd, wu)
            sb = each(lambda a: a.astype(BF16), s)
            for i, (a, b, y) in enumerate(zip(qp, sb, awu)):
                outs[i].append(jnp.dot(a[r0:r1], b, preferred_element_type=F32) + y[r0:r1, hd:])
            s = each(lambda a, l, mm, b: a * jnp.exp(l) - jnp.dot(mm[:, :hd].astype(BF16), b,
                                                                  preferred_element_type=F32) + mm[:, hd:],
                     s, gl, m, sb)
        for i, h in enumerate(hs_idx):
            st_scr[h] = s[i]
            o = jnp.concatenate(outs[i], axis=0)
            z = z_ref[:, pl.ds(hs[i], hd)]
            o_ref[:, pl.ds(hs[i], hd)] = (o * _rms_scale(o) * nw_ref[...] * _silu(z)).astype(o_ref.dtype)

    def trip(j, carry):
        heads_lockstep([GDN_HEADS_PER_TRIP * j + i for i in range(GDN_HEADS_PER_TRIP)])
        return carry

    lax.fori_loop(0, heads // GDN_HEADS_PER_TRIP, trip, 0)


def _gdn_masks():
    tt = GDN_TILE
    ri = lax.broadcasted_iota(jnp.int32, (tt, tt), 0)
    ci = lax.broadcasted_iota(jnp.int32, (tt, tt), 1)
    same = (ri // GDN_CHUNK) == (ci // GDN_CHUNK)
    causal = same & (ci <= ri)
    tril = causal.astype(F32)
    cmask = jnp.stack([tril, (((ri % 2) == 1) & (ci == ri - 1)).astype(F32)])
    levels = []
    d = 1
    while d < GDN_CHUNK:
        levels.append((((ri // d) % 2) == 1) & ((ci // d) == (ri // d) - 1))
        d *= 2
    return tril, cmask, jnp.stack(levels).astype(BF16)


def _gdn(proj, small, conv_w, head_params, norm_w, masks, cols, batch, seq, heads):
    t = proj.shape[0]
    tt = GDN_TILE
    width = heads * HEAD_DIM
    tiles = seq // tt
    cq, ck, cv, cz = cols
    tril, cmask, lmask = masks

    def cur(col):
        return pl.BlockSpec((tt, width), lambda b, i: (b * tiles + i, col))

    def halo(col):
        return pl.BlockSpec((8, width), lambda b, i: (jnp.maximum((b * tiles + i) * (tt // 8) - 1, 0), col))

    def cw(col):
        return pl.BlockSpec((CONV_K, width), lambda b, i: (0, col))

    kern = functools.partial(_gdn_kernel, heads=heads)
    return pl.pallas_call(
        kern,
        grid=(batch, tiles),
        in_specs=[cur(cq), cur(ck), cur(cv), halo(cq), halo(ck), halo(cv), cur(cz),
                  pl.BlockSpec((tt, 128), lambda b, i: (b * tiles + i, 0)),
                  cw(0), cw(1), cw(2),
                  pl.BlockSpec((8, 128), lambda b, i: (0, 0)),
                  pl.BlockSpec((1, HEAD_DIM), lambda b, i: (0, 0)),
                  pl.BlockSpec((tt, tt), lambda b, i: (0, 0)),
                  pl.BlockSpec(cmask.shape, lambda b, i: (0, 0, 0)),
                  pl.BlockSpec(lmask.shape, lambda b, i: (0, 0, 0))],
        out_specs=pl.BlockSpec((tt, width), lambda b, i: (b * tiles + i, 0)),
        out_shape=jax.ShapeDtypeStruct((t, width), BF16),
        scratch_shapes=[pltpu.VMEM((heads, tt, HEAD_DIM), F32),
                        pltpu.VMEM((heads, tt, HEAD_DIM), F32),
                        pltpu.VMEM((8, tt), F32),
                        pltpu.VMEM((heads, HEAD_DIM, HEAD_DIM), F32),
                        pltpu.VMEM((tt + 8, width), F32),
                        pltpu.VMEM((tt + 8, width), F32),
                        pltpu.VMEM((tt + 8, width), F32)],
        compiler_params=_cparams(("parallel", "arbitrary"), 40),
    )(proj, proj, proj, proj, proj, proj, proj, small, conv_w, conv_w, conv_w, head_params, norm_w,
      tril, cmask, lmask)


def _route(h, wr_hl, rb):
    h_hi = h.astype(BF16)
    h_lo = (h - h_hi.astype(F32)).astype(BF16)
    first = jnp.dot(h_hi, wr_hl[...], preferred_element_type=F32)
    logits = first[:, :128] + first[:, 128:] + jnp.dot(h_lo, wr_hl[:, :128], preferred_element_type=F32)
    logits = logits.T[:N_EXPERTS]
    scores = jax.nn.sigmoid(logits)
    sel = scores + rb
    sel_rows = [sel[e:e + 1, :] for e in range(N_EXPERTS)]
    score_rows = [scores[e:e + 1, :] for e in range(N_EXPERTS)]
    epg = EXPERTS_PER_GROUP
    best_score = None
    best_group = None
    for g in range(N_GROUPS):
        rows = sel_rows[g * epg:(g + 1) * epg]
        gs = None
        for a in range(epg):
            for b in range(a + 1, epg):
                pair = rows[a] + rows[b]
                gs = pair if gs is None else jnp.maximum(gs, pair)
        if g == 0:
            best_score, best_group = gs, jnp.zeros_like(gs)
        else:
            better = gs > best_score
            best_group = jnp.where(better, float(g), best_group)
            best_score = jnp.where(better, gs, best_score)
    in_sel = []
    in_score = []
    for j in range(epg):
        a = sel_rows[j]
        b = score_rows[j]
        for g in range(1, N_GROUPS):
            pick = best_group == float(g)
            a = jnp.where(pick, sel_rows[g * epg + j], a)
            b = jnp.where(pick, score_rows[g * epg + j], b)
        in_sel.append(a)
        in_score.append(b)
    i1, m1, w1 = jnp.zeros_like(in_sel[0]), in_sel[0], in_score[0]
    for j in range(1, epg):
        better = in_sel[j] > m1
        i1 = jnp.where(better, float(j), i1)
        m1 = jnp.where(better, in_sel[j], m1)
        w1 = jnp.where(better, in_score[j], w1)
    i2 = m2 = w2 = None
    for j in range(epg):
        cand = jnp.where(i1 == float(j), -jnp.inf, in_sel[j])
        if j == 0:
            i2, m2, w2 = jnp.zeros_like(cand), cand, in_score[0]
        else:
            better = cand > m2
            i2 = jnp.where(better, float(j), i2)
            m2 = jnp.where(better, cand, m2)
            w2 = jnp.where(better, in_score[j], w2)
    total = w1 + w2
    zero = jnp.zeros_like(w1)
    return jnp.concatenate([best_group * epg + i1, best_group * epg + i2, w1 / total, w2 / total,
                            zero, zero, zero, zero], axis=0)


def _merge_kernel(yl_ref, u_ref, sd_ref, wglu_ref, bglu_ref, yb_ref, ga_ref, gb_ref, x_ref, mod_ref, wa_ref, wb_ref,
                  wo_ref, nw_ref, wrt_ref, rb_ref, o_ref, h_ref, r_ref):
    y = jax.nn.gelu(yl_ref[...] + sd_ref[...] * u_ref[...])
    glu = jnp.dot(y.astype(BF16), wglu_ref[...], preferred_element_type=F32) + bglu_ref[...]
    ya = (y * jax.nn.sigmoid(glu)).astype(BF16)
    pa = jnp.dot(ya, wa_ref[...], preferred_element_type=F32)
    pb = jnp.dot(yb_ref[...], wb_ref[...], preferred_element_type=F32)
    merged = jax.nn.sigmoid(ga_ref[...]) * pa + jax.nn.sigmoid(gb_ref[...]) * pb
    out = jnp.dot(merged.astype(BF16), wo_ref[...], preferred_element_type=F32)
    xn = x_ref[...] + mod_ref[2:3, :] * out
    o_ref[...] = xn
    h = xn * _rms_scale(xn) * nw_ref[...] * (1.0 + mod_ref[4:5, :]) + mod_ref[3:4, :]
    h_ref[...] = _pack_halves(h)
    r_ref[...] = _route(h, wrt_ref, rb_ref[...])


def _merge(y_lin, yb, proj, x2, modl, s5_tail, wa, wb, wo, nw2, wr_t, rb, layer, seq, u_col, ga_col, gb_col):
    t, d = x2.shape
    w = y_lin.shape[1]
    d_skip, w_glu, b_glu = s5_tail
    tm = min(seq, 256)
    per_batch = seq // tm
    const = dict(pipeline_mode=pl.Buffered(1))
    return pl.pallas_call(
        _merge_kernel,
        grid=(t // tm,),
        in_specs=[pl.BlockSpec((tm, w), lambda i: (i, 0)),
                  pl.BlockSpec((tm, w), lambda i: (i, u_col)),
                  pl.BlockSpec((1, w), lambda i: (0, 0)),
                  pl.BlockSpec((None, w, w), lambda i: (layer, 0, 0), **const),
                  pl.BlockSpec((1, w), lambda i: (0, 0)),
                  pl.BlockSpec((tm, w), lambda i: (i, 0)),
                  pl.BlockSpec((tm, d), lambda i: (i, ga_col)),
                  pl.BlockSpec((tm, d), lambda i: (i, gb_col)),
                  pl.BlockSpec((tm, d), lambda i: (i, 0)),
                  pl.BlockSpec((None, 6, d), lambda i: (i // per_batch, 0, 0)),
                  pl.BlockSpec((None, w, d), lambda i: (layer, 0, 0), **const),
                  pl.BlockSpec((None, w, d), lambda i: (layer, 0, 0), **const),
                  pl.BlockSpec((None, d, d), lambda i: (layer, 0, 0), **const),
                  pl.BlockSpec((1, d), lambda i: (0, 0)),
                  pl.BlockSpec((d, 256), lambda i: (0, 0)),
                  pl.BlockSpec((N_EXPERTS, 1), lambda i: (0, 0))],
        out_specs=[pl.BlockSpec((tm, d), lambda i: (i, 0)),
                   pl.BlockSpec((tm, d // 2), lambda i: (i, 0)),
                   pl.BlockSpec((8, tm), lambda i: (0, i))],
        out_shape=[jax.ShapeDtypeStruct((t, d), F32), jax.ShapeDtypeStruct((t, d // 2), jnp.uint32),
                   jax.ShapeDtypeStruct((8, t), F32)],
        compiler_params=_cparams(("parallel",), 52),
    )(y_lin, proj, d_skip, w_glu, b_glu, yb, proj, proj, x2, modl, wa, wb, wo, nw2, wr_t, rb)


def _moe_kernel(te_ref, nu_ref, x_ref, wg_ref, wu_ref, wd_ref, o_ref):
    used = pl.program_id(0) < nu_ref[0]

    @pl.when(used)
    def _():
        x_lo, x_hi = _unpack_halves(x_ref[...])
        x_lo = x_lo.astype(BF16)
        x_hi = x_hi.astype(BF16)
        half = x_lo.shape[1]

        def up(w_ref):
            return (jnp.dot(x_lo, w_ref[:half, :], preferred_element_type=F32)
                    + jnp.dot(x_hi, w_ref[half:, :], preferred_element_type=F32))

        hid = (_silu(up(wg_ref)) * up(wu_ref)).astype(BF16)
        o_ref[...] = _pack_halves(jnp.dot(hid, wd_ref[...], preferred_element_type=F32))

    @pl.when(jnp.logical_not(used))
    def _():
        o_ref[...] = jnp.zeros_like(o_ref)


def _cast_kernel(w_ref, after_ref, o_ref, done_ref):
    del after_ref
    o_ref[...] = w_ref[...].astype(o_ref.dtype)
    done_ref[...] = jnp.zeros_like(done_ref)


def _expert_weights_bf16(w, layer, after):
    _, e, a, b = w.shape
    ta = min(a, (1 << 20) // b)
    return pl.pallas_call(
        _cast_kernel,
        grid=(e, a // ta),
        in_specs=[pl.BlockSpec((None, None, ta, b), lambda i, j: (layer, i, j, 0)),
                  pl.BlockSpec((8, 128), lambda i, j: (0, 0))],
        out_specs=[pl.BlockSpec((None, ta, b), lambda i, j: (i, j, 0)),
                   pl.BlockSpec((8, 128), lambda i, j: (0, 0))],
        out_shape=[jax.ShapeDtypeStruct((e, a, b), BF16), jax.ShapeDtypeStruct((8, 128), F32)],
        compiler_params=_cparams(("arbitrary", "arbitrary"), 32),
    )(w, after)


def _moe_experts(tile_expert, n_used, xs, wg, wu, wd, tm):
    p, dh = xs.shape
    d = 2 * dh
    f = wg.shape[-1]
    grid_spec = pltpu.PrefetchScalarGridSpec(
        num_scalar_prefetch=2,
        grid=(p // tm,),
        in_specs=[pl.BlockSpec((tm, dh), lambda i, te, nu: (i, 0)),
                  pl.BlockSpec((None, d, f), lambda i, te, nu: (te[i], 0, 0)),
                  pl.BlockSpec((None, d, f), lambda i, te, nu: (te[i], 0, 0)),
                  pl.BlockSpec((None, f, d), lambda i, te, nu: (te[i], 0, 0))],
        out_specs=pl.BlockSpec((tm, dh), lambda i, te, nu: (i, 0)),
    )
    return pl.pallas_call(
        _moe_kernel,
        grid_spec=grid_spec,
        out_shape=jax.ShapeDtypeStruct((p, dh), jnp.uint32),
        compiler_params=_cparams(("arbitrary",), 52),
    )(tile_expert, n_used, xs, wg, wu, wd)


def _dispatch_plan(expert_rows, tm, n_tiles):
    flat_e = expert_rows.astype(jnp.int32).reshape(-1)
    onehot = (flat_e[:, None] == jnp.arange(N_EXPERTS, dtype=jnp.int32)[None, :]).astype(jnp.int32)
    csum = jnp.cumsum(onehot, axis=0)
    rank = jnp.sum((csum - onehot) * onehot, axis=1)
    sizes = csum[-1]
    padded = ((sizes + tm - 1) // tm) * tm
    pad_end = jnp.cumsum(padded)
    pad_start = pad_end - padded
    dest = pad_start[flat_e] + rank
    tile_start = jnp.arange(n_tiles, dtype=jnp.int32) * tm
    tile_expert = jnp.minimum(jnp.sum((tile_start[:, None] >= pad_end[None, :]).astype(jnp.int32), axis=1),
                              N_EXPERTS - 1).astype(jnp.int32)
    n_used = (pad_end[-1] // tm).astype(jnp.int32).reshape(1)
    dest = dest.astype(jnp.int32)
    n_slots = flat_e.shape[0]
    tokens = n_slots // TOP_K
    n_fill = n_tiles * tm - n_slots
    fill_count = jnp.concatenate([padded - sizes, (n_tiles * tm - pad_end[-1])[None]])
    fill_end = jnp.cumsum(fill_count)
    first_free = jnp.concatenate([pad_start + sizes, pad_end[-1:]])
    j = jnp.arange(n_fill, dtype=jnp.int32)
    region = (j[:, None] >= fill_end[None, :]).astype(jnp.int32).sum(axis=1)
    shift = first_free - (fill_end - fill_count)
    fill_key = j + jnp.sum((region[:, None] == jnp.arange(N_EXPERTS + 1)[None, :]) * shift[None, :], axis=1)
    keys = jnp.concatenate([dest, fill_key.astype(jnp.int32)])
    vals = jnp.concatenate([jnp.arange(n_slots, dtype=jnp.int32) % tokens, j % tokens])
    _, src_tok = lax.sort_key_val(keys, vals)
    return dest, src_tok, tile_expert, n_used


def _combine_kernel(x_ref, y0_ref, y1_ref, gw_ref, mod_ref, fw_ref, *rest, final_norm):
    o_ref = rest[-1]
    gw = gw_ref[...].T[:, TOP_K:2 * TOP_K]
    y0_lo, y0_hi = _unpack_halves(y0_ref[...])
    y1_lo, y1_hi = _unpack_halves(y1_ref[...])
    moe = jnp.concatenate([gw[:, 0:1] * y0_lo + gw[:, 1:2] * y1_lo, gw[:, 0:1] * y0_hi + gw[:, 1:2] * y1_hi], axis=1)
    xn = x_ref[...] + mod_ref[5:6, :] * moe
    if final_norm:
        xn = xn * _rms_scale(xn) * fw_ref[...]
    o_ref[...] = xn


def _combine(x2, y_pairs, gate_w, modl, final_w, seq, final_norm, ties=()):
    t, d = x2.shape
    tm = min(seq, 512)
    per_batch = seq // tm
    nblk = t // tm
    return pl.pallas_call(
        functools.partial(_combine_kernel, final_norm=final_norm),
        grid=(nblk,),
        in_specs=[pl.BlockSpec((tm, d), lambda i: (i, 0)),
                  pl.BlockSpec((tm, d // 2), lambda i: (i, 0)),
                  pl.BlockSpec((tm, d // 2), lambda i: (i + nblk, 0)),
                  pl.BlockSpec((8, tm), lambda i: (0, i)),
                  pl.BlockSpec((None, 6, d), lambda i: (i // per_batch, 0, 0)),
                  pl.BlockSpec((1, d), lambda i: (0, 0))] + [pl.BlockSpec((8, 128), lambda i: (0, 0)) for _ in ties],
        out_specs=pl.BlockSpec((tm, d), lambda i: (i, 0)),
        out_shape=jax.ShapeDtypeStruct((t, d), F32),
        compiler_params=_cparams(("parallel",), 40),
    )(x2, y_pairs, y_pairs, gate_w, modl, final_w, *ties)


def kernel(x, c, w_ada, b_ada, norm1_w, norm2_w, w_in, s5_a_re, s5_a_im, s5_log_dt, s5_b_re, s5_b_im, s5_c_re, s5_c_im, s5_d, s5_w_glu, s5_b_glu, gdn_conv_w, gdn_a_log, gdn_dt_bias, gdn_norm_w, w_proj_a, w_proj_b, w_out, w_router, router_bias, w_gate, w_up, w_down, final_norm_w):
    batch, seq, d = x.shape
    depth = w_ada.shape[0]
    t = batch * seq
    s5w = s5_d.shape[1]
    gdw = w_proj_b.shape[1]
    heads = gdn_a_log.shape[1]
    assert gdw == heads * HEAD_DIM and s5w == gdw and d == 2 * gdw
    assert seq % GDN_TILE == 0 and seq % S5_CHUNK == 0
    n_levels = int(math.log2(seq // S5_CHUNK))
    assert S5_CHUNK << n_levels == seq and S5_OCT * S5_GROUP == 128

    c_pad = jnp.zeros((8, d), F32).at[:batch].set(c)
    mod = _ada(c_pad, w_ada, b_ada)

    o_u, o_q, o_k, o_v, o_z = 0, s5w, 2 * s5w, 3 * s5w, 4 * s5w
    o_ba = 5 * s5w
    o_ga = o_ba + 2 * heads
    o_gb = o_ga + d
    col_u, col_q, col_k, col_v, col_z = 4, 5, 6, 7, 8

    gdn_masks = _gdn_masks()
    s5_rep = _s5_replicate()
    s5_ops = jax.vmap(functools.partial(_s5_weights, n_levels=n_levels))(
        s5_a_re, s5_a_im, s5_log_dt, s5_b_re, s5_b_im, s5_c_re, s5_c_im)

    wr_hi = w_router.astype(BF16)
    wr_lo = (w_router - wr_hi.astype(F32)).astype(BF16)
    pad_e = ((0, 0), (0, 128 - N_EXPERTS))
    wr_t = jnp.concatenate([jnp.pad(wr_hi, pad_e), jnp.pad(wr_lo, pad_e)], axis=1)
    rb = router_bias.reshape(N_EXPERTS, 1).astype(F32)

    tm_e = min(512, t)
    n_tiles = (TOP_K * t) // tm_e + N_EXPERTS

    w_in_t = jnp.transpose(w_in, (0, 2, 1))
    w_big = _repack_in_proj(w_in_t, o_ga, 2 * d, o_ba)
    w_small = jnp.pad(w_in_t[:, o_ba:o_ga], ((0, 0), (0, 128 - 2 * heads), (0, 0))).astype(BF16)

    w_glu_b, w_proj_a_b, w_proj_b_b, w_out_b = (w.astype(BF16) for w in (s5_w_glu, w_proj_a, w_proj_b, w_out))

    x2 = x.reshape(t, d)
    for l in range(depth):
        modl = mod[l, :batch].reshape(batch, 6, d)
        proj, small = _inproj(x2, modl, norm1_w[l].reshape(1, d), w_big, w_small, l, seq)

        y_lin = _s5_core(proj, col_u * s5w // 128, *s5_ops, s5_rep, l, batch, seq)
        s5_tail = (s5_d[l].reshape(1, s5w), w_glu_b, s5_b_glu[l].reshape(1, s5w))

        head_params = jnp.zeros((8, 128), F32)
        head_params = head_params.at[0, heads:2 * heads].set(-jnp.exp(gdn_a_log[l]))
        head_params = head_params.at[1, heads:2 * heads].set(gdn_dt_bias[l])
        y_b = _gdn(proj, small, gdn_conv_w[l], head_params, gdn_norm_w[l].reshape(1, HEAD_DIM), gdn_masks,
                   (col_q, col_k, col_v, col_z), batch, seq, heads)

        x2, h2, route = _merge(y_lin, y_b, proj, x2, modl, s5_tail, w_proj_a_b, w_proj_b_b, w_out_b,
                               norm2_w[l].reshape(1, d), wr_t, rb, l, seq, col_u, 0, 1)

        dest, src_tok, tile_expert, n_used = _dispatch_plan(route[0:TOP_K], tm_e, n_tiles)
        xs = jnp.take(h2, src_tok, axis=0, mode='clip')
        if l == 0:
            (wg_l, _), (wu_l, _) = _expert_weights_bf16(w_gate, 0, h2), _expert_weights_bf16(w_up, 0, h2)
        wd_l, _ = _expert_weights_bf16(w_down, l, h2)
        ys = _moe_experts(tile_expert, n_used, xs, wg_l, wu_l, wd_l, tm_e)
        y_pairs = jnp.take(ys, dest, axis=0, mode='clip')
        ties = ()
        if l + 1 < depth:
            (wg_l, tie_g), (wu_l, tie_u) = _expert_weights_bf16(w_gate, l + 1, ys), _expert_weights_bf16(w_up, l + 1, ys)
            ties = (tie_g, tie_u)
        x2 = _combine(x2, y_pairs, route, modl, final_norm_w.reshape(1, d), seq, l == depth - 1, ties)

    return x2.reshape(batch, seq, d)
```

```python
import functools
import math

import jax
import jax.numpy as jnp
from jax import lax
from jax.experimental import pallas as pl
from jax.experimental.pallas import tpu as pltpu

F32 = jnp.float32
BF16 = jnp.bfloat16
HIGHEST = lax.Precision.HIGHEST

RMS_EPS = 1e-6
S5_GROUP = 16
S5_STATE = 64
S5_CHUNK = 16
S5_OCT = 8
HEAD_DIM = 128
GDN_CHUNK = 64
GDN_TILE = 256
GDN_HEADS_PER_TRIP = 8
CONV_K = 4
N_EXPERTS = 16
N_GROUPS = 4
EXPERTS_PER_GROUP = N_EXPERTS // N_GROUPS
TOP_K = 2
MIB = 1024 * 1024


def _cparams(semantics, vmem_mib):
    return pltpu.CompilerParams(dimension_semantics=semantics, vmem_limit_bytes=vmem_mib * MIB)


def _silu(x):
    return x * jax.nn.sigmoid(x)


def _softplus(x):
    return jnp.maximum(x, 0.0) + jnp.log(1.0 + jnp.exp(-jnp.abs(x)))


def _rms_scale(x):
    return lax.rsqrt(jnp.mean(x * x, axis=-1, keepdims=True) + RMS_EPS)


def _pack_halves(x):
    n = x.shape[1] // 2
    lo = lax.bitcast_convert_type(x[:, :n].astype(BF16).astype(F32), jnp.uint32)
    hi = lax.bitcast_convert_type(x[:, n:].astype(BF16).astype(F32), jnp.uint32)
    return (lo >> 16) | hi


def _unpack_halves(p):
    lo = lax.bitcast_convert_type(p << 16, F32)
    hi = lax.bitcast_convert_type(p & jnp.uint32(0xFFFF0000), F32)
    return lo, hi


def _nt_dot(a, b):
    return lax.dot_general(a, b, (((1,), (1,)), ((), ())), preferred_element_type=F32)


def _tn_dot(a, b):
    return lax.dot_general(a, b, (((0,), (0,)), ((), ())), preferred_element_type=F32)


def _ada_kernel(c_ref, w_ref, b_ref, o_ref):
    ca = _silu(c_ref[...]).astype(BF16)
    o_ref[...] = jnp.dot(ca, w_ref[...].astype(BF16), preferred_element_type=F32) + b_ref[...]


def _ada(c_pad, w_ada, b_ada):
    depth, d, n = w_ada.shape
    rows = c_pad.shape[0]
    tn = min(n, 1024)
    return pl.pallas_call(
        _ada_kernel,
        grid=(depth, n // tn),
        in_specs=[pl.BlockSpec((rows, d), lambda l, j: (0, 0)),
                  pl.BlockSpec((None, d, tn), lambda l, j: (l, 0, j)),
                  pl.BlockSpec((None, 1, tn), lambda l, j: (l, 0, j))],
        out_specs=pl.BlockSpec((None, rows, tn), lambda l, j: (l, 0, j)),
        out_shape=jax.ShapeDtypeStruct((depth, rows, n), F32),
        compiler_params=_cparams(("parallel", "parallel"), 40),
    )(c_pad, w_ada, b_ada.reshape(depth, 1, n))


def _inproj_kernel(x_ref, mod_ref, nw_ref, w_ref, wsm_ref, o_ref, osm_ref, h_scr):
    @pl.when(pl.program_id(1) == 0)
    def _():
        x = x_ref[...]
        h = x * _rms_scale(x) * nw_ref[...] * (1.0 + mod_ref[1:2, :]) + mod_ref[0:1, :]
        hb = h.astype(BF16)
        h_scr[...] = hb
        osm_ref[...] = _nt_dot(hb, wsm_ref[...])

    o_ref[...] = _nt_dot(h_scr[...], w_ref[...])


def _inproj(x2, modl, nw, w_big, w_small, layer, seq):
    t, d = x2.shape
    n = w_big.shape[1]
    tm = min(seq, 1024)
    tn = 1536
    per_batch = seq // tm
    return pl.pallas_call(
        _inproj_kernel,
        grid=(t // tm, n // tn),
        in_specs=[pl.BlockSpec((tm, d), lambda i, j: (i, 0)),
                  pl.BlockSpec((None, 6, d), lambda i, j: (i // per_batch, 0, 0)),
                  pl.BlockSpec((1, d), lambda i, j: (0, 0)),
                  pl.BlockSpec((None, tn, d), lambda i, j: (layer, j, 0)),
                  pl.BlockSpec((None, 128, d), lambda i, j: (layer, 0, 0))],
        out_specs=[pl.BlockSpec((tm, tn), lambda i, j: (i, j)),
                   pl.BlockSpec((tm, 128), lambda i, j: (i, 0))],
        out_shape=[jax.ShapeDtypeStruct((t, n), F32), jax.ShapeDtypeStruct((t, 128), F32)],
        scratch_shapes=[pltpu.VMEM((tm, d), BF16)],
        compiler_params=_cparams(("parallel", "arbitrary"), 52),
    )(x2, modl, nw, w_big, w_small)


def _repack_kernel(a_ref, b_ref, o_ref, *, gate_blocks, shift):
    c = pl.program_id(1)

    @pl.when(c < gate_blocks)
    def _():
        x = jnp.concatenate([a_ref[...], b_ref[...]], axis=0)
        o_ref[...] = x[shift:shift + o_ref.shape[0]].astype(o_ref.dtype)

    @pl.when(c >= gate_blocks)
    def _():
        o_ref[...] = a_ref[...].astype(o_ref.dtype)


def _repack_in_proj(w_in_t, gate_start, gate_cols, lead_cols):
    depth, _, d = w_in_t.shape
    rb = 512
    shift = gate_start % rb
    base = gate_start - shift
    assert shift % 16 == 0 and gate_cols % rb == 0 and lead_cols % rb == 0 and lead_cols == base
    gate_blocks = gate_cols // rb
    kern = functools.partial(_repack_kernel, gate_blocks=gate_blocks, shift=shift)
    return pl.pallas_call(
        kern,
        grid=(depth, (gate_cols + lead_cols) // rb),
        in_specs=[pl.BlockSpec((None, rb, d),
                               lambda l, c: (l, jnp.where(c < gate_blocks, base // rb + c, c - gate_blocks), 0)),
                  pl.BlockSpec((None, shift, d),
                               lambda l, c: (l, jnp.where(c < gate_blocks, (base + (c + 1) * rb) // shift, 0), 0))],
        out_specs=pl.BlockSpec((None, rb, d), lambda l, c: (l, c, 0)),
        out_shape=jax.ShapeDtypeStruct((depth, gate_cols + lead_cols, d), BF16),
        compiler_params=_cparams(("parallel", "arbitrary"), 32),
    )(w_in_t, w_in_t)


def _s5_weights(a_re, a_im, log_dt, b_re, b_im, c_re, c_im, n_levels):
    tc = S5_CHUNK
    groups, p = a_re.shape
    n = b_re.shape[-1]
    oc = S5_OCT
    octs = groups // oc
    lam = lax.complex(a_re, a_im)
    dt = jnp.exp(log_dt)[:, None]
    log_a = lam * dt
    a_bar = jnp.exp(log_a)
    b_bar = ((a_bar - 1.0) / lam)[..., None] * lax.complex(b_re, b_im)
    c_mat = lax.complex(c_re, c_im)
    apow = jnp.exp(log_a[:, :, None] * jnp.arange(tc + 1, dtype=F32)[None, None, :])
    lane = jnp.arange(tc * n)
    e_t = (jnp.arange(tc)[:, None] == (lane // n)[None, :]).astype(F32)
    e_n = (jnp.arange(n)[:, None] == (lane % n)[None, :]).astype(F32)

    def expand(z, e):
        f = lambda r: jnp.einsum('gpk,kj->gpj', r, e, precision=HIGHEST)
        return lax.complex(f(jnp.real(z)), f(jnp.imag(z)))

    def re_negim_rows(z):
        return jnp.concatenate([jnp.real(z), -jnp.imag(z)], axis=1)

    c_rep = expand(jnp.transpose(c_mat, (0, 2, 1)), e_n)
    lag_c = re_negim_rows(expand(apow[:, :, :tc], e_t) * c_rep)
    hoc = re_negim_rows(expand(apow[:, :, 1:], e_t) * c_rep)
    b_t = jnp.transpose(b_bar, (0, 2, 1))
    b_cat = jnp.concatenate([jnp.real(b_t), jnp.imag(b_t)], axis=-1)
    kern = jnp.einsum('gnk,gkj->gnj', b_cat, lag_c, precision=HIGHEST)
    tcomp = jnp.stack([jnp.pad(kern[:, :, :(tc - s) * n], ((0, 0), (0, 0), (s * n, 0))) for s in range(tc)],
                      axis=1)
    tcomp = jnp.transpose(tcomp.reshape(octs, oc, tc, n, tc * n), (0, 2, 1, 3, 4)).reshape(octs, tc * oc * n, tc * n)
    ap_rev = jnp.transpose(apow[:, :, tc - 1 - jnp.arange(tc)], (0, 2, 1))
    hin_c = ap_rev[:, :, None, :] * b_t[:, None, :, :]
    hc = jnp.concatenate([jnp.real(hin_c), jnp.imag(hin_c)], axis=-1)
    hc = jnp.transpose(hc.reshape(octs, oc, tc, n, 2 * p), (0, 2, 1, 3, 4)).reshape(octs, tc * oc * n, 2 * p)
    hoc = hoc.reshape(octs, oc * 2 * p, tc * n)
    steps = (tc * (2.0 ** jnp.arange(n_levels, dtype=F32)))
    amul = jnp.exp(log_a[:, None, :] * steps[None, :, None])
    a1 = jnp.concatenate([jnp.real(amul), jnp.real(amul)], axis=-1)
    a2 = jnp.concatenate([-jnp.imag(amul), jnp.imag(amul)], axis=-1)

    def lanes_by_group(a):
        return jnp.transpose(a.reshape(octs, oc, n_levels, 2 * p), (0, 2, 1, 3)).reshape(octs, n_levels, oc * 2 * p)

    return (tcomp.astype(BF16), hc.astype(BF16), hoc.astype(BF16),
            lanes_by_group(a1).astype(F32), lanes_by_group(a2).astype(F32))


def _s5_replicate():
    r = lax.broadcasted_iota(jnp.int32, (S5_CHUNK * S5_GROUP, S5_CHUNK * 128), 0)
    c = lax.broadcasted_iota(jnp.int32, (S5_CHUNK * S5_GROUP, S5_CHUNK * 128), 1)
    return ((r // S5_GROUP == c // 128) & (r % S5_GROUP == c % S5_GROUP)).astype(BF16)


def _s5_core_kernel(u_ref, tc_ref, hc_ref, hoc_ref, a1_ref, a2_ref, rep_ref, y_ref, toep_scr, hin_scr, hout_scr,
                    *, n_levels):
    tc = S5_CHUNK
    lanes = u_ref.shape[1]
    state_lanes = hc_ref.shape[1]

    @pl.when(pl.program_id(1) == 0)
    def _():
        kd = toep_scr.shape[1]
        rb = 256
        col_g = (lax.broadcasted_iota(jnp.int32, (rb, kd), 1) // S5_GROUP) % S5_OCT
        row_i = lax.broadcasted_iota(jnp.int32, (rb, kd), 0)
        for i in range(toep_scr.shape[0] // rb):
            full = jnp.dot(tc_ref[i * rb:(i + 1) * rb, :], rep_ref[...], preferred_element_type=F32)
            row_g = ((row_i + i * rb) // S5_GROUP) % S5_OCT
            toep_scr[i * rb:(i + 1) * rb, :] = jnp.where(row_g == col_g, full, 0.0).astype(BF16)
        for i in range(hout_scr.shape[0] // rb):
            full = jnp.dot(hoc_ref[i * rb:(i + 1) * rb, :], rep_ref[...], preferred_element_type=F32)
            row_g = (row_i + i * rb) // state_lanes
            hout_scr[i * rb:(i + 1) * rb, :] = jnp.where(row_g == col_g, full, 0.0).astype(BF16)
        hrow_g = (lax.broadcasted_iota(jnp.int32, hc_ref.shape, 0) // S5_GROUP) % S5_OCT
        for g in range(S5_OCT):
            hin_scr[:, g * state_lanes:(g + 1) * state_lanes] = jnp.where(hrow_g == g, hc_ref[...], 0.0).astype(BF16)

    rows = u_ref.shape[0] // tc
    x = jnp.concatenate([u_ref[pl.ds(t, rows, stride=tc), :] for t in range(tc)], axis=1).astype(BF16)
    s_loc = jnp.dot(x, hin_scr[...], preferred_element_type=F32)
    pos = lax.broadcasted_iota(jnp.int32, (rows, state_lanes), 0)
    s_in = []
    for g in range(S5_OCT):
        ln = slice(g * state_lanes, (g + 1) * state_lanes)
        s = s_loc[:, ln]
        for k in range(n_levels):
            d = 1 << k
            prev = jnp.where(pos >= d, pltpu.roll(s, d, axis=0), 0.0)
            s = s + a1_ref[k:k + 1, ln] * prev + a2_ref[k:k + 1, ln] * pltpu.roll(prev, state_lanes // 2, axis=1)
        s_in.append(jnp.where(pos >= 1, pltpu.roll(s, 1, axis=0), 0.0).astype(BF16))
    y = (jnp.dot(x, toep_scr[...], preferred_element_type=F32)
         + jnp.dot(jnp.concatenate(s_in, axis=1), hout_scr[...], preferred_element_type=F32))
    for t in range(tc):
        y_ref[pl.ds(t, rows, stride=tc), :] = y[:, t * lanes:(t + 1) * lanes]


def _s5_core(proj, u_lane_block, tcomp, hc, hoc, a1, a2, rep, layer, batch, seq):
    t = proj.shape[0]
    _, octs, kdim, tn = tcomp.shape
    sl = hc.shape[-1]
    sw = hoc.shape[2]
    n_levels = a1.shape[2]
    lanes = kdim // S5_CHUNK
    kern = functools.partial(_s5_core_kernel, n_levels=n_levels)
    return pl.pallas_call(
        kern,
        grid=(octs, batch),
        in_specs=[pl.BlockSpec((seq, lanes), lambda o, b: (b, u_lane_block + o)),
                  pl.BlockSpec((None, None, kdim, tn), lambda o, b: (layer, o, 0, 0)),
                  pl.BlockSpec((None, None, kdim, sl), lambda o, b: (layer, o, 0, 0)),
                  pl.BlockSpec((None, None, sw, tn), lambda o, b: (layer, o, 0, 0)),
                  pl.BlockSpec((None, None, n_levels, sw), lambda o, b: (layer, o, 0, 0)),
                  pl.BlockSpec((None, None, n_levels, sw), lambda o, b: (layer, o, 0, 0)),
                  pl.BlockSpec((tn, kdim), lambda o, b: (0, 0))],
        out_specs=pl.BlockSpec((seq, lanes), lambda o, b: (b, o)),
        out_shape=jax.ShapeDtypeStruct((t, octs * lanes), F32),
        scratch_shapes=[pltpu.VMEM((kdim, kdim), BF16), pltpu.VMEM((kdim, sw), BF16), pltpu.VMEM((sw, kdim), BF16)],
        compiler_params=_cparams(("parallel", "arbitrary"), 48),
    )(proj, tcomp, hc, hoc, a1, a2, rep)


def _gdn_kernel(q_ref, k_ref, v_ref, qh_ref, kh_ref, vh_ref, z_ref, sm_ref, cq_ref, ck_ref, cv_ref,
                hp_ref, nw_ref, tril_ref, cmask_ref, lmask_ref, o_ref, betab, gcb, grow, st_scr,
                cat_q, cat_k, cat_v, *, heads):
    tt = GDN_TILE
    hd = HEAD_DIM
    nck = tt // GDN_CHUNK
    first = pl.program_id(1) == 0

    @pl.when(first)
    def _():
        st_scr[...] = jnp.zeros_like(st_scr)

    sm = sm_ref[...]
    hp = hp_ref[...]
    beta_all = jax.nn.sigmoid(sm)
    g_all = hp[0:1, :] * _softplus(sm + hp[1:2, :])
    gc_all = jnp.dot(tril_ref[...], g_all, precision=HIGHEST, preferred_element_type=F32)
    gc_t = gc_all.T
    for h in range(heads):
        betab[h] = jnp.broadcast_to(beta_all[:, h:h + 1], (tt, hd))
        gcb[h] = jnp.broadcast_to(gc_all[:, heads + h:heads + h + 1], (tt, hd))
        grow[h:h + 1, :] = gc_t[heads + h:heads + h + 1, :]

    eye = (lax.broadcasted_iota(jnp.int32, (tt, tt), 0) == lax.broadcasted_iota(jnp.int32, (tt, tt), 1)).astype(F32)
    n_lv = lmask_ref.shape[0]

    for x_ref, halo_ref, cat in ((q_ref, qh_ref, cat_q), (k_ref, kh_ref, cat_k), (v_ref, vh_ref, cat_v)):
        cat[0:8, :] = jnp.where(first, 0.0, halo_ref[...])
        cat[8:8 + tt, :] = x_ref[...]

    def conv_silu(cat, cw_ref, hs):
        w = cw_ref[:, pl.ds(hs, hd)]
        acc = cat[pl.ds(8, tt), pl.ds(hs, hd)] * w[CONV_K - 1:CONV_K, :]
        for s in range(1, CONV_K):
            acc = acc + cat[pl.ds(8 - s, tt), pl.ds(hs, hd)] * w[CONV_K - 1 - s:CONV_K - s, :]
        return _silu(acc)

    def heads_lockstep(hs_idx):
        each = lambda f, *ls: [f(*a) for a in zip(*ls)]
        hs = [pl.multiple_of(h * hd, hd) for h in hs_idx]
        q = [conv_silu(cat_q, cq_ref, o) for o in hs]
        k = [conv_silu(cat_k, ck_ref, o) for o in hs]
        v = [conv_silu(cat_v, cv_ref, o) for o in hs]
        qn = each(lambda a: a * (lax.rsqrt(jnp.sum(a * a, axis=-1, keepdims=True) + RMS_EPS) * (hd ** -0.5)), q)
        kn = each(lambda a: a * lax.rsqrt(jnp.sum(a * a, axis=-1, keepdims=True) + RMS_EPS), k)
        bb = [betab[h] for h in hs_idx]
        gc = [gcb[h] for h in hs_idx]
        gr = [grow[pl.ds(h, 1), :] for h in hs_idx]
        decay = each(lambda c, r: jnp.exp(jnp.where(cmask_ref[0] > 0.0, jnp.concatenate([c, c], axis=1) - r, -1e30)),
                     gc, gr)
        kb = each(lambda a, b: a * b, kn, bb)
        knb = each(lambda a: a.astype(BF16), kn)
        lfull = each(lambda a, b, dc: _nt_dot(a.astype(BF16), b) * dc, kb, knb, decay)
        lmat = each(lambda a: a.astype(BF16), lfull)
        tinv = each(lambda a: eye - a * cmask_ref[1], lfull)
        for lv in range(1, n_lv):
            d = 1 << lv
            nb = tt // d
            tb = each(lambda a: a.astype(BF16), tinv)
            bd = each(lambda m: m * lmask_ref[lv], lmat)
            if d % 8 == 0:
                odd = each(lambda a: jnp.concatenate([a[i * d:(i + 1) * d] for i in range(1, nb, 2)], axis=0), tinv)
                tl = each(lambda a, b: jnp.dot(a.astype(BF16), b, preferred_element_type=F32).astype(BF16), odd, bd)
                upd = each(lambda a, b, c: a - jnp.dot(b, c, preferred_element_type=F32), odd, tl, tb)
                tinv = each(lambda a, u: jnp.concatenate(
                    [a[i * d:(i + 1) * d] if i % 2 == 0 else u[(i // 2) * d:(i // 2 + 1) * d] for i in range(nb)],
                    axis=0), tinv, upd)
            else:
                tl = each(lambda a, b: jnp.dot(a, b, preferred_element_type=F32).astype(BF16), tb, bd)
                tinv = each(lambda a, b, c: a - jnp.dot(b, c, preferred_element_type=F32), tinv, tl, tb)
        eg = each(jnp.exp, gc)
        rhs = each(lambda a, e, b, c: jnp.concatenate([a * e, b * c], axis=1).astype(BF16), kb, eg, v, bb)
        wu = each(lambda a, b: jnp.dot(a.astype(BF16), b, preferred_element_type=F32).astype(BF16), tinv, rhs)
        attn = each(lambda a, b, dc: (_nt_dot(a.astype(BF16), b) * dc).astype(BF16), qn, knb, decay)
        awu = each(lambda a, b: jnp.dot(a, b, preferred_element_type=F32), attn, wu)
        qp = each(lambda a, e, b: (a * e - b[:, :hd]).astype(BF16), qn, eg, awu)
        s = [st_scr[h] for h in hs_idx]
        outs = [[] for _ in hs_idx]
        for c in range(nck):
            r0 = c * GDN_CHUNK
            r1 = r0 + GDN_CHUNK
            gl = each(lambda a: a[r1 - 1:r1, :], gc)
            kd = each(lambda a, l, g: (a[r0:r1] * jnp.exp(l - g[r0:r1])).astype(BF16), kn, gl, gc)
            m = each(lambda a, b: _tn_dot(a, b[r0:r1]), kd, wu)
            sb = each(lambda a: a.astype(BF16), s)
            for i, (a, b, y) in enumerate(zip(qp, sb, awu)):
                outs[i].append(jnp.dot(a[r0:r1], b, preferred_element_type=F32) + y[r0:r1, hd:])
            s = each(lambda a, l, mm, b: a * jnp.exp(l) - jnp.dot(mm[:, :hd].astype(BF16), b,
                                                                  preferred_element_type=F32) + mm[:, hd:],
                     s, gl, m, sb)
        for i, h in enumerate(hs_idx):
            st_scr[h] = s[i]
            o = jnp.concatenate(outs[i], axis=0)
            z = z_ref[:, pl.ds(hs[i], hd)]
            o_ref[:, pl.ds(hs[i], hd)] = (o * _rms_scale(o) * nw_ref[...] * _silu(z)).astype(o_ref.dtype)

    def trip(j, carry):
        heads_lockstep([GDN_HEADS_PER_TRIP * j + i for i in range(GDN_HEADS_PER_TRIP)])
        return carry

    lax.fori_loop(0, heads // GDN_HEADS_PER_TRIP, trip, 0)


def _gdn_masks():
    tt = GDN_TILE
    ri = lax.broadcasted_iota(jnp.int32, (tt, tt), 0)
    ci = lax.broadcasted_iota(jnp.int32, (tt, tt), 1)
    same = (ri // GDN_CHUNK) == (ci // GDN_CHUNK)
    causal = same & (ci <= ri)
    tril = causal.astype(F32)
    cmask = jnp.stack([tril, (((ri % 2) == 1) & (ci == ri - 1)).astype(F32)])
    levels = []
    d = 1
    while d < GDN_CHUNK:
        levels.append((((ri // d) % 2) == 1) & ((ci // d) == (ri // d) - 1))
        d *= 2
    return tril, cmask, jnp.stack(levels).astype(BF16)


def _gdn(proj, small, conv_w, head_params, norm_w, masks, cols, batch, seq, heads):
    t = proj.shape[0]
    tt = GDN_TILE
    width = heads * HEAD_DIM
    tiles = seq // tt
    cq, ck, cv, cz = cols
    tril, cmask, lmask = masks

    def cur(col):
        return pl.BlockSpec((tt, width), lambda b, i: (b * tiles + i, col))

    def halo(col):
        return pl.BlockSpec((8, width), lambda b, i: (jnp.maximum((b * tiles + i) * (tt // 8) - 1, 0), col))

    def cw(col):
        return pl.BlockSpec((CONV_K, width), lambda b, i: (0, col))

    kern = functools.partial(_gdn_kernel, heads=heads)
    return pl.pallas_call(
        kern,
        grid=(batch, tiles),
        in_specs=[cur(cq), cur(ck), cur(cv), halo(cq), halo(ck), halo(cv), cur(cz),
                  pl.BlockSpec((tt, 128), lambda b, i: (b * tiles + i, 0)),
                  cw(0), cw(1), cw(2),
                  pl.BlockSpec((8, 128), lambda b, i: (0, 0)),
                  pl.BlockSpec((1, HEAD_DIM), lambda b, i: (0, 0)),
                  pl.BlockSpec((tt, tt), lambda b, i: (0, 0)),
                  pl.BlockSpec(cmask.shape, lambda b, i: (0, 0, 0)),
                  pl.BlockSpec(lmask.shape, lambda b, i: (0, 0, 0))],
        out_specs=pl.BlockSpec((tt, width), lambda b, i: (b * tiles + i, 0)),
        out_shape=jax.ShapeDtypeStruct((t, width), BF16),
        scratch_shapes=[pltpu.VMEM((heads, tt, HEAD_DIM), F32),
                        pltpu.VMEM((heads, tt, HEAD_DIM), F32),
                        pltpu.VMEM((8, tt), F32),
                        pltpu.VMEM((heads, HEAD_DIM, HEAD_DIM), F32),
                        pltpu.VMEM((tt + 8, width), F32),
                        pltpu.VMEM((tt + 8, width), F32),
                        pltpu.VMEM((tt + 8, width), F32)],
        compiler_params=_cparams(("parallel", "arbitrary"), 40),
    )(proj, proj, proj, proj, proj, proj, proj, small, conv_w, conv_w, conv_w, head_params, norm_w,
      tril, cmask, lmask)


def _route(h, wr_hl, rb):
    h_hi = h.astype(BF16)
    h_lo = (h - h_hi.astype(F32)).astype(BF16)
    first = jnp.dot(h_hi, wr_hl[...], preferred_element_type=F32)
    logits = first[:, :128] + first[:, 128:] + jnp.dot(h_lo, wr_hl[:, :128], preferred_element_type=F32)
    logits = logits.T[:N_EXPERTS]
    scores = jax.nn.sigmoid(logits)
    sel = scores + rb
    sel_rows = [sel[e:e + 1, :] for e in range(N_EXPERTS)]
    score_rows = [scores[e:e + 1, :] for e in range(N_EXPERTS)]
    epg = EXPERTS_PER_GROUP
    best_score = None
    best_group = None
    for g in range(N_GROUPS):
        rows = sel_rows[g * epg:(g + 1) * epg]
        gs = None
        for a in range(epg):
            for b in range(a + 1, epg):
                pair = rows[a] + rows[b]
                gs = pair if gs is None else jnp.maximum(gs, pair)
        if g == 0:
            best_score, best_group = gs, jnp.zeros_like(gs)
        else:
            better = gs > best_score
            best_group = jnp.where(better, float(g), best_group)
            best_score = jnp.where(better, gs, best_score)
    in_sel = []
    in_score = []
    for j in range(epg):
        a = sel_rows[j]
        b = score_rows[j]
        for g in range(1, N_GROUPS):
            pick = best_group == float(g)
            a = jnp.where(pick, sel_rows[g * epg + j], a)
            b = jnp.where(pick, score_rows[g * epg + j], b)
        in_sel.append(a)
        in_score.append(b)
    i1, m1, w1 = jnp.zeros_like(in_sel[0]), in_sel[0], in_score[0]
    for j in range(1, epg):
        better = in_sel[j] > m1
        i1 = jnp.where(better, float(j), i1)
        m1 = jnp.where(better, in_sel[j], m1)
        w1 = jnp.where(better, in_score[j], w1)
    i2 = m2 = w2 = None
    for j in range(epg):
        cand = jnp.where(i1 == float(j), -jnp.inf, in_sel[j])
        if j == 0:
            i2, m2, w2 = jnp.zeros_like(cand), cand, in_score[0]
        else:
            better = cand > m2
            i2 = jnp.where(better, float(j), i2)
            m2 = jnp.where(better, cand, m2)
            w2 = jnp.where(better, in_score[j], w2)
    total = w1 + w2
    zero = jnp.zeros_like(w1)
    return jnp.concatenate([best_group * epg + i1, best_group * epg + i2, w1 / total, w2 / total,
                            zero, zero, zero, zero], axis=0)


def _merge_kernel(yl_ref, u_ref, sd_ref, wglu_ref, bglu_ref, yb_ref, ga_ref, gb_ref, x_ref, mod_ref, wa_ref, wb_ref,
                  wo_ref, nw_ref, wrt_ref, rb_ref, o_ref, h_ref, r_ref):
    y = jax.nn.gelu(yl_ref[...] + sd_ref[...] * u_ref[...])
    glu = jnp.dot(y.astype(BF16), wglu_ref[...], preferred_element_type=F32) + bglu_ref[...]
    ya = (y * jax.nn.sigmoid(glu)).astype(BF16)
    pa = jnp.dot(ya, wa_ref[...], preferred_element_type=F32)
    pb = jnp.dot(yb_ref[...], wb_ref[...], preferred_element_type=F32)
    merged = jax.nn.sigmoid(ga_ref[...]) * pa + jax.nn.sigmoid(gb_ref[...]) * pb
    out = jnp.dot(merged.astype(BF16), wo_ref[...], preferred_element_type=F32)
    xn = x_ref[...] + mod_ref[2:3, :] * out
    o_ref[...] = xn
    h = xn * _rms_scale(xn) * nw_ref[...] * (1.0 + mod_ref[4:5, :]) + mod_ref[3:4, :]
    h_ref[...] = _pack_halves(h)
    r_ref[...] = _route(h, wrt_ref, rb_ref[...])


def _merge(y_lin, yb, proj, x2, modl, s5_tail, wa, wb, wo, nw2, wr_t, rb, layer, seq, u_col, ga_col, gb_col):
    t, d = x2.shape
    w = y_lin.shape[1]
    d_skip, w_glu, b_glu = s5_tail
    tm = min(seq, 256)
    per_batch = seq // tm
    const = dict(pipeline_mode=pl.Buffered(1))
    return pl.pallas_call(
        _merge_kernel,
        grid=(t // tm,),
        in_specs=[pl.BlockSpec((tm, w), lambda i: (i, 0)),
                  pl.BlockSpec((tm, w), lambda i: (i, u_col)),
                  pl.BlockSpec((1, w), lambda i: (0, 0)),
                  pl.BlockSpec((None, w, w), lambda i: (layer, 0, 0), **const),
                  pl.BlockSpec((1, w), lambda i: (0, 0)),
                  pl.BlockSpec((tm, w), lambda i: (i, 0)),
                  pl.BlockSpec((tm, d), lambda i: (i, ga_col)),
                  pl.BlockSpec((tm, d), lambda i: (i, gb_col)),
                  pl.BlockSpec((tm, d), lambda i: (i, 0)),
                  pl.BlockSpec((None, 6, d), lambda i: (i // per_batch, 0, 0)),
                  pl.BlockSpec((None, w, d), lambda i: (layer, 0, 0), **const),
                  pl.BlockSpec((None, w, d), lambda i: (layer, 0, 0), **const),
                  pl.BlockSpec((None, d, d), lambda i: (layer, 0, 0), **const),
                  pl.BlockSpec((1, d), lambda i: (0, 0)),
                  pl.BlockSpec((d, 256), lambda i: (0, 0)),
                  pl.BlockSpec((N_EXPERTS, 1), lambda i: (0, 0))],
        out_specs=[pl.BlockSpec((tm, d), lambda i: (i, 0)),
                   pl.BlockSpec((tm, d // 2), lambda i: (i, 0)),
                   pl.BlockSpec((8, tm), lambda i: (0, i))],
        out_shape=[jax.ShapeDtypeStruct((t, d), F32), jax.ShapeDtypeStruct((t, d // 2), jnp.uint32),
                   jax.ShapeDtypeStruct((8, t), F32)],
        compiler_params=_cparams(("parallel",), 52),
    )(y_lin, proj, d_skip, w_glu, b_glu, yb, proj, proj, x2, modl, wa, wb, wo, nw2, wr_t, rb)


def _moe_kernel(te_ref, nu_ref, x_ref, wg_ref, wu_ref, wd_ref, o_ref):
    used = pl.program_id(0) < nu_ref[0]

    @pl.when(used)
    def _():
        x_lo, x_hi = _unpack_halves(x_ref[...])
        x_lo = x_lo.astype(BF16)
        x_hi = x_hi.astype(BF16)
        half = x_lo.shape[1]

        def up(w_ref):
            return (jnp.dot(x_lo, w_ref[:half, :], preferred_element_type=F32)
                    + jnp.dot(x_hi, w_ref[half:, :], preferred_element_type=F32))

        hid = (_silu(up(wg_ref)) * up(wu_ref)).astype(BF16)
        o_ref[...] = _pack_halves(jnp.dot(hid, wd_ref[...], preferred_element_type=F32))

    @pl.when(jnp.logical_not(used))
    def _():
        o_ref[...] = jnp.zeros_like(o_ref)


def _cast_kernel(w_ref, after_ref, o_ref, done_ref):
    del after_ref
    o_ref[...] = w_ref[...].astype(o_ref.dtype)
    done_ref[...] = jnp.zeros_like(done_ref)


def _expert_weights_bf16(w, layer, after):
    _, e, a, b = w.shape
    ta = min(a, (1 << 20) // b)
    return pl.pallas_call(
        _cast_kernel,
        grid=(e, a // ta),
        in_specs=[pl.BlockSpec((None, None, ta, b), lambda i, j: (layer, i, j, 0)),
                  pl.BlockSpec((8, 128), lambda i, j: (0, 0))],
        out_specs=[pl.BlockSpec((None, ta, b), lambda i, j: (i, j, 0)),
                   pl.BlockSpec((8, 128), lambda i, j: (0, 0))],
        out_shape=[jax.ShapeDtypeStruct((e, a, b), BF16), jax.ShapeDtypeStruct((8, 128), F32)],
        compiler_params=_cparams(("arbitrary", "arbitrary"), 32),
    )(w, after)


def _moe_experts(tile_expert, n_used, xs, wg, wu, wd, tm):
    p, dh = xs.shape
    d = 2 * dh
    f = wg.shape[-1]
    grid_spec = pltpu.PrefetchScalarGridSpec(
        num_scalar_prefetch=2,
        grid=(p // tm,),
        in_specs=[pl.BlockSpec((tm, dh), lambda i, te, nu: (i, 0)),
                  pl.BlockSpec((None, d, f), lambda i, te, nu: (te[i], 0, 0)),
                  pl.BlockSpec((None, d, f), lambda i, te, nu: (te[i], 0, 0)),
                  pl.BlockSpec((None, f, d), lambda i, te, nu: (te[i], 0, 0))],
        out_specs=pl.BlockSpec((tm, dh), lambda i, te, nu: (i, 0)),
    )
    return pl.pallas_call(
        _moe_kernel,
        grid_spec=grid_spec,
        out_shape=jax.ShapeDtypeStruct((p, dh), jnp.uint32),
        compiler_params=_cparams(("arbitrary",), 52),
    )(tile_expert, n_used, xs, wg, wu, wd)


def _dispatch_plan(expert_rows, tm, n_tiles):
    flat_e = expert_rows.astype(jnp.int32).reshape(-1)
    onehot = (flat_e[:, None] == jnp.arange(N_EXPERTS, dtype=jnp.int32)[None, :]).astype(jnp.int32)
    csum = jnp.cumsum(onehot, axis=0)
    rank = jnp.sum((csum - onehot) * onehot, axis=1)
    sizes = csum[-1]
    padded = ((sizes + tm - 1) // tm) * tm
    pad_end = jnp.cumsum(padded)
    pad_start = pad_end - padded
    dest = pad_start[flat_e] + rank
    tile_start = jnp.arange(n_tiles, dtype=jnp.int32) * tm
    tile_expert = jnp.minimum(jnp.sum((tile_start[:, None] >= pad_end[None, :]).astype(jnp.int32), axis=1),
                              N_EXPERTS - 1).astype(jnp.int32)
    n_used = (pad_end[-1] // tm).astype(jnp.int32).reshape(1)
    dest = dest.astype(jnp.int32)
    n_slots = flat_e.shape[0]
    tokens = n_slots // TOP_K
    n_fill = n_tiles * tm - n_slots
    fill_count = jnp.concatenate([padded - sizes, (n_tiles * tm - pad_end[-1])[None]])
    fill_end = jnp.cumsum(fill_count)
    first_free = jnp.concatenate([pad_start + sizes, pad_end[-1:]])
    j = jnp.arange(n_fill, dtype=jnp.int32)
    region = (j[:, None] >= fill_end[None, :]).astype(jnp.int32).sum(axis=1)
    shift = first_free - (fill_end - fill_count)
    fill_key = j + jnp.sum((region[:, None] == jnp.arange(N_EXPERTS + 1)[None, :]) * shift[None, :], axis=1)
    keys = jnp.concatenate([dest, fill_key.astype(jnp.int32)])
    vals = jnp.concatenate([jnp.arange(n_slots, dtype=jnp.int32) % tokens, j % tokens])
    tok_bits = max(tokens - 1, 1).bit_length()
    assert (n_tiles * tm - 1).bit_length() + tok_bits <= 31
    src_tok = jnp.sort(keys * (1 << tok_bits) + vals) & ((1 << tok_bits) - 1)
    return dest, src_tok, tile_expert, n_used


def _combine_kernel(x_ref, y0_ref, y1_ref, gw_ref, mod_ref, fw_ref, *rest, final_norm):
    o_ref = rest[-1]
    gw = gw_ref[...].T[:, TOP_K:2 * TOP_K]
    y0_lo, y0_hi = _unpack_halves(y0_ref[...])
    y1_lo, y1_hi = _unpack_halves(y1_ref[...])
    moe = jnp.concatenate([gw[:, 0:1] * y0_lo + gw[:, 1:2] * y1_lo, gw[:, 0:1] * y0_hi + gw[:, 1:2] * y1_hi], axis=1)
    xn = x_ref[...] + mod_ref[5:6, :] * moe
    if final_norm:
        xn = xn * _rms_scale(xn) * fw_ref[...]
    o_ref[...] = xn


def _combine(x2, y_pairs, gate_w, modl, final_w, seq, final_norm, ties=()):
    t, d = x2.shape
    tm = min(seq, 512)
    per_batch = seq // tm
    nblk = t // tm
    return pl.pallas_call(
        functools.partial(_combine_kernel, final_norm=final_norm),
        grid=(nblk,),
        in_specs=[pl.BlockSpec((tm, d), lambda i: (i, 0)),
                  pl.BlockSpec((tm, d // 2), lambda i: (i, 0)),
                  pl.BlockSpec((tm, d // 2), lambda i: (i + nblk, 0)),
                  pl.BlockSpec((8, tm), lambda i: (0, i)),
                  pl.BlockSpec((None, 6, d), lambda i: (i // per_batch, 0, 0)),
                  pl.BlockSpec((1, d), lambda i: (0, 0))] + [pl.BlockSpec((8, 128), lambda i: (0, 0)) for _ in ties],
        out_specs=pl.BlockSpec((tm, d), lambda i: (i, 0)),
        out_shape=jax.ShapeDtypeStruct((t, d), F32),
        compiler_params=_cparams(("parallel",), 40),
    )(x2, y_pairs, y_pairs, gate_w, modl, final_w, *ties)


def kernel(x, c, w_ada, b_ada, norm1_w, norm2_w, w_in, s5_a_re, s5_a_im, s5_log_dt, s5_b_re, s5_b_im, s5_c_re, s5_c_im, s5_d, s5_w_glu, s5_b_glu, gdn_conv_w, gdn_a_log, gdn_dt_bias, gdn_norm_w, w_proj_a, w_proj_b, w_out, w_router, router_bias, w_gate, w_up, w_down, final_norm_w):
    batch, seq, d = x.shape
    depth = w_ada.shape[0]
    t = batch * seq
    s5w = s5_d.shape[1]
    gdw = w_proj_b.shape[1]
    heads = gdn_a_log.shape[1]
    assert gdw == heads * HEAD_DIM and s5w == gdw and d == 2 * gdw
    assert seq % GDN_TILE == 0 and seq % S5_CHUNK == 0
    n_levels = int(math.log2(seq // S5_CHUNK))
    assert S5_CHUNK << n_levels == seq and S5_OCT * S5_GROUP == 128

    c_pad = jnp.zeros((8, d), F32).at[:batch].set(c)
    mod = _ada(c_pad, w_ada, b_ada)

    o_u, o_q, o_k, o_v, o_z = 0, s5w, 2 * s5w, 3 * s5w, 4 * s5w
    o_ba = 5 * s5w
    o_ga = o_ba + 2 * heads
    o_gb = o_ga + d
    col_u, col_q, col_k, col_v, col_z = 4, 5, 6, 7, 8

    gdn_masks = _gdn_masks()
    s5_rep = _s5_replicate()
    s5_ops = jax.vmap(functools.partial(_s5_weights, n_levels=n_levels))(
        s5_a_re, s5_a_im, s5_log_dt, s5_b_re, s5_b_im, s5_c_re, s5_c_im)

    wr_hi = w_router.astype(BF16)
    wr_lo = (w_router - wr_hi.astype(F32)).astype(BF16)
    pad_e = ((0, 0), (0, 128 - N_EXPERTS))
    wr_t = jnp.concatenate([jnp.pad(wr_hi, pad_e), jnp.pad(wr_lo, pad_e)], axis=1)
    rb = router_bias.reshape(N_EXPERTS, 1).astype(F32)

    tm_e = min(512, t)
    n_tiles = (TOP_K * t) // tm_e + N_EXPERTS

    w_in_t = jnp.transpose(w_in, (0, 2, 1))
    w_big = _repack_in_proj(w_in_t, o_ga, 2 * d, o_ba)
    w_small = jnp.pad(w_in_t[:, o_ba:o_ga], ((0, 0), (0, 128 - 2 * heads), (0, 0))).astype(BF16)

    w_glu_b, w_proj_a_b, w_proj_b_b, w_out_b = (w.astype(BF16) for w in (s5_w_glu, w_proj_a, w_proj_b, w_out))

    x2 = x.reshape(t, d)
    for l in range(depth):
        modl = mod[l, :batch].reshape(batch, 6, d)
        proj, small = _inproj(x2, modl, norm1_w[l].reshape(1, d), w_big, w_small, l, seq)

        y_lin = _s5_core(proj, col_u * s5w // 128, *s5_ops, s5_rep, l, batch, seq)
        s5_tail = (s5_d[l].reshape(1, s5w), w_glu_b, s5_b_glu[l].reshape(1, s5w))

        head_params = jnp.zeros((8, 128), F32)
        head_params = head_params.at[0, heads:2 * heads].set(-jnp.exp(gdn_a_log[l]))
        head_params = head_params.at[1, heads:2 * heads].set(gdn_dt_bias[l])
        y_b = _gdn(proj, small, gdn_conv_w[l], head_params, gdn_norm_w[l].reshape(1, HEAD_DIM), gdn_masks,
                   (col_q, col_k, col_v, col_z), batch, seq, heads)

        x2, h2, route = _merge(y_lin, y_b, proj, x2, modl, s5_tail, w_proj_a_b, w_proj_b_b, w_out_b,
                               norm2_w[l].reshape(1, d), wr_t, rb, l, seq, col_u, 0, 1)

        dest, src_tok, tile_expert, n_used = _dispatch_plan(route[0:TOP_K], tm_e, n_tiles)
        xs = jnp.take(h2, src_tok, axis=0, mode='clip')
        if l == 0:
            (wg_l, _), (wu_l, _) = _expert_weights_bf16(w_gate, 0, h2), _expert_weights_bf16(w_up, 0, h2)
        wd_l, _ = _expert_weights_bf16(w_down, l, h2)
        ys = _moe_experts(tile_expert, n_used, xs, wg_l, wu_l, wd_l, tm_e)
        y_pairs = jnp.take(ys, dest, axis=0, mode='clip')
        ties = ()
        if l + 1 < depth:
            (wg_l, tie_g), (wu_l, tie_u) = _expert_weights_bf16(w_gate, l + 1, ys), _expert_weights_bf16(w_up, l + 1, ys)
            ties = (tie_g, tie_u)
        x2 = _combine(x2, y_pairs, route, modl, final_norm_w.reshape(1, d), seq, l == depth - 1, ties)

    return x2.reshape(batch, seq, d)
```

```python
import functools
import math

import jax
import jax.numpy as jnp
from jax import lax
from jax.experimental import pallas as pl
from jax.experimental.pallas import tpu as pltpu

F32 = jnp.float32
BF16 = jnp.bfloat16
HIGHEST = lax.Precision.HIGHEST

RMS_EPS = 1e-6
S5_GROUP = 16
S5_STATE = 64
S5_CHUNK = 16
S5_OCT = 8
HEAD_DIM = 128
GDN_CHUNK = 64
GDN_TILE = 128
GDN_HEADS_PER_TRIP = 8
CONV_K = 4
N_EXPERTS = 16
N_GROUPS = 4
EXPERTS_PER_GROUP = N_EXPERTS // N_GROUPS
TOP_K = 2
MIB = 1024 * 1024


def _cparams(semantics, vmem_mib):
    return pltpu.CompilerParams(dimension_semantics=semantics, vmem_limit_bytes=vmem_mib * MIB)


def _silu(x):
    return x * jax.nn.sigmoid(x)


def _softplus(x):
    return jnp.maximum(x, 0.0) + jnp.log(1.0 + jnp.exp(-jnp.abs(x)))


def _rms_scale(x):
    return lax.rsqrt(jnp.mean(x * x, axis=-1, keepdims=True) + RMS_EPS)


def _pack_halves(x):
    n = x.shape[1] // 2
    lo = lax.bitcast_convert_type(x[:, :n].astype(BF16).astype(F32), jnp.uint32)
    hi = lax.bitcast_convert_type(x[:, n:].astype(BF16).astype(F32), jnp.uint32)
    return (lo >> 16) | hi


def _unpack_halves(p):
    lo = lax.bitcast_convert_type(p << 16, F32)
    hi = lax.bitcast_convert_type(p & jnp.uint32(0xFFFF0000), F32)
    return lo, hi


def _nt_dot(a, b):
    return lax.dot_general(a, b, (((1,), (1,)), ((), ())), preferred_element_type=F32)


def _tn_dot(a, b):
    return lax.dot_general(a, b, (((0,), (0,)), ((), ())), preferred_element_type=F32)


def _ada_kernel(c_ref, w_ref, b_ref, o_ref):
    ca = _silu(c_ref[...]).astype(BF16)
    o_ref[...] = jnp.dot(ca, w_ref[...].astype(BF16), preferred_element_type=F32) + b_ref[...]


def _ada(c_pad, w_ada, b_ada):
    depth, d, n = w_ada.shape
    rows = c_pad.shape[0]
    tn = min(n, 1024)
    return pl.pallas_call(
        _ada_kernel,
        grid=(depth, n // tn),
        in_specs=[pl.BlockSpec((rows, d), lambda l, j: (0, 0)),
                  pl.BlockSpec((None, d, tn), lambda l, j: (l, 0, j)),
                  pl.BlockSpec((None, 1, tn), lambda l, j: (l, 0, j))],
        out_specs=pl.BlockSpec((None, rows, tn), lambda l, j: (l, 0, j)),
        out_shape=jax.ShapeDtypeStruct((depth, rows, n), F32),
        compiler_params=_cparams(("parallel", "parallel"), 40),
    )(c_pad, w_ada, b_ada.reshape(depth, 1, n))


def _inproj_kernel(x_ref, mod_ref, nw_ref, w_ref, wsm_ref, o_ref, osm_ref, h_scr):
    @pl.when(pl.program_id(1) == 0)
    def _():
        x = x_ref[...]
        h = x * _rms_scale(x) * nw_ref[...] * (1.0 + mod_ref[1:2, :]) + mod_ref[0:1, :]
        hb = h.astype(BF16)
        h_scr[...] = hb
        osm_ref[...] = _nt_dot(hb, wsm_ref[...])

    o_ref[...] = _nt_dot(h_scr[...], w_ref[...])


def _inproj(x2, modl, nw, w_big, w_small, layer, seq):
    t, d = x2.shape
    n = w_big.shape[1]
    tm = min(seq, 1024)
    tn = 1536
    per_batch = seq // tm
    return pl.pallas_call(
        _inproj_kernel,
        grid=(t // tm, n // tn),
        in_specs=[pl.BlockSpec((tm, d), lambda i, j: (i, 0)),
                  pl.BlockSpec((None, 6, d), lambda i, j: (i // per_batch, 0, 0)),
                  pl.BlockSpec((1, d), lambda i, j: (0, 0)),
                  pl.BlockSpec((None, tn, d), lambda i, j: (layer, j, 0)),
                  pl.BlockSpec((None, 128, d), lambda i, j: (layer, 0, 0))],
        out_specs=[pl.BlockSpec((tm, tn), lambda i, j: (i, j)),
                   pl.BlockSpec((tm, 128), lambda i, j: (i, 0))],
        out_shape=[jax.ShapeDtypeStruct((t, n), F32), jax.ShapeDtypeStruct((t, 128), F32)],
        scratch_shapes=[pltpu.VMEM((tm, d), BF16)],
        compiler_params=_cparams(("parallel", "arbitrary"), 52),
    )(x2, modl, nw, w_big, w_small)


def _repack_kernel(a_ref, b_ref, o_ref, *, gate_blocks, shift):
    c = pl.program_id(1)

    @pl.when(c < gate_blocks)
    def _():
        x = jnp.concatenate([a_ref[...], b_ref[...]], axis=0)
        o_ref[...] = x[shift:shift + o_ref.shape[0]].astype(o_ref.dtype)

    @pl.when(c >= gate_blocks)
    def _():
        o_ref[...] = a_ref[...].astype(o_ref.dtype)


def _repack_in_proj(w_in_t, gate_start, gate_cols, lead_cols):
    depth, _, d = w_in_t.shape
    rb = 512
    shift = gate_start % rb
    base = gate_start - shift
    assert shift % 16 == 0 and gate_cols % rb == 0 and lead_cols % rb == 0 and lead_cols == base
    gate_blocks = gate_cols // rb
    kern = functools.partial(_repack_kernel, gate_blocks=gate_blocks, shift=shift)
    return pl.pallas_call(
        kern,
        grid=(depth, (gate_cols + lead_cols) // rb),
        in_specs=[pl.BlockSpec((None, rb, d),
                               lambda l, c: (l, jnp.where(c < gate_blocks, base // rb + c, c - gate_blocks), 0)),
                  pl.BlockSpec((None, shift, d),
                               lambda l, c: (l, jnp.where(c < gate_blocks, (base + (c + 1) * rb) // shift, 0), 0))],
        out_specs=pl.BlockSpec((None, rb, d), lambda l, c: (l, c, 0)),
        out_shape=jax.ShapeDtypeStruct((depth, gate_cols + lead_cols, d), BF16),
        compiler_params=_cparams(("parallel", "arbitrary"), 32),
    )(w_in_t, w_in_t)


def _s5_weights(a_re, a_im, log_dt, b_re, b_im, c_re, c_im, n_levels):
    tc = S5_CHUNK
    groups, p = a_re.shape
    n = b_re.shape[-1]
    oc = S5_OCT
    octs = groups // oc
    lam = lax.complex(a_re, a_im)
    dt = jnp.exp(log_dt)[:, None]
    log_a = lam * dt
    a_bar = jnp.exp(log_a)
    b_bar = ((a_bar - 1.0) / lam)[..., None] * lax.complex(b_re, b_im)
    c_mat = lax.complex(c_re, c_im)
    apow = jnp.exp(log_a[:, :, None] * jnp.arange(tc + 1, dtype=F32)[None, None, :])
    lane = jnp.arange(tc * n)
    e_t = (jnp.arange(tc)[:, None] == (lane // n)[None, :]).astype(F32)
    e_n = (jnp.arange(n)[:, None] == (lane % n)[None, :]).astype(F32)

    def expand(z, e):
        f = lambda r: jnp.einsum('gpk,kj->gpj', r, e, precision=HIGHEST)
        return lax.complex(f(jnp.real(z)), f(jnp.imag(z)))

    def re_negim_rows(z):
        return jnp.concatenate([jnp.real(z), -jnp.imag(z)], axis=1)

    c_rep = expand(jnp.transpose(c_mat, (0, 2, 1)), e_n)
    lag_c = re_negim_rows(expand(apow[:, :, :tc], e_t) * c_rep)
    hoc = re_negim_rows(expand(apow[:, :, 1:], e_t) * c_rep)
    b_t = jnp.transpose(b_bar, (0, 2, 1))
    b_cat = jnp.concatenate([jnp.real(b_t), jnp.imag(b_t)], axis=-1)
    kern = jnp.einsum('gnk,gkj->gnj', b_cat, lag_c, precision=HIGHEST)
    tcomp = jnp.stack([jnp.pad(kern[:, :, :(tc - s) * n], ((0, 0), (0, 0), (s * n, 0))) for s in range(tc)],
                      axis=1)
    tcomp = jnp.transpose(tcomp.reshape(octs, oc, tc, n, tc * n), (0, 2, 1, 3, 4)).reshape(octs, tc * oc * n, tc * n)
    ap_rev = jnp.transpose(apow[:, :, tc - 1 - jnp.arange(tc)], (0, 2, 1))
    hin_c = ap_rev[:, :, None, :] * b_t[:, None, :, :]
    hc = jnp.concatenate([jnp.real(hin_c), jnp.imag(hin_c)], axis=-1)
    hc = jnp.transpose(hc.reshape(octs, oc, tc, n, 2 * p), (0, 2, 1, 3, 4)).reshape(octs, tc * oc * n, 2 * p)
    hoc = hoc.reshape(octs, oc * 2 * p, tc * n)
    steps = (tc * (2.0 ** jnp.arange(n_levels, dtype=F32)))
    amul = jnp.exp(log_a[:, None, :] * steps[None, :, None])
    a1 = jnp.concatenate([jnp.real(amul), jnp.real(amul)], axis=-1)
    a2 = jnp.concatenate([-jnp.imag(amul), jnp.imag(amul)], axis=-1)

    def lanes_by_group(a):
        return jnp.transpose(a.reshape(octs, oc, n_levels, 2 * p), (0, 2, 1, 3)).reshape(octs, n_levels, oc * 2 * p)

    return (tcomp.astype(BF16), hc.astype(BF16), hoc.astype(BF16),
            lanes_by_group(a1).astype(F32), lanes_by_group(a2).astype(F32))


def _s5_replicate():
    r = lax.broadcasted_iota(jnp.int32, (S5_CHUNK * S5_GROUP, S5_CHUNK * 128), 0)
    c = lax.broadcasted_iota(jnp.int32, (S5_CHUNK * S5_GROUP, S5_CHUNK * 128), 1)
    return ((r // S5_GROUP == c // 128) & (r % S5_GROUP == c % S5_GROUP)).astype(BF16)


def _s5_core_kernel(u_ref, tc_ref, hc_ref, hoc_ref, a1_ref, a2_ref, rep_ref, y_ref, toep_scr, hin_scr, hout_scr,
                    *, n_levels):
    tc = S5_CHUNK
    lanes = u_ref.shape[1]
    state_lanes = hc_ref.shape[1]

    @pl.when(pl.program_id(1) == 0)
    def _():
        kd = toep_scr.shape[1]
        rb = 256
        col_g = (lax.broadcasted_iota(jnp.int32, (rb, kd), 1) // S5_GROUP) % S5_OCT
        row_i = lax.broadcasted_iota(jnp.int32, (rb, kd), 0)
        for i in range(toep_scr.shape[0] // rb):
            full = jnp.dot(tc_ref[i * rb:(i + 1) * rb, :], rep_ref[...], preferred_element_type=F32)
            row_g = ((row_i + i * rb) // S5_GROUP) % S5_OCT
            toep_scr[i * rb:(i + 1) * rb, :] = jnp.where(row_g == col_g, full, 0.0).astype(BF16)
        for i in range(hout_scr.shape[0] // rb):
            full = jnp.dot(hoc_ref[i * rb:(i + 1) * rb, :], rep_ref[...], preferred_element_type=F32)
            row_g = (row_i + i * rb) // state_lanes
            hout_scr[i * rb:(i + 1) * rb, :] = jnp.where(row_g == col_g, full, 0.0).astype(BF16)
        hrow_g = (lax.broadcasted_iota(jnp.int32, hc_ref.shape, 0) // S5_GROUP) % S5_OCT
        for g in range(S5_OCT):
            hin_scr[:, g * state_lanes:(g + 1) * state_lanes] = jnp.where(hrow_g == g, hc_ref[...], 0.0).astype(BF16)

    rows = u_ref.shape[0] // tc
    x = jnp.concatenate([u_ref[pl.ds(t, rows, stride=tc), :] for t in range(tc)], axis=1).astype(BF16)
    s_loc = jnp.dot(x, hin_scr[...], preferred_element_type=F32)
    pos = lax.broadcasted_iota(jnp.int32, (rows, state_lanes), 0)
    s_in = []
    for g in range(S5_OCT):
        ln = slice(g * state_lanes, (g + 1) * state_lanes)
        s = s_loc[:, ln]
        for k in range(n_levels):
            d = 1 << k
            prev = jnp.where(pos >= d, pltpu.roll(s, d, axis=0), 0.0)
            s = s + a1_ref[k:k + 1, ln] * prev + a2_ref[k:k + 1, ln] * pltpu.roll(prev, state_lanes // 2, axis=1)
        s_in.append(jnp.where(pos >= 1, pltpu.roll(s, 1, axis=0), 0.0).astype(BF16))
    y = (jnp.dot(x, toep_scr[...], preferred_element_type=F32)
         + jnp.dot(jnp.concatenate(s_in, axis=1), hout_scr[...], preferred_element_type=F32))
    for t in range(tc):
        y_ref[pl.ds(t, rows, stride=tc), :] = y[:, t * lanes:(t + 1) * lanes]


def _s5_core(proj, u_lane_block, tcomp, hc, hoc, a1, a2, rep, layer, batch, seq):
    t = proj.shape[0]
    _, octs, kdim, tn = tcomp.shape
    sl = hc.shape[-1]
    sw = hoc.shape[2]
    n_levels = a1.shape[2]
    lanes = kdim // S5_CHUNK
    kern = functools.partial(_s5_core_kernel, n_levels=n_levels)
    return pl.pallas_call(
        kern,
        grid=(octs, batch),
        in_specs=[pl.BlockSpec((seq, lanes), lambda o, b: (b, u_lane_block + o)),
                  pl.BlockSpec((None, None, kdim, tn), lambda o, b: (layer, o, 0, 0)),
                  pl.BlockSpec((None, None, kdim, sl), lambda o, b: (layer, o, 0, 0)),
                  pl.BlockSpec((None, None, sw, tn), lambda o, b: (layer, o, 0, 0)),
                  pl.BlockSpec((None, None, n_levels, sw), lambda o, b: (layer, o, 0, 0)),
                  pl.BlockSpec((None, None, n_levels, sw), lambda o, b: (layer, o, 0, 0)),
                  pl.BlockSpec((tn, kdim), lambda o, b: (0, 0))],
        out_specs=pl.BlockSpec((seq, lanes), lambda o, b: (b, o)),
        out_shape=jax.ShapeDtypeStruct((t, octs * lanes), F32),
        scratch_shapes=[pltpu.VMEM((kdim, kdim), BF16), pltpu.VMEM((kdim, sw), BF16), pltpu.VMEM((sw, kdim), BF16)],
        compiler_params=_cparams(("parallel", "arbitrary"), 48),
    )(proj, tcomp, hc, hoc, a1, a2, rep)


def _gdn_kernel(q_ref, k_ref, v_ref, qh_ref, kh_ref, vh_ref, z_ref, sm_ref, cq_ref, ck_ref, cv_ref,
                hp_ref, nw_ref, tril_ref, cmask_ref, lmask_ref, o_ref, betab, gcb, grow, st_scr,
                cat_q, cat_k, cat_v, *, heads):
    tt = GDN_TILE
    hd = HEAD_DIM
    nck = tt // GDN_CHUNK
    first = pl.program_id(1) == 0

    @pl.when(first)
    def _():
        st_scr[...] = jnp.zeros_like(st_scr)

    sm = sm_ref[...]
    hp = hp_ref[...]
    beta_all = jax.nn.sigmoid(sm)
    g_all = hp[0:1, :] * _softplus(sm + hp[1:2, :])
    gc_all = jnp.dot(tril_ref[...], g_all, precision=HIGHEST, preferred_element_type=F32)
    gc_t = gc_all.T
    for h in range(heads):
        betab[h] = jnp.broadcast_to(beta_all[:, h:h + 1], (tt, hd))
        gcb[h] = jnp.broadcast_to(gc_all[:, heads + h:heads + h + 1], (tt, hd))
        grow[h:h + 1, :] = gc_t[heads + h:heads + h + 1, :]

    eye = (lax.broadcasted_iota(jnp.int32, (tt, tt), 0) == lax.broadcasted_iota(jnp.int32, (tt, tt), 1)).astype(F32)
    n_lv = lmask_ref.shape[0]

    for x_ref, halo_ref, cat in ((q_ref, qh_ref, cat_q), (k_ref, kh_ref, cat_k), (v_ref, vh_ref, cat_v)):
        cat[0:8, :] = jnp.where(first, 0.0, halo_ref[...])
        cat[8:8 + tt, :] = x_ref[...]

    def conv_silu(cat, cw_ref, hs):
        w = cw_ref[:, pl.ds(hs, hd)]
        acc = cat[pl.ds(8, tt), pl.ds(hs, hd)] * w[CONV_K - 1:CONV_K, :]
        for s in range(1, CONV_K):
            acc = acc + cat[pl.ds(8 - s, tt), pl.ds(hs, hd)] * w[CONV_K - 1 - s:CONV_K - s, :]
        return _silu(acc)

    def heads_lockstep(hs_idx):
        each = lambda f, *ls: [f(*a) for a in zip(*ls)]
        hs = [pl.multiple_of(h * hd, hd) for h in hs_idx]
        q = [conv_silu(cat_q, cq_ref, o) for o in hs]
        k = [conv_silu(cat_k, ck_ref, o) for o in hs]
        v = [conv_silu(cat_v, cv_ref, o) for o in hs]
        qn = each(lambda a: a * (lax.rsqrt(jnp.sum(a * a, axis=-1, keepdims=True) + RMS_EPS) * (hd ** -0.5)), q)
        kn = each(lambda a: a * lax.rsqrt(jnp.sum(a * a, axis=-1, keepdims=True) + RMS_EPS), k)
        bb = [betab[h] for h in hs_idx]
        gc = [gcb[h] for h in hs_idx]
        gr = [grow[pl.ds(h, 1), :] for h in hs_idx]
        decay = each(lambda c, r: jnp.exp(jnp.where(cmask_ref[0] > 0.0, jnp.concatenate([c] * (tt // hd), axis=1) - r, -1e30)),
                     gc, gr)
        kb = each(lambda a, b: a * b, kn, bb)
        knb = each(lambda a: a.astype(BF16), kn)
        lfull = each(lambda a, b, dc: _nt_dot(a.astype(BF16), b) * dc, kb, knb, decay)
        lmat = each(lambda a: a.astype(BF16), lfull)
        tinv = each(lambda a: eye - a * cmask_ref[1], lfull)
        for lv in range(1, n_lv):
            d = 1 << lv
            nb = tt // d
            tb = each(lambda a: a.astype(BF16), tinv)
            bd = each(lambda m: m * lmask_ref[lv], lmat)
            if d % 8 == 0:
                odd = each(lambda a: jnp.concatenate([a[i * d:(i + 1) * d] for i in range(1, nb, 2)], axis=0), tinv)
                tl = each(lambda a, b: jnp.dot(a.astype(BF16), b, preferred_element_type=F32).astype(BF16), odd, bd)
                upd = each(lambda a, b, c: a - jnp.dot(b, c, preferred_element_type=F32), odd, tl, tb)
                tinv = each(lambda a, u: jnp.concatenate(
                    [a[i * d:(i + 1) * d] if i % 2 == 0 else u[(i // 2) * d:(i // 2 + 1) * d] for i in range(nb)],
                    axis=0), tinv, upd)
            else:
                tl = each(lambda a, b: jnp.dot(a, b, preferred_element_type=F32).astype(BF16), tb, bd)
                tinv = each(lambda a, b, c: a - jnp.dot(b, c, preferred_element_type=F32), tinv, tl, tb)
        eg = each(jnp.exp, gc)
        rhs = each(lambda a, e, b, c: jnp.concatenate([a * e, b * c], axis=1).astype(BF16), kb, eg, v, bb)
        wu = each(lambda a, b: jnp.dot(a.astype(BF16), b, preferred_element_type=F32).astype(BF16), tinv, rhs)
        attn = each(lambda a, b, dc: (_nt_dot(a.astype(BF16), b) * dc).astype(BF16), qn, knb, decay)
        awu = each(lambda a, b: jnp.dot(a, b, preferred_element_type=F32), attn, wu)
        qp = each(lambda a, e, b: (a * e - b[:, :hd]).astype(BF16), qn, eg, awu)
        s = [st_scr[h] for h in hs_idx]
        outs = [[] for _ in hs_idx]
        for c in range(nck):
            r0 = c * GDN_CHUNK
            r1 = r0 + GDN_CHUNK
            gl = each(lambda a: a[r1 - 1:r1, :], gc)
            kd = each(lambda a, l, g: (a[r0:r1] * jnp.exp(l - g[r0:r1])).astype(BF16), kn, gl, gc)
            m = each(lambda a, b: _tn_dot(a, b[r0:r1]), kd, wu)
            sb = each(lambda a: a.astype(BF16), s)
            for i, (a, b, y) in enumerate(zip(qp, sb, awu)):
                outs[i].append(jnp.dot(a[r0:r1], b, preferred_element_type=F32) + y[r0:r1, hd:])
            s = each(lambda a, l, mm, b: a * jnp.exp(l) - jnp.dot(mm[:, :hd].astype(BF16), b,
                                                                  preferred_element_type=F32) + mm[:, hd:],
                     s, gl, m, sb)
        for i, h in enumerate(hs_idx):
            st_scr[h] = s[i]
            o = jnp.concatenate(outs[i], axis=0)
            z = z_ref[:, pl.ds(hs[i], hd)]
            o_ref[:, pl.ds(hs[i], hd)] = (o * _rms_scale(o) * nw_ref[...] * _silu(z)).astype(o_ref.dtype)

    def trip(j, carry):
        heads_lockstep([GDN_HEADS_PER_TRIP * j + i for i in range(GDN_HEADS_PER_TRIP)])
        return carry

    lax.fori_loop(0, heads // GDN_HEADS_PER_TRIP, trip, 0)


def _gdn_masks():
    tt = GDN_TILE
    ri = lax.broadcasted_iota(jnp.int32, (tt, tt), 0)
    ci = lax.broadcasted_iota(jnp.int32, (tt, tt), 1)
    same = (ri // GDN_CHUNK) == (ci // GDN_CHUNK)
    causal = same & (ci <= ri)
    tril = causal.astype(F32)
    cmask = jnp.stack([tril, (((ri % 2) == 1) & (ci == ri - 1)).astype(F32)])
    levels = []
    d = 1
    while d < GDN_CHUNK:
        levels.append((((ri // d) % 2) == 1) & ((ci // d) == (ri // d) - 1))
        d *= 2
    return tril, cmask, jnp.stack(levels).astype(BF16)


def _gdn(proj, small, conv_w, head_params, norm_w, masks, cols, batch, seq, heads):
    t = proj.shape[0]
    tt = GDN_TILE
    width = heads * HEAD_DIM
    tiles = seq // tt
    cq, ck, cv, cz = cols
    tril, cmask, lmask = masks

    def cur(col):
        return pl.BlockSpec((tt, width), lambda b, i: (b * tiles + i, col))

    def halo(col):
        return pl.BlockSpec((8, width), lambda b, i: (jnp.maximum((b * tiles + i) * (tt // 8) - 1, 0), col))

    def cw(col):
        return pl.BlockSpec((CONV_K, width), lambda b, i: (0, col))

    kern = functools.partial(_gdn_kernel, heads=heads)
    return pl.pallas_call(
        kern,
        grid=(batch, tiles),
        in_specs=[cur(cq), cur(ck), cur(cv), halo(cq), halo(ck), halo(cv), cur(cz),
                  pl.BlockSpec((tt, 128), lambda b, i: (b * tiles + i, 0)),
                  cw(0), cw(1), cw(2),
                  pl.BlockSpec((8, 128), lambda b, i: (0, 0)),
                  pl.BlockSpec((1, HEAD_DIM), lambda b, i: (0, 0)),
                  pl.BlockSpec((tt, tt), lambda b, i: (0, 0)),
                  pl.BlockSpec(cmask.shape, lambda b, i: (0, 0, 0)),
                  pl.BlockSpec(lmask.shape, lambda b, i: (0, 0, 0))],
        out_specs=pl.BlockSpec((tt, width), lambda b, i: (b * tiles + i, 0)),
        out_shape=jax.ShapeDtypeStruct((t, width), BF16),
        scratch_shapes=[pltpu.VMEM((heads, tt, HEAD_DIM), F32),
                        pltpu.VMEM((heads, tt, HEAD_DIM), F32),
                        pltpu.VMEM((8, tt), F32),
                        pltpu.VMEM((heads, HEAD_DIM, HEAD_DIM), F32),
                        pltpu.VMEM((tt + 8, width), F32),
                        pltpu.VMEM((tt + 8, width), F32),
                        pltpu.VMEM((tt + 8, width), F32)],
        compiler_params=_cparams(("parallel", "arbitrary"), 40),
    )(proj, proj, proj, proj, proj, proj, proj, small, conv_w, conv_w, conv_w, head_params, norm_w,
      tril, cmask, lmask)


def _route(h, wr_hl, rb):
    h_hi = h.astype(BF16)
    h_lo = (h - h_hi.astype(F32)).astype(BF16)
    first = jnp.dot(h_hi, wr_hl[...], preferred_element_type=F32)
    logits = first[:, :128] + first[:, 128:] + jnp.dot(h_lo, wr_hl[:, :128], preferred_element_type=F32)
    logits = logits.T[:N_EXPERTS]
    scores = jax.nn.sigmoid(logits)
    sel = scores + rb
    sel_rows = [sel[e:e + 1, :] for e in range(N_EXPERTS)]
    score_rows = [scores[e:e + 1, :] for e in range(N_EXPERTS)]
    epg = EXPERTS_PER_GROUP
    best_score = None
    best_group = None
    for g in range(N_GROUPS):
        rows = sel_rows[g * epg:(g + 1) * epg]
        gs = None
        for a in range(epg):
            for b in range(a + 1, epg):
                pair = rows[a] + rows[b]
                gs = pair if gs is None else jnp.maximum(gs, pair)
        if g == 0:
            best_score, best_group = gs, jnp.zeros_like(gs)
        else:
            better = gs > best_score
            best_group = jnp.where(better, float(g), best_group)
            best_score = jnp.where(better, gs, best_score)
    in_sel = []
    in_score = []
    for j in range(epg):
        a = sel_rows[j]
        b = score_rows[j]
        for g in range(1, N_GROUPS):
            pick = best_group == float(g)
            a = jnp.where(pick, sel_rows[g * epg + j], a)
            b = jnp.where(pick, score_rows[g * epg + j], b)
        in_sel.append(a)
        in_score.append(b)
    i1, m1, w1 = jnp.zeros_like(in_sel[0]), in_sel[0], in_score[0]
    for j in range(1, epg):
        better = in_sel[j] > m1
        i1 = jnp.where(better, float(j), i1)
        m1 = jnp.where(better, in_sel[j], m1)
        w1 = jnp.where(better, in_score[j], w1)
    i2 = m2 = w2 = None
    for j in range(epg):
        cand = jnp.where(i1 == float(j), -jnp.inf, in_sel[j])
        if j == 0:
            i2, m2, w2 = jnp.zeros_like(cand), cand, in_score[0]
        else:
            better = cand > m2
            i2 = jnp.where(better, float(j), i2)
            m2 = jnp.where(better, cand, m2)
            w2 = jnp.where(better, in_score[j], w2)
    total = w1 + w2
    zero = jnp.zeros_like(w1)
    return jnp.concatenate([best_group * epg + i1, best_group * epg + i2, w1 / total, w2 / total,
                            zero, zero, zero, zero], axis=0)


def _merge_kernel(yl_ref, u_ref, sd_ref, wglu_ref, bglu_ref, yb_ref, ga_ref, gb_ref, x_ref, mod_ref, wa_ref, wb_ref,
                  wo_ref, nw_ref, wrt_ref, rb_ref, o_ref, h_ref, r_ref):
    y = jax.nn.gelu(yl_ref[...] + sd_ref[...] * u_ref[...])
    glu = jnp.dot(y.astype(BF16), wglu_ref[...], preferred_element_type=F32) + bglu_ref[...]
    ya = (y * jax.nn.sigmoid(glu)).astype(BF16)
    pa = jnp.dot(ya, wa_ref[...], preferred_element_type=F32)
    pb = jnp.dot(yb_ref[...], wb_ref[...], preferred_element_type=F32)
    merged = jax.nn.sigmoid(ga_ref[...]) * pa + jax.nn.sigmoid(gb_ref[...]) * pb
    out = jnp.dot(merged.astype(BF16), wo_ref[...], preferred_element_type=F32)
    xn = x_ref[...] + mod_ref[2:3, :] * out
    o_ref[...] = xn
    h = xn * _rms_scale(xn) * nw_ref[...] * (1.0 + mod_ref[4:5, :]) + mod_ref[3:4, :]
    h_ref[...] = _pack_halves(h)
    r_ref[...] = _route(h, wrt_ref, rb_ref[...])


def _merge(y_lin, yb, proj, x2, modl, s5_tail, wa, wb, wo, nw2, wr_t, rb, layer, seq, u_col, ga_col, gb_col):
    t, d = x2.shape
    w = y_lin.shape[1]
    d_skip, w_glu, b_glu = s5_tail
    tm = min(seq, 256)
    per_batch = seq // tm
    const = dict(pipeline_mode=pl.Buffered(1))
    return pl.pallas_call(
        _merge_kernel,
        grid=(t // tm,),
        in_specs=[pl.BlockSpec((tm, w), lambda i: (i, 0)),
                  pl.BlockSpec((tm, w), lambda i: (i, u_col)),
                  pl.BlockSpec((1, w), lambda i: (0, 0)),
                  pl.BlockSpec((None, w, w), lambda i: (layer, 0, 0), **const),
                  pl.BlockSpec((1, w), lambda i: (0, 0)),
                  pl.BlockSpec((tm, w), lambda i: (i, 0)),
                  pl.BlockSpec((tm, d), lambda i: (i, ga_col)),
                  pl.BlockSpec((tm, d), lambda i: (i, gb_col)),
                  pl.BlockSpec((tm, d), lambda i: (i, 0)),
                  pl.BlockSpec((None, 6, d), lambda i: (i // per_batch, 0, 0)),
                  pl.BlockSpec((None, w, d), lambda i: (layer, 0, 0), **const),
                  pl.BlockSpec((None, w, d), lambda i: (layer, 0, 0), **const),
                  pl.BlockSpec((None, d, d), lambda i: (layer, 0, 0), **const),
                  pl.BlockSpec((1, d), lambda i: (0, 0)),
                  pl.BlockSpec((d, 256), lambda i: (0, 0)),
                  pl.BlockSpec((N_EXPERTS, 1), lambda i: (0, 0))],
        out_specs=[pl.BlockSpec((tm, d), lambda i: (i, 0)),
                   pl.BlockSpec((tm, d // 2), lambda i: (i, 0)),
                   pl.BlockSpec((8, tm), lambda i: (0, i))],
        out_shape=[jax.ShapeDtypeStruct((t, d), F32), jax.ShapeDtypeStruct((t, d // 2), jnp.uint32),
                   jax.ShapeDtypeStruct((8, t), F32)],
        compiler_params=_cparams(("parallel",), 52),
    )(y_lin, proj, d_skip, w_glu, b_glu, yb, proj, proj, x2, modl, wa, wb, wo, nw2, wr_t, rb)


def _moe_kernel(te_ref, nu_ref, x_ref, wg_ref, wu_ref, wd_ref, o_ref):
    used = pl.program_id(0) < nu_ref[0]

    @pl.when(used)
    def _():
        x_lo, x_hi = _unpack_halves(x_ref[...])
        x_lo = x_lo.astype(BF16)
        x_hi = x_hi.astype(BF16)
        half = x_lo.shape[1]

        def up(w_ref):
            return (jnp.dot(x_lo, w_ref[:half, :], preferred_element_type=F32)
                    + jnp.dot(x_hi, w_ref[half:, :], preferred_element_type=F32))

        hid = (_silu(up(wg_ref)) * up(wu_ref)).astype(BF16)
        o_ref[...] = _pack_halves(jnp.dot(hid, wd_ref[...], preferred_element_type=F32))

    @pl.when(jnp.logical_not(used))
    def _():
        o_ref[...] = jnp.zeros_like(o_ref)


def _cast_kernel(w_ref, after_ref, o_ref, done_ref):
    del after_ref
    o_ref[...] = w_ref[...].astype(o_ref.dtype)
    done_ref[...] = jnp.zeros_like(done_ref)


def _expert_weights_bf16(w, layer, after):
    _, e, a, b = w.shape
    ta = min(a, (1 << 20) // b)
    return pl.pallas_call(
        _cast_kernel,
        grid=(e, a // ta),
        in_specs=[pl.BlockSpec((None, None, ta, b), lambda i, j: (layer, i, j, 0)),
                  pl.BlockSpec((8, 128), lambda i, j: (0, 0))],
        out_specs=[pl.BlockSpec((None, ta, b), lambda i, j: (i, j, 0)),
                   pl.BlockSpec((8, 128), lambda i, j: (0, 0))],
        out_shape=[jax.ShapeDtypeStruct((e, a, b), BF16), jax.ShapeDtypeStruct((8, 128), F32)],
        compiler_params=_cparams(("arbitrary", "arbitrary"), 32),
    )(w, after)


def _moe_experts(tile_expert, n_used, xs, wg, wu, wd, tm):
    p, dh = xs.shape
    d = 2 * dh
    f = wg.shape[-1]
    grid_spec = pltpu.PrefetchScalarGridSpec(
        num_scalar_prefetch=2,
        grid=(p // tm,),
        in_specs=[pl.BlockSpec((tm, dh), lambda i, te, nu: (i, 0)),
                  pl.BlockSpec((None, d, f), lambda i, te, nu: (te[i], 0, 0)),
                  pl.BlockSpec((None, d, f), lambda i, te, nu: (te[i], 0, 0)),
                  pl.BlockSpec((None, f, d), lambda i, te, nu: (te[i], 0, 0))],
        out_specs=pl.BlockSpec((tm, dh), lambda i, te, nu: (i, 0)),
    )
    return pl.pallas_call(
        _moe_kernel,
        grid_spec=grid_spec,
        out_shape=jax.ShapeDtypeStruct((p, dh), jnp.uint32),
        compiler_params=_cparams(("arbitrary",), 52),
    )(tile_expert, n_used, xs, wg, wu, wd)


def _dispatch_plan(expert_rows, tm, n_tiles):
    flat_e = expert_rows.astype(jnp.int32).reshape(-1)
    onehot = (flat_e[:, None] == jnp.arange(N_EXPERTS, dtype=jnp.int32)[None, :]).astype(jnp.int32)
    csum = jnp.cumsum(onehot, axis=0)
    rank = jnp.sum((csum - onehot) * onehot, axis=1)
    sizes = csum[-1]
    padded = ((sizes + tm - 1) // tm) * tm
    pad_end = jnp.cumsum(padded)
    pad_start = pad_end - padded
    dest = pad_start[flat_e] + rank
    tile_start = jnp.arange(n_tiles, dtype=jnp.int32) * tm
    tile_expert = jnp.minimum(jnp.sum((tile_start[:, None] >= pad_end[None, :]).astype(jnp.int32), axis=1),
                              N_EXPERTS - 1).astype(jnp.int32)
    n_used = (pad_end[-1] // tm).astype(jnp.int32).reshape(1)
    dest = dest.astype(jnp.int32)
    n_slots = flat_e.shape[0]
    tokens = n_slots // TOP_K
    n_fill = n_tiles * tm - n_slots
    fill_count = jnp.concatenate([padded - sizes, (n_tiles * tm - pad_end[-1])[None]])
    fill_end = jnp.cumsum(fill_count)
    first_free = jnp.concatenate([pad_start + sizes, pad_end[-1:]])
    j = jnp.arange(n_fill, dtype=jnp.int32)
    region = (j[:, None] >= fill_end[None, :]).astype(jnp.int32).sum(axis=1)
    shift = first_free - (fill_end - fill_count)
    fill_key = j + jnp.sum((region[:, None] == jnp.arange(N_EXPERTS + 1)[None, :]) * shift[None, :], axis=1)
    keys = jnp.concatenate([dest, fill_key.astype(jnp.int32)])
    vals = jnp.concatenate([jnp.arange(n_slots, dtype=jnp.int32) % tokens, j % tokens])
    tok_bits = max(tokens - 1, 1).bit_length()
    assert (n_tiles * tm - 1).bit_length() + tok_bits <= 31
    src_tok = jnp.sort(keys * (1 << tok_bits) + vals) & ((1 << tok_bits) - 1)
    return dest, src_tok, tile_expert, n_used


def _combine_kernel(x_ref, y0_ref, y1_ref, gw_ref, mod_ref, fw_ref, *rest, final_norm):
    o_ref = rest[-1]
    gw = gw_ref[...].T[:, TOP_K:2 * TOP_K]
    y0_lo, y0_hi = _unpack_halves(y0_ref[...])
    y1_lo, y1_hi = _unpack_halves(y1_ref[...])
    moe = jnp.concatenate([gw[:, 0:1] * y0_lo + gw[:, 1:2] * y1_lo, gw[:, 0:1] * y0_hi + gw[:, 1:2] * y1_hi], axis=1)
    xn = x_ref[...] + mod_ref[5:6, :] * moe
    if final_norm:
        xn = xn * _rms_scale(xn) * fw_ref[...]
    o_ref[...] = xn


def _combine(x2, y_pairs, gate_w, modl, final_w, seq, final_norm, ties=()):
    t, d = x2.shape
    tm = min(seq, 512)
    per_batch = seq // tm
    nblk = t // tm
    return pl.pallas_call(
        functools.partial(_combine_kernel, final_norm=final_norm),
        grid=(nblk,),
        in_specs=[pl.BlockSpec((tm, d), lambda i: (i, 0)),
                  pl.BlockSpec((tm, d // 2), lambda i: (i, 0)),
                  pl.BlockSpec((tm, d // 2), lambda i: (i + nblk, 0)),
                  pl.BlockSpec((8, tm), lambda i: (0, i)),
                  pl.BlockSpec((None, 6, d), lambda i: (i // per_batch, 0, 0)),
                  pl.BlockSpec((1, d), lambda i: (0, 0))] + [pl.BlockSpec((8, 128), lambda i: (0, 0)) for _ in ties],
        out_specs=pl.BlockSpec((tm, d), lambda i: (i, 0)),
        out_shape=jax.ShapeDtypeStruct((t, d), F32),
        compiler_params=_cparams(("parallel",), 40),
    )(x2, y_pairs, y_pairs, gate_w, modl, final_w, *ties)


def kernel(x, c, w_ada, b_ada, norm1_w, norm2_w, w_in, s5_a_re, s5_a_im, s5_log_dt, s5_b_re, s5_b_im, s5_c_re, s5_c_im, s5_d, s5_w_glu, s5_b_glu, gdn_conv_w, gdn_a_log, gdn_dt_bias, gdn_norm_w, w_proj_a, w_proj_b, w_out, w_router, router_bias, w_gate, w_up, w_down, final_norm_w):
    batch, seq, d = x.shape
    depth = w_ada.shape[0]
    t = batch * seq
    s5w = s5_d.shape[1]
    gdw = w_proj_b.shape[1]
    heads = gdn_a_log.shape[1]
    assert gdw == heads * HEAD_DIM and s5w == gdw and d == 2 * gdw
    assert seq % GDN_TILE == 0 and seq % S5_CHUNK == 0
    n_levels = int(math.log2(seq // S5_CHUNK))
    assert S5_CHUNK << n_levels == seq and S5_OCT * S5_GROUP == 128

    c_pad = jnp.zeros((8, d), F32).at[:batch].set(c)
    mod = _ada(c_pad, w_ada, b_ada)

    o_u, o_q, o_k, o_v, o_z = 0, s5w, 2 * s5w, 3 * s5w, 4 * s5w
    o_ba = 5 * s5w
    o_ga = o_ba + 2 * heads
    o_gb = o_ga + d
    col_u, col_q, col_k, col_v, col_z = 4, 5, 6, 7, 8

    gdn_masks = _gdn_masks()
    s5_rep = _s5_replicate()
    s5_ops = jax.vmap(functools.partial(_s5_weights, n_levels=n_levels))(
        s5_a_re, s5_a_im, s5_log_dt, s5_b_re, s5_b_im, s5_c_re, s5_c_im)

    wr_hi = w_router.astype(BF16)
    wr_lo = (w_router - wr_hi.astype(F32)).astype(BF16)
    pad_e = ((0, 0), (0, 128 - N_EXPERTS))
    wr_t = jnp.concatenate([jnp.pad(wr_hi, pad_e), jnp.pad(wr_lo, pad_e)], axis=1)
    rb = router_bias.reshape(N_EXPERTS, 1).astype(F32)

    tm_e = min(512, t)
    n_tiles = (TOP_K * t) // tm_e + N_EXPERTS

    w_in_t = jnp.transpose(w_in, (0, 2, 1))
    w_big = _repack_in_proj(w_in_t, o_ga, 2 * d, o_ba)
    w_small = jnp.pad(w_in_t[:, o_ba:o_ga], ((0, 0), (0, 128 - 2 * heads), (0, 0))).astype(BF16)

    w_glu_b, w_proj_a_b, w_proj_b_b, w_out_b = (w.astype(BF16) for w in (s5_w_glu, w_proj_a, w_proj_b, w_out))

    x2 = x.reshape(t, d)
    for l in range(depth):
        modl = mod[l, :batch].reshape(batch, 6, d)
        proj, small = _inproj(x2, modl, norm1_w[l].reshape(1, d), w_big, w_small, l, seq)

        y_lin = _s5_core(proj, col_u * s5w // 128, *s5_ops, s5_rep, l, batch, seq)
        s5_tail = (s5_d[l].reshape(1, s5w), w_glu_b, s5_b_glu[l].reshape(1, s5w))

        head_params = jnp.zeros((8, 128), F32)
        head_params = head_params.at[0, heads:2 * heads].set(-jnp.exp(gdn_a_log[l]))
        head_params = head_params.at[1, heads:2 * heads].set(gdn_dt_bias[l])
        y_b = _gdn(proj, small, gdn_conv_w[l], head_params, gdn_norm_w[l].reshape(1, HEAD_DIM), gdn_masks,
                   (col_q, col_k, col_v, col_z), batch, seq, heads)

        x2, h2, route = _merge(y_lin, y_b, proj, x2, modl, s5_tail, w_proj_a_b, w_proj_b_b, w_out_b,
                               norm2_w[l].reshape(1, d), wr_t, rb, l, seq, col_u, 0, 1)

        dest, src_tok, tile_expert, n_used = _dispatch_plan(route[0:TOP_K], tm_e, n_tiles)
        xs = jnp.take(h2, src_tok, axis=0, mode='clip')
        if l == 0:
            (wg_l, _), (wu_l, _) = _expert_weights_bf16(w_gate, 0, h2), _expert_weights_bf16(w_up, 0, h2)
        wd_l, _ = _expert_weights_bf16(w_down, l, h2)
        ys = _moe_experts(tile_expert, n_used, xs, wg_l, wu_l, wd_l, tm_e)
        y_pairs = jnp.take(ys, dest, axis=0, mode='clip')
        ties = ()
        if l + 1 < depth:
            (wg_l, tie_g), (wu_l, tie_u) = _expert_weights_bf16(w_gate, l + 1, ys), _expert_weights_bf16(w_up, l + 1, ys)
            ties = (tie_g, tie_u)
        x2 = _combine(x2, y_pairs, route, modl, final_norm_w.reshape(1, d), seq, l == depth - 1, ties)

    return x2.reshape(batch, seq, d)
```
